```python
import math
import jax, jax.numpy as jnp
from jax import lax
import numpy as np

D_MODEL = 2048
BATCH = 4
SEQ = 2048
DEPTH = 1

DIFF_HEADS = 8
DIFF_QK_DIM = 64
DIFF_V_DIM = 128
DIFF_WIDTH = DIFF_HEADS * DIFF_V_DIM
SSD_HEADS = 8
SSD_HEAD_DIM = 64
SSD_WIDTH = SSD_HEADS * SSD_HEAD_DIM
SSD_GROUPS = 2
SSD_STATE = 128
SSD_CONV = 4
SSD_CHUNK = 128
SSD_CONV_DIM = SSD_WIDTH + 2 * SSD_GROUPS * SSD_STATE
XATTN_HEADS = 4
XATTN_HEAD_DIM = 128
XATTN_WIDTH = XATTN_HEADS * XATTN_HEAD_DIM
MEM_LEN = 256
D_MIX = DIFF_WIDTH + SSD_WIDTH + XATTN_WIDTH
ROPE_THETA = 10000.0
Q_BLOCK = 128
NORM_EPS = 1e-6

IN_SPLITS = (
    2 * DIFF_HEADS * DIFF_QK_DIM,
    2 * DIFF_HEADS * DIFF_QK_DIM,
    DIFF_WIDTH,
    DIFF_WIDTH,
    SSD_WIDTH,
    SSD_CONV_DIM,
    SSD_HEADS,
    XATTN_WIDTH,
    XATTN_WIDTH,
)
D_IN = sum(IN_SPLITS)

kernel_name = 'hymba_diffattn_ssd_memxattn_layer'


def rms_norm(x, w):
    xf = x.astype(jnp.float32)
    y = xf * lax.rsqrt(jnp.mean(xf * xf, axis=-1, keepdims=True) + NORM_EPS)
    return (y * w.astype(jnp.float32)).astype(x.dtype)


def rope_cos_sin(positions, dim):
    inv_freq = 1.0 / (ROPE_THETA ** (jnp.arange(0, dim, 2, dtype=jnp.float32) / dim))
    ang = positions.astype(jnp.float32)[..., None] * inv_freq
    ang = jnp.concatenate([ang, ang], axis=-1)
    return jnp.cos(ang), jnp.sin(ang)


def apply_rope(x, cos, sin):
    xf = x.astype(jnp.float32)
    x1, x2 = jnp.split(xf, 2, axis=-1)
    rot = jnp.concatenate([-x2, x1], axis=-1)
    c = cos[:, :, None, None, :]
    s = sin[:, :, None, None, :]
    return (xf * c + rot * s).astype(x.dtype)


def diff_attention(q, k, v, lam):
    B, H, _, S, Dk = q.shape
    Dv = v.shape[-1]
    nq = S // Q_BLOCK
    q_blocks = q.reshape(B, H, 2, nq, Q_BLOCK, Dk).transpose(3, 0, 1, 2, 4, 5)
    starts = jnp.arange(nq, dtype=jnp.int32) * Q_BLOCK
    key_pos = jnp.arange(S, dtype=jnp.int32)
    scale = Dk ** -0.5

    def one_block(args):
        q_blk, start = args
        s = jnp.einsum('bhcqd,bhckd->bhcqk', q_blk, k).astype(jnp.float32) * scale
        q_pos = start + jnp.arange(Q_BLOCK, dtype=jnp.int32)
        causal = key_pos[None, :] <= q_pos[:, None]
        s = jnp.where(causal, s, -jnp.inf)
        p = jax.nn.softmax(s, axis=-1)
        a = p[:, :, 0] - lam * p[:, :, 1]
        return jnp.einsum('bhqk,bhkv->bhqv', a.astype(v.dtype), v)

    out = lax.map(one_block, (q_blocks, starts))
    return out.transpose(1, 0, 3, 2, 4).reshape(B, S, H, Dv)


def causal_depthwise_conv(x, w, b):
    y = lax.conv_general_dilated(
        x, w[:, None, :].astype(x.dtype), window_strides=(1,),
        padding=[(SSD_CONV - 1, 0)], dimension_numbers=('NWC', 'WIO', 'NWC'),
        feature_group_count=x.shape[-1])
    return y + b.astype(x.dtype)


def ssd_chunked(xh, dt, A, Bg, Cg):
    Bsz, S, H, P = xh.shape
    N = Bg.shape[-1]
    rep = H // Bg.shape[2]
    L = SSD_CHUNK
    nc = S // L
    Bh = jnp.repeat(Bg.astype(jnp.float32), rep, axis=2).reshape(Bsz, nc, L, H, N)
    Ch = jnp.repeat(Cg.astype(jnp.float32), rep, axis=2).reshape(Bsz, nc, L, H, N)
    X = (xh.astype(jnp.float32) * dt[..., None]).reshape(Bsz, nc, L, H, P)
    a_dt = (dt * A).reshape(Bsz, nc, L, H).transpose(0, 3, 1, 2)
    a_cs = jnp.cumsum(a_dt, axis=-1)
    tri = jnp.tril(jnp.ones((L, L), dtype=bool))
    seg = a_cs[..., :, None] - a_cs[..., None, :]
    decay = jnp.exp(jnp.where(tri, seg, -jnp.inf))
    scores = jnp.einsum('bclhn,bcshn->bhcls', Ch, Bh) * decay
    y_diag = jnp.einsum('bhcls,bcshp->bclhp', scores, X)
    decay_to_end = jnp.exp(a_cs[..., -1:] - a_cs).transpose(0, 2, 3, 1)
    chunk_states = jnp.einsum('bclhn,bclhp->bchpn', Bh * decay_to_end[..., None], X)
    chunk_decay = jnp.exp(a_cs[..., -1])

    def step(state, inp):
        cs, dec = inp
        return state * dec[:, :, None, None] + cs, state

    init = jnp.zeros((Bsz, H, P, N), jnp.float32)
    _, prev_states = lax.scan(step, init, (chunk_states.transpose(1, 0, 2, 3, 4),
                                           chunk_decay.transpose(2, 0, 1)))
    prev_states = prev_states.transpose(1, 0, 2, 3, 4)
    decay_in = jnp.exp(a_cs).transpose(0, 2, 3, 1)
    y_off = jnp.einsum('bclhn,bchpn->bclhp', Ch, prev_states) * decay_in[..., None]
    return (y_diag + y_off).reshape(Bsz, S, H, P)


def hybrid_layer(x, mem, cos, sin, lambda_init, pre_norm_w, w_in, lambda_q1, lambda_k1,
                 lambda_q2, lambda_k2, diff_subln_w, conv_w, conv_b, dt_bias, a_log,
                 d_skip, ssd_norm_w, mem_norm_w, w_mem_kv, w_out, post_norm_w):
    B, S, _ = x.shape
    h = rms_norm(x, pre_norm_w)
    proj = h @ w_in
    offsets = [int(o) for o in np.cumsum(IN_SPLITS)[:-1]]
    dq, dk, dv, dg, z, xbc, dt_raw, xq, xg = jnp.split(proj, offsets, axis=-1)

    dq = apply_rope(dq.reshape(B, S, DIFF_HEADS, 2, DIFF_QK_DIM), cos, sin).transpose(0, 2, 3, 1, 4)
    dk = apply_rope(dk.reshape(B, S, DIFF_HEADS, 2, DIFF_QK_DIM), cos, sin).transpose(0, 2, 3, 1, 4)
    dv = dv.reshape(B, S, DIFF_HEADS, DIFF_V_DIM).transpose(0, 2, 1, 3)
    lam = (jnp.exp(jnp.sum(lambda_q1.astype(jnp.float32) * lambda_k1.astype(jnp.float32)))
           - jnp.exp(jnp.sum(lambda_q2.astype(jnp.float32) * lambda_k2.astype(jnp.float32)))
           + lambda_init)
    o = diff_attention(dq, dk, dv, lam)
    o = rms_norm(o, diff_subln_w) * (1.0 - lambda_init)
    diff_out = o.reshape(B, S, DIFF_WIDTH) * jax.nn.silu(dg)

    xbc = jax.nn.silu(causal_depthwise_conv(xbc, conv_w, conv_b))
    xs, bm, cm = jnp.split(xbc, [SSD_WIDTH, SSD_WIDTH + SSD_GROUPS * SSD_STATE], axis=-1)
    dt = jax.nn.softplus(dt_raw.astype(jnp.float32) + dt_bias.astype(jnp.float32))
    A = -jnp.exp(a_log.astype(jnp.float32))
    xh = xs.reshape(B, S, SSD_HEADS, SSD_HEAD_DIM)
    y = ssd_chunked(xh, dt, A, bm.reshape(B, S, SSD_GROUPS, SSD_STATE),
                    cm.reshape(B, S, SSD_GROUPS, SSD_STATE))
    y = y + xh.astype(jnp.float32) * d_skip.astype(jnp.float32)[:, None]
    y = y.reshape(B, S, SSD_WIDTH).astype(x.dtype) * jax.nn.silu(z)
    grp = SSD_WIDTH // SSD_GROUPS
    ssd_out = rms_norm(y.reshape(B, S, SSD_GROUPS, grp),
                       ssd_norm_w.reshape(SSD_GROUPS, grp)).reshape(B, S, SSD_WIDTH)

    mem_n = rms_norm(mem, mem_norm_w)
    mk, mv = jnp.split(mem_n @ w_mem_kv, 2, axis=-1)
    mk = mk.reshape(B, -1, XATTN_HEADS, XATTN_HEAD_DIM)
    mv = mv.reshape(B, -1, XATTN_HEADS, XATTN_HEAD_DIM)
    q = xq.reshape(B, S, XATTN_HEADS, XATTN_HEAD_DIM)
    s = jnp.einsum('bshd,bmhd->bhsm', q, mk).astype(jnp.float32) * (XATTN_HEAD_DIM ** -0.5)
    p = jax.nn.softmax(s, axis=-1)
    xo = jnp.einsum('bhsm,bmhd->bshd', p.astype(mv.dtype), mv).reshape(B, S, XATTN_WIDTH)
    xattn_out = xo * jax.nn.silu(xg)

    mixed = jnp.concatenate([diff_out, ssd_out, xattn_out], axis=-1)
    return x + rms_norm(mixed @ w_out, post_norm_w)


def setup_inputs(seed: int = 0) -> dict:
    key = jax.random.key(seed)
    ks = jax.random.split(key, 20)
    f32 = jnp.float32
    x = jax.random.normal(ks[0], (BATCH, SEQ, D_MODEL), f32)
    mem = jax.random.normal(ks[1], (BATCH, MEM_LEN, D_MODEL), f32)
    positions = jnp.broadcast_to(jnp.arange(SEQ, dtype=jnp.int32), (BATCH, SEQ))
    pre_norm_w = 1.0 + 0.02 * jax.random.normal(ks[2], (DEPTH, D_MODEL), f32)
    w_in = jax.random.normal(ks[3], (DEPTH, D_MODEL, D_IN), f32) * D_MODEL ** -0.5
    lambda_q1 = 0.1 * jax.random.normal(ks[4], (DEPTH, DIFF_QK_DIM), f32)
    lambda_k1 = 0.1 * jax.random.normal(ks[5], (DEPTH, DIFF_QK_DIM), f32)
    lambda_q2 = 0.1 * jax.random.normal(ks[6], (DEPTH, DIFF_QK_DIM), f32)
    lambda_k2 = 0.1 * jax.random.normal(ks[7], (DEPTH, DIFF_QK_DIM), f32)
    diff_subln_w = 1.0 + 0.02 * jax.random.normal(ks[8], (DEPTH, DIFF_V_DIM), f32)
    conv_w = jax.random.normal(ks[9], (DEPTH, SSD_CONV, SSD_CONV_DIM), f32) * SSD_CONV ** -0.5
    conv_b = 0.01 * jax.random.normal(ks[10], (DEPTH, SSD_CONV_DIM), f32)
    u = jax.random.uniform(ks[11], (DEPTH, SSD_HEADS), f32)
    dt0 = jnp.exp(u * (math.log(0.1) - math.log(0.001)) + math.log(0.001))
    dt_bias = dt0 + jnp.log(-jnp.expm1(-dt0))
    a_log = jnp.log(jax.random.uniform(ks[12], (DEPTH, SSD_HEADS), f32, 1.0, 16.0))
    d_skip = 1.0 + 0.1 * jax.random.normal(ks[13], (DEPTH, SSD_HEADS), f32)
    ssd_norm_w = 1.0 + 0.02 * jax.random.normal(ks[14], (DEPTH, SSD_WIDTH), f32)
    mem_norm_w = 1.0 + 0.02 * jax.random.normal(ks[15], (DEPTH, D_MODEL), f32)
    w_mem_kv = jax.random.normal(ks[16], (DEPTH, D_MODEL, 2 * XATTN_WIDTH), f32) * D_MODEL ** -0.5
    w_out = jax.random.normal(ks[17], (DEPTH, D_MIX, D_MODEL), f32) * D_MIX ** -0.5
    post_norm_w = 1.0 + 0.02 * jax.random.normal(ks[18], (DEPTH, D_MODEL), f32)
    return {'x': x, 'mem': mem, 'positions': positions, 'pre_norm_w': pre_norm_w,
            'w_in': w_in, 'lambda_q1': lambda_q1, 'lambda_k1': lambda_k1,
            'lambda_q2': lambda_q2, 'lambda_k2': lambda_k2, 'diff_subln_w': diff_subln_w,
            'conv_w': conv_w, 'conv_b': conv_b, 'dt_bias': dt_bias, 'a_log': a_log,
            'd_skip': d_skip, 'ssd_norm_w': ssd_norm_w, 'mem_norm_w': mem_norm_w,
            'w_mem_kv': w_mem_kv, 'w_out': w_out, 'post_norm_w': post_norm_w}


def reference(x, mem, positions, pre_norm_w, w_in, lambda_q1, lambda_k1, lambda_q2,
              lambda_k2, diff_subln_w, conv_w, conv_b, dt_bias, a_log, d_skip,
              ssd_norm_w, mem_norm_w, w_mem_kv, w_out, post_norm_w):
    cos, sin = rope_cos_sin(positions, DIFF_QK_DIM)
    h = x
    for i in range(DEPTH):
        lambda_init = 0.8 - 0.6 * math.exp(-0.3 * i)
        h = hybrid_layer(h, mem, cos, sin, lambda_init, pre_norm_w[i], w_in[i],
                         lambda_q1[i], lambda_k1[i], lambda_q2[i], lambda_k2[i],
                         diff_subln_w[i], conv_w[i], conv_b[i], dt_bias[i], a_log[i],
                         d_skip[i], ssd_norm_w[i], mem_norm_w[i], w_mem_kv[i],
                         w_out[i], post_norm_w[i])
    return h
```

```python
import functools
import math

import jax
import jax.numpy as jnp
from jax import lax
from jax.experimental import pallas as pl
from jax.experimental.pallas import tpu as pltpu

F32 = jnp.float32
BF16 = jnp.bfloat16

D_MODEL = 2048
DIFF_HEADS = 8
DIFF_QK_DIM = 64
DIFF_V_DIM = 128
DIFF_WIDTH = DIFF_HEADS * DIFF_V_DIM
SSD_HEADS = 8
SSD_HEAD_DIM = 64
SSD_WIDTH = SSD_HEADS * SSD_HEAD_DIM
SSD_GROUPS = 2
SSD_STATE = 128
SSD_CONV = 4
SSD_CHUNK = 128
SSD_CONV_DIM = SSD_WIDTH + 2 * SSD_GROUPS * SSD_STATE
XATTN_HEADS = 4
XATTN_HEAD_DIM = 128
XATTN_WIDTH = XATTN_HEADS * XATTN_HEAD_DIM
D_MIX = DIFF_WIDTH + SSD_WIDTH + XATTN_WIDTH
ROPE_THETA = 10000.0
NORM_EPS = 1e-6
LAMBDA_INIT = 0.8 - 0.6 * math.exp(-0.3 * 0)

LANES = 128
SUBLANES = 8
MIB = 1024 * 1024

COL_Q = 0
COL_K = COL_Q + DIFF_WIDTH
COL_V = COL_K + DIFF_WIDTH
COL_G = COL_V + DIFF_WIDTH
COL_Z = COL_G + DIFF_WIDTH
COL_XS = COL_Z + SSD_WIDTH
COL_B = COL_XS + SSD_WIDTH
COL_C = COL_B + SSD_GROUPS * SSD_STATE
COL_XQ = COL_C + SSD_GROUPS * SSD_STATE
COL_XG = COL_XQ + XATTN_WIDTH
N_MAIN = COL_XG + XATTN_WIDTH
REF_DT = COL_XQ
REF_XQ = REF_DT + SSD_HEADS

PROJ_TM = 1024
PROJ_TN = 512
NORM_ROWS = 256
ATT_TQ = 512
ATT_TK = 512
XATT_TQ = 512
OUT_TM = 512
ROPE_TM = 1024

NT_DIMS = (((1,), (1,)), ((), ()))
TN_DIMS = (((0,), (0,)), ((), ()))


def _rms_scale(x):
    return lax.rsqrt(jnp.mean(x * x, axis=-1, keepdims=True) + NORM_EPS)


def _silu(x):
    return x * jax.nn.sigmoid(x)


def _rope_table_kernel(pos_ref, invf_ref, cos_ref, sin_ref):
    ang = pos_ref[...].astype(F32) * invf_ref[...]
    lane = lax.broadcasted_iota(jnp.int32, ang.shape, 1)
    first_half = (lane % DIFF_QK_DIM) < (DIFF_QK_DIM // 2)
    s = jnp.sin(ang)
    cos_ref[...] = jnp.cos(ang)
    sin_ref[...] = jnp.where(first_half, -s, s)


def _rope_tables(positions):
    n = positions.size
    inv_freq = 1.0 / (ROPE_THETA ** (jnp.arange(0, DIFF_QK_DIM, 2, dtype=F32) / DIFF_QK_DIM))
    invf = jnp.tile(inv_freq, LANES // inv_freq.shape[0]).reshape(1, LANES)
    pos = positions.reshape(n, 1)
    return pl.pallas_call(
        _rope_table_kernel,
        grid=(n // ROPE_TM,),
        in_specs=[pl.BlockSpec((ROPE_TM, 1), lambda i: (i, 0)),
                  pl.BlockSpec((1, LANES), lambda i: (0, 0))],
        out_specs=[pl.BlockSpec((ROPE_TM, LANES), lambda i: (i, 0)),
                   pl.BlockSpec((ROPE_TM, LANES), lambda i: (i, 0))],
        out_shape=[jax.ShapeDtypeStruct((n, LANES), F32)] * 2,
        name="rope_tables",
    )(pos, invf)


def _rope(x, cos, sin_signed):
    lane = lax.broadcasted_iota(jnp.int32, x.shape, 1)
    first_half = (lane % DIFF_QK_DIM) < (DIFF_QK_DIM // 2)
    half = DIFF_QK_DIM // 2
    rot = jnp.where(first_half, pltpu.roll(x, LANES - half, 1), pltpu.roll(x, half, 1))
    return x * cos + rot * sin_signed


def _in_proj_kernel(x_ref, nw_ref, w_ref, wdt_ref, o_ref, dt_ref, h_ref):
    @pl.when(pl.program_id(1) == 0)
    def _():
        def norm_rows(i, carry):
            r = pl.ds(pl.multiple_of(i * NORM_ROWS, NORM_ROWS), NORM_ROWS)
            x = x_ref[r, :]
            h_ref[r, :] = (x * _rms_scale(x) * nw_ref[...]).astype(BF16)
            return carry
        lax.fori_loop(0, PROJ_TM // NORM_ROWS, norm_rows, 0)
        dt_ref[...] = jnp.dot(h_ref[...], wdt_ref[...], preferred_element_type=F32)

    o_ref[...] = jnp.dot(h_ref[...], w_ref[...], preferred_element_type=F32).astype(BF16)


def _in_proj(x2, pre_norm_w, w_main, w_dt):
    m = x2.shape[0]
    return pl.pallas_call(
        _in_proj_kernel,
        grid=(m // PROJ_TM, N_MAIN // PROJ_TN),
        in_specs=[pl.BlockSpec((PROJ_TM, D_MODEL), lambda i, j: (i, 0)),
                  pl.BlockSpec((1, D_MODEL), lambda i, j: (0, 0)),
                  pl.BlockSpec((D_MODEL, PROJ_TN), lambda i, j: (0, j)),
                  pl.BlockSpec((D_MODEL, LANES), lambda i, j: (0, 0))],
        out_specs=[pl.BlockSpec((PROJ_TM, PROJ_TN), lambda i, j: (i, j)),
                   pl.BlockSpec((PROJ_TM, LANES), lambda i, j: (i, 0))],
        out_shape=[jax.ShapeDtypeStruct((m, N_MAIN), BF16),
                   jax.ShapeDtypeStruct((m, LANES), F32)],
        scratch_shapes=[pltpu.VMEM((PROJ_TM, D_MODEL), BF16)],
        compiler_params=pltpu.CompilerParams(
            dimension_semantics=("arbitrary", "arbitrary"),
            vmem_limit_bytes=48 * MIB),
        name="in_proj",
    )(x2, pre_norm_w.reshape(1, D_MODEL), w_main, w_dt)


def _diff_attn_kernel(lam_ref, q_ref, k_ref, v_ref, g_ref, cos_ref, sin_ref, subw_ref,
                      o_ref, kr_ref, m_ref, l_ref, acc_ref, *, seq):
    qi = pl.program_id(2)

    @pl.when(qi == 0)
    def _():
        def rope_rows(i, carry):
            r = pl.ds(pl.multiple_of(i * ATT_TK, ATT_TK), ATT_TK)
            kr_ref[r, :] = _rope(k_ref[r, :].astype(F32), cos_ref[r, :], sin_ref[r, :]).astype(BF16)
            return carry
        lax.fori_loop(0, seq // ATT_TK, rope_rows, 0)

    rq = pl.ds(pl.multiple_of(qi * ATT_TQ, ATT_TQ), ATT_TQ)
    scale = DIFF_QK_DIM ** -0.5
    qf = _rope(q_ref[...].astype(F32), cos_ref[rq, :], sin_ref[rq, :]) * scale
    lane = lax.broadcasted_iota(jnp.int32, qf.shape, 1)
    q_maps = (jnp.where(lane < DIFF_QK_DIM, qf, 0.0).astype(BF16),
              jnp.where(lane >= DIFF_QK_DIM, qf, 0.0).astype(BF16))

    m_ref[...] = jnp.full(m_ref.shape, -jnp.inf, F32)
    l_ref[...] = jnp.zeros(l_ref.shape, F32)
    acc_ref[...] = jnp.zeros(acc_ref.shape, F32)

    row = lax.broadcasted_iota(jnp.int32, (ATT_TQ, ATT_TK), 0)
    col = lax.broadcasted_iota(jnp.int32, (ATT_TQ, ATT_TK), 1)
    causal = col <= row

    def tile(j, masked):
        rk = pl.ds(pl.multiple_of(j * ATT_TK, ATT_TK), ATT_TK)
        kt = kr_ref[rk, :]
        vt = v_ref[rk, :]
        for c in range(2):
            s = lax.dot_general(q_maps[c], kt, NT_DIMS, preferred_element_type=F32)
            if masked:
                s = jnp.where(causal, s, -jnp.inf)
            m_old = m_ref[c]
            m_new = jnp.maximum(m_old, jnp.max(s, axis=-1, keepdims=True))
            p = jnp.exp(s - m_new)
            alpha = jnp.exp(m_old - m_new)
            l_ref[c] = alpha * l_ref[c] + jnp.sum(p, axis=-1, keepdims=True)
            acc_ref[c] = alpha * acc_ref[c] + jnp.dot(p.astype(BF16), vt,
                                                      preferred_element_type=F32)
            m_ref[c] = m_new

    def full_tile(j, carry):
        tile(j, masked=False)
        return carry
    lax.fori_loop(0, qi, full_tile, 0)
    tile(qi, masked=True)

    lv = lam_ref[...]
    lam = (jnp.exp(jnp.sum(lv[0:1] * lv[1:2], axis=-1, keepdims=True))
           - jnp.exp(jnp.sum(lv[2:3] * lv[3:4], axis=-1, keepdims=True)) + LAMBDA_INIT)
    o = acc_ref[0] * (1.0 / l_ref[0]) - lam * (acc_ref[1] * (1.0 / l_ref[1]))
    o = o * _rms_scale(o) * subw_ref[...] * (1.0 - LAMBDA_INIT)
    o_ref[...] = (o * _silu(g_ref[...].astype(F32))).astype(BF16)


def _diff_attention(proj, cos, sin, lam_vecs, subln_w, batch, seq):
    assert ATT_TQ == ATT_TK and seq % ATT_TQ == 0
    nq = seq // ATT_TQ
    hb = DIFF_V_DIM // LANES
    kernel = functools.partial(_diff_attn_kernel, seq=seq)
    return pl.pallas_call(
        kernel,
        grid=(batch, DIFF_HEADS, nq),
        in_specs=[
            pl.BlockSpec((4, DIFF_QK_DIM), lambda b, h, q: (0, 0)),
            pl.BlockSpec((ATT_TQ, LANES), lambda b, h, q: (b * nq + q, COL_Q // LANES + h * hb)),
            pl.BlockSpec((seq, LANES), lambda b, h, q: (b, COL_K // LANES + h * hb)),
            pl.BlockSpec((seq, LANES), lambda b, h, q: (b, COL_V // LANES + h * hb)),
            pl.BlockSpec((ATT_TQ, LANES), lambda b, h, q: (b * nq + q, COL_G // LANES + h * hb)),
            pl.BlockSpec((seq, LANES), lambda b, h, q: (b, 0)),
            pl.BlockSpec((seq, LANES), lambda b, h, q: (b, 0)),
            pl.BlockSpec((1, DIFF_V_DIM), lambda b, h, q: (0, 0)),
        ],
        out_specs=pl.BlockSpec((ATT_TQ, LANES), lambda b, h, q: (b * nq + q, h * hb)),
        out_shape=jax.ShapeDtypeStruct((batch * seq, DIFF_WIDTH), BF16),
        scratch_shapes=[pltpu.VMEM((seq, LANES), BF16),
                        pltpu.VMEM((2, ATT_TQ, 1), F32),
                        pltpu.VMEM((2, ATT_TQ, 1), F32),
                        pltpu.VMEM((2, ATT_TQ, DIFF_V_DIM), F32)],
        compiler_params=pltpu.CompilerParams(
            dimension_semantics=("arbitrary", "arbitrary", "arbitrary"),
            vmem_limit_bytes=32 * MIB),
        name="diff_attn",
    )(lam_vecs, proj, proj, proj, proj, cos, sin, subln_w.reshape(1, DIFF_V_DIM))


CONV_HALO = SUBLANES


def _ssd_kernel(xs_ref, b_ref, c_ref, z_ref, dt_ref, cw_ref, cb_ref, dtb_ref, alog_ref,
                dsk_ref, nw_ref, o_ref, ext_ref, state_ref):
    L = SSD_CHUNK
    gs = SSD_GROUPS * SSD_STATE

    @pl.when(pl.program_id(1) == 0)
    def _():
        ext_ref[0:CONV_HALO, :] = jnp.zeros((CONV_HALO, SSD_CONV_DIM), F32)
        state_ref[...] = jnp.zeros(state_ref.shape, F32)

    ext_ref[CONV_HALO:CONV_HALO + L, 0:SSD_WIDTH] = xs_ref[...].astype(F32)
    ext_ref[CONV_HALO:CONV_HALO + L, SSD_WIDTH:SSD_WIDTH + gs] = b_ref[...].astype(F32)
    ext_ref[CONV_HALO:CONV_HALO + L, SSD_WIDTH + gs:SSD_CONV_DIM] = c_ref[...].astype(F32)
    conv = jnp.broadcast_to(cb_ref[...], (L, SSD_CONV_DIM))
    for k in range(SSD_CONV):
        start = CONV_HALO - (SSD_CONV - 1) + k
        conv = conv + ext_ref[start:start + L, :] * cw_ref[k:k + 1, :]
    ext_ref[0:CONV_HALO, :] = ext_ref[L:L + CONV_HALO, :]
    xbc = _silu(conv)
    xs = xbc[:, 0:SSD_WIDTH]
    bm = xbc[:, SSD_WIDTH:SSD_WIDTH + gs].astype(BF16)
    cm = xbc[:, SSD_WIDTH + gs:SSD_CONV_DIM].astype(BF16)

    dt_in = dt_ref[...] + dtb_ref[...]
    dt = jnp.maximum(dt_in, 0.0) + jnp.log1p(jnp.exp(-jnp.abs(dt_in)))
    a_dt = dt * (-jnp.exp(alog_ref[...]))
    row = lax.broadcasted_iota(jnp.int32, (L, L), 0)
    col = lax.broadcasted_iota(jnp.int32, (L, L), 1)
    lower = row >= col
    tril = lower.astype(BF16)
    hi = a_dt.astype(BF16)
    r1 = a_dt - hi.astype(F32)
    mid = r1.astype(BF16)
    lo = (r1 - mid.astype(F32)).astype(BF16)
    a_cs = (jnp.dot(tril, hi, preferred_element_type=F32)
            + jnp.dot(tril, mid, preferred_element_type=F32)
            + jnp.dot(tril, lo, preferred_element_type=F32))
    a_cs_t = a_cs.T
    a_end = a_cs[L - 1:L, :]
    decay_in = jnp.exp(a_cs)
    decay_to_end = jnp.exp(a_end - a_cs)
    chunk_decay = jnp.exp(a_end)

    lane = lax.broadcasted_iota(jnp.int32, (L, LANES), 1)
    left = lane < SSD_HEAD_DIM
    top = row < SSD_HEAD_DIM

    def pair_lanes(mat, p):
        return jnp.where(left, mat[:, 2 * p:2 * p + 1], mat[:, 2 * p + 1:2 * p + 2])

    ys = []
    pairs = SSD_HEADS // 2
    for p in range(pairs):
        g = (2 * p) // (SSD_HEADS // SSD_GROUPS)
        cg = cm[:, g * SSD_STATE:(g + 1) * SSD_STATE]
        bg = bm[:, g * SSD_STATE:(g + 1) * SSD_STATE]
        cb = lax.dot_general(cg, bg, NT_DIMS, preferred_element_type=F32)
        xs_p = xs[:, p * LANES:(p + 1) * LANES]
        x_dt = xs_p * pair_lanes(dt, p)
        x_dt_b = x_dt.astype(BF16)
        y_heads = []
        for hh in range(2):
            h = 2 * p + hh
            seg = a_cs[:, h:h + 1] - a_cs_t[h:h + 1, :]
            decay = jnp.exp(jnp.where(lower, seg, -jnp.inf))
            scores = (cb * decay).astype(BF16)
            y_heads.append(jnp.dot(scores, x_dt_b, preferred_element_type=F32))
        y_diag = jnp.where(left, y_heads[0], y_heads[1])
        st = state_ref[p]
        y_off = (lax.dot_general(cg, st.astype(BF16), NT_DIMS, preferred_element_type=F32)
                 * pair_lanes(decay_in, p))
        x_w = (x_dt * pair_lanes(decay_to_end, p)).astype(BF16)
        chunk_state = lax.dot_general(x_w, bg, TN_DIMS, preferred_element_type=F32)
        st_decay = jnp.where(top, chunk_decay[:, 2 * p:2 * p + 1],
                             chunk_decay[:, 2 * p + 1:2 * p + 2])
        state_ref[p] = st * st_decay + chunk_state
        y = y_diag + y_off + xs_p * dsk_ref[:, p * LANES:(p + 1) * LANES]
        ys.append(y * _silu(z_ref[:, p * LANES:(p + 1) * LANES].astype(F32)))

    grp_pairs = pairs // SSD_GROUPS
    grp_width = SSD_WIDTH // SSD_GROUPS
    for g in range(SSD_GROUPS):
        members = ys[g * grp_pairs:(g + 1) * grp_pairs]
        ssq = sum(jnp.sum(y * y, axis=-1, keepdims=True) for y in members)
        inv = lax.rsqrt(ssq / grp_width + NORM_EPS)
        for i, y in enumerate(members):
            c0 = (g * grp_pairs + i) * LANES
            o_ref[:, c0:c0 + LANES] = (y * inv * nw_ref[:, c0:c0 + LANES]).astype(BF16)


def _ssd(proj, dt_raw, conv_w, conv_b, dt_bias, a_log, d_skip, ssd_norm_w, batch, seq):
    nc = seq // SSD_CHUNK
    gs = SSD_GROUPS * SSD_STATE
    pad = LANES - SSD_HEADS
    row_map = lambda b, c: b * nc + c
    const = lambda b, c: (0, 0)
    return pl.pallas_call(
        _ssd_kernel,
        grid=(batch, nc),
        in_specs=[
            pl.BlockSpec((SSD_CHUNK, SSD_WIDTH), lambda b, c: (row_map(b, c), COL_XS // SSD_WIDTH)),
            pl.BlockSpec((SSD_CHUNK, gs), lambda b, c: (row_map(b, c), COL_B // gs)),
            pl.BlockSpec((SSD_CHUNK, gs), lambda b, c: (row_map(b, c), COL_C // gs)),
            pl.BlockSpec((SSD_CHUNK, SSD_WIDTH), lambda b, c: (row_map(b, c), COL_Z // SSD_WIDTH)),
            pl.BlockSpec((SSD_CHUNK, LANES), lambda b, c: (row_map(b, c), 0)),
            pl.BlockSpec((SSD_CONV, SSD_CONV_DIM), const),
            pl.BlockSpec((1, SSD_CONV_DIM), const),
            pl.BlockSpec((1, LANES), const),
            pl.BlockSpec((1, LANES), const),
            pl.BlockSpec((1, SSD_WIDTH), const),
            pl.BlockSpec((1, SSD_WIDTH), const),
        ],
        out_specs=pl.BlockSpec((SSD_CHUNK, SSD_WIDTH), lambda b, c: (row_map(b, c), 0)),
        out_shape=jax.ShapeDtypeStruct((batch * seq, SSD_WIDTH), BF16),
        scratch_shapes=[pltpu.VMEM((CONV_HALO + SSD_CHUNK, SSD_CONV_DIM), F32),
                        pltpu.VMEM((SSD_HEADS // 2, 2 * SSD_HEAD_DIM, SSD_STATE), F32)],
        compiler_params=pltpu.CompilerParams(
            dimension_semantics=("arbitrary", "arbitrary"),
            vmem_limit_bytes=32 * MIB),
        name="ssd",
    )(proj, proj, proj, proj, dt_raw, conv_w, conv_b.reshape(1, SSD_CONV_DIM),
      jnp.pad(dt_bias, (0, pad)).reshape(1, LANES), jnp.pad(a_log, (0, pad)).reshape(1, LANES),
      jnp.repeat(d_skip, SSD_HEAD_DIM).reshape(1, SSD_WIDTH), ssd_norm_w.reshape(1, SSD_WIDTH))


def _mem_kv_kernel(mem_ref, nw_ref, w_ref, o_ref):
    x = mem_ref[...]
    h = (x * _rms_scale(x) * nw_ref[...]).astype(BF16)
    o_ref[...] = jnp.dot(h, w_ref[...], preferred_element_type=F32).astype(BF16)


def _mem_kv(mem2, mem_norm_w, w_kv, batch, mem_len):
    return pl.pallas_call(
        _mem_kv_kernel,
        grid=(batch,),
        in_specs=[pl.BlockSpec((mem_len, D_MODEL), lambda b: (b, 0)),
                  pl.BlockSpec((1, D_MODEL), lambda b: (0, 0)),
                  pl.BlockSpec((D_MODEL, 2 * XATTN_WIDTH), lambda b: (0, 0))],
        out_specs=pl.BlockSpec((mem_len, 2 * XATTN_WIDTH), lambda b: (b, 0)),
        out_shape=jax.ShapeDtypeStruct((batch * mem_len, 2 * XATTN_WIDTH), BF16),
        compiler_params=pltpu.CompilerParams(
            dimension_semantics=("arbitrary",), vmem_limit_bytes=32 * MIB),
        name="mem_kv",
    )(mem2, mem_norm_w.reshape(1, D_MODEL), w_kv)


def _xattn_kernel(q_ref, g_ref, kv_ref, o_ref):
    scale = XATTN_HEAD_DIM ** -0.5
    for h in range(XATTN_HEADS):
        c = slice(h * XATTN_HEAD_DIM, (h + 1) * XATTN_HEAD_DIM)
        ck = slice(h * XATTN_HEAD_DIM, (h + 1) * XATTN_HEAD_DIM)
        cv = slice(XATTN_WIDTH + h * XATTN_HEAD_DIM, XATTN_WIDTH + (h + 1) * XATTN_HEAD_DIM)
        s = lax.dot_general(q_ref[:, c], kv_ref[:, ck], NT_DIMS,
                            preferred_element_type=F32) * scale
        p = jnp.exp(s - jnp.max(s, axis=-1, keepdims=True))
        inv_l = 1.0 / jnp.sum(p, axis=-1, keepdims=True)
        xo = jnp.dot(p.astype(BF16), kv_ref[:, cv], preferred_element_type=F32) * inv_l
        o_ref[:, c] = (xo * _silu(g_ref[:, c].astype(F32))).astype(BF16)


def _xattn(proj, mkv, batch, seq, mem_len):
    nq = seq // XATT_TQ
    return pl.pallas_call(
        _xattn_kernel,
        grid=(batch, nq),
        in_specs=[pl.BlockSpec((XATT_TQ, XATTN_WIDTH), lambda b, q: (b * nq + q, COL_XQ // XATTN_WIDTH)),
                  pl.BlockSpec((XATT_TQ, XATTN_WIDTH), lambda b, q: (b * nq + q, COL_XG // XATTN_WIDTH)),
                  pl.BlockSpec((mem_len, 2 * XATTN_WIDTH), lambda b, q: (b, 0))],
        out_specs=pl.BlockSpec((XATT_TQ, XATTN_WIDTH), lambda b, q: (b * nq + q, 0)),
        out_shape=jax.ShapeDtypeStruct((batch * seq, XATTN_WIDTH), BF16),
        compiler_params=pltpu.CompilerParams(
            dimension_semantics=("arbitrary", "arbitrary"), vmem_limit_bytes=32 * MIB),
        name="xattn",
    )(proj, proj, mkv)


def _out_proj_kernel(d_ref, s_ref, a_ref, w_ref, nw_ref, x_ref, o_ref):
    y = jnp.dot(d_ref[...], w_ref[0:DIFF_WIDTH, :], preferred_element_type=F32)
    y = y + jnp.dot(s_ref[...], w_ref[DIFF_WIDTH:DIFF_WIDTH + SSD_WIDTH, :],
                    preferred_element_type=F32)
    y = y + jnp.dot(a_ref[...], w_ref[DIFF_WIDTH + SSD_WIDTH:D_MIX, :],
                    preferred_element_type=F32)
    o_ref[...] = x_ref[...] + y * _rms_scale(y) * nw_ref[...]


def _out_proj(diff_out, ssd_out, xattn_out, w_out, post_norm_w, x2):
    m = x2.shape[0]
    return pl.pallas_call(
        _out_proj_kernel,
        grid=(m // OUT_TM,),
        in_specs=[pl.BlockSpec((OUT_TM, DIFF_WIDTH), lambda i: (i, 0)),
                  pl.BlockSpec((OUT_TM, SSD_WIDTH), lambda i: (i, 0)),
                  pl.BlockSpec((OUT_TM, XATTN_WIDTH), lambda i: (i, 0)),
                  pl.BlockSpec((D_MIX, D_MODEL), lambda i: (0, 0)),
                  pl.BlockSpec((1, D_MODEL), lambda i: (0, 0)),
                  pl.BlockSpec((OUT_TM, D_MODEL), lambda i: (i, 0))],
        out_specs=pl.BlockSpec((OUT_TM, D_MODEL), lambda i: (i, 0)),
        out_shape=jax.ShapeDtypeStruct((m, D_MODEL), F32),
        compiler_params=pltpu.CompilerParams(
            dimension_semantics=("arbitrary",), vmem_limit_bytes=48 * MIB),
        name="out_proj",
    )(diff_out, ssd_out, xattn_out, w_out, post_norm_w.reshape(1, D_MODEL), x2)


def kernel(x, mem, positions, pre_norm_w, w_in, lambda_q1, lambda_k1, lambda_q2, lambda_k2,
           diff_subln_w, conv_w, conv_b, dt_bias, a_log, d_skip, ssd_norm_w, mem_norm_w,
           w_mem_kv, w_out, post_norm_w):
    batch, seq, _ = x.shape
    mem_len = mem.shape[1]
    assert pre_norm_w.shape[0] == 1, "single-layer kernel"
    x2 = x.reshape(batch * seq, D_MODEL)

    w = w_in[0]
    w_main = jnp.concatenate([w[:, :REF_DT], w[:, REF_XQ:]], axis=1).astype(BF16)
    w_dt = jnp.pad(w[:, REF_DT:REF_XQ], ((0, 0), (0, LANES - SSD_HEADS))).astype(BF16)
    lam_vecs = jnp.stack([lambda_q1[0], lambda_k1[0], lambda_q2[0], lambda_k2[0]]).astype(F32)

    cos, sin = _rope_tables(positions)
    proj, dt_raw = _in_proj(x2, pre_norm_w[0], w_main, w_dt)
    diff_out = _diff_attention(proj, cos, sin, lam_vecs, diff_subln_w[0], batch, seq)
    ssd_out = _ssd(proj, dt_raw, conv_w[0], conv_b[0], dt_bias[0], a_log[0], d_skip[0],
                   ssd_norm_w[0], batch, seq)
    mkv = _mem_kv(mem.reshape(batch * mem_len, D_MODEL), mem_norm_w[0],
                  w_mem_kv[0].astype(BF16), batch, mem_len)
    xattn_out = _xattn(proj, mkv, batch, seq, mem_len)
    out = _out_proj(diff_out, ssd_out, xattn_out, w_out[0].astype(BF16), post_norm_w[0], x2)
    return out.reshape(batch, seq, D_MODEL)
```

```python
import functools
import math

import jax
import jax.numpy as jnp
from jax import lax
from jax.experimental import pallas as pl
from jax.experimental.pallas import tpu as pltpu

F32 = jnp.float32
BF16 = jnp.bfloat16

D_MODEL = 2048
DIFF_HEADS = 8
DIFF_QK_DIM = 64
DIFF_V_DIM = 128
DIFF_WIDTH = DIFF_HEADS * DIFF_V_DIM
SSD_HEADS = 8
SSD_HEAD_DIM = 64
SSD_WIDTH = SSD_HEADS * SSD_HEAD_DIM
SSD_GROUPS = 2
SSD_STATE = 128
SSD_CONV = 4
SSD_CHUNK = 128
SSD_CONV_DIM = SSD_WIDTH + 2 * SSD_GROUPS * SSD_STATE
XATTN_HEADS = 4
XATTN_HEAD_DIM = 128
XATTN_WIDTH = XATTN_HEADS * XATTN_HEAD_DIM
D_MIX = DIFF_WIDTH + SSD_WIDTH + XATTN_WIDTH
ROPE_THETA = 10000.0
NORM_EPS = 1e-6
LAMBDA_INIT = 0.8 - 0.6 * math.exp(-0.3 * 0)

LANES = 128
SUBLANES = 8
MIB = 1024 * 1024

COL_Q = 0
COL_K = COL_Q + DIFF_WIDTH
COL_V = COL_K + DIFF_WIDTH
COL_G = COL_V + DIFF_WIDTH
COL_Z = COL_G + DIFF_WIDTH
COL_XS = COL_Z + SSD_WIDTH
COL_B = COL_XS + SSD_WIDTH
COL_C = COL_B + SSD_GROUPS * SSD_STATE
COL_XQ = COL_C + SSD_GROUPS * SSD_STATE
COL_XG = COL_XQ + XATTN_WIDTH
N_MAIN = COL_XG + XATTN_WIDTH
REF_DT = COL_XQ
REF_XQ = REF_DT + SSD_HEADS

PROJ_TM = 1024
PROJ_TN = 512
NORM_ROWS = 256
ATT_TQ = 512
ATT_TK = 512
ATT_HEADS_PER_STEP = 4
ATT_SUM_ROWS = 16
XATT_TQ = 512
OUT_TM = 512
ROPE_TM = 1024

NT_DIMS = (((1,), (1,)), ((), ()))
TN_DIMS = (((0,), (0,)), ((), ()))


def _rms_scale(x):
    return lax.rsqrt(jnp.mean(x * x, axis=-1, keepdims=True) + NORM_EPS)


def _silu(x):
    return x * jax.nn.sigmoid(x)


def _rope_table_kernel(pos_ref, invf_ref, cos_ref, sin_ref):
    ang = pos_ref[...].astype(F32) * invf_ref[...]
    lane = lax.broadcasted_iota(jnp.int32, ang.shape, 1)
    first_half = (lane % DIFF_QK_DIM) < (DIFF_QK_DIM // 2)
    s = jnp.sin(ang)
    cos_ref[...] = jnp.cos(ang)
    sin_ref[...] = jnp.where(first_half, -s, s)


def _rope_tables(positions):
    n = positions.size
    inv_freq = 1.0 / (ROPE_THETA ** (jnp.arange(0, DIFF_QK_DIM, 2, dtype=F32) / DIFF_QK_DIM))
    invf = jnp.tile(inv_freq, LANES // inv_freq.shape[0]).reshape(1, LANES)
    pos = positions.reshape(n, 1)
    return pl.pallas_call(
        _rope_table_kernel,
        grid=(n // ROPE_TM,),
        in_specs=[pl.BlockSpec((ROPE_TM, 1), lambda i: (i, 0)),
                  pl.BlockSpec((1, LANES), lambda i: (0, 0))],
        out_specs=[pl.BlockSpec((ROPE_TM, LANES), lambda i: (i, 0)),
                   pl.BlockSpec((ROPE_TM, LANES), lambda i: (i, 0))],
        out_shape=[jax.ShapeDtypeStruct((n, LANES), F32)] * 2,
        name="rope_tables",
    )(pos, invf)


def _rope(x, cos, sin_signed):
    lane = lax.broadcasted_iota(jnp.int32, x.shape, 1)
    first_half = (lane % DIFF_QK_DIM) < (DIFF_QK_DIM // 2)
    half = DIFF_QK_DIM // 2
    rot = jnp.where(first_half, pltpu.roll(x, LANES - half, 1), pltpu.roll(x, half, 1))
    return x * cos + rot * sin_signed


def _in_proj_kernel(x_ref, nw_ref, w_ref, wdt_ref, o_ref, dt_ref, h_ref):
    @pl.when(pl.program_id(1) == 0)
    def _():
        def norm_rows(i, carry):
            r = pl.ds(pl.multiple_of(i * NORM_ROWS, NORM_ROWS), NORM_ROWS)
            x = x_ref[r, :]
            h_ref[r, :] = (x * _rms_scale(x) * nw_ref[...]).astype(BF16)
            return carry
        lax.fori_loop(0, PROJ_TM // NORM_ROWS, norm_rows, 0)
        dt_ref[...] = jnp.dot(h_ref[...], wdt_ref[...], preferred_element_type=F32)

    o_ref[...] = jnp.dot(h_ref[...], w_ref[...], preferred_element_type=F32).astype(BF16)


def _in_proj(x2, pre_norm_w, w_main, w_dt):
    m = x2.shape[0]
    return pl.pallas_call(
        _in_proj_kernel,
        grid=(m // PROJ_TM, N_MAIN // PROJ_TN),
        in_specs=[pl.BlockSpec((PROJ_TM, D_MODEL), lambda i, j: (i, 0)),
                  pl.BlockSpec((1, D_MODEL), lambda i, j: (0, 0)),
                  pl.BlockSpec((D_MODEL, PROJ_TN), lambda i, j: (0, j)),
                  pl.BlockSpec((D_MODEL, LANES), lambda i, j: (0, 0))],
        out_specs=[pl.BlockSpec((PROJ_TM, PROJ_TN), lambda i, j: (i, j)),
                   pl.BlockSpec((PROJ_TM, LANES), lambda i, j: (i, 0))],
        out_shape=[jax.ShapeDtypeStruct((m, N_MAIN), BF16),
                   jax.ShapeDtypeStruct((m, LANES), F32)],
        scratch_shapes=[pltpu.VMEM((PROJ_TM, D_MODEL), BF16)],
        compiler_params=pltpu.CompilerParams(
            dimension_semantics=("arbitrary", "arbitrary"),
            vmem_limit_bytes=48 * MIB),
        name="in_proj",
    )(x2, pre_norm_w.reshape(1, D_MODEL), w_main, w_dt)


def _diff_attn_kernel(lam_ref, q_ref, k_ref, v_ref, g_ref, cos_ref, sin_ref, subw_ref,
                      o_ref, kr_ref, vt_ref, m_ref, acc_ref, *, seq):
    qi = pl.program_id(2)
    heads = ATT_HEADS_PER_STEP

    @pl.when(qi == 0)
    def _():
        def prep_rows(i, carry):
            r = pl.ds(pl.multiple_of(i * ATT_TK, ATT_TK), ATT_TK)
            for g in range(heads):
                c = slice(g * LANES, (g + 1) * LANES)
                kr_ref[r, c] = _rope(k_ref[r, c].astype(F32), cos_ref[r, :],
                                     sin_ref[r, :]).astype(BF16)
                vt_ref[g, 0:DIFF_V_DIM, r] = v_ref[r, c].astype(F32).T.astype(BF16)
            return carry
        lax.fori_loop(0, seq // ATT_TK, prep_rows, 0)
        vt_ref[:, DIFF_V_DIM:, :] = jnp.ones((heads, ATT_SUM_ROWS, seq), BF16)

    rq = pl.ds(pl.multiple_of(qi * ATT_TQ, ATT_TQ), ATT_TQ)
    scale = DIFF_QK_DIM ** -0.5 * math.log2(math.e)
    lane = lax.broadcasted_iota(jnp.int32, (ATT_TQ, LANES), 1)
    q_maps = []
    for g in range(heads):
        qf = _rope(q_ref[:, g * LANES:(g + 1) * LANES].astype(F32), cos_ref[rq, :],
                   sin_ref[rq, :]) * scale
        q_maps.append((jnp.where(lane < DIFF_QK_DIM, qf, 0.0).astype(BF16),
                       jnp.where(lane >= DIFF_QK_DIM, qf, 0.0).astype(BF16)))

    m_ref[...] = jnp.full(m_ref.shape, -jnp.inf, F32)
    acc_ref[...] = jnp.zeros(acc_ref.shape, F32)

    key = lax.broadcasted_iota(jnp.int32, (ATT_TK, ATT_TQ), 0)
    qry = lax.broadcasted_iota(jnp.int32, (ATT_TK, ATT_TQ), 1)
    causal = key <= qry

    chains = [(g, c) for g in range(heads) for c in range(2)]

    def tile(j, masked):
        rk = pl.ds(pl.multiple_of(j * ATT_TK, ATT_TK), ATT_TK)

        def scores(g, c):
            kt = kr_ref[rk, g * LANES:(g + 1) * LANES]
            s = lax.dot_general(kt, q_maps[g][c], NT_DIMS, preferred_element_type=F32)
            return jnp.where(causal, s, -jnp.inf) if masked else s

        def softmax(i, s):
            m_old = m_ref[i]
            m_new = jnp.maximum(m_old, jnp.max(s, axis=0, keepdims=True))
            p = jnp.exp2(s - m_new)
            alpha = jnp.exp2(m_old - m_new)
            m_ref[i] = m_new
            return p.astype(BF16), alpha

        def accumulate(i, g, p, alpha):
            vt = vt_ref[g, :, rk]
            acc_ref[i] = alpha * acc_ref[i] + jnp.dot(vt, p, preferred_element_type=F32)

        ahead = 2
        s_vals = {}
        for i in range(min(ahead, len(chains))):
            s_vals[i] = scores(*chains[i])
        for i, (g, c) in enumerate(chains):
            p, alpha = softmax(i, s_vals.pop(i))
            if i + ahead < len(chains):
                s_vals[i + ahead] = scores(*chains[i + ahead])
            accumulate(i, g, p, alpha)

    def full_tile(j, carry):
        tile(j, masked=False)
        return carry
    lax.fori_loop(0, qi, full_tile, 0)
    tile(qi, masked=True)

    lv = lam_ref[...]
    lam = (jnp.exp(jnp.sum(lv[0:1] * lv[1:2], axis=-1, keepdims=True))
           - jnp.exp(jnp.sum(lv[2:3] * lv[3:4], axis=-1, keepdims=True)) + LAMBDA_INIT)
    for g in range(heads):
        c = slice(g * LANES, (g + 1) * LANES)
        a1, a2 = acc_ref[2 * g], acc_ref[2 * g + 1]
        l1 = a1[DIFF_V_DIM:DIFF_V_DIM + 1, :]
        l2 = a2[DIFF_V_DIM:DIFF_V_DIM + 1, :]
        o_t = (a1[0:DIFF_V_DIM, :] * (1.0 / l1)
               - lam * (a2[0:DIFF_V_DIM, :] * (1.0 / l2)))
        o = o_t.T
        o = o * _rms_scale(o) * subw_ref[...] * (1.0 - LAMBDA_INIT)
        o_ref[:, c] = (o * _silu(g_ref[:, c].astype(F32))).astype(BF16)


def _diff_attention(proj, cos, sin, lam_vecs, subln_w, batch, seq):
    assert ATT_TQ == ATT_TK and seq % ATT_TQ == 0 and DIFF_V_DIM == LANES
    nq = seq // ATT_TQ
    heads = ATT_HEADS_PER_STEP
    width = heads * LANES
    kernel = functools.partial(_diff_attn_kernel, seq=seq)
    return pl.pallas_call(
        kernel,
        grid=(batch, DIFF_HEADS // heads, nq),
        in_specs=[
            pl.BlockSpec((4, DIFF_QK_DIM), lambda b, h, q: (0, 0)),
            pl.BlockSpec((ATT_TQ, width), lambda b, h, q: (b * nq + q, COL_Q // width + h)),
            pl.BlockSpec((seq, width), lambda b, h, q: (b, COL_K // width + h)),
            pl.BlockSpec((seq, width), lambda b, h, q: (b, COL_V // width + h)),
            pl.BlockSpec((ATT_TQ, width), lambda b, h, q: (b * nq + q, COL_G // width + h)),
            pl.BlockSpec((seq, LANES), lambda b, h, q: (b, 0)),
            pl.BlockSpec((seq, LANES), lambda b, h, q: (b, 0)),
            pl.BlockSpec((1, DIFF_V_DIM), lambda b, h, q: (0, 0)),
        ],
        out_specs=pl.BlockSpec((ATT_TQ, width), lambda b, h, q: (b * nq + q, h)),
        out_shape=jax.ShapeDtypeStruct((batch * seq, DIFF_WIDTH), BF16),
        scratch_shapes=[pltpu.VMEM((seq, width), BF16),
                        pltpu.VMEM((heads, DIFF_V_DIM + ATT_SUM_ROWS, seq), BF16),
                        pltpu.VMEM((2 * heads, 1, ATT_TQ), F32),
                        pltpu.VMEM((2 * heads, DIFF_V_DIM + ATT_SUM_ROWS, ATT_TQ), F32)],
        compiler_params=pltpu.CompilerParams(
            dimension_semantics=("arbitrary", "arbitrary", "arbitrary"),
            vmem_limit_bytes=48 * MIB),
        name="diff_attn",
    )(lam_vecs, proj, proj, proj, proj, cos, sin, subln_w.reshape(1, DIFF_V_DIM))


CONV_HALO = SUBLANES


def _ssd_kernel(xs_ref, b_ref, c_ref, z_ref, dt_ref, cw_ref, cb_ref, dtb_ref, alog_ref,
                dsk_ref, nw_ref, o_ref, ext_ref, state_ref):
    L = SSD_CHUNK
    gs = SSD_GROUPS * SSD_STATE

    @pl.when(pl.program_id(1) == 0)
    def _():
        ext_ref[0:CONV_HALO, :] = jnp.zeros((CONV_HALO, SSD_CONV_DIM), F32)
        state_ref[...] = jnp.zeros(state_ref.shape, F32)

    ext_ref[CONV_HALO:CONV_HALO + L, 0:SSD_WIDTH] = xs_ref[...].astype(F32)
    ext_ref[CONV_HALO:CONV_HALO + L, SSD_WIDTH:SSD_WIDTH + gs] = b_ref[...].astype(F32)
    ext_ref[CONV_HALO:CONV_HALO + L, SSD_WIDTH + gs:SSD_CONV_DIM] = c_ref[...].astype(F32)
    conv = jnp.broadcast_to(cb_ref[...], (L, SSD_CONV_DIM))
    for k in range(SSD_CONV):
        start = CONV_HALO - (SSD_CONV - 1) + k
        conv = conv + ext_ref[start:start + L, :] * cw_ref[k:k + 1, :]
    ext_ref[0:CONV_HALO, :] = ext_ref[L:L + CONV_HALO, :]
    xbc = _silu(conv)
    xs = xbc[:, 0:SSD_WIDTH]
    bm = xbc[:, SSD_WIDTH:SSD_WIDTH + gs].astype(BF16)
    cm = xbc[:, SSD_WIDTH + gs:SSD_CONV_DIM].astype(BF16)

    dt_in = dt_ref[...] + dtb_ref[...]
    dt = jnp.maximum(dt_in, 0.0) + jnp.log1p(jnp.exp(-jnp.abs(dt_in)))
    a_dt = dt * (-jnp.exp(alog_ref[...]))
    row = lax.broadcasted_iota(jnp.int32, (L, L), 0)
    col = lax.broadcasted_iota(jnp.int32, (L, L), 1)
    lower = row >= col
    tril = lower.astype(BF16)
    hi = a_dt.astype(BF16)
    r1 = a_dt - hi.astype(F32)
    mid = r1.astype(BF16)
    lo = (r1 - mid.astype(F32)).astype(BF16)
    a_cs = (jnp.dot(tril, hi, preferred_element_type=F32)
            + jnp.dot(tril, mid, preferred_element_type=F32)
            + jnp.dot(tril, lo, preferred_element_type=F32))
    a_cs_t = a_cs.T
    a_end = a_cs[L - 1:L, :]
    decay_in = jnp.exp(a_cs)
    decay_to_end = jnp.exp(a_end - a_cs)
    chunk_decay = jnp.exp(a_end)

    lane = lax.broadcasted_iota(jnp.int32, (L, LANES), 1)
    left = lane < SSD_HEAD_DIM
    top = row < SSD_HEAD_DIM

    def pair_lanes(mat, p):
        return jnp.where(left, mat[:, 2 * p:2 * p + 1], mat[:, 2 * p + 1:2 * p + 2])

    ys = []
    pairs = SSD_HEADS // 2
    for p in range(pairs):
        g = (2 * p) // (SSD_HEADS // SSD_GROUPS)
        cg = cm[:, g * SSD_STATE:(g + 1) * SSD_STATE]
        bg = bm[:, g * SSD_STATE:(g + 1) * SSD_STATE]
        cb = lax.dot_general(cg, bg, NT_DIMS, preferred_element_type=F32)
        xs_p = xs[:, p * LANES:(p + 1) * LANES]
        x_dt = xs_p * pair_lanes(dt, p)
        x_dt_b = x_dt.astype(BF16)
        y_heads = []
        for hh in range(2):
            h = 2 * p + hh
            seg = a_cs[:, h:h + 1] - a_cs_t[h:h + 1, :]
            decay = jnp.exp(jnp.where(lower, seg, -jnp.inf))
            scores = (cb * decay).astype(BF16)
            y_heads.append(jnp.dot(scores, x_dt_b, preferred_element_type=F32))
        y_diag = jnp.where(left, y_heads[0], y_heads[1])
        st = state_ref[p]
        y_off = (lax.dot_general(cg, st.astype(BF16), NT_DIMS, preferred_element_type=F32)
                 * pair_lanes(decay_in, p))
        x_w = (x_dt * pair_lanes(decay_to_end, p)).astype(BF16)
        chunk_state = lax.dot_general(x_w, bg, TN_DIMS, preferred_element_type=F32)
        st_decay = jnp.where(top, chunk_decay[:, 2 * p:2 * p + 1],
                             chunk_decay[:, 2 * p + 1:2 * p + 2])
        state_ref[p] = st * st_decay + chunk_state
        y = y_diag + y_off + xs_p * dsk_ref[:, p * LANES:(p + 1) * LANES]
        ys.append(y * _silu(z_ref[:, p * LANES:(p + 1) * LANES].astype(F32)))

    grp_pairs = pairs // SSD_GROUPS
    grp_width = SSD_WIDTH // SSD_GROUPS
    for g in range(SSD_GROUPS):
        members = ys[g * grp_pairs:(g + 1) * grp_pairs]
        ssq = sum(jnp.sum(y * y, axis=-1, keepdims=True) for y in members)
        inv = lax.rsqrt(ssq / grp_width + NORM_EPS)
        for i, y in enumerate(members):
            c0 = (g * grp_pairs + i) * LANES
            o_ref[:, c0:c0 + LANES] = (y * inv * nw_ref[:, c0:c0 + LANES]).astype(BF16)


def _ssd(proj, dt_raw, conv_w, conv_b, dt_bias, a_log, d_skip, ssd_norm_w, batch, seq):
    nc = seq // SSD_CHUNK
    gs = SSD_GROUPS * SSD_STATE
    pad = LANES - SSD_HEADS
    row_map = lambda b, c: b * nc + c
    const = lambda b, c: (0, 0)
    return pl.pallas_call(
        _ssd_kernel,
        grid=(batch, nc),
        in_specs=[
            pl.BlockSpec((SSD_CHUNK, SSD_WIDTH), lambda b, c: (row_map(b, c), COL_XS // SSD_WIDTH)),
            pl.BlockSpec((SSD_CHUNK, gs), lambda b, c: (row_map(b, c), COL_B // gs)),
            pl.BlockSpec((SSD_CHUNK, gs), lambda b, c: (row_map(b, c), COL_C // gs)),
            pl.BlockSpec((SSD_CHUNK, SSD_WIDTH), lambda b, c: (row_map(b, c), COL_Z // SSD_WIDTH)),
            pl.BlockSpec((SSD_CHUNK, LANES), lambda b, c: (row_map(b, c), 0)),
            pl.BlockSpec((SSD_CONV, SSD_CONV_DIM), const),
            pl.BlockSpec((1, SSD_CONV_DIM), const),
            pl.BlockSpec((1, LANES), const),
            pl.BlockSpec((1, LANES), const),
            pl.BlockSpec((1, SSD_WIDTH), const),
            pl.BlockSpec((1, SSD_WIDTH), const),
        ],
        out_specs=pl.BlockSpec((SSD_CHUNK, SSD_WIDTH), lambda b, c: (row_map(b, c), 0)),
        out_shape=jax.ShapeDtypeStruct((batch * seq, SSD_WIDTH), BF16),
        scratch_shapes=[pltpu.VMEM((CONV_HALO + SSD_CHUNK, SSD_CONV_DIM), F32),
                        pltpu.VMEM((SSD_HEADS // 2, 2 * SSD_HEAD_DIM, SSD_STATE), F32)],
        compiler_params=pltpu.CompilerParams(
            dimension_semantics=("arbitrary", "arbitrary"),
            vmem_limit_bytes=32 * MIB),
        name="ssd",
    )(proj, proj, proj, proj, dt_raw, conv_w, conv_b.reshape(1, SSD_CONV_DIM),
      jnp.pad(dt_bias, (0, pad)).reshape(1, LANES), jnp.pad(a_log, (0, pad)).reshape(1, LANES),
      jnp.repeat(d_skip, SSD_HEAD_DIM).reshape(1, SSD_WIDTH), ssd_norm_w.reshape(1, SSD_WIDTH))


def _mem_kv_kernel(mem_ref, nw_ref, w_ref, o_ref):
    x = mem_ref[...]
    h = (x * _rms_scale(x) * nw_ref[...]).astype(BF16)
    o_ref[...] = jnp.dot(h, w_ref[...], preferred_element_type=F32).astype(BF16)


def _mem_kv(mem2, mem_norm_w, w_kv, batch, mem_len):
    return pl.pallas_call(
        _mem_kv_kernel,
        grid=(batch,),
        in_specs=[pl.BlockSpec((mem_len, D_MODEL), lambda b: (b, 0)),
                  pl.BlockSpec((1, D_MODEL), lambda b: (0, 0)),
                  pl.BlockSpec((D_MODEL, 2 * XATTN_WIDTH), lambda b: (0, 0))],
        out_specs=pl.BlockSpec((mem_len, 2 * XATTN_WIDTH), lambda b: (b, 0)),
        out_shape=jax.ShapeDtypeStruct((batch * mem_len, 2 * XATTN_WIDTH), BF16),
        compiler_params=pltpu.CompilerParams(
            dimension_semantics=("arbitrary",), vmem_limit_bytes=32 * MIB),
        name="mem_kv",
    )(mem2, mem_norm_w.reshape(1, D_MODEL), w_kv)


def _xattn_kernel(q_ref, g_ref, kv_ref, o_ref):
    scale = XATTN_HEAD_DIM ** -0.5
    for h in range(XATTN_HEADS):
        c = slice(h * XATTN_HEAD_DIM, (h + 1) * XATTN_HEAD_DIM)
        ck = slice(h * XATTN_HEAD_DIM, (h + 1) * XATTN_HEAD_DIM)
        cv = slice(XATTN_WIDTH + h * XATTN_HEAD_DIM, XATTN_WIDTH + (h + 1) * XATTN_HEAD_DIM)
        s = lax.dot_general(q_ref[:, c], kv_ref[:, ck], NT_DIMS,
                            preferred_element_type=F32) * scale
        p = jnp.exp(s - jnp.max(s, axis=-1, keepdims=True))
        inv_l = 1.0 / jnp.sum(p, axis=-1, keepdims=True)
        xo = jnp.dot(p.astype(BF16), kv_ref[:, cv], preferred_element_type=F32) * inv_l
        o_ref[:, c] = (xo * _silu(g_ref[:, c].astype(F32))).astype(BF16)


def _xattn(proj, mkv, batch, seq, mem_len):
    nq = seq // XATT_TQ
    return pl.pallas_call(
        _xattn_kernel,
        grid=(batch, nq),
        in_specs=[pl.BlockSpec((XATT_TQ, XATTN_WIDTH), lambda b, q: (b * nq + q, COL_XQ // XATTN_WIDTH)),
                  pl.BlockSpec((XATT_TQ, XATTN_WIDTH), lambda b, q: (b * nq + q, COL_XG // XATTN_WIDTH)),
                  pl.BlockSpec((mem_len, 2 * XATTN_WIDTH), lambda b, q: (b, 0))],
        out_specs=pl.BlockSpec((XATT_TQ, XATTN_WIDTH), lambda b, q: (b * nq + q, 0)),
        out_shape=jax.ShapeDtypeStruct((batch * seq, XATTN_WIDTH), BF16),
        compiler_params=pltpu.CompilerParams(
            dimension_semantics=("arbitrary", "arbitrary"), vmem_limit_bytes=32 * MIB),
        name="xattn",
    )(proj, proj, mkv)


def _out_proj_kernel(d_ref, s_ref, a_ref, w_ref, nw_ref, x_ref, o_ref):
    y = jnp.dot(d_ref[...], w_ref[0:DIFF_WIDTH, :], preferred_element_type=F32)
    y = y + jnp.dot(s_ref[...], w_ref[DIFF_WIDTH:DIFF_WIDTH + SSD_WIDTH, :],
                    preferred_element_type=F32)
    y = y + jnp.dot(a_ref[...], w_ref[DIFF_WIDTH + SSD_WIDTH:D_MIX, :],
                    preferred_element_type=F32)
    o_ref[...] = x_ref[...] + y * _rms_scale(y) * nw_ref[...]


def _out_proj(diff_out, ssd_out, xattn_out, w_out, post_norm_w, x2):
    m = x2.shape[0]
    return pl.pallas_call(
        _out_proj_kernel,
        grid=(m // OUT_TM,),
        in_specs=[pl.BlockSpec((OUT_TM, DIFF_WIDTH), lambda i: (i, 0)),
                  pl.BlockSpec((OUT_TM, SSD_WIDTH), lambda i: (i, 0)),
                  pl.BlockSpec((OUT_TM, XATTN_WIDTH), lambda i: (i, 0)),
                  pl.BlockSpec((D_MIX, D_MODEL), lambda i: (0, 0)),
                  pl.BlockSpec((1, D_MODEL), lambda i: (0, 0)),
                  pl.BlockSpec((OUT_TM, D_MODEL), lambda i: (i, 0))],
        out_specs=pl.BlockSpec((OUT_TM, D_MODEL), lambda i: (i, 0)),
        out_shape=jax.ShapeDtypeStruct((m, D_MODEL), F32),
        compiler_params=pltpu.CompilerParams(
            dimension_semantics=("arbitrary",), vmem_limit_bytes=48 * MIB),
        name="out_proj",
    )(diff_out, ssd_out, xattn_out, w_out, post_norm_w.reshape(1, D_MODEL), x2)


def kernel(x, mem, positions, pre_norm_w, w_in, lambda_q1, lambda_k1, lambda_q2, lambda_k2,
           diff_subln_w, conv_w, conv_b, dt_bias, a_log, d_skip, ssd_norm_w, mem_norm_w,
           w_mem_kv, w_out, post_norm_w):
    batch, seq, _ = x.shape
    mem_len = mem.shape[1]
    assert pre_norm_w.shape[0] == 1, "single-layer kernel"
    x2 = x.reshape(batch * seq, D_MODEL)

    w = w_in[0]
    w_main = jnp.concatenate([w[:, :REF_DT], w[:, REF_XQ:]], axis=1).astype(BF16)
    w_dt = jnp.pad(w[:, REF_DT:REF_XQ], ((0, 0), (0, LANES - SSD_HEADS))).astype(BF16)
    lam_vecs = jnp.stack([lambda_q1[0], lambda_k1[0], lambda_q2[0], lambda_k2[0]]).astype(F32)

    cos, sin = _rope_tables(positions)
    proj, dt_raw = _in_proj(x2, pre_norm_w[0], w_main, w_dt)
    diff_out = _diff_attention(proj, cos, sin, lam_vecs, diff_subln_w[0], batch, seq)
    ssd_out = _ssd(proj, dt_raw, conv_w[0], conv_b[0], dt_bias[0], a_log[0], d_skip[0],
                   ssd_norm_w[0], batch, seq)
    mkv = _mem_kv(mem.reshape(batch * mem_len, D_MODEL), mem_norm_w[0],
                  w_mem_kv[0].astype(BF16), batch, mem_len)
    xattn_out = _xattn(proj, mkv, batch, seq, mem_len)
    out = _out_proj(diff_out, ssd_out, xattn_out, w_out[0].astype(BF16), post_norm_w[0], x2)
    return out.reshape(batch, seq, D_MODEL)
```

```python
import functools
import math

import jax
import jax.numpy as jnp
from jax import lax
from jax.experimental import pallas as pl
from jax.experimental.pallas import tpu as pltpu

F32 = jnp.float32
BF16 = jnp.bfloat16

D_MODEL = 2048
DIFF_HEADS = 8
DIFF_QK_DIM = 64
DIFF_V_DIM = 128
DIFF_WIDTH = DIFF_HEADS * DIFF_V_DIM
SSD_HEADS = 8
SSD_HEAD_DIM = 64
SSD_WIDTH = SSD_HEADS * SSD_HEAD_DIM
SSD_GROUPS = 2
SSD_STATE = 128
SSD_CONV = 4
SSD_CHUNK = 128
SSD_CONV_DIM = SSD_WIDTH + 2 * SSD_GROUPS * SSD_STATE
XATTN_HEADS = 4
XATTN_HEAD_DIM = 128
XATTN_WIDTH = XATTN_HEADS * XATTN_HEAD_DIM
D_MIX = DIFF_WIDTH + SSD_WIDTH + XATTN_WIDTH
ROPE_THETA = 10000.0
NORM_EPS = 1e-6
LAMBDA_INIT = 0.8 - 0.6 * math.exp(-0.3 * 0)

LANES = 128
SUBLANES = 8
MIB = 1024 * 1024

COL_Q = 0
COL_K = COL_Q + DIFF_WIDTH
COL_V = COL_K + DIFF_WIDTH
COL_G = COL_V + DIFF_WIDTH
COL_Z = COL_G + DIFF_WIDTH
COL_XS = COL_Z + SSD_WIDTH
COL_B = COL_XS + SSD_WIDTH
COL_C = COL_B + SSD_GROUPS * SSD_STATE
COL_XQ = COL_C + SSD_GROUPS * SSD_STATE
COL_XG = COL_XQ + XATTN_WIDTH
N_MAIN = COL_XG + XATTN_WIDTH
REF_DT = COL_XQ
REF_XQ = REF_DT + SSD_HEADS

PROJ_TM = 512
PROJ_TN = 512
NORM_ROWS = 256
ATT_TQ = 512
ATT_TK = 512
ATT_HEADS_PER_STEP = 4
ATT_SUM_ROWS = 16
XATT_TQ = 512
OUT_TM = 512
OUT_SUB = 256
ROPE_TM = 1024
PREP_ROWS = 256

NT_DIMS = (((1,), (1,)), ((), ()))
TN_DIMS = (((0,), (0,)), ((), ()))


def _rms_scale(x):
    return lax.rsqrt(jnp.mean(x * x, axis=-1, keepdims=True) + NORM_EPS)


def _silu(x):
    return x * jax.nn.sigmoid(x)


def _rope_table_kernel(pos_ref, invf_ref, cos_ref, sin_ref):
    ang = pos_ref[...].astype(F32) * invf_ref[...]
    lane = lax.broadcasted_iota(jnp.int32, ang.shape, 1)
    first_half = (lane % DIFF_QK_DIM) < (DIFF_QK_DIM // 2)
    s = jnp.sin(ang)
    cos_ref[...] = jnp.cos(ang)
    sin_ref[...] = jnp.where(first_half, -s, s)


def _rope_tables(positions):
    n = positions.size
    inv_freq = 1.0 / (ROPE_THETA ** (jnp.arange(0, DIFF_QK_DIM, 2, dtype=F32) / DIFF_QK_DIM))
    invf = jnp.tile(inv_freq, LANES // inv_freq.shape[0]).reshape(1, LANES)
    pos = positions.reshape(n, 1)
    return pl.pallas_call(
        _rope_table_kernel,
        grid=(n // ROPE_TM,),
        in_specs=[pl.BlockSpec((ROPE_TM, 1), lambda i: (i, 0)),
                  pl.BlockSpec((1, LANES), lambda i: (0, 0))],
        out_specs=[pl.BlockSpec((ROPE_TM, LANES), lambda i: (i, 0)),
                   pl.BlockSpec((ROPE_TM, LANES), lambda i: (i, 0))],
        out_shape=[jax.ShapeDtypeStruct((n, LANES), F32)] * 2,
        name="rope_tables",
    )(pos, invf)


def _rope(x, cos, sin_signed):
    lane = lax.broadcasted_iota(jnp.int32, x.shape, 1)
    first_half = (lane % DIFF_QK_DIM) < (DIFF_QK_DIM // 2)
    half = DIFF_QK_DIM // 2
    rot = jnp.where(first_half, pltpu.roll(x, LANES - half, 1), pltpu.roll(x, half, 1))
    return x * cos + rot * sin_signed


def _w_in_prep_kernel(w_ref, wm_ref, wdt_ref):
    tail = N_MAIN - REF_DT
    wm_ref[:, 0:REF_DT] = w_ref[:, 0:REF_DT].astype(BF16)
    wm_ref[:, REF_DT:N_MAIN] = w_ref[:, REF_XQ:REF_XQ + tail].astype(BF16)
    lane = lax.broadcasted_iota(jnp.int32, (PREP_ROWS, LANES), 1)
    dt_cols = w_ref[:, REF_DT:REF_DT + LANES]
    wdt_ref[...] = jnp.where(lane < SSD_HEADS, dt_cols, 0.0).astype(BF16)


def _w_in_prep(w):
    k, n = w.shape
    assert n == REF_XQ + N_MAIN - REF_DT
    return pl.pallas_call(
        _w_in_prep_kernel,
        grid=(k // PREP_ROWS,),
        in_specs=[pl.BlockSpec((PREP_ROWS, n), lambda i: (i, 0))],
        out_specs=[pl.BlockSpec((PREP_ROWS, N_MAIN), lambda i: (i, 0)),
                   pl.BlockSpec((PREP_ROWS, LANES), lambda i: (i, 0))],
        out_shape=[jax.ShapeDtypeStruct((k, N_MAIN), BF16),
                   jax.ShapeDtypeStruct((k, LANES), BF16)],
        compiler_params=pltpu.CompilerParams(
            dimension_semantics=("arbitrary",), vmem_limit_bytes=32 * MIB),
        name="w_in_prep",
    )(w)


def _in_proj_kernel(x_ref, nw_ref, w_ref, wdt_ref, o_ref, dt_ref, h_ref):
    def norm_rows(i, carry):
        r = pl.ds(pl.multiple_of(i * NORM_ROWS, NORM_ROWS), NORM_ROWS)
        x = x_ref[r, :]
        h_ref[r, :] = (x * _rms_scale(x) * nw_ref[...]).astype(BF16)
        return carry
    lax.fori_loop(0, PROJ_TM // NORM_ROWS, norm_rows, 0)
    dt_ref[...] = jnp.dot(h_ref[...], wdt_ref[...], preferred_element_type=F32)
    for n in range(N_MAIN // PROJ_TN):
        c = slice(n * PROJ_TN, (n + 1) * PROJ_TN)
        o_ref[:, c] = jnp.dot(h_ref[...], w_ref[:, c],
                              preferred_element_type=F32).astype(BF16)


def _in_proj(x2, pre_norm_w, w_main, w_dt):
    m = x2.shape[0]
    resident = pl.Buffered(1)
    return pl.pallas_call(
        _in_proj_kernel,
        grid=(m // PROJ_TM,),
        in_specs=[pl.BlockSpec((PROJ_TM, D_MODEL), lambda i: (i, 0)),
                  pl.BlockSpec((1, D_MODEL), lambda i: (0, 0)),
                  pl.BlockSpec((D_MODEL, N_MAIN), lambda i: (0, 0), pipeline_mode=resident),
                  pl.BlockSpec((D_MODEL, LANES), lambda i: (0, 0), pipeline_mode=resident)],
        out_specs=[pl.BlockSpec((PROJ_TM, N_MAIN), lambda i: (i, 0)),
                   pl.BlockSpec((PROJ_TM, LANES), lambda i: (i, 0))],
        out_shape=[jax.ShapeDtypeStruct((m, N_MAIN), BF16),
                   jax.ShapeDtypeStruct((m, LANES), F32)],
        scratch_shapes=[pltpu.VMEM((PROJ_TM, D_MODEL), BF16)],
        compiler_params=pltpu.CompilerParams(
            dimension_semantics=("arbitrary",),
            vmem_limit_bytes=56 * MIB),
        name="in_proj",
    )(x2, pre_norm_w.reshape(1, D_MODEL), w_main, w_dt)


def _diff_attn_kernel(lam_ref, q_ref, k_ref, v_ref, g_ref, cos_ref, sin_ref, subw_ref,
                      o_ref, kr_ref, vt_ref, m_ref, acc_ref, *, seq):
    qi = pl.program_id(2)
    heads = ATT_HEADS_PER_STEP

    @pl.when(qi == 0)
    def _():
        def prep_rows(i, carry):
            r = pl.ds(pl.multiple_of(i * ATT_TK, ATT_TK), ATT_TK)
            for g in range(heads):
                c = slice(g * LANES, (g + 1) * LANES)
                kr_ref[r, c] = _rope(k_ref[r, c].astype(F32), cos_ref[r, :],
                                     sin_ref[r, :]).astype(BF16)
                vt_ref[g, 0:DIFF_V_DIM, r] = v_ref[r, c].astype(F32).T.astype(BF16)
            return carry
        lax.fori_loop(0, seq // ATT_TK, prep_rows, 0)
        vt_ref[:, DIFF_V_DIM:, :] = jnp.ones((heads, ATT_SUM_ROWS, seq), BF16)

    rq = pl.ds(pl.multiple_of(qi * ATT_TQ, ATT_TQ), ATT_TQ)
    scale = DIFF_QK_DIM ** -0.5 * math.log2(math.e)
    lane = lax.broadcasted_iota(jnp.int32, (ATT_TQ, LANES), 1)
    q_maps = []
    for g in range(heads):
        qf = _rope(q_ref[:, g * LANES:(g + 1) * LANES].astype(F32), cos_ref[rq, :],
                   sin_ref[rq, :]) * scale
        q_maps.append((jnp.where(lane < DIFF_QK_DIM, qf, 0.0).astype(BF16),
                       jnp.where(lane >= DIFF_QK_DIM, qf, 0.0).astype(BF16)))

    m_ref[...] = jnp.full(m_ref.shape, -jnp.inf, F32)
    acc_ref[...] = jnp.zeros(acc_ref.shape, F32)

    key = lax.broadcasted_iota(jnp.int32, (ATT_TK, ATT_TQ), 0)
    qry = lax.broadcasted_iota(jnp.int32, (ATT_TK, ATT_TQ), 1)
    causal = key <= qry

    chains = [(g, c) for g in range(heads) for c in range(2)]

    def tile(j, masked):
        rk = pl.ds(pl.multiple_of(j * ATT_TK, ATT_TK), ATT_TK)

        def scores(g, c):
            kt = kr_ref[rk, g * LANES:(g + 1) * LANES]
            s = lax.dot_general(kt, q_maps[g][c], NT_DIMS, preferred_element_type=F32)
            return jnp.where(causal, s, -jnp.inf) if masked else s

        def softmax(i, s):
            m_old = m_ref[i]
            m_new = jnp.maximum(m_old, jnp.max(s, axis=0, keepdims=True))
            p = jnp.exp2(s - m_new)
            alpha = jnp.exp2(m_old - m_new)
            m_ref[i] = m_new
            return p.astype(BF16), alpha

        def accumulate(i, g, p, alpha):
            vt = vt_ref[g, :, rk]
            acc_ref[i] = alpha * acc_ref[i] + jnp.dot(vt, p, preferred_element_type=F32)

        ahead = 2
        s_vals = {}
        for i in range(min(ahead, len(chains))):
            s_vals[i] = scores(*chains[i])
        for i, (g, c) in enumerate(chains):
            p, alpha = softmax(i, s_vals.pop(i))
            if i + ahead < len(chains):
                s_vals[i + ahead] = scores(*chains[i + ahead])
            accumulate(i, g, p, alpha)

    def full_tile(j, carry):
        tile(j, masked=False)
        return carry
    lax.fori_loop(0, qi, full_tile, 0)
    tile(qi, masked=True)

    lv = lam_ref[...]
    lam = (jnp.exp(jnp.sum(lv[0:1] * lv[1:2], axis=-1, keepdims=True))
           - jnp.exp(jnp.sum(lv[2:3] * lv[3:4], axis=-1, keepdims=True)) + LAMBDA_INIT)
    for g in range(heads):
        c = slice(g * LANES, (g + 1) * LANES)
        a1, a2 = acc_ref[2 * g], acc_ref[2 * g + 1]
        l1 = a1[DIFF_V_DIM:DIFF_V_DIM + 1, :]
        l2 = a2[DIFF_V_DIM:DIFF_V_DIM + 1, :]
        o_t = (a1[0:DIFF_V_DIM, :] * (1.0 / l1)
               - lam * (a2[0:DIFF_V_DIM, :] * (1.0 / l2)))
        o = o_t.T
        o = o * _rms_scale(o) * subw_ref[...] * (1.0 - LAMBDA_INIT)
        o_ref[:, c] = (o * _silu(g_ref[:, c].astype(F32))).astype(BF16)


def _diff_attention(proj, cos, sin, lam_vecs, subln_w, batch, seq):
    assert ATT_TQ == ATT_TK and seq % ATT_TQ == 0 and DIFF_V_DIM == LANES
    nq = seq // ATT_TQ
    heads = ATT_HEADS_PER_STEP
    width = heads * LANES
    kernel = functools.partial(_diff_attn_kernel, seq=seq)
    return pl.pallas_call(
        kernel,
        grid=(batch, DIFF_HEADS // heads, nq),
        in_specs=[
            pl.BlockSpec((4, DIFF_QK_DIM), lambda b, h, q: (0, 0)),
            pl.BlockSpec((ATT_TQ, width), lambda b, h, q: (b * nq + q, COL_Q // width + h)),
            pl.BlockSpec((seq, width), lambda b, h, q: (b, COL_K // width + h)),
            pl.BlockSpec((seq, width), lambda b, h, q: (b, COL_V // width + h)),
            pl.BlockSpec((ATT_TQ, width), lambda b, h, q: (b * nq + q, COL_G // width + h)),
            pl.BlockSpec((seq, LANES), lambda b, h, q: (b, 0)),
            pl.BlockSpec((seq, LANES), lambda b, h, q: (b, 0)),
            pl.BlockSpec((1, DIFF_V_DIM), lambda b, h, q: (0, 0)),
        ],
        out_specs=pl.BlockSpec((ATT_TQ, width), lambda b, h, q: (b * nq + q, h)),
        out_shape=jax.ShapeDtypeStruct((batch * seq, DIFF_WIDTH), BF16),
        scratch_shapes=[pltpu.VMEM((seq, width), BF16),
                        pltpu.VMEM((heads, DIFF_V_DIM + ATT_SUM_ROWS, seq), BF16),
                        pltpu.VMEM((2 * heads, 1, ATT_TQ), F32),
                        pltpu.VMEM((2 * heads, DIFF_V_DIM + ATT_SUM_ROWS, ATT_TQ), F32)],
        compiler_params=pltpu.CompilerParams(
            dimension_semantics=("arbitrary", "arbitrary", "arbitrary"),
            vmem_limit_bytes=48 * MIB),
        name="diff_attn",
    )(lam_vecs, proj, proj, proj, proj, cos, sin, subln_w.reshape(1, DIFF_V_DIM))


CONV_HALO = SUBLANES


def _ssd_kernel(xs_ref, b_ref, c_ref, z_ref, dt_ref, cw_ref, cb_ref, dtb_ref, alog_ref,
                dsk_ref, nw_ref, o_ref, ext_ref, state_ref):
    L = SSD_CHUNK
    gs = SSD_GROUPS * SSD_STATE

    @pl.when(pl.program_id(1) == 0)
    def _():
        ext_ref[0:CONV_HALO, :] = jnp.zeros((CONV_HALO, SSD_CONV_DIM), F32)
        state_ref[...] = jnp.zeros(state_ref.shape, F32)

    ext_ref[CONV_HALO:CONV_HALO + L, 0:SSD_WIDTH] = xs_ref[...].astype(F32)
    ext_ref[CONV_HALO:CONV_HALO + L, SSD_WIDTH:SSD_WIDTH + gs] = b_ref[...].astype(F32)
    ext_ref[CONV_HALO:CONV_HALO + L, SSD_WIDTH + gs:SSD_CONV_DIM] = c_ref[...].astype(F32)
    conv = jnp.broadcast_to(cb_ref[...], (L, SSD_CONV_DIM))
    for k in range(SSD_CONV):
        start = CONV_HALO - (SSD_CONV - 1) + k
        conv = conv + ext_ref[start:start + L, :] * cw_ref[k:k + 1, :]
    ext_ref[0:CONV_HALO, :] = ext_ref[L:L + CONV_HALO, :]
    xbc = _silu(conv)
    xs = xbc[:, 0:SSD_WIDTH]
    bm = xbc[:, SSD_WIDTH:SSD_WIDTH + gs].astype(BF16)
    cm = xbc[:, SSD_WIDTH + gs:SSD_CONV_DIM].astype(BF16)

    dt_in = dt_ref[...] + dtb_ref[...]
    dt = jnp.maximum(dt_in, 0.0) + jnp.log1p(jnp.exp(-jnp.abs(dt_in)))
    a_dt = dt * (-jnp.exp(alog_ref[...]))
    row = lax.broadcasted_iota(jnp.int32, (L, L), 0)
    col = lax.broadcasted_iota(jnp.int32, (L, L), 1)
    lower = row >= col
    tril = lower.astype(BF16)
    hi = a_dt.astype(BF16)
    r1 = a_dt - hi.astype(F32)
    mid = r1.astype(BF16)
    lo = (r1 - mid.astype(F32)).astype(BF16)
    a_cs = (jnp.dot(tril, hi, preferred_element_type=F32)
            + jnp.dot(tril, mid, preferred_element_type=F32)
            + jnp.dot(tril, lo, preferred_element_type=F32))
    a_cs_t = a_cs.T
    a_end = a_cs[L - 1:L, :]
    decay_in = jnp.exp(a_cs)
    decay_to_end = jnp.exp(a_end - a_cs)
    chunk_decay = jnp.exp(a_end)

    lane = lax.broadcasted_iota(jnp.int32, (L, LANES), 1)
    left = lane < SSD_HEAD_DIM
    top = row < SSD_HEAD_DIM

    def pair_lanes(mat, p):
        return jnp.where(left, mat[:, 2 * p:2 * p + 1], mat[:, 2 * p + 1:2 * p + 2])

    ys = []
    pairs = SSD_HEADS // 2
    for p in range(pairs):
        g = (2 * p) // (SSD_HEADS // SSD_GROUPS)
        cg = cm[:, g * SSD_STATE:(g + 1) * SSD_STATE]
        bg = bm[:, g * SSD_STATE:(g + 1) * SSD_STATE]
        cb = lax.dot_general(cg, bg, NT_DIMS, preferred_element_type=F32)
        xs_p = xs[:, p * LANES:(p + 1) * LANES]
        x_dt = xs_p * pair_lanes(dt, p)
        x_dt_b = x_dt.astype(BF16)
        y_heads = []
        for hh in range(2):
            h = 2 * p + hh
            seg = a_cs[:, h:h + 1] - a_cs_t[h:h + 1, :]
            decay = jnp.exp(jnp.where(lower, seg, -jnp.inf))
            scores = (cb * decay).astype(BF16)
            y_heads.append(jnp.dot(scores, x_dt_b, preferred_element_type=F32))
        y_diag = jnp.where(left, y_heads[0], y_heads[1])
        st = state_ref[p]
        y_off = (lax.dot_general(cg, st.astype(BF16), NT_DIMS, preferred_element_type=F32)
                 * pair_lanes(decay_in, p))
        x_w = (x_dt * pair_lanes(decay_to_end, p)).astype(BF16)
        chunk_state = lax.dot_general(x_w, bg, TN_DIMS, preferred_element_type=F32)
        st_decay = jnp.where(top, chunk_decay[:, 2 * p:2 * p + 1],
                             chunk_decay[:, 2 * p + 1:2 * p + 2])
        state_ref[p] = st * st_decay + chunk_state
        y = y_diag + y_off + xs_p * dsk_ref[:, p * LANES:(p + 1) * LANES]
        ys.append(y * _silu(z_ref[:, p * LANES:(p + 1) * LANES].astype(F32)))

    grp_pairs = pairs // SSD_GROUPS
    grp_width = SSD_WIDTH // SSD_GROUPS
    for g in range(SSD_GROUPS):
        members = ys[g * grp_pairs:(g + 1) * grp_pairs]
        ssq = sum(jnp.sum(y * y, axis=-1, keepdims=True) for y in members)
        inv = lax.rsqrt(ssq / grp_width + NORM_EPS)
        for i, y in enumerate(members):
            c0 = (g * grp_pairs + i) * LANES
            o_ref[:, c0:c0 + LANES] = (y * inv * nw_ref[:, c0:c0 + LANES]).astype(BF16)


def _ssd(proj, dt_raw, conv_w, conv_b, dt_bias, a_log, d_skip, ssd_norm_w, batch, seq):
    nc = seq // SSD_CHUNK
    gs = SSD_GROUPS * SSD_STATE
    pad = LANES - SSD_HEADS
    row_map = lambda b, c: b * nc + c
    const = lambda b, c: (0, 0)
    return pl.pallas_call(
        _ssd_kernel,
        grid=(batch, nc),
        in_specs=[
            pl.BlockSpec((SSD_CHUNK, SSD_WIDTH), lambda b, c: (row_map(b, c), COL_XS // SSD_WIDTH)),
            pl.BlockSpec((SSD_CHUNK, gs), lambda b, c: (row_map(b, c), COL_B // gs)),
            pl.BlockSpec((SSD_CHUNK, gs), lambda b, c: (row_map(b, c), COL_C // gs)),
            pl.BlockSpec((SSD_CHUNK, SSD_WIDTH), lambda b, c: (row_map(b, c), COL_Z // SSD_WIDTH)),
            pl.BlockSpec((SSD_CHUNK, LANES), lambda b, c: (row_map(b, c), 0)),
            pl.BlockSpec((SSD_CONV, SSD_CONV_DIM), const),
            pl.BlockSpec((1, SSD_CONV_DIM), const),
            pl.BlockSpec((1, LANES), const),
            pl.BlockSpec((1, LANES), const),
            pl.BlockSpec((1, SSD_WIDTH), const),
            pl.BlockSpec((1, SSD_WIDTH), const),
        ],
        out_specs=pl.BlockSpec((SSD_CHUNK, SSD_WIDTH), lambda b, c: (row_map(b, c), 0)),
        out_shape=jax.ShapeDtypeStruct((batch * seq, SSD_WIDTH), BF16),
        scratch_shapes=[pltpu.VMEM((CONV_HALO + SSD_CHUNK, SSD_CONV_DIM), F32),
                        pltpu.VMEM((SSD_HEADS // 2, 2 * SSD_HEAD_DIM, SSD_STATE), F32)],
        compiler_params=pltpu.CompilerParams(
            dimension_semantics=("arbitrary", "arbitrary"),
            vmem_limit_bytes=32 * MIB),
        name="ssd",
    )(proj, proj, proj, proj, dt_raw, conv_w, conv_b.reshape(1, SSD_CONV_DIM),
      jnp.pad(dt_bias, (0, pad)).reshape(1, LANES), jnp.pad(a_log, (0, pad)).reshape(1, LANES),
      jnp.repeat(d_skip, SSD_HEAD_DIM).reshape(1, SSD_WIDTH), ssd_norm_w.reshape(1, SSD_WIDTH))


def _mem_kv_kernel(mem_ref, nw_ref, w_ref, o_ref):
    x = mem_ref[...]
    h = (x * _rms_scale(x) * nw_ref[...]).astype(BF16)
    o_ref[...] = jnp.dot(h, w_ref[...], preferred_element_type=F32).astype(BF16)


def _mem_kv(mem2, mem_norm_w, w_kv, batch, mem_len):
    return pl.pallas_call(
        _mem_kv_kernel,
        grid=(batch,),
        in_specs=[pl.BlockSpec((mem_len, D_MODEL), lambda b: (b, 0)),
                  pl.BlockSpec((1, D_MODEL), lambda b: (0, 0)),
                  pl.BlockSpec((D_MODEL, 2 * XATTN_WIDTH), lambda b: (0, 0))],
        out_specs=pl.BlockSpec((mem_len, 2 * XATTN_WIDTH), lambda b: (b, 0)),
        out_shape=jax.ShapeDtypeStruct((batch * mem_len, 2 * XATTN_WIDTH), BF16),
        compiler_params=pltpu.CompilerParams(
            dimension_semantics=("arbitrary",), vmem_limit_bytes=32 * MIB),
        name="mem_kv",
    )(mem2, mem_norm_w.reshape(1, D_MODEL), w_kv)


def _xattn_kernel(q_ref, g_ref, kv_ref, o_ref):
    scale = XATTN_HEAD_DIM ** -0.5
    for h in range(XATTN_HEADS):
        c = slice(h * XATTN_HEAD_DIM, (h + 1) * XATTN_HEAD_DIM)
        ck = slice(h * XATTN_HEAD_DIM, (h + 1) * XATTN_HEAD_DIM)
        cv = slice(XATTN_WIDTH + h * XATTN_HEAD_DIM, XATTN_WIDTH + (h + 1) * XATTN_HEAD_DIM)
        s = lax.dot_general(q_ref[:, c], kv_ref[:, ck], NT_DIMS,
                            preferred_element_type=F32) * scale
        p = jnp.exp(s - jnp.max(s, axis=-1, keepdims=True))
        inv_l = 1.0 / jnp.sum(p, axis=-1, keepdims=True)
        xo = jnp.dot(p.astype(BF16), kv_ref[:, cv], preferred_element_type=F32) * inv_l
        o_ref[:, c] = (xo * _silu(g_ref[:, c].astype(F32))).astype(BF16)


def _xattn(proj, mkv, batch, seq, mem_len):
    nq = seq // XATT_TQ
    return pl.pallas_call(
        _xattn_kernel,
        grid=(batch, nq),
        in_specs=[pl.BlockSpec((XATT_TQ, XATTN_WIDTH), lambda b, q: (b * nq + q, COL_XQ // XATTN_WIDTH)),
                  pl.BlockSpec((XATT_TQ, XATTN_WIDTH), lambda b, q: (b * nq + q, COL_XG // XATTN_WIDTH)),
                  pl.BlockSpec((mem_len, 2 * XATTN_WIDTH), lambda b, q: (b, 0))],
        out_specs=pl.BlockSpec((XATT_TQ, XATTN_WIDTH), lambda b, q: (b * nq + q, 0)),
        out_shape=jax.ShapeDtypeStruct((batch * seq, XATTN_WIDTH), BF16),
        compiler_params=pltpu.CompilerParams(
            dimension_semantics=("arbitrary", "arbitrary"), vmem_limit_bytes=32 * MIB),
        name="xattn",
    )(proj, proj, mkv)


def _out_proj_kernel(d_ref, s_ref, a_ref, w_ref, nw_ref, x_ref, o_ref):
    for i in range(OUT_TM // OUT_SUB):
        r = slice(i * OUT_SUB, (i + 1) * OUT_SUB)
        y = jnp.dot(d_ref[r, :], w_ref[0:DIFF_WIDTH, :], preferred_element_type=F32)
        y = y + jnp.dot(s_ref[r, :], w_ref[DIFF_WIDTH:DIFF_WIDTH + SSD_WIDTH, :],
                        preferred_element_type=F32)
        y = y + jnp.dot(a_ref[r, :], w_ref[DIFF_WIDTH + SSD_WIDTH:D_MIX, :],
                        preferred_element_type=F32)
        o_ref[r, :] = x_ref[r, :] + y * _rms_scale(y) * nw_ref[...]


def _out_proj(diff_out, ssd_out, xattn_out, w_out, post_norm_w, x2):
    m = x2.shape[0]
    return pl.pallas_call(
        _out_proj_kernel,
        grid=(m // OUT_TM,),
        in_specs=[pl.BlockSpec((OUT_TM, DIFF_WIDTH), lambda i: (i, 0)),
                  pl.BlockSpec((OUT_TM, SSD_WIDTH), lambda i: (i, 0)),
                  pl.BlockSpec((OUT_TM, XATTN_WIDTH), lambda i: (i, 0)),
                  pl.BlockSpec((D_MIX, D_MODEL), lambda i: (0, 0)),
                  pl.BlockSpec((1, D_MODEL), lambda i: (0, 0)),
                  pl.BlockSpec((OUT_TM, D_MODEL), lambda i: (i, 0))],
        out_specs=pl.BlockSpec((OUT_TM, D_MODEL), lambda i: (i, 0)),
        out_shape=jax.ShapeDtypeStruct((m, D_MODEL), F32),
        compiler_params=pltpu.CompilerParams(
            dimension_semantics=("arbitrary",), vmem_limit_bytes=48 * MIB),
        name="out_proj",
    )(diff_out, ssd_out, xattn_out, w_out, post_norm_w.reshape(1, D_MODEL), x2)


def kernel(x, mem, positions, pre_norm_w, w_in, lambda_q1, lambda_k1, lambda_q2, lambda_k2,
           diff_subln_w, conv_w, conv_b, dt_bias, a_log, d_skip, ssd_norm_w, mem_norm_w,
           w_mem_kv, w_out, post_norm_w):
    batch, seq, _ = x.shape
    mem_len = mem.shape[1]
    assert pre_norm_w.shape[0] == 1, "single-layer kernel"
    x2 = x.reshape(batch * seq, D_MODEL)

    w_main, w_dt = _w_in_prep(w_in[0])
    lam_vecs = jnp.stack([lambda_q1[0], lambda_k1[0], lambda_q2[0], lambda_k2[0]]).astype(F32)

    cos, sin = _rope_tables(positions)
    proj, dt_raw = _in_proj(x2, pre_norm_w[0], w_main, w_dt)
    diff_out = _diff_attention(proj, cos, sin, lam_vecs, diff_subln_w[0], batch, seq)
    ssd_out = _ssd(proj, dt_raw, conv_w[0], conv_b[0], dt_bias[0], a_log[0], d_skip[0],
                   ssd_norm_w[0], batch, seq)
    mkv = _mem_kv(mem.reshape(batch * mem_len, D_MODEL), mem_norm_w[0],
                  w_mem_kv[0].astype(BF16), batch, mem_len)
    xattn_out = _xattn(proj, mkv, batch, seq, mem_len)
    out = _out_proj(diff_out, ssd_out, xattn_out, w_out[0].astype(BF16), post_norm_w[0], x2)
    return out.reshape(batch, seq, D_MODEL)
```

```python
import functools
import math

import jax
import jax.numpy as jnp
from jax import lax
from jax.experimental import pallas as pl
from jax.experimental.pallas import tpu as pltpu

F32 = jnp.float32
BF16 = jnp.bfloat16

D_MODEL = 2048
DIFF_HEADS = 8
DIFF_QK_DIM = 64
DIFF_V_DIM = 128
DIFF_WIDTH = DIFF_HEADS * DIFF_V_DIM
SSD_HEADS = 8
SSD_HEAD_DIM = 64
SSD_WIDTH = SSD_HEADS * SSD_HEAD_DIM
SSD_GROUPS = 2
SSD_STATE = 128
SSD_CONV = 4
SSD_CHUNK = 128
SSD_CONV_DIM = SSD_WIDTH + 2 * SSD_GROUPS * SSD_STATE
XATTN_HEADS = 4
XATTN_HEAD_DIM = 128
XATTN_WIDTH = XATTN_HEADS * XATTN_HEAD_DIM
D_MIX = DIFF_WIDTH + SSD_WIDTH + XATTN_WIDTH
ROPE_THETA = 10000.0
NORM_EPS = 1e-6
LAMBDA_INIT = 0.8 - 0.6 * math.exp(-0.3 * 0)

LANES = 128
SUBLANES = 8
MIB = 1024 * 1024

COL_Q = 0
COL_K = COL_Q + DIFF_WIDTH
COL_V = COL_K + DIFF_WIDTH
COL_G = COL_V + DIFF_WIDTH
COL_Z = COL_G + DIFF_WIDTH
COL_XS = COL_Z + SSD_WIDTH
COL_B = COL_XS + SSD_WIDTH
COL_C = COL_B + SSD_GROUPS * SSD_STATE
COL_XQ = COL_C + SSD_GROUPS * SSD_STATE
COL_XG = COL_XQ + XATTN_WIDTH
N_MAIN = COL_XG + XATTN_WIDTH
REF_DT = COL_XQ
REF_XQ = REF_DT + SSD_HEADS

PROJ_TM = 512
PROJ_TN = 512
NORM_ROWS = 256
ATT_TQ = 512
ATT_TK = 512
ATT_HEADS_PER_STEP = 4
ATT_SUM_ROWS = 16
XATT_TQ = 512
OUT_TM = 512
OUT_SUB = 256
ROPE_TM = 1024
PREP_ROWS = 256

NT_DIMS = (((1,), (1,)), ((), ()))
TN_DIMS = (((0,), (0,)), ((), ()))


def _rms_scale(x):
    return lax.rsqrt(jnp.mean(x * x, axis=-1, keepdims=True) + NORM_EPS)


def _silu(x):
    return x * jax.nn.sigmoid(x)


def _rope_table_kernel(pos_ref, invf_ref, cos_ref, sin_ref):
    ang = pos_ref[...].astype(F32) * invf_ref[...]
    lane = lax.broadcasted_iota(jnp.int32, ang.shape, 1)
    first_half = (lane % DIFF_QK_DIM) < (DIFF_QK_DIM // 2)
    s = jnp.sin(ang)
    cos_ref[...] = jnp.cos(ang)
    sin_ref[...] = jnp.where(first_half, -s, s)


def _rope_tables(positions):
    n = positions.size
    inv_freq = 1.0 / (ROPE_THETA ** (jnp.arange(0, DIFF_QK_DIM, 2, dtype=F32) / DIFF_QK_DIM))
    invf = jnp.tile(inv_freq, LANES // inv_freq.shape[0]).reshape(1, LANES)
    pos = positions.reshape(n, 1)
    return pl.pallas_call(
        _rope_table_kernel,
        grid=(n // ROPE_TM,),
        in_specs=[pl.BlockSpec((ROPE_TM, 1), lambda i: (i, 0)),
                  pl.BlockSpec((1, LANES), lambda i: (0, 0))],
        out_specs=[pl.BlockSpec((ROPE_TM, LANES), lambda i: (i, 0)),
                   pl.BlockSpec((ROPE_TM, LANES), lambda i: (i, 0))],
        out_shape=[jax.ShapeDtypeStruct((n, LANES), F32)] * 2,
        name="rope_tables",
    )(pos, invf)


def _rope(x, cos, sin_signed):
    lane = lax.broadcasted_iota(jnp.int32, x.shape, 1)
    first_half = (lane % DIFF_QK_DIM) < (DIFF_QK_DIM // 2)
    half = DIFF_QK_DIM // 2
    rot = jnp.where(first_half, pltpu.roll(x, LANES - half, 1), pltpu.roll(x, half, 1))
    return x * cos + rot * sin_signed


def _w_in_prep_kernel(wt_ref, dt_rows_ref, wm_ref, wdt_ref):
    wm_ref[...] = wt_ref[...].T.astype(BF16)

    @pl.when(pl.program_id(0) == 0)
    def _():
        lane = lax.broadcasted_iota(jnp.int32, (D_MODEL, LANES), 1)
        wdt_ref[...] = jnp.where(lane < SSD_HEADS, dt_rows_ref[...].T, 0.0).astype(BF16)


def _w_in_prep(w_t):
    n, k = w_t.shape
    assert n == REF_XQ + N_MAIN - REF_DT and k == D_MODEL
    dt_block = REF_DT // PROJ_TN

    def src_row(j):
        return pl.multiple_of(j * PROJ_TN + jnp.where(j >= dt_block, SSD_HEADS, 0), SUBLANES)

    return pl.pallas_call(
        _w_in_prep_kernel,
        grid=(N_MAIN // PROJ_TN,),
        in_specs=[pl.BlockSpec((pl.Element(PROJ_TN), pl.Element(D_MODEL)),
                               lambda j: (src_row(j), 0)),
                  pl.BlockSpec((LANES, D_MODEL), lambda j: (REF_DT // LANES, 0))],
        out_specs=[pl.BlockSpec((D_MODEL, PROJ_TN), lambda j: (0, j)),
                   pl.BlockSpec((D_MODEL, LANES), lambda j: (0, 0))],
        out_shape=[jax.ShapeDtypeStruct((k, N_MAIN), BF16),
                   jax.ShapeDtypeStruct((k, LANES), BF16)],
        compiler_params=pltpu.CompilerParams(
            dimension_semantics=("arbitrary",), vmem_limit_bytes=32 * MIB),
        name="w_in_prep",
    )(w_t, w_t)


def _in_proj_kernel(x_ref, nw_ref, w_ref, wdt_ref, o_ref, dt_ref, h_ref):
    def norm_rows(i, carry):
        r = pl.ds(pl.multiple_of(i * NORM_ROWS, NORM_ROWS), NORM_ROWS)
        x = x_ref[r, :]
        h_ref[r, :] = (x * _rms_scale(x) * nw_ref[...]).astype(BF16)
        return carry
    lax.fori_loop(0, PROJ_TM // NORM_ROWS, norm_rows, 0)
    dt_ref[...] = jnp.dot(h_ref[...], wdt_ref[...], preferred_element_type=F32)
    for n in range(N_MAIN // PROJ_TN):
        c = slice(n * PROJ_TN, (n + 1) * PROJ_TN)
        o_ref[:, c] = jnp.dot(h_ref[...], w_ref[:, c],
                              preferred_element_type=F32).astype(BF16)


def _in_proj(x2, pre_norm_w, w_main, w_dt):
    m = x2.shape[0]
    resident = pl.Buffered(1)
    return pl.pallas_call(
        _in_proj_kernel,
        grid=(m // PROJ_TM,),
        in_specs=[pl.BlockSpec((PROJ_TM, D_MODEL), lambda i: (i, 0)),
                  pl.BlockSpec((1, D_MODEL), lambda i: (0, 0)),
                  pl.BlockSpec((D_MODEL, N_MAIN), lambda i: (0, 0), pipeline_mode=resident),
                  pl.BlockSpec((D_MODEL, LANES), lambda i: (0, 0), pipeline_mode=resident)],
        out_specs=[pl.BlockSpec((PROJ_TM, N_MAIN), lambda i: (i, 0)),
                   pl.BlockSpec((PROJ_TM, LANES), lambda i: (i, 0))],
        out_shape=[jax.ShapeDtypeStruct((m, N_MAIN), BF16),
                   jax.ShapeDtypeStruct((m, LANES), F32)],
        scratch_shapes=[pltpu.VMEM((PROJ_TM, D_MODEL), BF16)],
        compiler_params=pltpu.CompilerParams(
            dimension_semantics=("arbitrary",),
            vmem_limit_bytes=56 * MIB),
        name="in_proj",
    )(x2, pre_norm_w.reshape(1, D_MODEL), w_main, w_dt)


def _diff_attn_kernel(lam_ref, q_ref, k_ref, v_ref, g_ref, cos_ref, sin_ref, subw_ref,
                      o_ref, kr_ref, vt_ref, m_ref, acc_ref, *, seq):
    qi = pl.program_id(2)
    heads = ATT_HEADS_PER_STEP

    @pl.when(qi == 0)
    def _():
        def prep_rows(i, carry):
            r = pl.ds(pl.multiple_of(i * ATT_TK, ATT_TK), ATT_TK)
            for g in range(heads):
                c = slice(g * LANES, (g + 1) * LANES)
                kr_ref[r, c] = _rope(k_ref[r, c].astype(F32), cos_ref[r, :],
                                     sin_ref[r, :]).astype(BF16)
                vt_ref[g, 0:DIFF_V_DIM, r] = v_ref[r, c].astype(F32).T.astype(BF16)
            return carry
        lax.fori_loop(0, seq // ATT_TK, prep_rows, 0)
        vt_ref[:, DIFF_V_DIM:, :] = jnp.ones((heads, ATT_SUM_ROWS, seq), BF16)

    rq = pl.ds(pl.multiple_of(qi * ATT_TQ, ATT_TQ), ATT_TQ)
    scale = DIFF_QK_DIM ** -0.5 * math.log2(math.e)
    lane = lax.broadcasted_iota(jnp.int32, (ATT_TQ, LANES), 1)
    q_maps = []
    for g in range(heads):
        qf = _rope(q_ref[:, g * LANES:(g + 1) * LANES].astype(F32), cos_ref[rq, :],
                   sin_ref[rq, :]) * scale
        q_maps.append((jnp.where(lane < DIFF_QK_DIM, qf, 0.0).astype(BF16),
                       jnp.where(lane >= DIFF_QK_DIM, qf, 0.0).astype(BF16)))

    m_ref[...] = jnp.full(m_ref.shape, -jnp.inf, F32)
    acc_ref[...] = jnp.zeros(acc_ref.shape, F32)

    key = lax.broadcasted_iota(jnp.int32, (ATT_TK, ATT_TQ), 0)
    qry = lax.broadcasted_iota(jnp.int32, (ATT_TK, ATT_TQ), 1)
    causal = key <= qry

    chains = [(g, c) for g in range(heads) for c in range(2)]

    def tile(j, masked):
        rk = pl.ds(pl.multiple_of(j * ATT_TK, ATT_TK), ATT_TK)

        def scores(g, c):
            kt = kr_ref[rk, g * LANES:(g + 1) * LANES]
            s = lax.dot_general(kt, q_maps[g][c], NT_DIMS, preferred_element_type=F32)
            return jnp.where(causal, s, -jnp.inf) if masked else s

        def softmax(i, s):
            m_old = m_ref[i]
            m_new = jnp.maximum(m_old, jnp.max(s, axis=0, keepdims=True))
            p = jnp.exp2(s - m_new)
            alpha = jnp.exp2(m_old - m_new)
            m_ref[i] = m_new
            return p.astype(BF16), alpha

        def accumulate(i, g, p, alpha):
            vt = vt_ref[g, :, rk]
            acc_ref[i] = alpha * acc_ref[i] + jnp.dot(vt, p, preferred_element_type=F32)

        ahead = 2
        s_vals = {}
        for i in range(min(ahead, len(chains))):
            s_vals[i] = scores(*chains[i])
        for i, (g, c) in enumerate(chains):
            p, alpha = softmax(i, s_vals.pop(i))
            if i + ahead < len(chains):
                s_vals[i + ahead] = scores(*chains[i + ahead])
            accumulate(i, g, p, alpha)

    def full_tile(j, carry):
        tile(j, masked=False)
        return carry
    lax.fori_loop(0, qi, full_tile, 0)
    tile(qi, masked=True)

    lv = lam_ref[...]
    lam = (jnp.exp(jnp.sum(lv[0:1] * lv[1:2], axis=-1, keepdims=True))
           - jnp.exp(jnp.sum(lv[2:3] * lv[3:4], axis=-1, keepdims=True)) + LAMBDA_INIT)
    for g in range(heads):
        c = slice(g * LANES, (g + 1) * LANES)
        a1, a2 = acc_ref[2 * g], acc_ref[2 * g + 1]
        l1 = a1[DIFF_V_DIM:DIFF_V_DIM + 1, :]
        l2 = a2[DIFF_V_DIM:DIFF_V_DIM + 1, :]
        o_t = (a1[0:DIFF_V_DIM, :] * (1.0 / l1)
               - lam * (a2[0:DIFF_V_DIM, :] * (1.0 / l2)))
        o = o_t.T
        o = o * _rms_scale(o) * subw_ref[...] * (1.0 - LAMBDA_INIT)
        o_ref[:, c] = (o * _silu(g_ref[:, c].astype(F32))).astype(BF16)


def _diff_attention(proj, cos, sin, lam_vecs, subln_w, batch, seq):
    assert ATT_TQ == ATT_TK and seq % ATT_TQ == 0 and DIFF_V_DIM == LANES
    nq = seq // ATT_TQ
    heads = ATT_HEADS_PER_STEP
    width = heads * LANES
    kernel = functools.partial(_diff_attn_kernel, seq=seq)
    return pl.pallas_call(
        kernel,
        grid=(batch, DIFF_HEADS // heads, nq),
        in_specs=[
            pl.BlockSpec((4, DIFF_QK_DIM), lambda b, h, q: (0, 0)),
            pl.BlockSpec((ATT_TQ, width), lambda b, h, q: (b * nq + q, COL_Q // width + h)),
            pl.BlockSpec((seq, width), lambda b, h, q: (b, COL_K // width + h)),
            pl.BlockSpec((seq, width), lambda b, h, q: (b, COL_V // width + h)),
            pl.BlockSpec((ATT_TQ, width), lambda b, h, q: (b * nq + q, COL_G // width + h)),
            pl.BlockSpec((seq, LANES), lambda b, h, q: (b, 0)),
            pl.BlockSpec((seq, LANES), lambda b, h, q: (b, 0)),
            pl.BlockSpec((1, DIFF_V_DIM), lambda b, h, q: (0, 0)),
        ],
        out_specs=pl.BlockSpec((ATT_TQ, width), lambda b, h, q: (b * nq + q, h)),
        out_shape=jax.ShapeDtypeStruct((batch * seq, DIFF_WIDTH), BF16),
        scratch_shapes=[pltpu.VMEM((seq, width), BF16),
                        pltpu.VMEM((heads, DIFF_V_DIM + ATT_SUM_ROWS, seq), BF16),
                        pltpu.VMEM((2 * heads, 1, ATT_TQ), F32),
                        pltpu.VMEM((2 * heads, DIFF_V_DIM + ATT_SUM_ROWS, ATT_TQ), F32)],
        compiler_params=pltpu.CompilerParams(
            dimension_semantics=("arbitrary", "arbitrary", "arbitrary"),
            vmem_limit_bytes=48 * MIB),
        name="diff_attn",
    )(lam_vecs, proj, proj, proj, proj, cos, sin, subln_w.reshape(1, DIFF_V_DIM))


CONV_HALO = SUBLANES


def _ssd_kernel(xs_ref, b_ref, c_ref, z_ref, dt_ref, cw_ref, cb_ref, dtb_ref, alog_ref,
                dsk_ref, nw_ref, o_ref, ext_ref, state_ref):
    L = SSD_CHUNK
    gs = SSD_GROUPS * SSD_STATE

    @pl.when(pl.program_id(1) == 0)
    def _():
        ext_ref[0:CONV_HALO, :] = jnp.zeros((CONV_HALO, SSD_CONV_DIM), F32)
        state_ref[...] = jnp.zeros(state_ref.shape, F32)

    ext_ref[CONV_HALO:CONV_HALO + L, 0:SSD_WIDTH] = xs_ref[...].astype(F32)
    ext_ref[CONV_HALO:CONV_HALO + L, SSD_WIDTH:SSD_WIDTH + gs] = b_ref[...].astype(F32)
    ext_ref[CONV_HALO:CONV_HALO + L, SSD_WIDTH + gs:SSD_CONV_DIM] = c_ref[...].astype(F32)
    conv = jnp.broadcast_to(cb_ref[...], (L, SSD_CONV_DIM))
    for k in range(SSD_CONV):
        start = CONV_HALO - (SSD_CONV - 1) + k
        conv = conv + ext_ref[start:start + L, :] * cw_ref[k:k + 1, :]
    ext_ref[0:CONV_HALO, :] = ext_ref[L:L + CONV_HALO, :]
    xbc = _silu(conv)
    xs = xbc[:, 0:SSD_WIDTH]
    bm = xbc[:, SSD_WIDTH:SSD_WIDTH + gs].astype(BF16)
    cm = xbc[:, SSD_WIDTH + gs:SSD_CONV_DIM].astype(BF16)

    dt_in = dt_ref[...] + dtb_ref[...]
    dt = jnp.maximum(dt_in, 0.0) + jnp.log1p(jnp.exp(-jnp.abs(dt_in)))
    a_dt = dt * (-jnp.exp(alog_ref[...]))
    row = lax.broadcasted_iota(jnp.int32, (L, L), 0)
    col = lax.broadcasted_iota(jnp.int32, (L, L), 1)
    lower = row >= col
    tril = lower.astype(BF16)
    hi = a_dt.astype(BF16)
    r1 = a_dt - hi.astype(F32)
    mid = r1.astype(BF16)
    lo = (r1 - mid.astype(F32)).astype(BF16)
    a_cs = (jnp.dot(tril, hi, preferred_element_type=F32)
            + jnp.dot(tril, mid, preferred_element_type=F32)
            + jnp.dot(tril, lo, preferred_element_type=F32))
    a_cs_t = a_cs.T
    a_end = a_cs[L - 1:L, :]
    decay_in = jnp.exp(a_cs)
    decay_to_end = jnp.exp(a_end - a_cs)
    chunk_decay = jnp.exp(a_end)

    lane = lax.broadcasted_iota(jnp.int32, (L, LANES), 1)
    left = lane < SSD_HEAD_DIM
    top = row < SSD_HEAD_DIM

    def pair_lanes(mat, p):
        return jnp.where(left, mat[:, 2 * p:2 * p + 1], mat[:, 2 * p + 1:2 * p + 2])

    ys = []
    pairs = SSD_HEADS // 2
    for p in range(pairs):
        g = (2 * p) // (SSD_HEADS // SSD_GROUPS)
        cg = cm[:, g * SSD_STATE:(g + 1) * SSD_STATE]
        bg = bm[:, g * SSD_STATE:(g + 1) * SSD_STATE]
        cb = lax.dot_general(cg, bg, NT_DIMS, preferred_element_type=F32)
        xs_p = xs[:, p * LANES:(p + 1) * LANES]
        x_dt = xs_p * pair_lanes(dt, p)
        x_dt_b = x_dt.astype(BF16)
        y_heads = []
        for hh in range(2):
            h = 2 * p + hh
            seg = a_cs[:, h:h + 1] - a_cs_t[h:h + 1, :]
            decay = jnp.exp(jnp.where(lower, seg, -jnp.inf))
            scores = (cb * decay).astype(BF16)
            y_heads.append(jnp.dot(scores, x_dt_b, preferred_element_type=F32))
        y_diag = jnp.where(left, y_heads[0], y_heads[1])
        st = state_ref[p]
        y_off = (lax.dot_general(cg, st.astype(BF16), NT_DIMS, preferred_element_type=F32)
                 * pair_lanes(decay_in, p))
        x_w = (x_dt * pair_lanes(decay_to_end, p)).astype(BF16)
        chunk_state = lax.dot_general(x_w, bg, TN_DIMS, preferred_element_type=F32)
        st_decay = jnp.where(top, chunk_decay[:, 2 * p:2 * p + 1],
                             chunk_decay[:, 2 * p + 1:2 * p + 2])
        state_ref[p] = st * st_decay + chunk_state
        y = y_diag + y_off + xs_p * dsk_ref[:, p * LANES:(p + 1) * LANES]
        ys.append(y * _silu(z_ref[:, p * LANES:(p + 1) * LANES].astype(F32)))

    grp_pairs = pairs // SSD_GROUPS
    grp_width = SSD_WIDTH // SSD_GROUPS
    for g in range(SSD_GROUPS):
        members = ys[g * grp_pairs:(g + 1) * grp_pairs]
        ssq = sum(jnp.sum(y * y, axis=-1, keepdims=True) for y in members)
        inv = lax.rsqrt(ssq / grp_width + NORM_EPS)
        for i, y in enumerate(members):
            c0 = (g * grp_pairs + i) * LANES
            o_ref[:, c0:c0 + LANES] = (y * inv * nw_ref[:, c0:c0 + LANES]).astype(BF16)


def _ssd(proj, dt_raw, conv_w, conv_b, dt_bias, a_log, d_skip, ssd_norm_w, batch, seq):
    nc = seq // SSD_CHUNK
    gs = SSD_GROUPS * SSD_STATE
    pad = LANES - SSD_HEADS
    row_map = lambda b, c: b * nc + c
    const = lambda b, c: (0, 0)
    return pl.pallas_call(
        _ssd_kernel,
        grid=(batch, nc),
        in_specs=[
            pl.BlockSpec((SSD_CHUNK, SSD_WIDTH), lambda b, c: (row_map(b, c), COL_XS // SSD_WIDTH)),
            pl.BlockSpec((SSD_CHUNK, gs), lambda b, c: (row_map(b, c), COL_B // gs)),
            pl.BlockSpec((SSD_CHUNK, gs), lambda b, c: (row_map(b, c), COL_C // gs)),
            pl.BlockSpec((SSD_CHUNK, SSD_WIDTH), lambda b, c: (row_map(b, c), COL_Z // SSD_WIDTH)),
            pl.BlockSpec((SSD_CHUNK, LANES), lambda b, c: (row_map(b, c), 0)),
            pl.BlockSpec((SSD_CONV, SSD_CONV_DIM), const),
            pl.BlockSpec((1, SSD_CONV_DIM), const),
            pl.BlockSpec((1, LANES), const),
            pl.BlockSpec((1, LANES), const),
            pl.BlockSpec((1, SSD_WIDTH), const),
            pl.BlockSpec((1, SSD_WIDTH), const),
        ],
        out_specs=pl.BlockSpec((SSD_CHUNK, SSD_WIDTH), lambda b, c: (row_map(b, c), 0)),
        out_shape=jax.ShapeDtypeStruct((batch * seq, SSD_WIDTH), BF16),
        scratch_shapes=[pltpu.VMEM((CONV_HALO + SSD_CHUNK, SSD_CONV_DIM), F32),
                        pltpu.VMEM((SSD_HEADS // 2, 2 * SSD_HEAD_DIM, SSD_STATE), F32)],
        compiler_params=pltpu.CompilerParams(
            dimension_semantics=("arbitrary", "arbitrary"),
            vmem_limit_bytes=32 * MIB),
        name="ssd",
    )(proj, proj, proj, proj, dt_raw, conv_w, conv_b.reshape(1, SSD_CONV_DIM),
      jnp.pad(dt_bias, (0, pad)).reshape(1, LANES), jnp.pad(a_log, (0, pad)).reshape(1, LANES),
      jnp.repeat(d_skip, SSD_HEAD_DIM).reshape(1, SSD_WIDTH), ssd_norm_w.reshape(1, SSD_WIDTH))


def _mem_kv_kernel(mem_ref, nw_ref, w_ref, o_ref):
    x = mem_ref[...]
    h = (x * _rms_scale(x) * nw_ref[...]).astype(BF16)
    o_ref[...] = jnp.dot(h, w_ref[...], preferred_element_type=F32).astype(BF16)


def _mem_kv(mem2, mem_norm_w, w_kv, batch, mem_len):
    return pl.pallas_call(
        _mem_kv_kernel,
        grid=(batch,),
        in_specs=[pl.BlockSpec((mem_len, D_MODEL), lambda b: (b, 0)),
                  pl.BlockSpec((1, D_MODEL), lambda b: (0, 0)),
                  pl.BlockSpec((D_MODEL, 2 * XATTN_WIDTH), lambda b: (0, 0))],
        out_specs=pl.BlockSpec((mem_len, 2 * XATTN_WIDTH), lambda b: (b, 0)),
        out_shape=jax.ShapeDtypeStruct((batch * mem_len, 2 * XATTN_WIDTH), BF16),
        compiler_params=pltpu.CompilerParams(
            dimension_semantics=("arbitrary",), vmem_limit_bytes=32 * MIB),
        name="mem_kv",
    )(mem2, mem_norm_w.reshape(1, D_MODEL), w_kv)


def _xattn_kernel(q_ref, g_ref, kv_ref, o_ref):
    scale = XATTN_HEAD_DIM ** -0.5
    for h in range(XATTN_HEADS):
        c = slice(h * XATTN_HEAD_DIM, (h + 1) * XATTN_HEAD_DIM)
        ck = slice(h * XATTN_HEAD_DIM, (h + 1) * XATTN_HEAD_DIM)
        cv = slice(XATTN_WIDTH + h * XATTN_HEAD_DIM, XATTN_WIDTH + (h + 1) * XATTN_HEAD_DIM)
        s = lax.dot_general(q_ref[:, c], kv_ref[:, ck], NT_DIMS,
                            preferred_element_type=F32) * scale
        p = jnp.exp(s - jnp.max(s, axis=-1, keepdims=True))
        inv_l = 1.0 / jnp.sum(p, axis=-1, keepdims=True)
        xo = jnp.dot(p.astype(BF16), kv_ref[:, cv], preferred_element_type=F32) * inv_l
        o_ref[:, c] = (xo * _silu(g_ref[:, c].astype(F32))).astype(BF16)


def _xattn(proj, mkv, batch, seq, mem_len):
    nq = seq // XATT_TQ
    return pl.pallas_call(
        _xattn_kernel,
        grid=(batch, nq),
        in_specs=[pl.BlockSpec((XATT_TQ, XATTN_WIDTH), lambda b, q: (b * nq + q, COL_XQ // XATTN_WIDTH)),
                  pl.BlockSpec((XATT_TQ, XATTN_WIDTH), lambda b, q: (b * nq + q, COL_XG // XATTN_WIDTH)),
                  pl.BlockSpec((mem_len, 2 * XATTN_WIDTH), lambda b, q: (b, 0))],
        out_specs=pl.BlockSpec((XATT_TQ, XATTN_WIDTH), lambda b, q: (b * nq + q, 0)),
        out_shape=jax.ShapeDtypeStruct((batch * seq, XATTN_WIDTH), BF16),
        compiler_params=pltpu.CompilerParams(
            dimension_semantics=("arbitrary", "arbitrary"), vmem_limit_bytes=32 * MIB),
        name="xattn",
    )(proj, proj, mkv)


def _out_proj_kernel(d_ref, s_ref, a_ref, w_ref, nw_ref, x_ref, o_ref):
    for i in range(OUT_TM // OUT_SUB):
        r = slice(i * OUT_SUB, (i + 1) * OUT_SUB)
        y = jnp.dot(d_ref[r, :], w_ref[0:DIFF_WIDTH, :], preferred_element_type=F32)
        y = y + jnp.dot(s_ref[r, :], w_ref[DIFF_WIDTH:DIFF_WIDTH + SSD_WIDTH, :],
                        preferred_element_type=F32)
        y = y + jnp.dot(a_ref[r, :], w_ref[DIFF_WIDTH + SSD_WIDTH:D_MIX, :],
                        preferred_element_type=F32)
        o_ref[r, :] = x_ref[r, :] + y * _rms_scale(y) * nw_ref[...]


def _out_proj(diff_out, ssd_out, xattn_out, w_out, post_norm_w, x2):
    m = x2.shape[0]
    return pl.pallas_call(
        _out_proj_kernel,
        grid=(m // OUT_TM,),
        in_specs=[pl.BlockSpec((OUT_TM, DIFF_WIDTH), lambda i: (i, 0)),
                  pl.BlockSpec((OUT_TM, SSD_WIDTH), lambda i: (i, 0)),
                  pl.BlockSpec((OUT_TM, XATTN_WIDTH), lambda i: (i, 0)),
                  pl.BlockSpec((D_MIX, D_MODEL), lambda i: (0, 0)),
                  pl.BlockSpec((1, D_MODEL), lambda i: (0, 0)),
                  pl.BlockSpec((OUT_TM, D_MODEL), lambda i: (i, 0))],
        out_specs=pl.BlockSpec((OUT_TM, D_MODEL), lambda i: (i, 0)),
        out_shape=jax.ShapeDtypeStruct((m, D_MODEL), F32),
        compiler_params=pltpu.CompilerParams(
            dimension_semantics=("arbitrary",), vmem_limit_bytes=48 * MIB),
        name="out_proj",
    )(diff_out, ssd_out, xattn_out, w_out, post_norm_w.reshape(1, D_MODEL), x2)


def kernel(x, mem, positions, pre_norm_w, w_in, lambda_q1, lambda_k1, lambda_q2, lambda_k2,
           diff_subln_w, conv_w, conv_b, dt_bias, a_log, d_skip, ssd_norm_w, mem_norm_w,
           w_mem_kv, w_out, post_norm_w):
    batch, seq, _ = x.shape
    mem_len = mem.shape[1]
    assert pre_norm_w.shape[0] == 1, "single-layer kernel"
    x2 = x.reshape(batch * seq, D_MODEL)

    w_main, w_dt = _w_in_prep(jnp.transpose(w_in[0]))
    lam_vecs = jnp.stack([lambda_q1[0], lambda_k1[0], lambda_q2[0], lambda_k2[0]]).astype(F32)

    cos, sin = _rope_tables(positions)
    proj, dt_raw = _in_proj(x2, pre_norm_w[0], w_main, w_dt)
    diff_out = _diff_attention(proj, cos, sin, lam_vecs, diff_subln_w[0], batch, seq)
    ssd_out = _ssd(proj, dt_raw, conv_w[0], conv_b[0], dt_bias[0], a_log[0], d_skip[0],
                   ssd_norm_w[0], batch, seq)
    mkv = _mem_kv(mem.reshape(batch * mem_len, D_MODEL), mem_norm_w[0],
                  w_mem_kv[0].astype(BF16), batch, mem_len)
    xattn_out = _xattn(proj, mkv, batch, seq, mem_len)
    out = _out_proj(diff_out, ssd_out, xattn_out, w_out[0].astype(BF16), post_norm_w[0], x2)
    return out.reshape(batch, seq, D_MODEL)
```

```python
import functools
import math

import jax
import jax.numpy as jnp
from jax import lax
from jax.experimental import pallas as pl
from jax.experimental.pallas import tpu as pltpu

F32 = jnp.float32
BF16 = jnp.bfloat16

D_MODEL = 2048
DIFF_HEADS = 8
DIFF_QK_DIM = 64
DIFF_V_DIM = 128
DIFF_WIDTH = DIFF_HEADS * DIFF_V_DIM
SSD_HEADS = 8
SSD_HEAD_DIM = 64
SSD_WIDTH = SSD_HEADS * SSD_HEAD_DIM
SSD_GROUPS = 2
SSD_STATE = 128
SSD_CONV = 4
SSD_CHUNK = 128
SSD_CONV_DIM = SSD_WIDTH + 2 * SSD_GROUPS * SSD_STATE
XATTN_HEADS = 4
XATTN_HEAD_DIM = 128
XATTN_WIDTH = XATTN_HEADS * XATTN_HEAD_DIM
D_MIX = DIFF_WIDTH + SSD_WIDTH + XATTN_WIDTH
ROPE_THETA = 10000.0
NORM_EPS = 1e-6
LAMBDA_INIT = 0.8 - 0.6 * math.exp(-0.3 * 0)

LANES = 128
SUBLANES = 8
MIB = 1024 * 1024

COL_Q = 0
COL_K = COL_Q + DIFF_WIDTH
COL_V = COL_K + DIFF_WIDTH
COL_G = COL_V + DIFF_WIDTH
COL_Z = COL_G + DIFF_WIDTH
COL_XS = COL_Z + SSD_WIDTH
COL_B = COL_XS + SSD_WIDTH
COL_C = COL_B + SSD_GROUPS * SSD_STATE
COL_XQ = COL_C + SSD_GROUPS * SSD_STATE
COL_XG = COL_XQ + XATTN_WIDTH
N_MAIN = COL_XG + XATTN_WIDTH
REF_DT = COL_XQ
REF_XQ = REF_DT + SSD_HEADS

PROJ_TM = 512
PROJ_TN = 512
NORM_ROWS = 256
ATT_TQ = 512
ATT_TK = 512
ATT_HEADS_PER_STEP = 4
ATT_SUM_ROWS = 16
ATT_AHEAD = 2
XATT_TQ = 512
OUT_TM = 512
OUT_SUB = 256
ROPE_TM = 1024
PREP_ROWS = 256

NT_DIMS = (((1,), (1,)), ((), ()))
TN_DIMS = (((0,), (0,)), ((), ()))


def _rms_scale(x):
    return lax.rsqrt(jnp.mean(x * x, axis=-1, keepdims=True) + NORM_EPS)


def _silu(x):
    return x * jax.nn.sigmoid(x)


def _rope_table_kernel(pos_ref, invf_ref, cos_ref, sin_ref):
    ang = pos_ref[...].astype(F32) * invf_ref[...]
    lane = lax.broadcasted_iota(jnp.int32, ang.shape, 1)
    first_half = (lane % DIFF_QK_DIM) < (DIFF_QK_DIM // 2)
    s = jnp.sin(ang)
    cos_ref[...] = jnp.cos(ang)
    sin_ref[...] = jnp.where(first_half, -s, s)


def _rope_tables(positions):
    n = positions.size
    inv_freq = 1.0 / (ROPE_THETA ** (jnp.arange(0, DIFF_QK_DIM, 2, dtype=F32) / DIFF_QK_DIM))
    invf = jnp.tile(inv_freq, LANES // inv_freq.shape[0]).reshape(1, LANES)
    pos = positions.reshape(n, 1)
    return pl.pallas_call(
        _rope_table_kernel,
        grid=(n // ROPE_TM,),
        in_specs=[pl.BlockSpec((ROPE_TM, 1), lambda i: (i, 0)),
                  pl.BlockSpec((1, LANES), lambda i: (0, 0))],
        out_specs=[pl.BlockSpec((ROPE_TM, LANES), lambda i: (i, 0)),
                   pl.BlockSpec((ROPE_TM, LANES), lambda i: (i, 0))],
        out_shape=[jax.ShapeDtypeStruct((n, LANES), F32)] * 2,
        name="rope_tables",
    )(pos, invf)


def _rope(x, cos, sin_signed):
    lane = lax.broadcasted_iota(jnp.int32, x.shape, 1)
    first_half = (lane % DIFF_QK_DIM) < (DIFF_QK_DIM // 2)
    half = DIFF_QK_DIM // 2
    rot = jnp.where(first_half, pltpu.roll(x, LANES - half, 1), pltpu.roll(x, half, 1))
    return x * cos + rot * sin_signed


def _w_in_prep_kernel(wt_ref, dt_rows_ref, wm_ref, wdt_ref):
    wm_ref[...] = wt_ref[...].T.astype(BF16)

    @pl.when(pl.program_id(0) == 0)
    def _():
        lane = lax.broadcasted_iota(jnp.int32, (D_MODEL, LANES), 1)
        wdt_ref[...] = jnp.where(lane < SSD_HEADS, dt_rows_ref[...].T, 0.0).astype(BF16)


def _w_in_prep(w_t):
    n, k = w_t.shape
    assert n == REF_XQ + N_MAIN - REF_DT and k == D_MODEL
    dt_block = REF_DT // PROJ_TN

    def src_row(j):
        return pl.multiple_of(j * PROJ_TN + jnp.where(j >= dt_block, SSD_HEADS, 0), SUBLANES)

    return pl.pallas_call(
        _w_in_prep_kernel,
        grid=(N_MAIN // PROJ_TN,),
        in_specs=[pl.BlockSpec((pl.Element(PROJ_TN), pl.Element(D_MODEL)),
                               lambda j: (src_row(j), 0)),
                  pl.BlockSpec((LANES, D_MODEL), lambda j: (REF_DT // LANES, 0))],
        out_specs=[pl.BlockSpec((D_MODEL, PROJ_TN), lambda j: (0, j)),
                   pl.BlockSpec((D_MODEL, LANES), lambda j: (0, 0))],
        out_shape=[jax.ShapeDtypeStruct((k, N_MAIN), BF16),
                   jax.ShapeDtypeStruct((k, LANES), BF16)],
        compiler_params=pltpu.CompilerParams(
            dimension_semantics=("arbitrary",), vmem_limit_bytes=32 * MIB),
        name="w_in_prep",
    )(w_t, w_t)


def _in_proj_kernel(x_ref, nw_ref, w_ref, wdt_ref, o_ref, dt_ref, h_ref):
    def norm_rows(i, carry):
        r = pl.ds(pl.multiple_of(i * NORM_ROWS, NORM_ROWS), NORM_ROWS)
        x = x_ref[r, :]
        h_ref[r, :] = (x * _rms_scale(x) * nw_ref[...]).astype(BF16)
        return carry
    lax.fori_loop(0, PROJ_TM // NORM_ROWS, norm_rows, 0)
    dt_ref[...] = jnp.dot(h_ref[...], wdt_ref[...], preferred_element_type=F32)
    for n in range(N_MAIN // PROJ_TN):
        c = slice(n * PROJ_TN, (n + 1) * PROJ_TN)
        o_ref[:, c] = jnp.dot(h_ref[...], w_ref[:, c],
                              preferred_element_type=F32).astype(BF16)


def _in_proj(x2, pre_norm_w, w_main, w_dt):
    m = x2.shape[0]
    resident = pl.Buffered(1)
    return pl.pallas_call(
        _in_proj_kernel,
        grid=(m // PROJ_TM,),
        in_specs=[pl.BlockSpec((PROJ_TM, D_MODEL), lambda i: (i, 0)),
                  pl.BlockSpec((1, D_MODEL), lambda i: (0, 0)),
                  pl.BlockSpec((D_MODEL, N_MAIN), lambda i: (0, 0), pipeline_mode=resident),
                  pl.BlockSpec((D_MODEL, LANES), lambda i: (0, 0), pipeline_mode=resident)],
        out_specs=[pl.BlockSpec((PROJ_TM, N_MAIN), lambda i: (i, 0)),
                   pl.BlockSpec((PROJ_TM, LANES), lambda i: (i, 0))],
        out_shape=[jax.ShapeDtypeStruct((m, N_MAIN), BF16),
                   jax.ShapeDtypeStruct((m, LANES), F32)],
        scratch_shapes=[pltpu.VMEM((PROJ_TM, D_MODEL), BF16)],
        compiler_params=pltpu.CompilerParams(
            dimension_semantics=("arbitrary",),
            vmem_limit_bytes=56 * MIB),
        name="in_proj",
    )(x2, pre_norm_w.reshape(1, D_MODEL), w_main, w_dt)


def _diff_attn_kernel(lam_ref, q_ref, k_ref, v_ref, g_ref, cos_ref, sin_ref, subw_ref,
                      o_ref, kr_ref, vt_ref, *, seq):
    qi = pl.program_id(2)
    heads = ATT_HEADS_PER_STEP

    @pl.when(qi == 0)
    def _():
        def prep_rows(i, carry):
            r = pl.ds(pl.multiple_of(i * ATT_TK, ATT_TK), ATT_TK)
            for g in range(heads):
                c = slice(g * LANES, (g + 1) * LANES)
                kr_ref[r, c] = _rope(k_ref[r, c].astype(F32), cos_ref[r, :],
                                     sin_ref[r, :]).astype(BF16)
                vt_ref[g, 0:DIFF_V_DIM, r] = v_ref[r, c].astype(F32).T.astype(BF16)
            return carry
        lax.fori_loop(0, seq // ATT_TK, prep_rows, 0)
        vt_ref[:, DIFF_V_DIM:, :] = jnp.ones((heads, ATT_SUM_ROWS, seq), BF16)

    rq = pl.ds(pl.multiple_of(qi * ATT_TQ, ATT_TQ), ATT_TQ)
    scale = DIFF_QK_DIM ** -0.5 * math.log2(math.e)
    lane = lax.broadcasted_iota(jnp.int32, (ATT_TQ, LANES), 1)
    q_maps = []
    for g in range(heads):
        qf = _rope(q_ref[:, g * LANES:(g + 1) * LANES].astype(F32), cos_ref[rq, :],
                   sin_ref[rq, :]) * scale
        q_maps.append((jnp.where(lane < DIFF_QK_DIM, qf, 0.0).astype(BF16),
                       jnp.where(lane >= DIFF_QK_DIM, qf, 0.0).astype(BF16)))

    lv = lam_ref[...]
    lam = (jnp.exp(jnp.sum(lv[0:1] * lv[1:2], axis=-1, keepdims=True))
           - jnp.exp(jnp.sum(lv[2:3] * lv[3:4], axis=-1, keepdims=True)) + LAMBDA_INIT)

    def run(n_tiles):
        diag = (n_tiles - 1) * ATT_TK
        half = ATT_TK // 2
        blocks = [(j * ATT_TK, (j + 1) * ATT_TK, 0, False) for j in range(n_tiles - 1)]
        blocks += [(diag, diag + half, 0, True), (diag + half, diag + ATT_TK, half, True)]
        items = [(blk, g, c) for blk in blocks for g in range(heads) for c in range(2)]

        def scores(blk, g, c):
            k0, k1, q0, masked = blk
            kt = kr_ref[k0:k1, g * LANES:(g + 1) * LANES]
            s = lax.dot_general(kt, q_maps[g][c][q0:, :], NT_DIMS, preferred_element_type=F32)
            if not masked:
                return s
            key = lax.broadcasted_iota(jnp.int32, s.shape, 0) + (k0 - diag)
            qry = lax.broadcasted_iota(jnp.int32, s.shape, 1) + q0
            return jnp.where(key <= qry, s, -jnp.inf)

        pending = {t: scores(*items[t]) for t in range(min(ATT_AHEAD, len(items)))}
        m, acc = {}, {}
        for t, (blk, g, c) in enumerate(items):
            k0, k1, q0, _ = blk
            i = 2 * g + c
            s = pending.pop(t)
            m_new = jnp.max(s, axis=0, keepdims=True)
            first = i not in m
            if not first:
                m_old = m[i][:, q0:]
                m_new = jnp.maximum(m_old, m_new)
            p = jnp.exp2(s - m_new).astype(BF16)
            if t + ATT_AHEAD < len(items):
                pending[t + ATT_AHEAD] = scores(*items[t + ATT_AHEAD])
            pv = jnp.dot(vt_ref[g, :, k0:k1], p, preferred_element_type=F32)
            if first:
                m[i], acc[i] = m_new, pv
            else:
                new_acc = jnp.exp2(m_old - m_new) * acc[i][:, q0:] + pv
                if q0:
                    m_new = jnp.concatenate([m[i][:, :q0], m_new], axis=1)
                    new_acc = jnp.concatenate([acc[i][:, :q0], new_acc], axis=1)
                m[i], acc[i] = m_new, new_acc

        for g in range(heads):
            c = slice(g * LANES, (g + 1) * LANES)
            a1, a2 = acc[2 * g], acc[2 * g + 1]
            l1 = a1[DIFF_V_DIM:DIFF_V_DIM + 1, :]
            l2 = a2[DIFF_V_DIM:DIFF_V_DIM + 1, :]
            o_t = (a1[0:DIFF_V_DIM, :] * (1.0 / l1)
                   - lam * (a2[0:DIFF_V_DIM, :] * (1.0 / l2)))
            o = o_t.T
            o = o * _rms_scale(o) * subw_ref[...] * (1.0 - LAMBDA_INIT)
            o_ref[:, c] = (o * _silu(g_ref[:, c].astype(F32))).astype(BF16)

    for n in range(seq // ATT_TQ):
        pl.when(qi == n)(functools.partial(run, n + 1))


def _diff_attention(proj, cos, sin, lam_vecs, subln_w, batch, seq):
    assert ATT_TQ == ATT_TK and seq % ATT_TQ == 0 and DIFF_V_DIM == LANES
    nq = seq // ATT_TQ
    heads = ATT_HEADS_PER_STEP
    width = heads * LANES
    kernel = functools.partial(_diff_attn_kernel, seq=seq)
    return pl.pallas_call(
        kernel,
        grid=(batch, DIFF_HEADS // heads, nq),
        in_specs=[
            pl.BlockSpec((4, DIFF_QK_DIM), lambda b, h, q: (0, 0)),
            pl.BlockSpec((ATT_TQ, width), lambda b, h, q: (b * nq + q, COL_Q // width + h)),
            pl.BlockSpec((seq, width), lambda b, h, q: (b, COL_K // width + h)),
            pl.BlockSpec((seq, width), lambda b, h, q: (b, COL_V // width + h)),
            pl.BlockSpec((ATT_TQ, width), lambda b, h, q: (b * nq + q, COL_G // width + h)),
            pl.BlockSpec((seq, LANES), lambda b, h, q: (b, 0)),
            pl.BlockSpec((seq, LANES), lambda b, h, q: (b, 0)),
            pl.BlockSpec((1, DIFF_V_DIM), lambda b, h, q: (0, 0)),
        ],
        out_specs=pl.BlockSpec((ATT_TQ, width), lambda b, h, q: (b * nq + q, h)),
        out_shape=jax.ShapeDtypeStruct((batch * seq, DIFF_WIDTH), BF16),
        scratch_shapes=[pltpu.VMEM((seq, width), BF16),
                        pltpu.VMEM((heads, DIFF_V_DIM + ATT_SUM_ROWS, seq), BF16)],
        compiler_params=pltpu.CompilerParams(
            dimension_semantics=("arbitrary", "arbitrary", "arbitrary"),
            vmem_limit_bytes=48 * MIB),
        name="diff_attn",
    )(lam_vecs, proj, proj, proj, proj, cos, sin, subln_w.reshape(1, DIFF_V_DIM))


CONV_HALO = SUBLANES


def _ssd_kernel(xs_ref, b_ref, c_ref, z_ref, dt_ref, cw_ref, cb_ref, dtb_ref, alog_ref,
                dsk_ref, nw_ref, o_ref, ext_ref, state_ref):
    L = SSD_CHUNK
    gs = SSD_GROUPS * SSD_STATE

    @pl.when(pl.program_id(1) == 0)
    def _():
        ext_ref[0:CONV_HALO, :] = jnp.zeros((CONV_HALO, SSD_CONV_DIM), F32)
        state_ref[...] = jnp.zeros(state_ref.shape, F32)

    ext_ref[CONV_HALO:CONV_HALO + L, 0:SSD_WIDTH] = xs_ref[...].astype(F32)
    ext_ref[CONV_HALO:CONV_HALO + L, SSD_WIDTH:SSD_WIDTH + gs] = b_ref[...].astype(F32)
    ext_ref[CONV_HALO:CONV_HALO + L, SSD_WIDTH + gs:SSD_CONV_DIM] = c_ref[...].astype(F32)
    conv = jnp.broadcast_to(cb_ref[...], (L, SSD_CONV_DIM))
    for k in range(SSD_CONV):
        start = CONV_HALO - (SSD_CONV - 1) + k
        conv = conv + ext_ref[start:start + L, :] * cw_ref[k:k + 1, :]
    ext_ref[0:CONV_HALO, :] = ext_ref[L:L + CONV_HALO, :]
    xbc = _silu(conv)
    xs = xbc[:, 0:SSD_WIDTH]
    bm = xbc[:, SSD_WIDTH:SSD_WIDTH + gs].astype(BF16)
    cm = xbc[:, SSD_WIDTH + gs:SSD_CONV_DIM].astype(BF16)

    dt_in = dt_ref[...] + dtb_ref[...]
    dt = jnp.maximum(dt_in, 0.0) + jnp.log1p(jnp.exp(-jnp.abs(dt_in)))
    a_dt = dt * (-jnp.exp(alog_ref[...]))
    row = lax.broadcasted_iota(jnp.int32, (L, L), 0)
    col = lax.broadcasted_iota(jnp.int32, (L, L), 1)
    lower = row >= col
    tril = lower.astype(BF16)
    hi = a_dt.astype(BF16)
    r1 = a_dt - hi.astype(F32)
    mid = r1.astype(BF16)
    lo = (r1 - mid.astype(F32)).astype(BF16)
    a_cs = (jnp.dot(tril, hi, preferred_element_type=F32)
            + jnp.dot(tril, mid, preferred_element_type=F32)
            + jnp.dot(tril, lo, preferred_element_type=F32))
    a_cs_t = a_cs.T
    a_end = a_cs[L - 1:L, :]
    decay_in = jnp.exp(a_cs)
    decay_to_end = jnp.exp(a_end - a_cs)
    chunk_decay = jnp.exp(a_end)

    lane = lax.broadcasted_iota(jnp.int32, (L, LANES), 1)
    left = lane < SSD_HEAD_DIM
    top = row < SSD_HEAD_DIM

    def pair_lanes(mat, p):
        return jnp.where(left, mat[:, 2 * p:2 * p + 1], mat[:, 2 * p + 1:2 * p + 2])

    ys = []
    pairs = SSD_HEADS // 2
    for p in range(pairs):
        g = (2 * p) // (SSD_HEADS // SSD_GROUPS)
        cg = cm[:, g * SSD_STATE:(g + 1) * SSD_STATE]
        bg = bm[:, g * SSD_STATE:(g + 1) * SSD_STATE]
        cb = lax.dot_general(cg, bg, NT_DIMS, preferred_element_type=F32)
        xs_p = xs[:, p * LANES:(p + 1) * LANES]
        x_dt = xs_p * pair_lanes(dt, p)
        x_dt_b = x_dt.astype(BF16)
        y_heads = []
        for hh in range(2):
            h = 2 * p + hh
            seg = a_cs[:, h:h + 1] - a_cs_t[h:h + 1, :]
            decay = jnp.exp(jnp.where(lower, seg, -jnp.inf))
            scores = (cb * decay).astype(BF16)
            y_heads.append(jnp.dot(scores, x_dt_b, preferred_element_type=F32))
        y_diag = jnp.where(left, y_heads[0], y_heads[1])
        st = state_ref[p]
        y_off = (lax.dot_general(cg, st.astype(BF16), NT_DIMS, preferred_element_type=F32)
                 * pair_lanes(decay_in, p))
        x_w = (x_dt * pair_lanes(decay_to_end, p)).astype(BF16)
        chunk_state = lax.dot_general(x_w, bg, TN_DIMS, preferred_element_type=F32)
        st_decay = jnp.where(top, chunk_decay[:, 2 * p:2 * p + 1],
                             chunk_decay[:, 2 * p + 1:2 * p + 2])
        state_ref[p] = st * st_decay + chunk_state
        y = y_diag + y_off + xs_p * dsk_ref[:, p * LANES:(p + 1) * LANES]
        ys.append(y * _silu(z_ref[:, p * LANES:(p + 1) * LANES].astype(F32)))

    grp_pairs = pairs // SSD_GROUPS
    grp_width = SSD_WIDTH // SSD_GROUPS
    for g in range(SSD_GROUPS):
        members = ys[g * grp_pairs:(g + 1) * grp_pairs]
        ssq = sum(jnp.sum(y * y, axis=-1, keepdims=True) for y in members)
        inv = lax.rsqrt(ssq / grp_width + NORM_EPS)
        for i, y in enumerate(members):
            c0 = (g * grp_pairs + i) * LANES
            o_ref[:, c0:c0 + LANES] = (y * inv * nw_ref[:, c0:c0 + LANES]).astype(BF16)


def _ssd(proj, dt_raw, conv_w, conv_b, dt_bias, a_log, d_skip, ssd_norm_w, batch, seq):
    nc = seq // SSD_CHUNK
    gs = SSD_GROUPS * SSD_STATE
    pad = LANES - SSD_HEADS
    row_map = lambda b, c: b * nc + c
    const = lambda b, c: (0, 0)
    return pl.pallas_call(
        _ssd_kernel,
        grid=(batch, nc),
        in_specs=[
            pl.BlockSpec((SSD_CHUNK, SSD_WIDTH), lambda b, c: (row_map(b, c), COL_XS // SSD_WIDTH)),
            pl.BlockSpec((SSD_CHUNK, gs), lambda b, c: (row_map(b, c), COL_B // gs)),
            pl.BlockSpec((SSD_CHUNK, gs), lambda b, c: (row_map(b, c), COL_C // gs)),
            pl.BlockSpec((SSD_CHUNK, SSD_WIDTH), lambda b, c: (row_map(b, c), COL_Z // SSD_WIDTH)),
            pl.BlockSpec((SSD_CHUNK, LANES), lambda b, c: (row_map(b, c), 0)),
            pl.BlockSpec((SSD_CONV, SSD_CONV_DIM), const),
            pl.BlockSpec((1, SSD_CONV_DIM), const),
            pl.BlockSpec((1, LANES), const),
            pl.BlockSpec((1, LANES), const),
            pl.BlockSpec((1, SSD_WIDTH), const),
            pl.BlockSpec((1, SSD_WIDTH), const),
        ],
        out_specs=pl.BlockSpec((SSD_CHUNK, SSD_WIDTH), lambda b, c: (row_map(b, c), 0)),
        out_shape=jax.ShapeDtypeStruct((batch * seq, SSD_WIDTH), BF16),
        scratch_shapes=[pltpu.VMEM((CONV_HALO + SSD_CHUNK, SSD_CONV_DIM), F32),
                        pltpu.VMEM((SSD_HEADS // 2, 2 * SSD_HEAD_DIM, SSD_STATE), F32)],
        compiler_params=pltpu.CompilerParams(
            dimension_semantics=("arbitrary", "arbitrary"),
            vmem_limit_bytes=32 * MIB),
        name="ssd",
    )(proj, proj, proj, proj, dt_raw, conv_w, conv_b.reshape(1, SSD_CONV_DIM),
      jnp.pad(dt_bias, (0, pad)).reshape(1, LANES), jnp.pad(a_log, (0, pad)).reshape(1, LANES),
      jnp.repeat(d_skip, SSD_HEAD_DIM).reshape(1, SSD_WIDTH), ssd_norm_w.reshape(1, SSD_WIDTH))


def _mem_kv_kernel(mem_ref, nw_ref, w_ref, o_ref):
    x = mem_ref[...]
    h = (x * _rms_scale(x) * nw_ref[...]).astype(BF16)
    o_ref[...] = jnp.dot(h, w_ref[...], preferred_element_type=F32).astype(BF16)


def _mem_kv(mem2, mem_norm_w, w_kv, batch, mem_len):
    return pl.pallas_call(
        _mem_kv_kernel,
        grid=(batch,),
        in_specs=[pl.BlockSpec((mem_len, D_MODEL), lambda b: (b, 0)),
                  pl.BlockSpec((1, D_MODEL), lambda b: (0, 0)),
                  pl.BlockSpec((D_MODEL, 2 * XATTN_WIDTH), lambda b: (0, 0))],
        out_specs=pl.BlockSpec((mem_len, 2 * XATTN_WIDTH), lambda b: (b, 0)),
        out_shape=jax.ShapeDtypeStruct((batch * mem_len, 2 * XATTN_WIDTH), BF16),
        compiler_params=pltpu.CompilerParams(
            dimension_semantics=("arbitrary",), vmem_limit_bytes=32 * MIB),
        name="mem_kv",
    )(mem2, mem_norm_w.reshape(1, D_MODEL), w_kv)


def _xattn_kernel(q_ref, g_ref, kv_ref, o_ref):
    scale = XATTN_HEAD_DIM ** -0.5
    for h in range(XATTN_HEADS):
        c = slice(h * XATTN_HEAD_DIM, (h + 1) * XATTN_HEAD_DIM)
        ck = slice(h * XATTN_HEAD_DIM, (h + 1) * XATTN_HEAD_DIM)
        cv = slice(XATTN_WIDTH + h * XATTN_HEAD_DIM, XATTN_WIDTH + (h + 1) * XATTN_HEAD_DIM)
        s = lax.dot_general(q_ref[:, c], kv_ref[:, ck], NT_DIMS,
                            preferred_element_type=F32) * scale
        p = jnp.exp(s - jnp.max(s, axis=-1, keepdims=True))
        inv_l = 1.0 / jnp.sum(p, axis=-1, keepdims=True)
        xo = jnp.dot(p.astype(BF16), kv_ref[:, cv], preferred_element_type=F32) * inv_l
        o_ref[:, c] = (xo * _silu(g_ref[:, c].astype(F32))).astype(BF16)


def _xattn(proj, mkv, batch, seq, mem_len):
    nq = seq // XATT_TQ
    return pl.pallas_call(
        _xattn_kernel,
        grid=(batch, nq),
        in_specs=[pl.BlockSpec((XATT_TQ, XATTN_WIDTH), lambda b, q: (b * nq + q, COL_XQ // XATTN_WIDTH)),
                  pl.BlockSpec((XATT_TQ, XATTN_WIDTH), lambda b, q: (b * nq + q, COL_XG // XATTN_WIDTH)),
                  pl.BlockSpec((mem_len, 2 * XATTN_WIDTH), lambda b, q: (b, 0))],
        out_specs=pl.BlockSpec((XATT_TQ, XATTN_WIDTH), lambda b, q: (b * nq + q, 0)),
        out_shape=jax.ShapeDtypeStruct((batch * seq, XATTN_WIDTH), BF16),
        compiler_params=pltpu.CompilerParams(
            dimension_semantics=("arbitrary", "arbitrary"), vmem_limit_bytes=32 * MIB),
        name="xattn",
    )(proj, proj, mkv)


def _out_proj_kernel(d_ref, s_ref, a_ref, w_ref, nw_ref, x_ref, o_ref):
    for i in range(OUT_TM // OUT_SUB):
        r = slice(i * OUT_SUB, (i + 1) * OUT_SUB)
        y = jnp.dot(d_ref[r, :], w_ref[0:DIFF_WIDTH, :], preferred_element_type=F32)
        y = y + jnp.dot(s_ref[r, :], w_ref[DIFF_WIDTH:DIFF_WIDTH + SSD_WIDTH, :],
                        preferred_element_type=F32)
        y = y + jnp.dot(a_ref[r, :], w_ref[DIFF_WIDTH + SSD_WIDTH:D_MIX, :],
                        preferred_element_type=F32)
        o_ref[r, :] = x_ref[r, :] + y * _rms_scale(y) * nw_ref[...]


def _out_proj(diff_out, ssd_out, xattn_out, w_out, post_norm_w, x2):
    m = x2.shape[0]
    return pl.pallas_call(
        _out_proj_kernel,
        grid=(m // OUT_TM,),
        in_specs=[pl.BlockSpec((OUT_TM, DIFF_WIDTH), lambda i: (i, 0)),
                  pl.BlockSpec((OUT_TM, SSD_WIDTH), lambda i: (i, 0)),
                  pl.BlockSpec((OUT_TM, XATTN_WIDTH), lambda i: (i, 0)),
                  pl.BlockSpec((D_MIX, D_MODEL), lambda i: (0, 0)),
                  pl.BlockSpec((1, D_MODEL), lambda i: (0, 0)),
                  pl.BlockSpec((OUT_TM, D_MODEL), lambda i: (i, 0))],
        out_specs=pl.BlockSpec((OUT_TM, D_MODEL), lambda i: (i, 0)),
        out_shape=jax.ShapeDtypeStruct((m, D_MODEL), F32),
        compiler_params=pltpu.CompilerParams(
            dimension_semantics=("arbitrary",), vmem_limit_bytes=48 * MIB),
        name="out_proj",
    )(diff_out, ssd_out, xattn_out, w_out, post_norm_w.reshape(1, D_MODEL), x2)


def kernel(x, mem, positions, pre_norm_w, w_in, lambda_q1, lambda_k1, lambda_q2, lambda_k2,
           diff_subln_w, conv_w, conv_b, dt_bias, a_log, d_skip, ssd_norm_w, mem_norm_w,
           w_mem_kv, w_out, post_norm_w):
    batch, seq, _ = x.shape
    mem_len = mem.shape[1]
    assert pre_norm_w.shape[0] == 1, "single-layer kernel"
    x2 = x.reshape(batch * seq, D_MODEL)

    w_main, w_dt = _w_in_prep(jnp.transpose(w_in[0]))
    lam_vecs = jnp.stack([lambda_q1[0], lambda_k1[0], lambda_q2[0], lambda_k2[0]]).astype(F32)

    cos, sin = _rope_tables(positions)
    proj, dt_raw = _in_proj(x2, pre_norm_w[0], w_main, w_dt)
    diff_out = _diff_attention(proj, cos, sin, lam_vecs, diff_subln_w[0], batch, seq)
    ssd_out = _ssd(proj, dt_raw, conv_w[0], conv_b[0], dt_bias[0], a_log[0], d_skip[0],
                   ssd_norm_w[0], batch, seq)
    mkv = _mem_kv(mem.reshape(batch * mem_len, D_MODEL), mem_norm_w[0],
                  w_mem_kv[0].astype(BF16), batch, mem_len)
    xattn_out = _xattn(proj, mkv, batch, seq, mem_len)
    out = _out_proj(diff_out, ssd_out, xattn_out, w_out[0].astype(BF16), post_norm_w[0], x2)
    return out.reshape(batch, seq, D_MODEL)
```

```python
import functools
import math

import jax
import jax.numpy as jnp
from jax import lax
from jax.experimental import pallas as pl
from jax.experimental.pallas import tpu as pltpu

F32 = jnp.float32
BF16 = jnp.bfloat16

D_MODEL = 2048
DIFF_HEADS = 8
DIFF_QK_DIM = 64
DIFF_V_DIM = 128
DIFF_WIDTH = DIFF_HEADS * DIFF_V_DIM
SSD_HEADS = 8
SSD_HEAD_DIM = 64
SSD_WIDTH = SSD_HEADS * SSD_HEAD_DIM
SSD_GROUPS = 2
SSD_STATE = 128
SSD_CONV = 4
SSD_CHUNK = 128
SSD_CONV_DIM = SSD_WIDTH + 2 * SSD_GROUPS * SSD_STATE
XATTN_HEADS = 4
XATTN_HEAD_DIM = 128
XATTN_WIDTH = XATTN_HEADS * XATTN_HEAD_DIM
D_MIX = DIFF_WIDTH + SSD_WIDTH + XATTN_WIDTH
ROPE_THETA = 10000.0
NORM_EPS = 1e-6
LAMBDA_INIT = 0.8 - 0.6 * math.exp(-0.3 * 0)

LANES = 128
SUBLANES = 8
MIB = 1024 * 1024

COL_Q = 0
COL_K = COL_Q + DIFF_WIDTH
COL_V = COL_K + DIFF_WIDTH
COL_G = COL_V + DIFF_WIDTH
COL_Z = COL_G + DIFF_WIDTH
COL_XS = COL_Z + SSD_WIDTH
COL_B = COL_XS + SSD_WIDTH
COL_C = COL_B + SSD_GROUPS * SSD_STATE
COL_XQ = COL_C + SSD_GROUPS * SSD_STATE
COL_XG = COL_XQ + XATTN_WIDTH
N_MAIN = COL_XG + XATTN_WIDTH
REF_DT = COL_XQ
REF_XQ = REF_DT + SSD_HEADS

PROJ_TM = 512
PROJ_TN = 512
NORM_ROWS = 256
ATT_TQ = 512
ATT_TK = 512
ATT_HEADS_PER_STEP = 4
ATT_SUM_ROWS = 16
ATT_AHEAD = 2
XATT_TQ = 512
OUT_TM = 512
OUT_SUB = 256
ROPE_TM = 1024
SSD_CHUNKS_PER_STEP = 4
DT_ROWS = 16

NT_DIMS = (((1,), (1,)), ((), ()))
TN_DIMS = (((0,), (0,)), ((), ()))


def _rms_scale(x):
    return lax.rsqrt(jnp.mean(x * x, axis=-1, keepdims=True) + NORM_EPS)


def _silu(x):
    return x * jax.nn.sigmoid(x)


def _rope_table_kernel(pos_ref, invf_ref, cos_ref, sin_ref):
    ang = pos_ref[...].astype(F32) * invf_ref[...]
    lane = lax.broadcasted_iota(jnp.int32, ang.shape, 1)
    first_half = (lane % DIFF_QK_DIM) < (DIFF_QK_DIM // 2)
    s = jnp.sin(ang)
    cos_ref[...] = jnp.cos(ang)
    sin_ref[...] = jnp.where(first_half, -s, s)


def _rope_tables(positions):
    n = positions.size
    inv_freq = 1.0 / (ROPE_THETA ** (jnp.arange(0, DIFF_QK_DIM, 2, dtype=F32) / DIFF_QK_DIM))
    invf = jnp.tile(inv_freq, LANES // inv_freq.shape[0]).reshape(1, LANES)
    pos = positions.reshape(n, 1)
    return pl.pallas_call(
        _rope_table_kernel,
        grid=(n // ROPE_TM,),
        in_specs=[pl.BlockSpec((ROPE_TM, 1), lambda i: (i, 0)),
                  pl.BlockSpec((1, LANES), lambda i: (0, 0))],
        out_specs=[pl.BlockSpec((ROPE_TM, LANES), lambda i: (i, 0)),
                   pl.BlockSpec((ROPE_TM, LANES), lambda i: (i, 0))],
        out_shape=[jax.ShapeDtypeStruct((n, LANES), F32)] * 2,
        name="rope_tables",
    )(pos, invf)


def _rope(x, cos, sin_signed):
    lane = lax.broadcasted_iota(jnp.int32, x.shape, 1)
    first_half = (lane % DIFF_QK_DIM) < (DIFF_QK_DIM // 2)
    half = DIFF_QK_DIM // 2
    rot = jnp.where(first_half, pltpu.roll(x, LANES - half, 1), pltpu.roll(x, half, 1))
    return x * cos + rot * sin_signed


def _w_in_prep_kernel(wt_ref, dt_rows_ref, wm_ref, wdt_ref):
    wm_ref[...] = wt_ref[...].T.astype(BF16)

    @pl.when(pl.program_id(0) == 0)
    def _():
        row = lax.broadcasted_iota(jnp.int32, (DT_ROWS, D_MODEL), 0)
        wdt_ref[...] = jnp.where(row < SSD_HEADS, dt_rows_ref[...], 0.0).astype(BF16)


def _w_in_prep(w_t):
    n, k = w_t.shape
    assert n == REF_XQ + N_MAIN - REF_DT and k == D_MODEL
    dt_block = REF_DT // PROJ_TN

    def src_row(j):
        return pl.multiple_of(j * PROJ_TN + jnp.where(j >= dt_block, SSD_HEADS, 0), SUBLANES)

    return pl.pallas_call(
        _w_in_prep_kernel,
        grid=(N_MAIN // PROJ_TN,),
        in_specs=[pl.BlockSpec((pl.Element(PROJ_TN), pl.Element(D_MODEL)),
                               lambda j: (src_row(j), 0)),
                  pl.BlockSpec((DT_ROWS, D_MODEL), lambda j: (REF_DT // DT_ROWS, 0))],
        out_specs=[pl.BlockSpec((D_MODEL, PROJ_TN), lambda j: (0, j)),
                   pl.BlockSpec((DT_ROWS, D_MODEL), lambda j: (0, 0))],
        out_shape=[jax.ShapeDtypeStruct((k, N_MAIN), BF16),
                   jax.ShapeDtypeStruct((DT_ROWS, k), BF16)],
        compiler_params=pltpu.CompilerParams(
            dimension_semantics=("arbitrary",), vmem_limit_bytes=32 * MIB),
        name="w_in_prep",
    )(w_t, w_t)


def _in_proj_kernel(x_ref, nw_ref, w_ref, wdt_ref, o_ref, dt_ref, h_ref):
    def norm_rows(i, carry):
        r = pl.ds(pl.multiple_of(i * NORM_ROWS, NORM_ROWS), NORM_ROWS)
        x = x_ref[r, :]
        h_ref[r, :] = (x * _rms_scale(x) * nw_ref[...]).astype(BF16)
        return carry
    lax.fori_loop(0, PROJ_TM // NORM_ROWS, norm_rows, 0)
    dt_t = lax.dot_general(wdt_ref[...], h_ref[...], NT_DIMS, preferred_element_type=F32)
    dt_ref[...] = dt_t[0:SSD_HEADS, :]
    for n in range(N_MAIN // PROJ_TN):
        c = slice(n * PROJ_TN, (n + 1) * PROJ_TN)
        o_ref[:, c] = jnp.dot(h_ref[...], w_ref[:, c],
                              preferred_element_type=F32).astype(BF16)


def _in_proj(x2, pre_norm_w, w_main, w_dt):
    m = x2.shape[0]
    resident = pl.Buffered(1)
    return pl.pallas_call(
        _in_proj_kernel,
        grid=(m // PROJ_TM,),
        in_specs=[pl.BlockSpec((PROJ_TM, D_MODEL), lambda i: (i, 0)),
                  pl.BlockSpec((1, D_MODEL), lambda i: (0, 0)),
                  pl.BlockSpec((D_MODEL, N_MAIN), lambda i: (0, 0), pipeline_mode=resident),
                  pl.BlockSpec((DT_ROWS, D_MODEL), lambda i: (0, 0), pipeline_mode=resident)],
        out_specs=[pl.BlockSpec((PROJ_TM, N_MAIN), lambda i: (i, 0)),
                   pl.BlockSpec((SSD_HEADS, PROJ_TM), lambda i: (0, i))],
        out_shape=[jax.ShapeDtypeStruct((m, N_MAIN), BF16),
                   jax.ShapeDtypeStruct((SSD_HEADS, m), F32)],
        scratch_shapes=[pltpu.VMEM((PROJ_TM, D_MODEL), BF16)],
        compiler_params=pltpu.CompilerParams(
            dimension_semantics=("arbitrary",),
            vmem_limit_bytes=56 * MIB),
        name="in_proj",
    )(x2, pre_norm_w.reshape(1, D_MODEL), w_main, w_dt)


def _diff_attn_kernel(lam_ref, q_ref, k_ref, v_ref, g_ref, cos_ref, sin_ref, subw_ref,
                      o_ref, kr_ref, vt_ref, *, seq):
    qi = pl.program_id(2)
    heads = ATT_HEADS_PER_STEP

    @pl.when(qi == 0)
    def _():
        def prep_rows(i, carry):
            r = pl.ds(pl.multiple_of(i * ATT_TK, ATT_TK), ATT_TK)
            for g in range(heads):
                c = slice(g * LANES, (g + 1) * LANES)
                kr_ref[r, c] = _rope(k_ref[r, c].astype(F32), cos_ref[r, :],
                                     sin_ref[r, :]).astype(BF16)
                vt_ref[g, 0:DIFF_V_DIM, r] = v_ref[r, c].astype(F32).T.astype(BF16)
            return carry
        lax.fori_loop(0, seq // ATT_TK, prep_rows, 0)
        vt_ref[:, DIFF_V_DIM:, :] = jnp.ones((heads, ATT_SUM_ROWS, seq), BF16)

    rq = pl.ds(pl.multiple_of(qi * ATT_TQ, ATT_TQ), ATT_TQ)
    scale = DIFF_QK_DIM ** -0.5 * math.log2(math.e)
    lane = lax.broadcasted_iota(jnp.int32, (ATT_TQ, LANES), 1)
    q_maps = []
    for g in range(heads):
        qf = _rope(q_ref[:, g * LANES:(g + 1) * LANES].astype(F32), cos_ref[rq, :],
                   sin_ref[rq, :]) * scale
        q_maps.append((jnp.where(lane < DIFF_QK_DIM, qf, 0.0).astype(BF16),
                       jnp.where(lane >= DIFF_QK_DIM, qf, 0.0).astype(BF16)))

    lv = lam_ref[...]
    lam = (jnp.exp(jnp.sum(lv[0:1] * lv[1:2], axis=-1, keepdims=True))
           - jnp.exp(jnp.sum(lv[2:3] * lv[3:4], axis=-1, keepdims=True)) + LAMBDA_INIT)

    def run(n_tiles):
        diag = (n_tiles - 1) * ATT_TK
        half = ATT_TK // 2
        blocks = [(j * ATT_TK, (j + 1) * ATT_TK, 0, False) for j in range(n_tiles - 1)]
        blocks += [(diag, diag + half, 0, True), (diag + half, diag + ATT_TK, half, True)]
        items = [(blk, g, c) for blk in blocks for g in range(heads) for c in range(2)]

        def scores(blk, g, c):
            k0, k1, q0, masked = blk
            kt = kr_ref[k0:k1, g * LANES:(g + 1) * LANES]
            s = lax.dot_general(kt, q_maps[g][c][q0:, :], NT_DIMS, preferred_element_type=F32)
            if not masked:
                return s
            key = lax.broadcasted_iota(jnp.int32, s.shape, 0) + (k0 - diag)
            qry = lax.broadcasted_iota(jnp.int32, s.shape, 1) + q0
            return jnp.where(key <= qry, s, -jnp.inf)

        pending = {t: scores(*items[t]) for t in range(min(ATT_AHEAD, len(items)))}
        m, acc = {}, {}
        for t, (blk, g, c) in enumerate(items):
            k0, k1, q0, _ = blk
            i = 2 * g + c
            s = pending.pop(t)
            m_new = jnp.max(s, axis=0, keepdims=True)
            first = i not in m
            if not first:
                m_old = m[i][:, q0:]
                m_new = jnp.maximum(m_old, m_new)
            p = jnp.exp2(s - m_new).astype(BF16)
            if t + ATT_AHEAD < len(items):
                pending[t + ATT_AHEAD] = scores(*items[t + ATT_AHEAD])
            pv = jnp.dot(vt_ref[g, :, k0:k1], p, preferred_element_type=F32)
            if first:
                m[i], acc[i] = m_new, pv
            else:
                new_acc = jnp.exp2(m_old - m_new) * acc[i][:, q0:] + pv
                if q0:
                    m_new = jnp.concatenate([m[i][:, :q0], m_new], axis=1)
                    new_acc = jnp.concatenate([acc[i][:, :q0], new_acc], axis=1)
                m[i], acc[i] = m_new, new_acc

        for g in range(heads):
            c = slice(g * LANES, (g + 1) * LANES)
            a1, a2 = acc[2 * g], acc[2 * g + 1]
            l1 = a1[DIFF_V_DIM:DIFF_V_DIM + 1, :]
            l2 = a2[DIFF_V_DIM:DIFF_V_DIM + 1, :]
            o_t = (a1[0:DIFF_V_DIM, :] * (1.0 / l1)
                   - lam * (a2[0:DIFF_V_DIM, :] * (1.0 / l2)))
            o = o_t.T
            o = o * _rms_scale(o) * subw_ref[...] * (1.0 - LAMBDA_INIT)
            o_ref[:, c] = (o * _silu(g_ref[:, c].astype(F32))).astype(BF16)

    for n in range(seq // ATT_TQ):
        pl.when(qi == n)(functools.partial(run, n + 1))


def _diff_attention(proj, cos, sin, lam_vecs, subln_w, batch, seq):
    assert ATT_TQ == ATT_TK and seq % ATT_TQ == 0 and DIFF_V_DIM == LANES
    nq = seq // ATT_TQ
    heads = ATT_HEADS_PER_STEP
    width = heads * LANES
    kernel = functools.partial(_diff_attn_kernel, seq=seq)
    return pl.pallas_call(
        kernel,
        grid=(batch, DIFF_HEADS // heads, nq),
        in_specs=[
            pl.BlockSpec((4, DIFF_QK_DIM), lambda b, h, q: (0, 0)),
            pl.BlockSpec((ATT_TQ, width), lambda b, h, q: (b * nq + q, COL_Q // width + h)),
            pl.BlockSpec((seq, width), lambda b, h, q: (b, COL_K // width + h)),
            pl.BlockSpec((seq, width), lambda b, h, q: (b, COL_V // width + h)),
            pl.BlockSpec((ATT_TQ, width), lambda b, h, q: (b * nq + q, COL_G // width + h)),
            pl.BlockSpec((seq, LANES), lambda b, h, q: (b, 0)),
            pl.BlockSpec((seq, LANES), lambda b, h, q: (b, 0)),
            pl.BlockSpec((1, DIFF_V_DIM), lambda b, h, q: (0, 0)),
        ],
        out_specs=pl.BlockSpec((ATT_TQ, width), lambda b, h, q: (b * nq + q, h)),
        out_shape=jax.ShapeDtypeStruct((batch * seq, DIFF_WIDTH), BF16),
        scratch_shapes=[pltpu.VMEM((seq, width), BF16),
                        pltpu.VMEM((heads, DIFF_V_DIM + ATT_SUM_ROWS, seq), BF16)],
        compiler_params=pltpu.CompilerParams(
            dimension_semantics=("arbitrary", "arbitrary", "arbitrary"),
            vmem_limit_bytes=48 * MIB),
        name="diff_attn",
    )(lam_vecs, proj, proj, proj, proj, cos, sin, subln_w.reshape(1, DIFF_V_DIM))


def _ssd_kernel(xs_ref, b_ref, c_ref, z_ref, dt_ref, shift_ref, triu_ref, sel_ref, cw_ref,
                cb_ref, dtb_ref, alog_ref, dsk_ref, nw_ref, o_ref, xext_ref, state_ref):
    L = SSD_CHUNK
    T = SSD_CHUNKS_PER_STEP
    gs = SSD_GROUPS * SSD_STATE
    pairs = SSD_HEADS // 2

    @pl.when(pl.program_id(1) == 0)
    def _():
        xext_ref[0:L, :] = jnp.zeros((L, SSD_CONV_DIM), BF16)
        state_ref[...] = jnp.zeros(state_ref.shape, F32)

    xext_ref[L:(T + 1) * L, 0:SSD_WIDTH] = xs_ref[...]
    xext_ref[L:(T + 1) * L, SSD_WIDTH:SSD_WIDTH + gs] = b_ref[...]
    xext_ref[L:(T + 1) * L, SSD_WIDTH + gs:SSD_CONV_DIM] = c_ref[...]

    carry = {"state": [state_ref[p] for p in range(pairs)], "cols_split": [None] * T}
    programs = [_ssd_chunk(c, carry, xext_ref, z_ref, dt_ref, shift_ref, triu_ref, cw_ref,
                           cb_ref, dtb_ref, alog_ref, dsk_ref, nw_ref, o_ref)
                for c in range(T)]

    def stage():
        for prog in programs:
            next(prog)

    stage()
    stage()
    carry["spread"] = jnp.dot(jnp.concatenate(carry["cols_split"], axis=0), sel_ref[...],
                              preferred_element_type=F32)
    stage()
    stage()
    stage()
    for p in range(pairs):
        state_ref[p] = carry["state"][p]
    xext_ref[0:L, :] = xext_ref[T * L:(T + 1) * L, :]


SSD_PAIR_KINDS = (0, 2, 3)


def _ssd_chunk(c, carry, xext_ref, z_ref, dt_ref, shift_ref, triu_ref, cw_ref,
               cb_ref, dtb_ref, alog_ref, dsk_ref, nw_ref, o_ref):
    L = SSD_CHUNK
    H = SSD_HEADS
    gs = SSD_GROUPS * SSD_STATE
    rows_c = slice(c * L, (c + 1) * L)

    delayed = jnp.dot(shift_ref[...], xext_ref[c * L:(c + 2) * L, :],
                      preferred_element_type=F32)
    conv = (cb_ref[...] + xext_ref[(c + 1) * L:(c + 2) * L, :].astype(F32)
            * cw_ref[SSD_CONV - 1:SSD_CONV, :])
    for k in range(SSD_CONV - 1):
        conv = conv + delayed[k * L:(k + 1) * L, :] * cw_ref[k:k + 1, :]
    xbc = _silu(conv)
    xs = xbc[:, 0:SSD_WIDTH]
    bm = xbc[:, SSD_WIDTH:SSD_WIDTH + gs].astype(BF16)
    cm = xbc[:, SSD_WIDTH + gs:SSD_CONV_DIM].astype(BF16)
    yield

    dt_in = dt_ref[:, rows_c] + dtb_ref[...]
    dt = jnp.maximum(dt_in, 0.0) + jnp.log1p(jnp.exp(-jnp.abs(dt_in)))
    a_dt = dt * (-jnp.exp(alog_ref[...]))
    hi = a_dt.astype(BF16)
    r1 = a_dt - hi.astype(F32)
    mid = r1.astype(BF16)
    lo = (r1 - mid.astype(F32)).astype(BF16)
    triu = triu_ref[...]
    a_cs = (jnp.dot(hi, triu, preferred_element_type=F32)
            + jnp.dot(mid, triu, preferred_element_type=F32)
            + jnp.dot(lo, triu, preferred_element_type=F32))
    a_end = a_cs[:, L - 1:L]
    chunk_decay = jnp.exp(a_end)
    rows = jnp.concatenate([dt, a_cs, jnp.exp(a_cs), jnp.exp(a_end - a_cs),
                            jnp.zeros((L - 4 * H, L), F32)], axis=0)
    cols = rows.T
    c_hi = cols.astype(BF16)
    c_r1 = cols - c_hi.astype(F32)
    c_mid = c_r1.astype(BF16)
    c_lo = (c_r1 - c_mid.astype(F32)).astype(BF16)
    carry["cols_split"][c] = jnp.concatenate([c_hi, c_mid, c_lo], axis=1)
    yield
    spread = carry["spread"][rows_c, :]

    row = lax.broadcasted_iota(jnp.int32, (L, L), 0)
    col = lax.broadcasted_iota(jnp.int32, (L, L), 1)
    lower = row >= col
    lane = lax.broadcasted_iota(jnp.int32, (L, LANES), 1)
    left = lane < SSD_HEAD_DIM
    top = row < SSD_HEAD_DIM
    pairs = SSD_HEADS // 2

    def pair_lanes(slot, p):
        i = slot * pairs + p
        return spread[:, i * LANES:(i + 1) * LANES]

    def head_lanes(h):
        i = len(SSD_PAIR_KINDS) * pairs + h
        return spread[:, i * LANES:(i + 1) * LANES]

    group_cb = []
    for g in range(SSD_GROUPS):
        group_cb.append(lax.dot_general(cm[:, g * SSD_STATE:(g + 1) * SSD_STATE],
                                        bm[:, g * SSD_STATE:(g + 1) * SSD_STATE],
                                        NT_DIMS, preferred_element_type=F32))
    yield

    group_of = lambda p: (2 * p) // (SSD_HEADS // SSD_GROUPS)
    y_diag, chunk_state = [], []
    for p in range(pairs):
        bg = bm[:, group_of(p) * SSD_STATE:(group_of(p) + 1) * SSD_STATE]
        x_dt = xs[:, p * LANES:(p + 1) * LANES] * pair_lanes(0, p)
        x_dt_b = x_dt.astype(BF16)
        y_heads = []
        for hh in range(2):
            h = 2 * p + hh
            seg = head_lanes(h) - a_cs[h:h + 1, :]
            decay = jnp.exp(jnp.where(lower, seg, -jnp.inf))
            scores = (group_cb[group_of(p)] * decay).astype(BF16)
            y_heads.append(jnp.dot(scores, x_dt_b, preferred_element_type=F32))
        y_diag.append(jnp.where(left, y_heads[0], y_heads[1]))
        x_w = (x_dt * pair_lanes(2, p)).astype(BF16)
        chunk_state.append(lax.dot_general(x_w, bg, TN_DIMS, preferred_element_type=F32))
    yield

    ys, new_state = [], []
    for p in range(pairs):
        cg = cm[:, group_of(p) * SSD_STATE:(group_of(p) + 1) * SSD_STATE]
        st = carry["state"][p]
        y_off = (lax.dot_general(cg, st.astype(BF16), NT_DIMS, preferred_element_type=F32)
                 * pair_lanes(1, p))
        st_decay = jnp.where(top, chunk_decay[2 * p:2 * p + 1, :],
                             chunk_decay[2 * p + 1:2 * p + 2, :])
        new_state.append(st * st_decay + chunk_state[p])
        y = (y_diag[p] + y_off
             + xs[:, p * LANES:(p + 1) * LANES] * dsk_ref[:, p * LANES:(p + 1) * LANES])
        ys.append(y * _silu(z_ref[rows_c, p * LANES:(p + 1) * LANES].astype(F32)))
    carry["state"] = new_state

    grp_pairs = pairs // SSD_GROUPS
    grp_width = SSD_WIDTH // SSD_GROUPS
    for g in range(SSD_GROUPS):
        members = ys[g * grp_pairs:(g + 1) * grp_pairs]
        ssq = sum(jnp.sum(y * y, axis=-1, keepdims=True) for y in members)
        inv = lax.rsqrt(ssq / grp_width + NORM_EPS)
        for i, y in enumerate(members):
            c0 = (g * grp_pairs + i) * LANES
            o_ref[rows_c, c0:c0 + LANES] = (y * inv * nw_ref[:, c0:c0 + LANES]).astype(BF16)
    yield


def _ssd(proj, dt_raw_t, conv_w, conv_b, dt_bias, a_log, d_skip, ssd_norm_w, batch, seq):
    gs = SSD_GROUPS * SSD_STATE
    L = SSD_CHUNK
    rows = SSD_CHUNKS_PER_STEP * L
    nc = seq // rows
    row_map = lambda b, c: b * nc + c
    const = lambda b, c: (0, 0)
    t = jnp.arange(L)[:, None]
    src = jnp.arange(2 * L)[None, :]
    shift = jnp.concatenate([(src == L + t - (SSD_CONV - 1 - k)) for k in range(SSD_CONV - 1)],
                            axis=0).astype(BF16)
    triu = (jnp.arange(L)[:, None] <= jnp.arange(L)[None, :]).astype(BF16)
    lane = jnp.arange(LANES)
    sel_blocks = []
    for kind in SSD_PAIR_KINDS:
        for p in range(SSD_HEADS // 2):
            src_col = kind * SSD_HEADS + 2 * p + (lane >= SSD_HEAD_DIM)
            sel_blocks.append(jnp.arange(L)[:, None] == src_col[None, :])
    for h in range(SSD_HEADS):
        sel_blocks.append(jnp.broadcast_to(jnp.arange(L)[:, None] == SSD_HEADS + h, (L, LANES)))
    sel = jnp.tile(jnp.concatenate(sel_blocks, axis=1), (3, 1)).astype(BF16)
    per_head = lambda v: jnp.broadcast_to(v.astype(F32)[:, None], (SSD_HEADS, L))
    return pl.pallas_call(
        _ssd_kernel,
        grid=(batch, nc),
        in_specs=[
            pl.BlockSpec((rows, SSD_WIDTH), lambda b, c: (row_map(b, c), COL_XS // SSD_WIDTH)),
            pl.BlockSpec((rows, gs), lambda b, c: (row_map(b, c), COL_B // gs)),
            pl.BlockSpec((rows, gs), lambda b, c: (row_map(b, c), COL_C // gs)),
            pl.BlockSpec((rows, SSD_WIDTH), lambda b, c: (row_map(b, c), COL_Z // SSD_WIDTH)),
            pl.BlockSpec((SSD_HEADS, rows), lambda b, c: (0, row_map(b, c))),
            pl.BlockSpec(((SSD_CONV - 1) * L, 2 * L), const),
            pl.BlockSpec((L, L), const),
            pl.BlockSpec(sel.shape, const),
            pl.BlockSpec((SSD_CONV, SSD_CONV_DIM), const),
            pl.BlockSpec((1, SSD_CONV_DIM), const),
            pl.BlockSpec((SSD_HEADS, L), const),
            pl.BlockSpec((SSD_HEADS, L), const),
            pl.BlockSpec((1, SSD_WIDTH), const),
            pl.BlockSpec((1, SSD_WIDTH), const),
        ],
        out_specs=pl.BlockSpec((rows, SSD_WIDTH), lambda b, c: (row_map(b, c), 0)),
        out_shape=jax.ShapeDtypeStruct((batch * seq, SSD_WIDTH), BF16),
        scratch_shapes=[pltpu.VMEM((L + rows, SSD_CONV_DIM), BF16),
                        pltpu.VMEM((SSD_HEADS // 2, 2 * SSD_HEAD_DIM, SSD_STATE), F32)],
        compiler_params=pltpu.CompilerParams(
            dimension_semantics=("arbitrary", "arbitrary"),
            vmem_limit_bytes=32 * MIB),
        name="ssd",
    )(proj, proj, proj, proj, dt_raw_t, shift, triu, sel, conv_w, conv_b.reshape(1, SSD_CONV_DIM),
      per_head(dt_bias), per_head(a_log),
      jnp.repeat(d_skip, SSD_HEAD_DIM).reshape(1, SSD_WIDTH), ssd_norm_w.reshape(1, SSD_WIDTH))


def _mem_kv_kernel(mem_ref, nw_ref, w_ref, o_ref):
    x = mem_ref[...]
    h = (x * _rms_scale(x) * nw_ref[...]).astype(BF16)
    o_ref[...] = jnp.dot(h, w_ref[...], preferred_element_type=F32).astype(BF16)


def _mem_kv(mem2, mem_norm_w, w_kv, batch, mem_len):
    return pl.pallas_call(
        _mem_kv_kernel,
        grid=(batch,),
        in_specs=[pl.BlockSpec((mem_len, D_MODEL), lambda b: (b, 0)),
                  pl.BlockSpec((1, D_MODEL), lambda b: (0, 0)),
                  pl.BlockSpec((D_MODEL, 2 * XATTN_WIDTH), lambda b: (0, 0))],
        out_specs=pl.BlockSpec((mem_len, 2 * XATTN_WIDTH), lambda b: (b, 0)),
        out_shape=jax.ShapeDtypeStruct((batch * mem_len, 2 * XATTN_WIDTH), BF16),
        compiler_params=pltpu.CompilerParams(
            dimension_semantics=("arbitrary",), vmem_limit_bytes=32 * MIB),
        name="mem_kv",
    )(mem2, mem_norm_w.reshape(1, D_MODEL), w_kv)


def _xattn_kernel(q_ref, g_ref, kv_ref, o_ref):
    scale = XATTN_HEAD_DIM ** -0.5
    for h in range(XATTN_HEADS):
        c = slice(h * XATTN_HEAD_DIM, (h + 1) * XATTN_HEAD_DIM)
        ck = slice(h * XATTN_HEAD_DIM, (h + 1) * XATTN_HEAD_DIM)
        cv = slice(XATTN_WIDTH + h * XATTN_HEAD_DIM, XATTN_WIDTH + (h + 1) * XATTN_HEAD_DIM)
        s = lax.dot_general(q_ref[:, c], kv_ref[:, ck], NT_DIMS,
                            preferred_element_type=F32) * scale
        p = jnp.exp(s - jnp.max(s, axis=-1, keepdims=True))
        inv_l = 1.0 / jnp.sum(p, axis=-1, keepdims=True)
        xo = jnp.dot(p.astype(BF16), kv_ref[:, cv], preferred_element_type=F32) * inv_l
        o_ref[:, c] = (xo * _silu(g_ref[:, c].astype(F32))).astype(BF16)


def _xattn(proj, mkv, batch, seq, mem_len):
    nq = seq // XATT_TQ
    return pl.pallas_call(
        _xattn_kernel,
        grid=(batch, nq),
        in_specs=[pl.BlockSpec((XATT_TQ, XATTN_WIDTH), lambda b, q: (b * nq + q, COL_XQ // XATTN_WIDTH)),
                  pl.BlockSpec((XATT_TQ, XATTN_WIDTH), lambda b, q: (b * nq + q, COL_XG // XATTN_WIDTH)),
                  pl.BlockSpec((mem_len, 2 * XATTN_WIDTH), lambda b, q: (b, 0))],
        out_specs=pl.BlockSpec((XATT_TQ, XATTN_WIDTH), lambda b, q: (b * nq + q, 0)),
        out_shape=jax.ShapeDtypeStruct((batch * seq, XATTN_WIDTH), BF16),
        compiler_params=pltpu.CompilerParams(
            dimension_semantics=("arbitrary", "arbitrary"), vmem_limit_bytes=32 * MIB),
        name="xattn",
    )(proj, proj, mkv)


def _out_proj_kernel(d_ref, s_ref, a_ref, w_ref, nw_ref, x_ref, o_ref):
    for i in range(OUT_TM // OUT_SUB):
        r = slice(i * OUT_SUB, (i + 1) * OUT_SUB)
        y = jnp.dot(d_ref[r, :], w_ref[0:DIFF_WIDTH, :], preferred_element_type=F32)
        y = y + jnp.dot(s_ref[r, :], w_ref[DIFF_WIDTH:DIFF_WIDTH + SSD_WIDTH, :],
                        preferred_element_type=F32)
        y = y + jnp.dot(a_ref[r, :], w_ref[DIFF_WIDTH + SSD_WIDTH:D_MIX, :],
                        preferred_element_type=F32)
        o_ref[r, :] = x_ref[r, :] + y * _rms_scale(y) * nw_ref[...]


def _out_proj(diff_out, ssd_out, xattn_out, w_out, post_norm_w, x2):
    m = x2.shape[0]
    return pl.pallas_call(
        _out_proj_kernel,
        grid=(m // OUT_TM,),
        in_specs=[pl.BlockSpec((OUT_TM, DIFF_WIDTH), lambda i: (i, 0)),
                  pl.BlockSpec((OUT_TM, SSD_WIDTH), lambda i: (i, 0)),
                  pl.BlockSpec((OUT_TM, XATTN_WIDTH), lambda i: (i, 0)),
                  pl.BlockSpec((D_MIX, D_MODEL), lambda i: (0, 0)),
                  pl.BlockSpec((1, D_MODEL), lambda i: (0, 0)),
                  pl.BlockSpec((OUT_TM, D_MODEL), lambda i: (i, 0))],
        out_specs=pl.BlockSpec((OUT_TM, D_MODEL), lambda i: (i, 0)),
        out_shape=jax.ShapeDtypeStruct((m, D_MODEL), F32),
        compiler_params=pltpu.CompilerParams(
            dimension_semantics=("arbitrary",), vmem_limit_bytes=48 * MIB),
        name="out_proj",
    )(diff_out, ssd_out, xattn_out, w_out, post_norm_w.reshape(1, D_MODEL), x2)


def kernel(x, mem, positions, pre_norm_w, w_in, lambda_q1, lambda_k1, lambda_q2, lambda_k2,
           diff_subln_w, conv_w, conv_b, dt_bias, a_log, d_skip, ssd_norm_w, mem_norm_w,
           w_mem_kv, w_out, post_norm_w):
    batch, seq, _ = x.shape
    mem_len = mem.shape[1]
    assert pre_norm_w.shape[0] == 1, "single-layer kernel"
    x2 = x.reshape(batch * seq, D_MODEL)

    w_main, w_dt = _w_in_prep(jnp.transpose(w_in[0]))
    lam_vecs = jnp.stack([lambda_q1[0], lambda_k1[0], lambda_q2[0], lambda_k2[0]]).astype(F32)

    cos, sin = _rope_tables(positions)
    proj, dt_raw_t = _in_proj(x2, pre_norm_w[0], w_main, w_dt)
    diff_out = _diff_attention(proj, cos, sin, lam_vecs, diff_subln_w[0], batch, seq)
    ssd_out = _ssd(proj, dt_raw_t, conv_w[0], conv_b[0], dt_bias[0], a_log[0], d_skip[0],
                   ssd_norm_w[0], batch, seq)
    mkv = _mem_kv(mem.reshape(batch * mem_len, D_MODEL), mem_norm_w[0],
                  w_mem_kv[0].astype(BF16), batch, mem_len)
    xattn_out = _xattn(proj, mkv, batch, seq, mem_len)
    out = _out_proj(diff_out, ssd_out, xattn_out, w_out[0].astype(BF16), post_norm_w[0], x2)
    return out.reshape(batch, seq, D_MODEL)
```

```python
import functools
import math

import jax
import jax.numpy as jnp
from jax import lax
from jax.experimental import pallas as pl
from jax.experimental.pallas import tpu as pltpu

F32 = jnp.float32
BF16 = jnp.bfloat16

D_MODEL = 2048
DIFF_HEADS = 8
DIFF_QK_DIM = 64
DIFF_V_DIM = 128
DIFF_WIDTH = DIFF_HEADS * DIFF_V_DIM
SSD_HEADS = 8
SSD_HEAD_DIM = 64
SSD_WIDTH = SSD_HEADS * SSD_HEAD_DIM
SSD_GROUPS = 2
SSD_STATE = 128
SSD_CONV = 4
SSD_CHUNK = 128
SSD_CONV_DIM = SSD_WIDTH + 2 * SSD_GROUPS * SSD_STATE
XATTN_HEADS = 4
XATTN_HEAD_DIM = 128
XATTN_WIDTH = XATTN_HEADS * XATTN_HEAD_DIM
D_MIX = DIFF_WIDTH + SSD_WIDTH + XATTN_WIDTH
ROPE_THETA = 10000.0
NORM_EPS = 1e-6
LAMBDA_INIT = 0.8 - 0.6 * math.exp(-0.3 * 0)

LANES = 128
SUBLANES = 8
MIB = 1024 * 1024

COL_Q = 0
COL_K = COL_Q + DIFF_WIDTH
COL_G = COL_K + DIFF_WIDTH
COL_Z = COL_G + DIFF_WIDTH
COL_XS = COL_Z + SSD_WIDTH
COL_B = COL_XS + SSD_WIDTH
COL_C = COL_B + SSD_GROUPS * SSD_STATE
COL_XQ = COL_C + SSD_GROUPS * SSD_STATE
COL_XG = COL_XQ + XATTN_WIDTH
N_MAIN = COL_XG + XATTN_WIDTH
REF_V = 2 * DIFF_WIDTH
REF_G = REF_V + DIFF_WIDTH
REF_DT = REF_G + DIFF_WIDTH + SSD_WIDTH + SSD_CONV_DIM
REF_XQ = REF_DT + SSD_HEADS
D_IN = REF_XQ + 2 * XATTN_WIDTH

PROJ_TM = 512
PROJ_TN = 512
NORM_ROWS = 256
ATT_TQ = 512
ATT_TK = 512
ATT_HEADS_PER_STEP = 4
ATT_SUM_ROWS = 16
ATT_AHEAD = 2
XATT_TQ = 512
OUT_TM = 512
OUT_SUB = 256
ROPE_TM = 1024
SSD_CHUNKS_PER_STEP = 4
DT_ROWS = 16

NT_DIMS = (((1,), (1,)), ((), ()))
TN_DIMS = (((0,), (0,)), ((), ()))


def _rms_scale(x):
    return lax.rsqrt(jnp.mean(x * x, axis=-1, keepdims=True) + NORM_EPS)


def _silu(x):
    return x * jax.nn.sigmoid(x)


def _rope_table_kernel(pos_ref, invf_ref, cos_ref, sin_ref):
    ang = pos_ref[...].astype(F32) * invf_ref[...]
    lane = lax.broadcasted_iota(jnp.int32, ang.shape, 1)
    first_half = (lane % DIFF_QK_DIM) < (DIFF_QK_DIM // 2)
    s = jnp.sin(ang)
    cos_ref[...] = jnp.cos(ang)
    sin_ref[...] = jnp.where(first_half, -s, s)


def _rope_tables(positions):
    n = positions.size
    inv_freq = 1.0 / (ROPE_THETA ** (jnp.arange(0, DIFF_QK_DIM, 2, dtype=F32) / DIFF_QK_DIM))
    invf = jnp.tile(inv_freq, LANES // inv_freq.shape[0]).reshape(1, LANES)
    pos = positions.reshape(n, 1)
    return pl.pallas_call(
        _rope_table_kernel,
        grid=(n // ROPE_TM,),
        in_specs=[pl.BlockSpec((ROPE_TM, 1), lambda i: (i, 0)),
                  pl.BlockSpec((1, LANES), lambda i: (0, 0))],
        out_specs=[pl.BlockSpec((ROPE_TM, LANES), lambda i: (i, 0)),
                   pl.BlockSpec((ROPE_TM, LANES), lambda i: (i, 0))],
        out_shape=[jax.ShapeDtypeStruct((n, LANES), F32)] * 2,
        name="rope_tables",
    )(pos, invf)


def _rope(x, cos, sin_signed):
    lane = lax.broadcasted_iota(jnp.int32, x.shape, 1)
    first_half = (lane % DIFF_QK_DIM) < (DIFF_QK_DIM // 2)
    half = DIFF_QK_DIM // 2
    rot = jnp.where(first_half, pltpu.roll(x, LANES - half, 1), pltpu.roll(x, half, 1))
    return x * cos + rot * sin_signed


def _w_in_prep_kernel(wt_ref, dt_rows_ref, wm_ref, wv_ref, wdt_ref):
    j = pl.program_id(0)

    @pl.when(j < N_MAIN // PROJ_TN)
    def _():
        wm_ref[...] = wt_ref[...].T.astype(BF16)

    @pl.when(j >= N_MAIN // PROJ_TN)
    def _():
        wv_ref[...] = wt_ref[...].astype(BF16)

    @pl.when(j == 0)
    def _():
        row = lax.broadcasted_iota(jnp.int32, (DT_ROWS, D_MODEL), 0)
        wdt_ref[...] = jnp.where(row < SSD_HEADS, dt_rows_ref[...], 0.0).astype(BF16)


def _w_in_prep(w_t):
    n, k = w_t.shape
    assert n == D_IN and k == D_MODEL
    main_blocks = N_MAIN // PROJ_TN
    v_blocks = DIFF_WIDTH // PROJ_TN

    def src_row(j):
        main = jnp.where(j < REF_V // PROJ_TN, j * PROJ_TN,
                         jnp.where(j < (REF_DT - DIFF_WIDTH) // PROJ_TN,
                                   j * PROJ_TN + DIFF_WIDTH,
                                   j * PROJ_TN + DIFF_WIDTH + SSD_HEADS))
        return pl.multiple_of(jnp.where(j < main_blocks, main,
                                        REF_V + (j - main_blocks) * PROJ_TN), SUBLANES)

    return pl.pallas_call(
        _w_in_prep_kernel,
        grid=(main_blocks + v_blocks,),
        in_specs=[pl.BlockSpec((pl.Element(PROJ_TN), pl.Element(D_MODEL)),
                               lambda j: (src_row(j), 0)),
                  pl.BlockSpec((DT_ROWS, D_MODEL), lambda j: (REF_DT // DT_ROWS, 0))],
        out_specs=[pl.BlockSpec((D_MODEL, PROJ_TN),
                                lambda j: (0, jnp.minimum(j, main_blocks - 1))),
                   pl.BlockSpec((PROJ_TN, D_MODEL),
                                lambda j: (jnp.maximum(j - main_blocks, 0), 0)),
                   pl.BlockSpec((DT_ROWS, D_MODEL), lambda j: (0, 0))],
        out_shape=[jax.ShapeDtypeStruct((k, N_MAIN), BF16),
                   jax.ShapeDtypeStruct((DIFF_WIDTH, k), BF16),
                   jax.ShapeDtypeStruct((DT_ROWS, k), BF16)],
        compiler_params=pltpu.CompilerParams(
            dimension_semantics=("arbitrary",), vmem_limit_bytes=32 * MIB),
        name="w_in_prep",
    )(w_t, w_t)


def _in_proj_kernel(x_ref, nw_ref, w_ref, wv_ref, wdt_ref, cos_ref, sin_ref,
                    o_ref, vt_ref, dt_ref, h_ref):
    def norm_rows(i, carry):
        r = pl.ds(pl.multiple_of(i * NORM_ROWS, NORM_ROWS), NORM_ROWS)
        x = x_ref[r, :]
        h_ref[r, :] = (x * _rms_scale(x) * nw_ref[...]).astype(BF16)
        return carry
    lax.fori_loop(0, PROJ_TM // NORM_ROWS, norm_rows, 0)
    dt_t = lax.dot_general(wdt_ref[...], h_ref[...], NT_DIMS, preferred_element_type=F32)
    dt_ref[...] = dt_t[0:SSD_HEADS, :]
    vt_ref[...] = lax.dot_general(wv_ref[...], h_ref[...], NT_DIMS,
                                  preferred_element_type=F32).astype(BF16)
    q_scale = DIFF_QK_DIM ** -0.5 * math.log2(math.e)
    for n in range(N_MAIN // PROJ_TN):
        c0 = n * PROJ_TN
        y = jnp.dot(h_ref[...], w_ref[:, c0:c0 + PROJ_TN], preferred_element_type=F32)
        if c0 < COL_G:
            scale = q_scale if c0 < COL_K else 1.0
            for hd in range(PROJ_TN // LANES):
                y_h = _rope(y[:, hd * LANES:(hd + 1) * LANES], cos_ref[...], sin_ref[...])
                o_ref[:, c0 + hd * LANES:c0 + (hd + 1) * LANES] = (y_h * scale).astype(BF16)
        else:
            o_ref[:, c0:c0 + PROJ_TN] = y.astype(BF16)


def _in_proj(x2, pre_norm_w, w_main, w_v, w_dt, cos, sin):
    m = x2.shape[0]
    resident = pl.Buffered(1)
    return pl.pallas_call(
        _in_proj_kernel,
        grid=(m // PROJ_TM,),
        in_specs=[pl.BlockSpec((PROJ_TM, D_MODEL), lambda i: (i, 0)),
                  pl.BlockSpec((1, D_MODEL), lambda i: (0, 0)),
                  pl.BlockSpec((D_MODEL, N_MAIN), lambda i: (0, 0), pipeline_mode=resident),
                  pl.BlockSpec((DIFF_WIDTH, D_MODEL), lambda i: (0, 0), pipeline_mode=resident),
                  pl.BlockSpec((DT_ROWS, D_MODEL), lambda i: (0, 0), pipeline_mode=resident),
                  pl.BlockSpec((PROJ_TM, LANES), lambda i: (i, 0)),
                  pl.BlockSpec((PROJ_TM, LANES), lambda i: (i, 0))],
        out_specs=[pl.BlockSpec((PROJ_TM, N_MAIN), lambda i: (i, 0)),
                   pl.BlockSpec((DIFF_WIDTH, PROJ_TM), lambda i: (0, i)),
                   pl.BlockSpec((SSD_HEADS, PROJ_TM), lambda i: (0, i))],
        out_shape=[jax.ShapeDtypeStruct((m, N_MAIN), BF16),
                   jax.ShapeDtypeStruct((DIFF_WIDTH, m), BF16),
                   jax.ShapeDtypeStruct((SSD_HEADS, m), F32)],
        scratch_shapes=[pltpu.VMEM((PROJ_TM, D_MODEL), BF16)],
        compiler_params=pltpu.CompilerParams(
            dimension_semantics=("arbitrary",),
            vmem_limit_bytes=56 * MIB),
        name="in_proj",
    )(x2, pre_norm_w.reshape(1, D_MODEL), w_main, w_v, w_dt, cos, sin)


def _diff_attn_kernel(lam_ref, q_ref, k_ref, v_ref, g_ref, subw_ref, o_ref, vt_ref, *, seq):
    qi = pl.program_id(2)
    heads = ATT_HEADS_PER_STEP

    @pl.when(qi == 0)
    def _():
        for g in range(heads):
            vt_ref[g, 0:DIFF_V_DIM, :] = v_ref[g * DIFF_V_DIM:(g + 1) * DIFF_V_DIM, :]
        vt_ref[:, DIFF_V_DIM:, :] = jnp.ones((heads, ATT_SUM_ROWS, seq), BF16)

    lane = lax.broadcasted_iota(jnp.int32, (ATT_TQ, LANES), 1)
    q_maps = []
    for g in range(heads):
        q = q_ref[:, g * LANES:(g + 1) * LANES]
        q_maps.append((jnp.where(lane < DIFF_QK_DIM, q, jnp.zeros_like(q)),
                       jnp.where(lane >= DIFF_QK_DIM, q, jnp.zeros_like(q))))

    lv = lam_ref[...]
    lam = (jnp.exp(jnp.sum(lv[0:1] * lv[1:2], axis=-1, keepdims=True))
           - jnp.exp(jnp.sum(lv[2:3] * lv[3:4], axis=-1, keepdims=True)) + LAMBDA_INIT)

    def run(n_tiles):
        diag = (n_tiles - 1) * ATT_TK
        half = ATT_TK // 2
        blocks = [(j * ATT_TK, (j + 1) * ATT_TK, 0, False) for j in range(n_tiles - 1)]
        blocks += [(diag, diag + half, 0, True), (diag + half, diag + ATT_TK, half, True)]
        items = [(blk, g, c) for blk in blocks for g in range(heads) for c in range(2)]

        def scores(blk, g, c):
            k0, k1, q0, masked = blk
            kt = k_ref[k0:k1, g * LANES:(g + 1) * LANES]
            s = lax.dot_general(kt, q_maps[g][c][q0:, :], NT_DIMS, preferred_element_type=F32)
            if not masked:
                return s
            key = lax.broadcasted_iota(jnp.int32, s.shape, 0) + (k0 - diag)
            qry = lax.broadcasted_iota(jnp.int32, s.shape, 1) + q0
            return jnp.where(key <= qry, s, -jnp.inf)

        pending = {t: scores(*items[t]) for t in range(min(ATT_AHEAD, len(items)))}
        m, acc = {}, {}
        for t, (blk, g, c) in enumerate(items):
            k0, k1, q0, _ = blk
            i = 2 * g + c
            s = pending.pop(t)
            m_new = jnp.max(s, axis=0, keepdims=True)
            first = i not in m
            if not first:
                m_old = m[i][:, q0:]
                m_new = jnp.maximum(m_old, m_new)
            p = jnp.exp2(s - m_new).astype(BF16)
            if t + ATT_AHEAD < len(items):
                pending[t + ATT_AHEAD] = scores(*items[t + ATT_AHEAD])
            pv = jnp.dot(vt_ref[g, :, k0:k1], p, preferred_element_type=F32)
            if first:
                m[i], acc[i] = m_new, pv
            else:
                new_acc = jnp.exp2(m_old - m_new) * acc[i][:, q0:] + pv
                if q0:
                    m_new = jnp.concatenate([m[i][:, :q0], m_new], axis=1)
                    new_acc = jnp.concatenate([acc[i][:, :q0], new_acc], axis=1)
                m[i], acc[i] = m_new, new_acc

        for g in range(heads):
            c = slice(g * LANES, (g + 1) * LANES)
            a1, a2 = acc[2 * g], acc[2 * g + 1]
            l1 = a1[DIFF_V_DIM:DIFF_V_DIM + 1, :]
            l2 = a2[DIFF_V_DIM:DIFF_V_DIM + 1, :]
            o_t = (a1[0:DIFF_V_DIM, :] * (1.0 / l1)
                   - lam * (a2[0:DIFF_V_DIM, :] * (1.0 / l2)))
            o = o_t.T
            o = o * _rms_scale(o) * subw_ref[...] * (1.0 - LAMBDA_INIT)
            o_ref[:, c] = (o * _silu(g_ref[:, c].astype(F32))).astype(BF16)

    for n in range(seq // ATT_TQ):
        pl.when(qi == n)(functools.partial(run, n + 1))


def _diff_attention(proj, v_t, lam_vecs, subln_w, batch, seq):
    assert ATT_TQ == ATT_TK and seq % ATT_TQ == 0 and DIFF_V_DIM == LANES
    nq = seq // ATT_TQ
    heads = ATT_HEADS_PER_STEP
    width = heads * LANES
    kernel = functools.partial(_diff_attn_kernel, seq=seq)
    return pl.pallas_call(
        kernel,
        grid=(batch, DIFF_HEADS // heads, nq),
        in_specs=[
            pl.BlockSpec((4, DIFF_QK_DIM), lambda b, h, q: (0, 0)),
            pl.BlockSpec((ATT_TQ, width), lambda b, h, q: (b * nq + q, COL_Q // width + h)),
            pl.BlockSpec((seq, width), lambda b, h, q: (b, COL_K // width + h)),
            pl.BlockSpec((width, seq), lambda b, h, q: (h, b)),
            pl.BlockSpec((ATT_TQ, width), lambda b, h, q: (b * nq + q, COL_G // width + h)),
            pl.BlockSpec((1, DIFF_V_DIM), lambda b, h, q: (0, 0)),
        ],
        out_specs=pl.BlockSpec((ATT_TQ, width), lambda b, h, q: (b * nq + q, h)),
        out_shape=jax.ShapeDtypeStruct((batch * seq, DIFF_WIDTH), BF16),
        scratch_shapes=[pltpu.VMEM((heads, DIFF_V_DIM + ATT_SUM_ROWS, seq), BF16)],
        compiler_params=pltpu.CompilerParams(
            dimension_semantics=("arbitrary", "arbitrary", "arbitrary"),
            vmem_limit_bytes=48 * MIB),
        name="diff_attn",
    )(lam_vecs, proj, proj, v_t, proj, subln_w.reshape(1, DIFF_V_DIM))


def _ssd_kernel(xs_ref, b_ref, c_ref, z_ref, dt_ref, shift_ref, triu_ref, sel_ref, cw_ref,
                cb_ref, dtb_ref, alog_ref, dsk_ref, nw_ref, o_ref, xext_ref, state_ref):
    L = SSD_CHUNK
    T = SSD_CHUNKS_PER_STEP
    gs = SSD_GROUPS * SSD_STATE
    pairs = SSD_HEADS // 2

    @pl.when(pl.program_id(1) == 0)
    def _():
        xext_ref[0:L, :] = jnp.zeros((L, SSD_CONV_DIM), BF16)
        state_ref[...] = jnp.zeros(state_ref.shape, F32)

    xext_ref[L:(T + 1) * L, 0:SSD_WIDTH] = xs_ref[...]
    xext_ref[L:(T + 1) * L, SSD_WIDTH:SSD_WIDTH + gs] = b_ref[...]
    xext_ref[L:(T + 1) * L, SSD_WIDTH + gs:SSD_CONV_DIM] = c_ref[...]

    carry = {"state": [state_ref[p] for p in range(pairs)], "cols_split": [None] * T}
    programs = [_ssd_chunk(c, carry, xext_ref, z_ref, dt_ref, shift_ref, triu_ref, cw_ref,
                           cb_ref, dtb_ref, alog_ref, dsk_ref, nw_ref, o_ref)
                for c in range(T)]

    def stage():
        for prog in programs:
            next(prog)

    stage()
    stage()
    carry["spread"] = jnp.dot(jnp.concatenate(carry["cols_split"], axis=0), sel_ref[...],
                              preferred_element_type=F32)
    stage()
    stage()
    stage()
    for p in range(pairs):
        state_ref[p] = carry["state"][p]
    xext_ref[0:L, :] = xext_ref[T * L:(T + 1) * L, :]


SSD_PAIR_KINDS = (0, 2, 3)


def _ssd_chunk(c, carry, xext_ref, z_ref, dt_ref, shift_ref, triu_ref, cw_ref,
               cb_ref, dtb_ref, alog_ref, dsk_ref, nw_ref, o_ref):
    L = SSD_CHUNK
    H = SSD_HEADS
    gs = SSD_GROUPS * SSD_STATE
    rows_c = slice(c * L, (c + 1) * L)

    delayed = jnp.dot(shift_ref[...], xext_ref[c * L:(c + 2) * L, :],
                      preferred_element_type=F32)
    conv = (cb_ref[...] + xext_ref[(c + 1) * L:(c + 2) * L, :].astype(F32)
            * cw_ref[SSD_CONV - 1:SSD_CONV, :])
    for k in range(SSD_CONV - 1):
        conv = conv + delayed[k * L:(k + 1) * L, :] * cw_ref[k:k + 1, :]
    xbc = _silu(conv)
    xs = xbc[:, 0:SSD_WIDTH]
    bm = xbc[:, SSD_WIDTH:SSD_WIDTH + gs].astype(BF16)
    cm = xbc[:, SSD_WIDTH + gs:SSD_CONV_DIM].astype(BF16)
    yield

    dt_in = dt_ref[:, rows_c] + dtb_ref[...]
    dt = jnp.maximum(dt_in, 0.0) + jnp.log1p(jnp.exp(-jnp.abs(dt_in)))
    a_dt = dt * (-jnp.exp(alog_ref[...]))
    hi = a_dt.astype(BF16)
    r1 = a_dt - hi.astype(F32)
    mid = r1.astype(BF16)
    lo = (r1 - mid.astype(F32)).astype(BF16)
    triu = triu_ref[...]
    a_cs = (jnp.dot(hi, triu, preferred_element_type=F32)
            + jnp.dot(mid, triu, preferred_element_type=F32)
            + jnp.dot(lo, triu, preferred_element_type=F32))
    a_end = a_cs[:, L - 1:L]
    chunk_decay = jnp.exp(a_end)
    rows = jnp.concatenate([dt, a_cs, jnp.exp(a_cs), jnp.exp(a_end - a_cs),
                            jnp.zeros((L - 4 * H, L), F32)], axis=0)
    cols = rows.T
    c_hi = cols.astype(BF16)
    c_r1 = cols - c_hi.astype(F32)
    c_mid = c_r1.astype(BF16)
    c_lo = (c_r1 - c_mid.astype(F32)).astype(BF16)
    carry["cols_split"][c] = jnp.concatenate([c_hi, c_mid, c_lo], axis=1)
    yield
    spread = carry["spread"][rows_c, :]

    row = lax.broadcasted_iota(jnp.int32, (L, L), 0)
    col = lax.broadcasted_iota(jnp.int32, (L, L), 1)
    lower = row >= col
    lane = lax.broadcasted_iota(jnp.int32, (L, LANES), 1)
    left = lane < SSD_HEAD_DIM
    top = row < SSD_HEAD_DIM
    pairs = SSD_HEADS // 2

    def pair_lanes(slot, p):
        i = slot * pairs + p
        return spread[:, i * LANES:(i + 1) * LANES]

    def head_lanes(h):
        i = len(SSD_PAIR_KINDS) * pairs + h
        return spread[:, i * LANES:(i + 1) * LANES]

    group_cb = []
    for g in range(SSD_GROUPS):
        group_cb.append(lax.dot_general(cm[:, g * SSD_STATE:(g + 1) * SSD_STATE],
                                        bm[:, g * SSD_STATE:(g + 1) * SSD_STATE],
                                        NT_DIMS, preferred_element_type=F32))
    yield

    group_of = lambda p: (2 * p) // (SSD_HEADS // SSD_GROUPS)
    y_diag, chunk_state = [], []
    for p in range(pairs):
        bg = bm[:, group_of(p) * SSD_STATE:(group_of(p) + 1) * SSD_STATE]
        x_dt = xs[:, p * LANES:(p + 1) * LANES] * pair_lanes(0, p)
        x_dt_b = x_dt.astype(BF16)
        y_heads = []
        for hh in range(2):
            h = 2 * p + hh
            seg = head_lanes(h) - a_cs[h:h + 1, :]
            decay = jnp.exp(jnp.where(lower, seg, -jnp.inf))
            scores = (group_cb[group_of(p)] * decay).astype(BF16)
            y_heads.append(jnp.dot(scores, x_dt_b, preferred_element_type=F32))
        y_diag.append(jnp.where(left, y_heads[0], y_heads[1]))
        x_w = (x_dt * pair_lanes(2, p)).astype(BF16)
        chunk_state.append(lax.dot_general(x_w, bg, TN_DIMS, preferred_element_type=F32))
    yield

    ys, new_state = [], []
    for p in range(pairs):
        cg = cm[:, group_of(p) * SSD_STATE:(group_of(p) + 1) * SSD_STATE]
        st = carry["state"][p]
        y_off = (lax.dot_general(cg, st.astype(BF16), NT_DIMS, preferred_element_type=F32)
                 * pair_lanes(1, p))
        st_decay = jnp.where(top, chunk_decay[2 * p:2 * p + 1, :],
                             chunk_decay[2 * p + 1:2 * p + 2, :])
        new_state.append(st * st_decay + chunk_state[p])
        y = (y_diag[p] + y_off
             + xs[:, p * LANES:(p + 1) * LANES] * dsk_ref[:, p * LANES:(p + 1) * LANES])
        ys.append(y * _silu(z_ref[rows_c, p * LANES:(p + 1) * LANES].astype(F32)))
    carry["state"] = new_state

    grp_pairs = pairs // SSD_GROUPS
    grp_width = SSD_WIDTH // SSD_GROUPS
    for g in range(SSD_GROUPS):
        members = ys[g * grp_pairs:(g + 1) * grp_pairs]
        ssq = sum(jnp.sum(y * y, axis=-1, keepdims=True) for y in members)
        inv = lax.rsqrt(ssq / grp_width + NORM_EPS)
        for i, y in enumerate(members):
            c0 = (g * grp_pairs + i) * LANES
            o_ref[rows_c, c0:c0 + LANES] = (y * inv * nw_ref[:, c0:c0 + LANES]).astype(BF16)
    yield


def _ssd(proj, dt_raw_t, conv_w, conv_b, dt_bias, a_log, d_skip, ssd_norm_w, batch, seq):
    gs = SSD_GROUPS * SSD_STATE
    L = SSD_CHUNK
    rows = SSD_CHUNKS_PER_STEP * L
    nc = seq // rows
    row_map = lambda b, c: b * nc + c
    const = lambda b, c: (0, 0)
    t = jnp.arange(L)[:, None]
    src = jnp.arange(2 * L)[None, :]
    shift = jnp.concatenate([(src == L + t - (SSD_CONV - 1 - k)) for k in range(SSD_CONV - 1)],
                            axis=0).astype(BF16)
    triu = (jnp.arange(L)[:, None] <= jnp.arange(L)[None, :]).astype(BF16)
    lane = jnp.arange(LANES)
    sel_blocks = []
    for kind in SSD_PAIR_KINDS:
        for p in range(SSD_HEADS // 2):
            src_col = kind * SSD_HEADS + 2 * p + (lane >= SSD_HEAD_DIM)
            sel_blocks.append(jnp.arange(L)[:, None] == src_col[None, :])
    for h in range(SSD_HEADS):
        sel_blocks.append(jnp.broadcast_to(jnp.arange(L)[:, None] == SSD_HEADS + h, (L, LANES)))
    sel = jnp.tile(jnp.concatenate(sel_blocks, axis=1), (3, 1)).astype(BF16)
    per_head = lambda v: jnp.broadcast_to(v.astype(F32)[:, None], (SSD_HEADS, L))
    return pl.pallas_call(
        _ssd_kernel,
        grid=(batch, nc),
        in_specs=[
            pl.BlockSpec((rows, SSD_WIDTH), lambda b, c: (row_map(b, c), COL_XS // SSD_WIDTH)),
            pl.BlockSpec((rows, gs), lambda b, c: (row_map(b, c), COL_B // gs)),
            pl.BlockSpec((rows, gs), lambda b, c: (row_map(b, c), COL_C // gs)),
            pl.BlockSpec((rows, SSD_WIDTH), lambda b, c: (row_map(b, c), COL_Z // SSD_WIDTH)),
            pl.BlockSpec((SSD_HEADS, rows), lambda b, c: (0, row_map(b, c))),
            pl.BlockSpec(((SSD_CONV - 1) * L, 2 * L), const),
            pl.BlockSpec((L, L), const),
            pl.BlockSpec(sel.shape, const),
            pl.BlockSpec((SSD_CONV, SSD_CONV_DIM), const),
            pl.BlockSpec((1, SSD_CONV_DIM), const),
            pl.BlockSpec((SSD_HEADS, L), const),
            pl.BlockSpec((SSD_HEADS, L), const),
            pl.BlockSpec((1, SSD_WIDTH), const),
            pl.BlockSpec((1, SSD_WIDTH), const),
        ],
        out_specs=pl.BlockSpec((rows, SSD_WIDTH), lambda b, c: (row_map(b, c), 0)),
        out_shape=jax.ShapeDtypeStruct((batch * seq, SSD_WIDTH), BF16),
        scratch_shapes=[pltpu.VMEM((L + rows, SSD_CONV_DIM), BF16),
                        pltpu.VMEM((SSD_HEADS // 2, 2 * SSD_HEAD_DIM, SSD_STATE), F32)],
        compiler_params=pltpu.CompilerParams(
            dimension_semantics=("arbitrary", "arbitrary"),
            vmem_limit_bytes=32 * MIB),
        name="ssd",
    )(proj, proj, proj, proj, dt_raw_t, shift, triu, sel, conv_w, conv_b.reshape(1, SSD_CONV_DIM),
      per_head(dt_bias), per_head(a_log),
      jnp.repeat(d_skip, SSD_HEAD_DIM).reshape(1, SSD_WIDTH), ssd_norm_w.reshape(1, SSD_WIDTH))


def _mem_kv_kernel(mem_ref, nw_ref, w_ref, o_ref):
    x = mem_ref[...]
    h = (x * _rms_scale(x) * nw_ref[...]).astype(BF16)
    o_ref[...] = jnp.dot(h, w_ref[...], preferred_element_type=F32).astype(BF16)


def _mem_kv(mem2, mem_norm_w, w_kv, batch, mem_len):
    return pl.pallas_call(
        _mem_kv_kernel,
        grid=(batch,),
        in_specs=[pl.BlockSpec((mem_len, D_MODEL), lambda b: (b, 0)),
                  pl.BlockSpec((1, D_MODEL), lambda b: (0, 0)),
                  pl.BlockSpec((D_MODEL, 2 * XATTN_WIDTH), lambda b: (0, 0))],
        out_specs=pl.BlockSpec((mem_len, 2 * XATTN_WIDTH), lambda b: (b, 0)),
        out_shape=jax.ShapeDtypeStruct((batch * mem_len, 2 * XATTN_WIDTH), BF16),
        compiler_params=pltpu.CompilerParams(
            dimension_semantics=("arbitrary",), vmem_limit_bytes=32 * MIB),
        name="mem_kv",
    )(mem2, mem_norm_w.reshape(1, D_MODEL), w_kv)


def _xattn_kernel(q_ref, g_ref, kv_ref, o_ref):
    scale = XATTN_HEAD_DIM ** -0.5
    for h in range(XATTN_HEADS):
        c = slice(h * XATTN_HEAD_DIM, (h + 1) * XATTN_HEAD_DIM)
        ck = slice(h * XATTN_HEAD_DIM, (h + 1) * XATTN_HEAD_DIM)
        cv = slice(XATTN_WIDTH + h * XATTN_HEAD_DIM, XATTN_WIDTH + (h + 1) * XATTN_HEAD_DIM)
        s = lax.dot_general(q_ref[:, c], kv_ref[:, ck], NT_DIMS,
                            preferred_element_type=F32) * scale
        p = jnp.exp(s - jnp.max(s, axis=-1, keepdims=True))
        inv_l = 1.0 / jnp.sum(p, axis=-1, keepdims=True)
        xo = jnp.dot(p.astype(BF16), kv_ref[:, cv], preferred_element_type=F32) * inv_l
        o_ref[:, c] = (xo * _silu(g_ref[:, c].astype(F32))).astype(BF16)


def _xattn(proj, mkv, batch, seq, mem_len):
    nq = seq // XATT_TQ
    return pl.pallas_call(
        _xattn_kernel,
        grid=(batch, nq),
        in_specs=[pl.BlockSpec((XATT_TQ, XATTN_WIDTH), lambda b, q: (b * nq + q, COL_XQ // XATTN_WIDTH)),
                  pl.BlockSpec((XATT_TQ, XATTN_WIDTH), lambda b, q: (b * nq + q, COL_XG // XATTN_WIDTH)),
                  pl.BlockSpec((mem_len, 2 * XATTN_WIDTH), lambda b, q: (b, 0))],
        out_specs=pl.BlockSpec((XATT_TQ, XATTN_WIDTH), lambda b, q: (b * nq + q, 0)),
        out_shape=jax.ShapeDtypeStruct((batch * seq, XATTN_WIDTH), BF16),
        compiler_params=pltpu.CompilerParams(
            dimension_semantics=("arbitrary", "arbitrary"), vmem_limit_bytes=32 * MIB),
        name="xattn",
    )(proj, proj, mkv)


def _out_proj_kernel(d_ref, s_ref, a_ref, w_ref, nw_ref, x_ref, o_ref):
    for i in range(OUT_TM // OUT_SUB):
        r = slice(i * OUT_SUB, (i + 1) * OUT_SUB)
        y = jnp.dot(d_ref[r, :], w_ref[0:DIFF_WIDTH, :], preferred_element_type=F32)
        y = y + jnp.dot(s_ref[r, :], w_ref[DIFF_WIDTH:DIFF_WIDTH + SSD_WIDTH, :],
                        preferred_element_type=F32)
        y = y + jnp.dot(a_ref[r, :], w_ref[DIFF_WIDTH + SSD_WIDTH:D_MIX, :],
                        preferred_element_type=F32)
        o_ref[r, :] = x_ref[r, :] + y * _rms_scale(y) * nw_ref[...]


def _out_proj(diff_out, ssd_out, xattn_out, w_out, post_norm_w, x2):
    m = x2.shape[0]
    return pl.pallas_call(
        _out_proj_kernel,
        grid=(m // OUT_TM,),
        in_specs=[pl.BlockSpec((OUT_TM, DIFF_WIDTH), lambda i: (i, 0)),
                  pl.BlockSpec((OUT_TM, SSD_WIDTH), lambda i: (i, 0)),
                  pl.BlockSpec((OUT_TM, XATTN_WIDTH), lambda i: (i, 0)),
                  pl.BlockSpec((D_MIX, D_MODEL), lambda i: (0, 0)),
                  pl.BlockSpec((1, D_MODEL), lambda i: (0, 0)),
                  pl.BlockSpec((OUT_TM, D_MODEL), lambda i: (i, 0))],
        out_specs=pl.BlockSpec((OUT_TM, D_MODEL), lambda i: (i, 0)),
        out_shape=jax.ShapeDtypeStruct((m, D_MODEL), F32),
        compiler_params=pltpu.CompilerParams(
            dimension_semantics=("arbitrary",), vmem_limit_bytes=48 * MIB),
        name="out_proj",
    )(diff_out, ssd_out, xattn_out, w_out, post_norm_w.reshape(1, D_MODEL), x2)


def kernel(x, mem, positions, pre_norm_w, w_in, lambda_q1, lambda_k1, lambda_q2, lambda_k2,
           diff_subln_w, conv_w, conv_b, dt_bias, a_log, d_skip, ssd_norm_w, mem_norm_w,
           w_mem_kv, w_out, post_norm_w):
    batch, seq, _ = x.shape
    mem_len = mem.shape[1]
    assert pre_norm_w.shape[0] == 1, "single-layer kernel"
    x2 = x.reshape(batch * seq, D_MODEL)

    w_main, w_v, w_dt = _w_in_prep(jnp.transpose(w_in[0]))
    lam_vecs = jnp.stack([lambda_q1[0], lambda_k1[0], lambda_q2[0], lambda_k2[0]]).astype(F32)

    in_hbm = lambda a: pltpu.with_memory_space_constraint(a, pltpu.HBM)
    cos, sin = _rope_tables(positions)
    proj, v_t, dt_raw_t = _in_proj(x2, pre_norm_w[0], w_main, w_v, w_dt,
                                   in_hbm(cos), in_hbm(sin))
    diff_out = _diff_attention(proj, in_hbm(v_t), lam_vecs, diff_subln_w[0], batch, seq)
    ssd_out = _ssd(proj, dt_raw_t, conv_w[0], conv_b[0], dt_bias[0], a_log[0], d_skip[0],
                   ssd_norm_w[0], batch, seq)
    mkv = _mem_kv(mem.reshape(batch * mem_len, D_MODEL), mem_norm_w[0],
                  w_mem_kv[0].astype(BF16), batch, mem_len)
    xattn_out = _xattn(proj, in_hbm(mkv), batch, seq, mem_len)
    out = _out_proj(in_hbm(diff_out), in_hbm(ssd_out), in_hbm(xattn_out),
                    w_out[0].astype(BF16), post_norm_w[0], x2)
    return out.reshape(batch, seq, D_MODEL)
```

```python
import functools
import math

import jax
import jax.numpy as jnp
from jax import lax
from jax.experimental import pallas as pl
from jax.experimental.pallas import tpu as pltpu

F32 = jnp.float32
BF16 = jnp.bfloat16

D_MODEL = 2048
DIFF_HEADS = 8
DIFF_QK_DIM = 64
DIFF_V_DIM = 128
DIFF_WIDTH = DIFF_HEADS * DIFF_V_DIM
SSD_HEADS = 8
SSD_HEAD_DIM = 64
SSD_WIDTH = SSD_HEADS * SSD_HEAD_DIM
SSD_GROUPS = 2
SSD_STATE = 128
SSD_CONV = 4
SSD_CHUNK = 128
SSD_CONV_DIM = SSD_WIDTH + 2 * SSD_GROUPS * SSD_STATE
XATTN_HEADS = 4
XATTN_HEAD_DIM = 128
XATTN_WIDTH = XATTN_HEADS * XATTN_HEAD_DIM
D_MIX = DIFF_WIDTH + SSD_WIDTH + XATTN_WIDTH
ROPE_THETA = 10000.0
NORM_EPS = 1e-6
LAMBDA_INIT = 0.8 - 0.6 * math.exp(-0.3 * 0)

LANES = 128
SUBLANES = 8
MIB = 1024 * 1024

COL_Q = 0
COL_K = COL_Q + DIFF_WIDTH
COL_G = COL_K + DIFF_WIDTH
COL_Z = COL_G + DIFF_WIDTH
COL_XS = COL_Z + SSD_WIDTH
COL_B = COL_XS + SSD_WIDTH
COL_C = COL_B + SSD_GROUPS * SSD_STATE
COL_XQ = COL_C + SSD_GROUPS * SSD_STATE
COL_XG = COL_XQ + XATTN_WIDTH
N_MAIN = COL_XG + XATTN_WIDTH
REF_V = 2 * DIFF_WIDTH
REF_G = REF_V + DIFF_WIDTH
REF_DT = REF_G + DIFF_WIDTH + SSD_WIDTH + SSD_CONV_DIM
REF_XQ = REF_DT + SSD_HEADS
D_IN = REF_XQ + 2 * XATTN_WIDTH

PROJ_TM = 512
PROJ_TN = 512
NORM_ROWS = 256
ATT_TQ = 512
ATT_TK = 512
ATT_HEADS_PER_STEP = 4
ATT_SUM_ROWS = 16
ATT_AHEAD = 2
XATT_TQ = 512
OUT_TM = 512
OUT_SUB = 256
ROPE_TM = 512
SSD_CHUNKS_PER_STEP = 4
DT_ROWS = 16

NT_DIMS = (((1,), (1,)), ((), ()))
TN_DIMS = (((0,), (0,)), ((), ()))


def _rms_scale(x):
    return lax.rsqrt(jnp.mean(x * x, axis=-1, keepdims=True) + NORM_EPS)


def _silu(x):
    return x * jax.nn.sigmoid(x)


def _rope_table_rows(pos_ref, invf_ref, cos_ref, sin_ref):
    ang = pos_ref[...].astype(F32) * invf_ref[...]
    lane = lax.broadcasted_iota(jnp.int32, ang.shape, 1)
    first_half = (lane % DIFF_QK_DIM) < (DIFF_QK_DIM // 2)
    s = jnp.sin(ang)
    cos_ref[...] = jnp.cos(ang)
    sin_ref[...] = jnp.where(first_half, -s, s)


def _rope(x, cos, sin_signed):
    lane = lax.broadcasted_iota(jnp.int32, x.shape, 1)
    first_half = (lane % DIFF_QK_DIM) < (DIFF_QK_DIM // 2)
    half = DIFF_QK_DIM // 2
    rot = jnp.where(first_half, pltpu.roll(x, LANES - half, 1), pltpu.roll(x, half, 1))
    return x * cos + rot * sin_signed


def _w_in_prep_kernel(wt_ref, dt_rows_ref, pos_ref, invf_ref,
                      wm_ref, wv_ref, wdt_ref, cos_ref, sin_ref):
    j = pl.program_id(0)
    main_blocks = N_MAIN // PROJ_TN
    _rope_table_rows(pos_ref, invf_ref, cos_ref, sin_ref)

    @pl.when(j < main_blocks)
    def _():
        wm_ref[...] = wt_ref[...].T.astype(BF16)

    @pl.when((j >= main_blocks) & (j < main_blocks + DIFF_WIDTH // PROJ_TN))
    def _():
        wv_ref[...] = wt_ref[...].astype(BF16)

    @pl.when(j == 0)
    def _():
        row = lax.broadcasted_iota(jnp.int32, (DT_ROWS, D_MODEL), 0)
        wdt_ref[...] = jnp.where(row < SSD_HEADS, dt_rows_ref[...], 0.0).astype(BF16)


def _w_in_prep(w_t, positions):
    n, k = w_t.shape
    assert n == D_IN and k == D_MODEL
    main_blocks = N_MAIN // PROJ_TN
    v_blocks = DIFF_WIDTH // PROJ_TN
    last_w = main_blocks + v_blocks - 1
    n_pos = positions.size
    steps = n_pos // ROPE_TM
    assert steps > last_w
    inv_freq = 1.0 / (ROPE_THETA ** (jnp.arange(0, DIFF_QK_DIM, 2, dtype=F32) / DIFF_QK_DIM))
    invf = jnp.tile(inv_freq, LANES // inv_freq.shape[0]).reshape(1, LANES)

    def src_row(j):
        main = jnp.where(j < REF_V // PROJ_TN, j * PROJ_TN,
                         jnp.where(j < (REF_DT - DIFF_WIDTH) // PROJ_TN,
                                   j * PROJ_TN + DIFF_WIDTH,
                                   j * PROJ_TN + DIFF_WIDTH + SSD_HEADS))
        jw = jnp.minimum(j, last_w)
        return pl.multiple_of(jnp.where(jw < main_blocks, main,
                                        REF_V + (jw - main_blocks) * PROJ_TN), SUBLANES)

    return pl.pallas_call(
        _w_in_prep_kernel,
        grid=(steps,),
        in_specs=[pl.BlockSpec((pl.Element(PROJ_TN), pl.Element(D_MODEL)),
                               lambda j: (src_row(j), 0)),
                  pl.BlockSpec((DT_ROWS, D_MODEL), lambda j: (REF_DT // DT_ROWS, 0)),
                  pl.BlockSpec((ROPE_TM, 1), lambda j: (j, 0)),
                  pl.BlockSpec((1, LANES), lambda j: (0, 0))],
        out_specs=[pl.BlockSpec((D_MODEL, PROJ_TN),
                                lambda j: (0, jnp.minimum(j, main_blocks - 1))),
                   pl.BlockSpec((PROJ_TN, D_MODEL),
                                lambda j: (jnp.clip(j - main_blocks, 0, v_blocks - 1), 0)),
                   pl.BlockSpec((DT_ROWS, D_MODEL), lambda j: (0, 0)),
                   pl.BlockSpec((ROPE_TM, LANES), lambda j: (j, 0)),
                   pl.BlockSpec((ROPE_TM, LANES), lambda j: (j, 0))],
        out_shape=[jax.ShapeDtypeStruct((k, N_MAIN), BF16),
                   jax.ShapeDtypeStruct((DIFF_WIDTH, k), BF16),
                   jax.ShapeDtypeStruct((DT_ROWS, k), BF16),
                   jax.ShapeDtypeStruct((n_pos, LANES), F32),
                   jax.ShapeDtypeStruct((n_pos, LANES), F32)],
        compiler_params=pltpu.CompilerParams(
            dimension_semantics=("arbitrary",), vmem_limit_bytes=32 * MIB),
        name="w_in_prep",
    )(w_t, w_t, positions.reshape(n_pos, 1), invf)


def _in_proj_kernel(x_ref, nw_ref, w_ref, wv_ref, wdt_ref, cos_ref, sin_ref,
                    o_ref, vt_ref, dt_ref, h_ref):
    def norm_rows(i, carry):
        r = pl.ds(pl.multiple_of(i * NORM_ROWS, NORM_ROWS), NORM_ROWS)
        x = x_ref[r, :]
        h_ref[r, :] = (x * _rms_scale(x) * nw_ref[...]).astype(BF16)
        return carry
    lax.fori_loop(0, PROJ_TM // NORM_ROWS, norm_rows, 0)
    dt_t = lax.dot_general(wdt_ref[...], h_ref[...], NT_DIMS, preferred_element_type=F32)
    dt_ref[...] = dt_t[0:SSD_HEADS, :]
    vt_ref[...] = lax.dot_general(wv_ref[...], h_ref[...], NT_DIMS,
                                  preferred_element_type=F32).astype(BF16)
    q_scale = DIFF_QK_DIM ** -0.5 * math.log2(math.e)
    for n in range(N_MAIN // PROJ_TN):
        c0 = n * PROJ_TN
        y = jnp.dot(h_ref[...], w_ref[:, c0:c0 + PROJ_TN], preferred_element_type=F32)
        if c0 < COL_G:
            scale = q_scale if c0 < COL_K else 1.0
            for hd in range(PROJ_TN // LANES):
                y_h = _rope(y[:, hd * LANES:(hd + 1) * LANES], cos_ref[...], sin_ref[...])
                o_ref[:, c0 + hd * LANES:c0 + (hd + 1) * LANES] = (y_h * scale).astype(BF16)
        else:
            o_ref[:, c0:c0 + PROJ_TN] = y.astype(BF16)


def _in_proj(x2, pre_norm_w, w_main, w_v, w_dt, cos, sin):
    m = x2.shape[0]
    resident = pl.Buffered(1)
    return pl.pallas_call(
        _in_proj_kernel,
        grid=(m // PROJ_TM,),
        in_specs=[pl.BlockSpec((PROJ_TM, D_MODEL), lambda i: (i, 0)),
                  pl.BlockSpec((1, D_MODEL), lambda i: (0, 0)),
                  pl.BlockSpec((D_MODEL, N_MAIN), lambda i: (0, 0), pipeline_mode=resident),
                  pl.BlockSpec((DIFF_WIDTH, D_MODEL), lambda i: (0, 0), pipeline_mode=resident),
                  pl.BlockSpec((DT_ROWS, D_MODEL), lambda i: (0, 0), pipeline_mode=resident),
                  pl.BlockSpec((PROJ_TM, LANES), lambda i: (i, 0)),
                  pl.BlockSpec((PROJ_TM, LANES), lambda i: (i, 0))],
        out_specs=[pl.BlockSpec((PROJ_TM, N_MAIN), lambda i: (i, 0)),
                   pl.BlockSpec((DIFF_WIDTH, PROJ_TM), lambda i: (0, i)),
                   pl.BlockSpec((SSD_HEADS, PROJ_TM), lambda i: (0, i))],
        out_shape=[jax.ShapeDtypeStruct((m, N_MAIN), BF16),
                   jax.ShapeDtypeStruct((DIFF_WIDTH, m), BF16),
                   jax.ShapeDtypeStruct((SSD_HEADS, m), F32)],
        scratch_shapes=[pltpu.VMEM((PROJ_TM, D_MODEL), BF16)],
        compiler_params=pltpu.CompilerParams(
            dimension_semantics=("arbitrary",),
            vmem_limit_bytes=56 * MIB),
        name="in_proj",
    )(x2, pre_norm_w.reshape(1, D_MODEL), w_main, w_v, w_dt, cos, sin)


def _diff_attn_kernel(lam_ref, q_ref, k_ref, v_ref, g_ref, subw_ref, o_ref, vt_ref, *, seq):
    qi = pl.program_id(2)
    heads = ATT_HEADS_PER_STEP

    @pl.when(qi == 0)
    def _():
        for g in range(heads):
            vt_ref[g, 0:DIFF_V_DIM, :] = v_ref[g * DIFF_V_DIM:(g + 1) * DIFF_V_DIM, :]
        vt_ref[:, DIFF_V_DIM:, :] = jnp.ones((heads, ATT_SUM_ROWS, seq), BF16)

    lane = lax.broadcasted_iota(jnp.int32, (ATT_TQ, LANES), 1)
    q_maps = []
    for g in range(heads):
        q = q_ref[:, g * LANES:(g + 1) * LANES]
        q_maps.append((jnp.where(lane < DIFF_QK_DIM, q, jnp.zeros_like(q)),
                       jnp.where(lane >= DIFF_QK_DIM, q, jnp.zeros_like(q))))

    lv = lam_ref[...]
    lam = (jnp.exp(jnp.sum(lv[0:1] * lv[1:2], axis=-1, keepdims=True))
           - jnp.exp(jnp.sum(lv[2:3] * lv[3:4], axis=-1, keepdims=True)) + LAMBDA_INIT)

    def run(n_tiles):
        diag = (n_tiles - 1) * ATT_TK
        half = ATT_TK // 2
        blocks = [(j * ATT_TK, (j + 1) * ATT_TK, 0, False) for j in range(n_tiles - 1)]
        blocks += [(diag, diag + half, 0, True), (diag + half, diag + ATT_TK, half, True)]
        items = [(blk, g, c) for blk in blocks for g in range(heads) for c in range(2)]

        def scores(blk, g, c):
            k0, k1, q0, masked = blk
            kt = k_ref[k0:k1, g * LANES:(g + 1) * LANES]
            s = lax.dot_general(kt, q_maps[g][c][q0:, :], NT_DIMS, preferred_element_type=F32)
            if not masked:
                return s
            key = lax.broadcasted_iota(jnp.int32, s.shape, 0) + (k0 - diag)
            qry = lax.broadcasted_iota(jnp.int32, s.shape, 1) + q0
            return jnp.where(key <= qry, s, -jnp.inf)

        pending = {t: scores(*items[t]) for t in range(min(ATT_AHEAD, len(items)))}
        m, acc = {}, {}
        for t, (blk, g, c) in enumerate(items):
            k0, k1, q0, _ = blk
            i = 2 * g + c
            s = pending.pop(t)
            m_new = jnp.max(s, axis=0, keepdims=True)
            first = i not in m
            if not first:
                m_old = m[i][:, q0:]
                m_new = jnp.maximum(m_old, m_new)
            p = jnp.exp2(s - m_new).astype(BF16)
            if t + ATT_AHEAD < len(items):
                pending[t + ATT_AHEAD] = scores(*items[t + ATT_AHEAD])
            pv = jnp.dot(vt_ref[g, :, k0:k1], p, preferred_element_type=F32)
            if first:
                m[i], acc[i] = m_new, pv
            else:
                new_acc = jnp.exp2(m_old - m_new) * acc[i][:, q0:] + pv
                if q0:
                    m_new = jnp.concatenate([m[i][:, :q0], m_new], axis=1)
                    new_acc = jnp.concatenate([acc[i][:, :q0], new_acc], axis=1)
                m[i], acc[i] = m_new, new_acc

        for g in range(heads):
            c = slice(g * LANES, (g + 1) * LANES)
            a1, a2 = acc[2 * g], acc[2 * g + 1]
            l1 = a1[DIFF_V_DIM:DIFF_V_DIM + 1, :]
            l2 = a2[DIFF_V_DIM:DIFF_V_DIM + 1, :]
            o_t = (a1[0:DIFF_V_DIM, :] * (1.0 / l1)
                   - lam * (a2[0:DIFF_V_DIM, :] * (1.0 / l2)))
            o = o_t.T
            o = o * _rms_scale(o) * subw_ref[...] * (1.0 - LAMBDA_INIT)
            o_ref[:, c] = (o * _silu(g_ref[:, c].astype(F32))).astype(BF16)

    for n in range(seq // ATT_TQ):
        pl.when(qi == n)(functools.partial(run, n + 1))


def _diff_attention(proj, v_t, lam_vecs, subln_w, batch, seq):
    assert ATT_TQ == ATT_TK and seq % ATT_TQ == 0 and DIFF_V_DIM == LANES
    nq = seq // ATT_TQ
    heads = ATT_HEADS_PER_STEP
    width = heads * LANES
    kernel = functools.partial(_diff_attn_kernel, seq=seq)
    return pl.pallas_call(
        kernel,
        grid=(batch, DIFF_HEADS // heads, nq),
        in_specs=[
            pl.BlockSpec((4, DIFF_QK_DIM), lambda b, h, q: (0, 0)),
            pl.BlockSpec((ATT_TQ, width), lambda b, h, q: (b * nq + q, COL_Q // width + h)),
            pl.BlockSpec((seq, width), lambda b, h, q: (b, COL_K // width + h)),
            pl.BlockSpec((width, seq), lambda b, h, q: (h, b)),
            pl.BlockSpec((ATT_TQ, width), lambda b, h, q: (b * nq + q, COL_G // width + h)),
            pl.BlockSpec((1, DIFF_V_DIM), lambda b, h, q: (0, 0)),
        ],
        out_specs=pl.BlockSpec((ATT_TQ, width), lambda b, h, q: (b * nq + q, h)),
        out_shape=jax.ShapeDtypeStruct((batch * seq, DIFF_WIDTH), BF16),
        scratch_shapes=[pltpu.VMEM((heads, DIFF_V_DIM + ATT_SUM_ROWS, seq), BF16)],
        compiler_params=pltpu.CompilerParams(
            dimension_semantics=("arbitrary", "arbitrary", "arbitrary"),
            vmem_limit_bytes=48 * MIB),
        name="diff_attn",
    )(lam_vecs, proj, proj, v_t, proj, subln_w.reshape(1, DIFF_V_DIM))


def _ssd_kernel(xs_ref, b_ref, c_ref, z_ref, dt_ref, shift_ref, triu_ref, sel_ref, cw_ref,
                cb_ref, dtb_ref, alog_ref, dsk_ref, nw_ref, o_ref, xext_ref, state_ref):
    L = SSD_CHUNK
    T = SSD_CHUNKS_PER_STEP
    gs = SSD_GROUPS * SSD_STATE
    pairs = SSD_HEADS // 2

    @pl.when(pl.program_id(1) == 0)
    def _():
        xext_ref[0:L, :] = jnp.zeros((L, SSD_CONV_DIM), BF16)
        state_ref[...] = jnp.zeros(state_ref.shape, F32)

    xext_ref[L:(T + 1) * L, 0:SSD_WIDTH] = xs_ref[...]
    xext_ref[L:(T + 1) * L, SSD_WIDTH:SSD_WIDTH + gs] = b_ref[...]
    xext_ref[L:(T + 1) * L, SSD_WIDTH + gs:SSD_CONV_DIM] = c_ref[...]

    carry = {"state": [state_ref[p] for p in range(pairs)], "cols_split": [None] * T}
    programs = [_ssd_chunk(c, carry, xext_ref, z_ref, dt_ref, shift_ref, triu_ref, cw_ref,
                           cb_ref, dtb_ref, alog_ref, dsk_ref, nw_ref, o_ref)
                for c in range(T)]

    def stage():
        for prog in programs:
            next(prog)

    stage()
    stage()
    carry["spread"] = jnp.dot(jnp.concatenate(carry["cols_split"], axis=0), sel_ref[...],
                              preferred_element_type=F32)
    stage()
    stage()
    stage()
    for p in range(pairs):
        state_ref[p] = carry["state"][p]
    xext_ref[0:L, :] = xext_ref[T * L:(T + 1) * L, :]


SSD_PAIR_KINDS = (0, 2, 3)


def _ssd_chunk(c, carry, xext_ref, z_ref, dt_ref, shift_ref, triu_ref, cw_ref,
               cb_ref, dtb_ref, alog_ref, dsk_ref, nw_ref, o_ref):
    L = SSD_CHUNK
    H = SSD_HEADS
    gs = SSD_GROUPS * SSD_STATE
    rows_c = slice(c * L, (c + 1) * L)

    delayed = jnp.dot(shift_ref[...], xext_ref[c * L:(c + 2) * L, :],
                      preferred_element_type=F32)
    conv = (cb_ref[...] + xext_ref[(c + 1) * L:(c + 2) * L, :].astype(F32)
            * cw_ref[SSD_CONV - 1:SSD_CONV, :])
    for k in range(SSD_CONV - 1):
        conv = conv + delayed[k * L:(k + 1) * L, :] * cw_ref[k:k + 1, :]
    xbc = _silu(conv)
    xs = xbc[:, 0:SSD_WIDTH]
    bm = xbc[:, SSD_WIDTH:SSD_WIDTH + gs].astype(BF16)
    cm = xbc[:, SSD_WIDTH + gs:SSD_CONV_DIM].astype(BF16)
    yield

    dt_in = dt_ref[:, rows_c] + dtb_ref[...]
    dt = jnp.maximum(dt_in, 0.0) + jnp.log1p(jnp.exp(-jnp.abs(dt_in)))
    a_dt = dt * (-jnp.exp(alog_ref[...]))
    hi = a_dt.astype(BF16)
    r1 = a_dt - hi.astype(F32)
    mid = r1.astype(BF16)
    lo = (r1 - mid.astype(F32)).astype(BF16)
    triu = triu_ref[...]
    a_cs = (jnp.dot(hi, triu, preferred_element_type=F32)
            + jnp.dot(mid, triu, preferred_element_type=F32)
            + jnp.dot(lo, triu, preferred_element_type=F32))
    a_end = a_cs[:, L - 1:L]
    chunk_decay = jnp.exp(a_end)
    rows = jnp.concatenate([dt, a_cs, jnp.exp(a_cs), jnp.exp(a_end - a_cs),
                            jnp.zeros((L - 4 * H, L), F32)], axis=0)
    cols = rows.T
    c_hi = cols.astype(BF16)
    c_r1 = cols - c_hi.astype(F32)
    c_mid = c_r1.astype(BF16)
    c_lo = (c_r1 - c_mid.astype(F32)).astype(BF16)
    carry["cols_split"][c] = jnp.concatenate([c_hi, c_mid, c_lo], axis=1)
    yield
    spread = carry["spread"][rows_c, :]

    row = lax.broadcasted_iota(jnp.int32, (L, L), 0)
    col = lax.broadcasted_iota(jnp.int32, (L, L), 1)
    lower = row >= col
    lane = lax.broadcasted_iota(jnp.int32, (L, LANES), 1)
    left = lane < SSD_HEAD_DIM
    top = row < SSD_HEAD_DIM
    pairs = SSD_HEADS // 2

    def pair_lanes(slot, p):
        i = slot * pairs + p
        return spread[:, i * LANES:(i + 1) * LANES]

    def head_lanes(h):
        i = len(SSD_PAIR_KINDS) * pairs + h
        return spread[:, i * LANES:(i + 1) * LANES]

    group_cb = []
    for g in range(SSD_GROUPS):
        group_cb.append(lax.dot_general(cm[:, g * SSD_STATE:(g + 1) * SSD_STATE],
                                        bm[:, g * SSD_STATE:(g + 1) * SSD_STATE],
                                        NT_DIMS, preferred_element_type=F32))
    yield

    group_of = lambda p: (2 * p) // (SSD_HEADS // SSD_GROUPS)
    y_diag, chunk_state = [], []
    for p in range(pairs):
        bg = bm[:, group_of(p) * SSD_STATE:(group_of(p) + 1) * SSD_STATE]
        x_dt = xs[:, p * LANES:(p + 1) * LANES] * pair_lanes(0, p)
        x_dt_b = x_dt.astype(BF16)
        y_heads = []
        for hh in range(2):
            h = 2 * p + hh
            seg = head_lanes(h) - a_cs[h:h + 1, :]
            decay = jnp.exp(jnp.where(lower, seg, -jnp.inf))
            scores = (group_cb[group_of(p)] * decay).astype(BF16)
            y_heads.append(jnp.dot(scores, x_dt_b, preferred_element_type=F32))
        y_diag.append(jnp.where(left, y_heads[0], y_heads[1]))
        x_w = (x_dt * pair_lanes(2, p)).astype(BF16)
        chunk_state.append(lax.dot_general(x_w, bg, TN_DIMS, preferred_element_type=F32))
    yield

    ys, new_state = [], []
    for p in range(pairs):
        cg = cm[:, group_of(p) * SSD_STATE:(group_of(p) + 1) * SSD_STATE]
        st = carry["state"][p]
        y_off = (lax.dot_general(cg, st.astype(BF16), NT_DIMS, preferred_element_type=F32)
                 * pair_lanes(1, p))
        st_decay = jnp.where(top, chunk_decay[2 * p:2 * p + 1, :],
                             chunk_decay[2 * p + 1:2 * p + 2, :])
        new_state.append(st * st_decay + chunk_state[p])
        y = (y_diag[p] + y_off
             + xs[:, p * LANES:(p + 1) * LANES] * dsk_ref[:, p * LANES:(p + 1) * LANES])
        ys.append(y * _silu(z_ref[rows_c, p * LANES:(p + 1) * LANES].astype(F32)))
    carry["state"] = new_state

    grp_pairs = pairs // SSD_GROUPS
    grp_width = SSD_WIDTH // SSD_GROUPS
    for g in range(SSD_GROUPS):
        members = ys[g * grp_pairs:(g + 1) * grp_pairs]
        ssq = sum(jnp.sum(y * y, axis=-1, keepdims=True) for y in members)
        inv = lax.rsqrt(ssq / grp_width + NORM_EPS)
        for i, y in enumerate(members):
            c0 = (g * grp_pairs + i) * LANES
            o_ref[rows_c, c0:c0 + LANES] = (y * inv * nw_ref[:, c0:c0 + LANES]).astype(BF16)
    yield


def _ssd(proj, dt_raw_t, conv_w, conv_b, dt_bias, a_log, d_skip, ssd_norm_w, batch, seq):
    gs = SSD_GROUPS * SSD_STATE
    L = SSD_CHUNK
    rows = SSD_CHUNKS_PER_STEP * L
    nc = seq // rows
    row_map = lambda b, c: b * nc + c
    const = lambda b, c: (0, 0)
    t = jnp.arange(L)[:, None]
    src = jnp.arange(2 * L)[None, :]
    shift = jnp.concatenate([(src == L + t - (SSD_CONV - 1 - k)) for k in range(SSD_CONV - 1)],
                            axis=0).astype(BF16)
    triu = (jnp.arange(L)[:, None] <= jnp.arange(L)[None, :]).astype(BF16)
    lane = jnp.arange(LANES)
    sel_blocks = []
    for kind in SSD_PAIR_KINDS:
        for p in range(SSD_HEADS // 2):
            src_col = kind * SSD_HEADS + 2 * p + (lane >= SSD_HEAD_DIM)
            sel_blocks.append(jnp.arange(L)[:, None] == src_col[None, :])
    for h in range(SSD_HEADS):
        sel_blocks.append(jnp.broadcast_to(jnp.arange(L)[:, None] == SSD_HEADS + h, (L, LANES)))
    sel = jnp.tile(jnp.concatenate(sel_blocks, axis=1), (3, 1)).astype(BF16)
    per_head = lambda v: jnp.broadcast_to(v.astype(F32)[:, None], (SSD_HEADS, L))
    return pl.pallas_call(
        _ssd_kernel,
        grid=(batch, nc),
        in_specs=[
            pl.BlockSpec((rows, SSD_WIDTH), lambda b, c: (row_map(b, c), COL_XS // SSD_WIDTH)),
            pl.BlockSpec((rows, gs), lambda b, c: (row_map(b, c), COL_B // gs)),
            pl.BlockSpec((rows, gs), lambda b, c: (row_map(b, c), COL_C // gs)),
            pl.BlockSpec((rows, SSD_WIDTH), lambda b, c: (row_map(b, c), COL_Z // SSD_WIDTH)),
            pl.BlockSpec((SSD_HEADS, rows), lambda b, c: (0, row_map(b, c))),
            pl.BlockSpec(((SSD_CONV - 1) * L, 2 * L), const),
            pl.BlockSpec((L, L), const),
            pl.BlockSpec(sel.shape, const),
            pl.BlockSpec((SSD_CONV, SSD_CONV_DIM), const),
            pl.BlockSpec((1, SSD_CONV_DIM), const),
            pl.BlockSpec((SSD_HEADS, L), const),
            pl.BlockSpec((SSD_HEADS, L), const),
            pl.BlockSpec((1, SSD_WIDTH), const),
            pl.BlockSpec((1, SSD_WIDTH), const),
        ],
        out_specs=pl.BlockSpec((rows, SSD_WIDTH), lambda b, c: (row_map(b, c), 0)),
        out_shape=jax.ShapeDtypeStruct((batch * seq, SSD_WIDTH), BF16),
        scratch_shapes=[pltpu.VMEM((L + rows, SSD_CONV_DIM), BF16),
                        pltpu.VMEM((SSD_HEADS // 2, 2 * SSD_HEAD_DIM, SSD_STATE), F32)],
        compiler_params=pltpu.CompilerParams(
            dimension_semantics=("arbitrary", "arbitrary"),
            vmem_limit_bytes=32 * MIB),
        name="ssd",
    )(proj, proj, proj, proj, dt_raw_t, shift, triu, sel, conv_w, conv_b.reshape(1, SSD_CONV_DIM),
      per_head(dt_bias), per_head(a_log),
      jnp.repeat(d_skip, SSD_HEAD_DIM).reshape(1, SSD_WIDTH), ssd_norm_w.reshape(1, SSD_WIDTH))


def _mem_kv_kernel(mem_ref, nw_ref, w_ref, o_ref):
    x = mem_ref[...]
    h = (x * _rms_scale(x) * nw_ref[...]).astype(BF16)
    o_ref[...] = jnp.dot(h, w_ref[...], preferred_element_type=F32).astype(BF16)


def _mem_kv(mem2, mem_norm_w, w_kv, batch, mem_len):
    return pl.pallas_call(
        _mem_kv_kernel,
        grid=(batch,),
        in_specs=[pl.BlockSpec((mem_len, D_MODEL), lambda b: (b, 0)),
                  pl.BlockSpec((1, D_MODEL), lambda b: (0, 0)),
                  pl.BlockSpec((D_MODEL, 2 * XATTN_WIDTH), lambda b: (0, 0))],
        out_specs=pl.BlockSpec((mem_len, 2 * XATTN_WIDTH), lambda b: (b, 0)),
        out_shape=jax.ShapeDtypeStruct((batch * mem_len, 2 * XATTN_WIDTH), BF16),
        compiler_params=pltpu.CompilerParams(
            dimension_semantics=("arbitrary",), vmem_limit_bytes=32 * MIB),
        name="mem_kv",
    )(mem2, mem_norm_w.reshape(1, D_MODEL), w_kv)


def _xattn_kernel(q_ref, g_ref, kv_ref, o_ref):
    scale = XATTN_HEAD_DIM ** -0.5
    for h in range(XATTN_HEADS):
        c = slice(h * XATTN_HEAD_DIM, (h + 1) * XATTN_HEAD_DIM)
        ck = slice(h * XATTN_HEAD_DIM, (h + 1) * XATTN_HEAD_DIM)
        cv = slice(XATTN_WIDTH + h * XATTN_HEAD_DIM, XATTN_WIDTH + (h + 1) * XATTN_HEAD_DIM)
        s = lax.dot_general(q_ref[:, c], kv_ref[:, ck], NT_DIMS,
                            preferred_element_type=F32) * scale
        p = jnp.exp(s - jnp.max(s, axis=-1, keepdims=True))
        inv_l = 1.0 / jnp.sum(p, axis=-1, keepdims=True)
        xo = jnp.dot(p.astype(BF16), kv_ref[:, cv], preferred_element_type=F32) * inv_l
        o_ref[:, c] = (xo * _silu(g_ref[:, c].astype(F32))).astype(BF16)


def _xattn(proj, mkv, batch, seq, mem_len):
    nq = seq // XATT_TQ
    return pl.pallas_call(
        _xattn_kernel,
        grid=(batch, nq),
        in_specs=[pl.BlockSpec((XATT_TQ, XATTN_WIDTH), lambda b, q: (b * nq + q, COL_XQ // XATTN_WIDTH)),
                  pl.BlockSpec((XATT_TQ, XATTN_WIDTH), lambda b, q: (b * nq + q, COL_XG // XATTN_WIDTH)),
                  pl.BlockSpec((mem_len, 2 * XATTN_WIDTH), lambda b, q: (b, 0))],
        out_specs=pl.BlockSpec((XATT_TQ, XATTN_WIDTH), lambda b, q: (b * nq + q, 0)),
        out_shape=jax.ShapeDtypeStruct((batch * seq, XATTN_WIDTH), BF16),
        compiler_params=pltpu.CompilerParams(
            dimension_semantics=("arbitrary", "arbitrary"), vmem_limit_bytes=32 * MIB),
        name="xattn",
    )(proj, proj, mkv)


def _out_proj_kernel(d_ref, s_ref, a_ref, w_ref, nw_ref, x_ref, o_ref):
    for i in range(OUT_TM // OUT_SUB):
        r = slice(i * OUT_SUB, (i + 1) * OUT_SUB)
        y = jnp.dot(d_ref[r, :], w_ref[0:DIFF_WIDTH, :], preferred_element_type=F32)
        y = y + jnp.dot(s_ref[r, :], w_ref[DIFF_WIDTH:DIFF_WIDTH + SSD_WIDTH, :],
                        preferred_element_type=F32)
        y = y + jnp.dot(a_ref[r, :], w_ref[DIFF_WIDTH + SSD_WIDTH:D_MIX, :],
                        preferred_element_type=F32)
        o_ref[r, :] = x_ref[r, :] + y * _rms_scale(y) * nw_ref[...]


def _out_proj(diff_out, ssd_out, xattn_out, w_out, post_norm_w, x2):
    m = x2.shape[0]
    return pl.pallas_call(
        _out_proj_kernel,
        grid=(m // OUT_TM,),
        in_specs=[pl.BlockSpec((OUT_TM, DIFF_WIDTH), lambda i: (i, 0)),
                  pl.BlockSpec((OUT_TM, SSD_WIDTH), lambda i: (i, 0)),
                  pl.BlockSpec((OUT_TM, XATTN_WIDTH), lambda i: (i, 0)),
                  pl.BlockSpec((D_MIX, D_MODEL), lambda i: (0, 0)),
                  pl.BlockSpec((1, D_MODEL), lambda i: (0, 0)),
                  pl.BlockSpec((OUT_TM, D_MODEL), lambda i: (i, 0))],
        out_specs=pl.BlockSpec((OUT_TM, D_MODEL), lambda i: (i, 0)),
        out_shape=jax.ShapeDtypeStruct((m, D_MODEL), F32),
        compiler_params=pltpu.CompilerParams(
            dimension_semantics=("arbitrary",), vmem_limit_bytes=48 * MIB),
        name="out_proj",
    )(diff_out, ssd_out, xattn_out, w_out, post_norm_w.reshape(1, D_MODEL), x2)


def kernel(x, mem, positions, pre_norm_w, w_in, lambda_q1, lambda_k1, lambda_q2, lambda_k2,
           diff_subln_w, conv_w, conv_b, dt_bias, a_log, d_skip, ssd_norm_w, mem_norm_w,
           w_mem_kv, w_out, post_norm_w):
    batch, seq, _ = x.shape
    mem_len = mem.shape[1]
    assert pre_norm_w.shape[0] == 1, "single-layer kernel"
    x2 = x.reshape(batch * seq, D_MODEL)

    w_main, w_v, w_dt, cos, sin = _w_in_prep(jnp.transpose(w_in[0]), positions)
    lam_vecs = jnp.stack([lambda_q1[0], lambda_k1[0], lambda_q2[0], lambda_k2[0]]).astype(F32)

    proj, v_t, dt_raw_t = _in_proj(x2, pre_norm_w[0], w_main, w_v, w_dt, cos, sin)
    diff_out = _diff_attention(proj, v_t, lam_vecs, diff_subln_w[0], batch, seq)
    ssd_out = _ssd(proj, dt_raw_t, conv_w[0], conv_b[0], dt_bias[0], a_log[0], d_skip[0],
                   ssd_norm_w[0], batch, seq)
    mkv = _mem_kv(mem.reshape(batch * mem_len, D_MODEL), mem_norm_w[0],
                  w_mem_kv[0].astype(BF16), batch, mem_len)
    xattn_out = _xattn(proj, mkv, batch, seq, mem_len)
    out = _out_proj(diff_out, ssd_out, xattn_out, w_out[0].astype(BF16), post_norm_w[0], x2)
    return out.reshape(batch, seq, D_MODEL)
```

```python
import functools
import math

import jax
import jax.numpy as jnp
from jax import lax
from jax.experimental import pallas as pl
from jax.experimental.pallas import tpu as pltpu

F32 = jnp.float32
BF16 = jnp.bfloat16

D_MODEL = 2048
DIFF_HEADS = 8
DIFF_QK_DIM = 64
DIFF_V_DIM = 128
DIFF_WIDTH = DIFF_HEADS * DIFF_V_DIM
SSD_HEADS = 8
SSD_HEAD_DIM = 64
SSD_WIDTH = SSD_HEADS * SSD_HEAD_DIM
SSD_GROUPS = 2
SSD_STATE = 128
SSD_CONV = 4
SSD_CHUNK = 128
SSD_CONV_DIM = SSD_WIDTH + 2 * SSD_GROUPS * SSD_STATE
XATTN_HEADS = 4
XATTN_HEAD_DIM = 128
XATTN_WIDTH = XATTN_HEADS * XATTN_HEAD_DIM
D_MIX = DIFF_WIDTH + SSD_WIDTH + XATTN_WIDTH
ROPE_THETA = 10000.0
NORM_EPS = 1e-6
LAMBDA_INIT = 0.8 - 0.6 * math.exp(-0.3 * 0)

LANES = 128
SUBLANES = 8
MIB = 1024 * 1024

COL_Q = 0
COL_K = COL_Q + DIFF_WIDTH
COL_G = COL_K + DIFF_WIDTH
COL_Z = COL_G + DIFF_WIDTH
COL_XS = COL_Z + SSD_WIDTH
COL_B = COL_XS + SSD_WIDTH
COL_C = COL_B + SSD_GROUPS * SSD_STATE
COL_XQ = COL_C + SSD_GROUPS * SSD_STATE
COL_XG = COL_XQ + XATTN_WIDTH
N_MAIN = COL_XG + XATTN_WIDTH
REF_V = 2 * DIFF_WIDTH
REF_G = REF_V + DIFF_WIDTH
REF_DT = REF_G + DIFF_WIDTH + SSD_WIDTH + SSD_CONV_DIM
REF_XQ = REF_DT + SSD_HEADS
D_IN = REF_XQ + 2 * XATTN_WIDTH

PROJ_TM = 512
PROJ_TN = 512
NORM_ROWS = 256
ATT_TQ = 512
ATT_TK = 512
ATT_HEADS_PER_STEP = 4
ATT_SUM_ROWS = 16
ATT_AHEAD = 2
XATT_TQ = 512
OUT_TM = 512
OUT_SUB = 256
ROPE_TM = 512
SSD_CHUNKS_PER_STEP = 4
DT_ROWS = 16

NT_DIMS = (((1,), (1,)), ((), ()))
TN_DIMS = (((0,), (0,)), ((), ()))


def _rms_scale(x):
    return lax.rsqrt(jnp.mean(x * x, axis=-1, keepdims=True) + NORM_EPS)


def _silu(x):
    return x * jax.nn.sigmoid(x)


def _rope_table_rows(pos_ref, invf_ref, cos_ref, sin_ref):
    ang = pos_ref[...].astype(F32) * invf_ref[...]
    lane = lax.broadcasted_iota(jnp.int32, ang.shape, 1)
    first_half = (lane % DIFF_QK_DIM) < (DIFF_QK_DIM // 2)
    s = jnp.sin(ang)
    cos_ref[...] = jnp.cos(ang)
    sin_ref[...] = jnp.where(first_half, -s, s)


def _rope(x, cos, sin_signed):
    lane = lax.broadcasted_iota(jnp.int32, x.shape, 1)
    first_half = (lane % DIFF_QK_DIM) < (DIFF_QK_DIM // 2)
    half = DIFF_QK_DIM // 2
    rot = jnp.where(first_half, pltpu.roll(x, LANES - half, 1), pltpu.roll(x, half, 1))
    return x * cos + rot * sin_signed


def _w_in_prep_kernel(wt_ref, dt_rows_ref, pos_ref, invf_ref,
                      wm_ref, wv_ref, wdt_ref, cos_ref, sin_ref):
    j = pl.program_id(0)
    main_blocks = N_MAIN // PROJ_TN
    _rope_table_rows(pos_ref, invf_ref, cos_ref, sin_ref)

    @pl.when(j < main_blocks)
    def _():
        wm_ref[...] = wt_ref[...].T.astype(BF16)

    @pl.when((j >= main_blocks) & (j < main_blocks + DIFF_WIDTH // PROJ_TN))
    def _():
        wv_ref[...] = wt_ref[...].astype(BF16)

    @pl.when(j == 0)
    def _():
        row = lax.broadcasted_iota(jnp.int32, (DT_ROWS, D_MODEL), 0)
        wdt_ref[...] = jnp.where(row < SSD_HEADS, dt_rows_ref[...], 0.0).astype(BF16)


def _w_in_prep(w_t, positions):
    n, k = w_t.shape
    assert n == D_IN and k == D_MODEL
    main_blocks = N_MAIN // PROJ_TN
    v_blocks = DIFF_WIDTH // PROJ_TN
    last_w = main_blocks + v_blocks - 1
    n_pos = positions.size
    steps = n_pos // ROPE_TM
    assert steps > last_w
    inv_freq = 1.0 / (ROPE_THETA ** (jnp.arange(0, DIFF_QK_DIM, 2, dtype=F32) / DIFF_QK_DIM))
    invf = jnp.tile(inv_freq, LANES // inv_freq.shape[0]).reshape(1, LANES)

    def src_row(j):
        main = jnp.where(j < REF_V // PROJ_TN, j * PROJ_TN,
                         jnp.where(j < (REF_DT - DIFF_WIDTH) // PROJ_TN,
                                   j * PROJ_TN + DIFF_WIDTH,
                                   j * PROJ_TN + DIFF_WIDTH + SSD_HEADS))
        jw = jnp.minimum(j, last_w)
        return pl.multiple_of(jnp.where(jw < main_blocks, main,
                                        REF_V + (jw - main_blocks) * PROJ_TN), SUBLANES)

    return pl.pallas_call(
        _w_in_prep_kernel,
        grid=(steps,),
        in_specs=[pl.BlockSpec((pl.Element(PROJ_TN), pl.Element(D_MODEL)),
                               lambda j: (src_row(j), 0)),
                  pl.BlockSpec((DT_ROWS, D_MODEL), lambda j: (REF_DT // DT_ROWS, 0)),
                  pl.BlockSpec((ROPE_TM, 1), lambda j: (j, 0)),
                  pl.BlockSpec((1, LANES), lambda j: (0, 0))],
        out_specs=[pl.BlockSpec((D_MODEL, PROJ_TN),
                                lambda j: (0, jnp.minimum(j, main_blocks - 1))),
                   pl.BlockSpec((PROJ_TN, D_MODEL),
                                lambda j: (jnp.clip(j - main_blocks, 0, v_blocks - 1), 0)),
                   pl.BlockSpec((DT_ROWS, D_MODEL), lambda j: (0, 0)),
                   pl.BlockSpec((ROPE_TM, LANES), lambda j: (j, 0)),
                   pl.BlockSpec((ROPE_TM, LANES), lambda j: (j, 0))],
        out_shape=[jax.ShapeDtypeStruct((k, N_MAIN), BF16),
                   jax.ShapeDtypeStruct((DIFF_WIDTH, k), BF16),
                   jax.ShapeDtypeStruct((DT_ROWS, k), BF16),
                   jax.ShapeDtypeStruct((n_pos, LANES), F32),
                   jax.ShapeDtypeStruct((n_pos, LANES), F32)],
        compiler_params=pltpu.CompilerParams(
            dimension_semantics=("arbitrary",), vmem_limit_bytes=32 * MIB),
        name="w_in_prep",
    )(w_t, w_t, positions.reshape(n_pos, 1), invf)


def _in_proj_kernel(x_ref, nw_ref, w_ref, wv_ref, wdt_ref, cos_ref, sin_ref,
                    o_ref, vt_ref, dt_ref, h_ref):
    def norm_rows(i, carry):
        r = pl.ds(pl.multiple_of(i * NORM_ROWS, NORM_ROWS), NORM_ROWS)
        x = x_ref[r, :]
        h_ref[r, :] = (x * _rms_scale(x) * nw_ref[...]).astype(BF16)
        return carry
    lax.fori_loop(0, PROJ_TM // NORM_ROWS, norm_rows, 0)
    dt_t = lax.dot_general(wdt_ref[...], h_ref[...], NT_DIMS, preferred_element_type=F32)
    dt_ref[...] = dt_t[0:SSD_HEADS, :]
    vt_ref[...] = lax.dot_general(wv_ref[...], h_ref[...], NT_DIMS,
                                  preferred_element_type=F32).astype(BF16)
    q_scale = DIFF_QK_DIM ** -0.5 * math.log2(math.e)
    for n in range(N_MAIN // PROJ_TN):
        c0 = n * PROJ_TN
        y = jnp.dot(h_ref[...], w_ref[:, c0:c0 + PROJ_TN], preferred_element_type=F32)
        if c0 < COL_G:
            scale = q_scale if c0 < COL_K else 1.0
            for hd in range(PROJ_TN // LANES):
                y_h = _rope(y[:, hd * LANES:(hd + 1) * LANES], cos_ref[...], sin_ref[...])
                o_ref[:, c0 + hd * LANES:c0 + (hd + 1) * LANES] = (y_h * scale).astype(BF16)
        else:
            o_ref[:, c0:c0 + PROJ_TN] = y.astype(BF16)


def _in_proj(x2, pre_norm_w, w_main, w_v, w_dt, cos, sin):
    m = x2.shape[0]
    resident = pl.Buffered(1)
    return pl.pallas_call(
        _in_proj_kernel,
        grid=(m // PROJ_TM,),
        in_specs=[pl.BlockSpec((PROJ_TM, D_MODEL), lambda i: (i, 0)),
                  pl.BlockSpec((1, D_MODEL), lambda i: (0, 0)),
                  pl.BlockSpec((D_MODEL, N_MAIN), lambda i: (0, 0), pipeline_mode=resident),
                  pl.BlockSpec((DIFF_WIDTH, D_MODEL), lambda i: (0, 0), pipeline_mode=resident),
                  pl.BlockSpec((DT_ROWS, D_MODEL), lambda i: (0, 0), pipeline_mode=resident),
                  pl.BlockSpec((PROJ_TM, LANES), lambda i: (i, 0)),
                  pl.BlockSpec((PROJ_TM, LANES), lambda i: (i, 0))],
        out_specs=[pl.BlockSpec((PROJ_TM, N_MAIN), lambda i: (i, 0)),
                   pl.BlockSpec((DIFF_WIDTH, PROJ_TM), lambda i: (0, i)),
                   pl.BlockSpec((SSD_HEADS, PROJ_TM), lambda i: (0, i))],
        out_shape=[jax.ShapeDtypeStruct((m, N_MAIN), BF16),
                   jax.ShapeDtypeStruct((DIFF_WIDTH, m), BF16),
                   jax.ShapeDtypeStruct((SSD_HEADS, m), F32)],
        scratch_shapes=[pltpu.VMEM((PROJ_TM, D_MODEL), BF16)],
        compiler_params=pltpu.CompilerParams(
            dimension_semantics=("arbitrary",),
            vmem_limit_bytes=56 * MIB),
        name="in_proj",
    )(x2, pre_norm_w.reshape(1, D_MODEL), w_main, w_v, w_dt, cos, sin)


def _diff_attn_kernel(lam_ref, q_ref, k_ref, v_ref, g_ref, subw_ref, o_ref, vt_ref, *, seq):
    qi = pl.program_id(2)
    heads = ATT_HEADS_PER_STEP

    @pl.when(qi == 0)
    def _():
        for g in range(heads):
            vt_ref[g, 0:DIFF_V_DIM, :] = v_ref[g * DIFF_V_DIM:(g + 1) * DIFF_V_DIM, :]
        vt_ref[:, DIFF_V_DIM:, :] = jnp.ones((heads, ATT_SUM_ROWS, seq), BF16)

    lane = lax.broadcasted_iota(jnp.int32, (ATT_TQ, LANES), 1)
    q_maps = []
    for g in range(heads):
        q = q_ref[:, g * LANES:(g + 1) * LANES]
        q_maps.append((jnp.where(lane < DIFF_QK_DIM, q, jnp.zeros_like(q)),
                       jnp.where(lane >= DIFF_QK_DIM, q, jnp.zeros_like(q))))

    lv = lam_ref[...]
    lam = (jnp.exp(jnp.sum(lv[0:1] * lv[1:2], axis=-1, keepdims=True))
           - jnp.exp(jnp.sum(lv[2:3] * lv[3:4], axis=-1, keepdims=True)) + LAMBDA_INIT)

    def run(n_tiles):
        diag = (n_tiles - 1) * ATT_TK
        half = ATT_TK // 2
        blocks = [(j * ATT_TK, (j + 1) * ATT_TK, 0, False) for j in range(n_tiles - 1)]
        blocks += [(diag, diag + half, 0, True), (diag + half, diag + ATT_TK, half, True)]
        items = [(blk, g, c) for blk in blocks for g in range(heads) for c in range(2)]

        def scores(blk, g, c):
            k0, k1, q0, masked = blk
            kt = k_ref[k0:k1, g * LANES:(g + 1) * LANES]
            s = lax.dot_general(kt, q_maps[g][c][q0:, :], NT_DIMS, preferred_element_type=F32)
            if not masked:
                return s
            key = lax.broadcasted_iota(jnp.int32, s.shape, 0) + (k0 - diag)
            qry = lax.broadcasted_iota(jnp.int32, s.shape, 1) + q0
            return jnp.where(key <= qry, s, -jnp.inf)

        pending = {t: scores(*items[t]) for t in range(min(ATT_AHEAD, len(items)))}
        m, acc = {}, {}
        for t, (blk, g, c) in enumerate(items):
            k0, k1, q0, _ = blk
            i = 2 * g + c
            s = pending.pop(t)
            m_new = jnp.max(s, axis=0, keepdims=True)
            first = i not in m
            if not first:
                m_old = m[i][:, q0:]
                m_new = jnp.maximum(m_old, m_new)
            p = jnp.exp2(s - m_new).astype(BF16)
            if t + ATT_AHEAD < len(items):
                pending[t + ATT_AHEAD] = scores(*items[t + ATT_AHEAD])
            pv = jnp.dot(vt_ref[g, :, k0:k1], p, preferred_element_type=F32)
            if first:
                m[i], acc[i] = m_new, pv
            else:
                new_acc = jnp.exp2(m_old - m_new) * acc[i][:, q0:] + pv
                if q0:
                    m_new = jnp.concatenate([m[i][:, :q0], m_new], axis=1)
                    new_acc = jnp.concatenate([acc[i][:, :q0], new_acc], axis=1)
                m[i], acc[i] = m_new, new_acc

        for g in range(heads):
            c = slice(g * LANES, (g + 1) * LANES)
            a1, a2 = acc[2 * g], acc[2 * g + 1]
            l1 = a1[DIFF_V_DIM:DIFF_V_DIM + 1, :]
            l2 = a2[DIFF_V_DIM:DIFF_V_DIM + 1, :]
            o_t = (a1[0:DIFF_V_DIM, :] * (1.0 / l1)
                   - lam * (a2[0:DIFF_V_DIM, :] * (1.0 / l2)))
            o = o_t.T
            o = o * _rms_scale(o) * subw_ref[...] * (1.0 - LAMBDA_INIT)
            o_ref[:, c] = (o * _silu(g_ref[:, c].astype(F32))).astype(BF16)

    for n in range(seq // ATT_TQ):
        pl.when(qi == n)(functools.partial(run, n + 1))


def _diff_attention(proj, v_t, lam_vecs, subln_w, batch, seq):
    assert ATT_TQ == ATT_TK and seq % ATT_TQ == 0 and DIFF_V_DIM == LANES
    nq = seq // ATT_TQ
    heads = ATT_HEADS_PER_STEP
    width = heads * LANES
    kernel = functools.partial(_diff_attn_kernel, seq=seq)
    return pl.pallas_call(
        kernel,
        grid=(batch, DIFF_HEADS // heads, nq),
        in_specs=[
            pl.BlockSpec((4, DIFF_QK_DIM), lambda b, h, q: (0, 0)),
            pl.BlockSpec((ATT_TQ, width), lambda b, h, q: (b * nq + q, COL_Q // width + h)),
            pl.BlockSpec((seq, width), lambda b, h, q: (b, COL_K // width + h)),
            pl.BlockSpec((width, seq), lambda b, h, q: (h, b)),
            pl.BlockSpec((ATT_TQ, width), lambda b, h, q: (b * nq + q, COL_G // width + h)),
            pl.BlockSpec((1, DIFF_V_DIM), lambda b, h, q: (0, 0)),
        ],
        out_specs=pl.BlockSpec((ATT_TQ, width), lambda b, h, q: (b * nq + q, h)),
        out_shape=jax.ShapeDtypeStruct((batch * seq, DIFF_WIDTH), BF16),
        scratch_shapes=[pltpu.VMEM((heads, DIFF_V_DIM + ATT_SUM_ROWS, seq), BF16)],
        compiler_params=pltpu.CompilerParams(
            dimension_semantics=("arbitrary", "arbitrary", "arbitrary"),
            vmem_limit_bytes=48 * MIB),
        name="diff_attn",
    )(lam_vecs, proj, proj, v_t, proj, subln_w.reshape(1, DIFF_V_DIM))


def _ssd_kernel(xs_ref, b_ref, c_ref, z_ref, dt_ref, shift_ref, triu_ref, sel_ref, cw_ref,
                cb_ref, dtb_ref, alog_ref, dsk_ref, nw_ref, o_ref, xext_ref, state_ref):
    L = SSD_CHUNK
    T = SSD_CHUNKS_PER_STEP
    gs = SSD_GROUPS * SSD_STATE
    pairs = SSD_HEADS // 2

    @pl.when(pl.program_id(1) == 0)
    def _():
        xext_ref[0:L, :] = jnp.zeros((L, SSD_CONV_DIM), BF16)
        state_ref[...] = jnp.zeros(state_ref.shape, F32)

    xext_ref[L:(T + 1) * L, 0:SSD_WIDTH] = xs_ref[...]
    xext_ref[L:(T + 1) * L, SSD_WIDTH:SSD_WIDTH + gs] = b_ref[...]
    xext_ref[L:(T + 1) * L, SSD_WIDTH + gs:SSD_CONV_DIM] = c_ref[...]

    carry = {"state": [state_ref[p] for p in range(pairs)], "cols_split": [None] * T}
    programs = [_ssd_chunk(c, carry, xext_ref, z_ref, dt_ref, shift_ref, triu_ref, cw_ref,
                           cb_ref, dtb_ref, alog_ref, dsk_ref, nw_ref, o_ref)
                for c in range(T)]

    def stage():
        for prog in programs:
            next(prog)

    stage()
    stage()
    carry["spread"] = jnp.dot(jnp.concatenate(carry["cols_split"], axis=0), sel_ref[...],
                              preferred_element_type=F32)
    stage()
    stage()
    stage()
    for p in range(pairs):
        state_ref[p] = carry["state"][p]
    xext_ref[0:L, :] = xext_ref[T * L:(T + 1) * L, :]


SSD_PAIR_KINDS = (0, 2, 3)


def _ssd_chunk(c, carry, xext_ref, z_ref, dt_ref, shift_ref, triu_ref, cw_ref,
               cb_ref, dtb_ref, alog_ref, dsk_ref, nw_ref, o_ref):
    L = SSD_CHUNK
    H = SSD_HEADS
    gs = SSD_GROUPS * SSD_STATE
    rows_c = slice(c * L, (c + 1) * L)

    delayed = jnp.dot(shift_ref[...], xext_ref[c * L:(c + 2) * L, :],
                      preferred_element_type=F32)
    conv = (cb_ref[...] + xext_ref[(c + 1) * L:(c + 2) * L, :].astype(F32)
            * cw_ref[SSD_CONV - 1:SSD_CONV, :])
    for k in range(SSD_CONV - 1):
        conv = conv + delayed[k * L:(k + 1) * L, :] * cw_ref[k:k + 1, :]
    xbc = _silu(conv)
    xs = xbc[:, 0:SSD_WIDTH]
    bm = xbc[:, SSD_WIDTH:SSD_WIDTH + gs].astype(BF16)
    cm = xbc[:, SSD_WIDTH + gs:SSD_CONV_DIM].astype(BF16)
    yield

    dt_in = dt_ref[:, rows_c] + dtb_ref[...]
    dt = jnp.maximum(dt_in, 0.0) + jnp.log1p(jnp.exp(-jnp.abs(dt_in)))
    a_dt = dt * (-jnp.exp(alog_ref[...]))
    hi = a_dt.astype(BF16)
    r1 = a_dt - hi.astype(F32)
    mid = r1.astype(BF16)
    lo = (r1 - mid.astype(F32)).astype(BF16)
    triu = triu_ref[...]
    a_cs = (jnp.dot(hi, triu, preferred_element_type=F32)
            + jnp.dot(mid, triu, preferred_element_type=F32)
            + jnp.dot(lo, triu, preferred_element_type=F32))
    a_end = a_cs[:, L - 1:L]
    chunk_decay = jnp.exp(a_end)
    rows = jnp.concatenate([dt, a_cs, jnp.exp(a_cs), jnp.exp(a_end - a_cs),
                            jnp.zeros((L - 4 * H, L), F32)], axis=0)
    cols = rows.T
    c_hi = cols.astype(BF16)
    c_r1 = cols - c_hi.astype(F32)
    c_mid = c_r1.astype(BF16)
    c_lo = (c_r1 - c_mid.astype(F32)).astype(BF16)
    carry["cols_split"][c] = jnp.concatenate([c_hi, c_mid, c_lo], axis=1)
    yield
    spread = carry["spread"][rows_c, :]

    row = lax.broadcasted_iota(jnp.int32, (L, L), 0)
    col = lax.broadcasted_iota(jnp.int32, (L, L), 1)
    lower = row >= col
    lane = lax.broadcasted_iota(jnp.int32, (L, LANES), 1)
    left = lane < SSD_HEAD_DIM
    top = row < SSD_HEAD_DIM
    pairs = SSD_HEADS // 2

    def pair_lanes(slot, p):
        i = slot * pairs + p
        return spread[:, i * LANES:(i + 1) * LANES]

    def head_lanes(h):
        i = len(SSD_PAIR_KINDS) * pairs + h
        return spread[:, i * LANES:(i + 1) * LANES]

    group_cb = []
    for g in range(SSD_GROUPS):
        group_cb.append(lax.dot_general(cm[:, g * SSD_STATE:(g + 1) * SSD_STATE],
                                        bm[:, g * SSD_STATE:(g + 1) * SSD_STATE],
                                        NT_DIMS, preferred_element_type=F32))
    yield

    group_of = lambda p: (2 * p) // (SSD_HEADS // SSD_GROUPS)
    y_diag, chunk_state = [], []
    for p in range(pairs):
        bg = bm[:, group_of(p) * SSD_STATE:(group_of(p) + 1) * SSD_STATE]
        x_dt = xs[:, p * LANES:(p + 1) * LANES] * pair_lanes(0, p)
        x_dt_b = x_dt.astype(BF16)
        y_heads = []
        for hh in range(2):
            h = 2 * p + hh
            seg = head_lanes(h) - a_cs[h:h + 1, :]
            decay = jnp.exp(jnp.where(lower, seg, -jnp.inf))
            scores = (group_cb[group_of(p)] * decay).astype(BF16)
            y_heads.append(jnp.dot(scores, x_dt_b, preferred_element_type=F32))
        y_diag.append(jnp.where(left, y_heads[0], y_heads[1]))
        x_w = (x_dt * pair_lanes(2, p)).astype(BF16)
        chunk_state.append(lax.dot_general(x_w, bg, TN_DIMS, preferred_element_type=F32))
    yield

    ys, new_state = [], []
    for p in range(pairs):
        cg = cm[:, group_of(p) * SSD_STATE:(group_of(p) + 1) * SSD_STATE]
        st = carry["state"][p]
        y_off = (lax.dot_general(cg, st.astype(BF16), NT_DIMS, preferred_element_type=F32)
                 * pair_lanes(1, p))
        st_decay = jnp.where(top, chunk_decay[2 * p:2 * p + 1, :],
                             chunk_decay[2 * p + 1:2 * p + 2, :])
        new_state.append(st * st_decay + chunk_state[p])
        y = (y_diag[p] + y_off
             + xs[:, p * LANES:(p + 1) * LANES] * dsk_ref[:, p * LANES:(p + 1) * LANES])
        ys.append(y * _silu(z_ref[rows_c, p * LANES:(p + 1) * LANES].astype(F32)))
    carry["state"] = new_state

    grp_pairs = pairs // SSD_GROUPS
    grp_width = SSD_WIDTH // SSD_GROUPS
    for g in range(SSD_GROUPS):
        members = ys[g * grp_pairs:(g + 1) * grp_pairs]
        ssq = sum(jnp.sum(y * y, axis=-1, keepdims=True) for y in members)
        inv = lax.rsqrt(ssq / grp_width + NORM_EPS)
        for i, y in enumerate(members):
            c0 = (g * grp_pairs + i) * LANES
            o_ref[rows_c, c0:c0 + LANES] = (y * inv * nw_ref[:, c0:c0 + LANES]).astype(BF16)
    yield


def _ssd(proj, dt_raw_t, conv_w, conv_b, dt_bias, a_log, d_skip, ssd_norm_w, batch, seq):
    gs = SSD_GROUPS * SSD_STATE
    L = SSD_CHUNK
    rows = SSD_CHUNKS_PER_STEP * L
    nc = seq // rows
    row_map = lambda b, c: b * nc + c
    const = lambda b, c: (0, 0)
    t = jnp.arange(L)[:, None]
    src = jnp.arange(2 * L)[None, :]
    shift = jnp.concatenate([(src == L + t - (SSD_CONV - 1 - k)) for k in range(SSD_CONV - 1)],
                            axis=0).astype(BF16)
    triu = (jnp.arange(L)[:, None] <= jnp.arange(L)[None, :]).astype(BF16)
    lane = jnp.arange(LANES)
    sel_blocks = []
    for kind in SSD_PAIR_KINDS:
        for p in range(SSD_HEADS // 2):
            src_col = kind * SSD_HEADS + 2 * p + (lane >= SSD_HEAD_DIM)
            sel_blocks.append(jnp.arange(L)[:, None] == src_col[None, :])
    for h in range(SSD_HEADS):
        sel_blocks.append(jnp.broadcast_to(jnp.arange(L)[:, None] == SSD_HEADS + h, (L, LANES)))
    sel = jnp.tile(jnp.concatenate(sel_blocks, axis=1), (3, 1)).astype(BF16)
    per_head = lambda v: jnp.broadcast_to(v.astype(F32)[:, None], (SSD_HEADS, L))
    return pl.pallas_call(
        _ssd_kernel,
        grid=(batch, nc),
        in_specs=[
            pl.BlockSpec((rows, SSD_WIDTH), lambda b, c: (row_map(b, c), COL_XS // SSD_WIDTH)),
            pl.BlockSpec((rows, gs), lambda b, c: (row_map(b, c), COL_B // gs)),
            pl.BlockSpec((rows, gs), lambda b, c: (row_map(b, c), COL_C // gs)),
            pl.BlockSpec((rows, SSD_WIDTH), lambda b, c: (row_map(b, c), COL_Z // SSD_WIDTH)),
            pl.BlockSpec((SSD_HEADS, rows), lambda b, c: (0, row_map(b, c))),
            pl.BlockSpec(((SSD_CONV - 1) * L, 2 * L), const),
            pl.BlockSpec((L, L), const),
            pl.BlockSpec(sel.shape, const),
            pl.BlockSpec((SSD_CONV, SSD_CONV_DIM), const),
            pl.BlockSpec((1, SSD_CONV_DIM), const),
            pl.BlockSpec((SSD_HEADS, L), const),
            pl.BlockSpec((SSD_HEADS, L), const),
            pl.BlockSpec((1, SSD_WIDTH), const),
            pl.BlockSpec((1, SSD_WIDTH), const),
        ],
        out_specs=pl.BlockSpec((rows, SSD_WIDTH), lambda b, c: (row_map(b, c), 0)),
        out_shape=jax.ShapeDtypeStruct((batch * seq, SSD_WIDTH), BF16),
        scratch_shapes=[pltpu.VMEM((L + rows, SSD_CONV_DIM), BF16),
                        pltpu.VMEM((SSD_HEADS // 2, 2 * SSD_HEAD_DIM, SSD_STATE), F32)],
        compiler_params=pltpu.CompilerParams(
            dimension_semantics=("arbitrary", "arbitrary"),
            vmem_limit_bytes=32 * MIB),
        name="ssd",
    )(proj, proj, proj, proj, dt_raw_t, shift, triu, sel, conv_w, conv_b.reshape(1, SSD_CONV_DIM),
      per_head(dt_bias), per_head(a_log),
      jnp.repeat(d_skip, SSD_HEAD_DIM).reshape(1, SSD_WIDTH), ssd_norm_w.reshape(1, SSD_WIDTH))


XATT_V_ROWS = XATTN_HEAD_DIM + ATT_SUM_ROWS


def _mem_kv_kernel(mem_ref, nw_ref, w_ref, k_ref, vt_ref, wb_ref):
    @pl.when(pl.program_id(0) == 0)
    def _():
        wb_ref[...] = w_ref[...].astype(BF16)

    x = mem_ref[...]
    h = (x * _rms_scale(x) * nw_ref[...]).astype(BF16)
    kv = jnp.dot(h, wb_ref[...], preferred_element_type=F32)
    k_ref[...] = kv[:, 0:XATTN_WIDTH].astype(BF16)
    ones = jnp.ones((ATT_SUM_ROWS, x.shape[0]), BF16)
    for hd in range(XATTN_HEADS):
        c0 = XATTN_WIDTH + hd * XATTN_HEAD_DIM
        r0 = hd * XATT_V_ROWS
        vt_ref[r0:r0 + XATTN_HEAD_DIM, :] = kv[:, c0:c0 + XATTN_HEAD_DIM].T.astype(BF16)
        vt_ref[r0 + XATTN_HEAD_DIM:r0 + XATT_V_ROWS, :] = ones


def _mem_kv(mem2, mem_norm_w, w_kv, batch, mem_len):
    return pl.pallas_call(
        _mem_kv_kernel,
        grid=(batch,),
        in_specs=[pl.BlockSpec((mem_len, D_MODEL), lambda b: (b, 0)),
                  pl.BlockSpec((1, D_MODEL), lambda b: (0, 0)),
                  pl.BlockSpec((D_MODEL, 2 * XATTN_WIDTH), lambda b: (0, 0),
                               pipeline_mode=pl.Buffered(1))],
        out_specs=[pl.BlockSpec((mem_len, XATTN_WIDTH), lambda b: (b, 0)),
                   pl.BlockSpec((XATTN_HEADS * XATT_V_ROWS, mem_len), lambda b: (b, 0))],
        out_shape=[jax.ShapeDtypeStruct((batch * mem_len, XATTN_WIDTH), BF16),
                   jax.ShapeDtypeStruct((batch * XATTN_HEADS * XATT_V_ROWS, mem_len), BF16)],
        scratch_shapes=[pltpu.VMEM((D_MODEL, 2 * XATTN_WIDTH), BF16)],
        compiler_params=pltpu.CompilerParams(
            dimension_semantics=("arbitrary",), vmem_limit_bytes=32 * MIB),
        name="mem_kv",
    )(mem2, mem_norm_w.reshape(1, D_MODEL), w_kv)


def _xattn_kernel(q_ref, g_ref, k_ref, vt_ref, o_ref):
    scale = XATTN_HEAD_DIM ** -0.5 * math.log2(math.e)
    cols = [slice(h * XATTN_HEAD_DIM, (h + 1) * XATTN_HEAD_DIM) for h in range(XATTN_HEADS)]
    scores = [lax.dot_general(k_ref[:, c], q_ref[:, c], NT_DIMS, preferred_element_type=F32)
              for c in cols]
    for h, c in enumerate(cols):
        s = scores[h] * scale
        p = jnp.exp2(s - jnp.max(s, axis=0, keepdims=True)).astype(BF16)
        pv = jnp.dot(vt_ref[h * XATT_V_ROWS:(h + 1) * XATT_V_ROWS, :], p,
                     preferred_element_type=F32)
        xo = (pv[0:XATTN_HEAD_DIM, :]
              * (1.0 / pv[XATTN_HEAD_DIM:XATTN_HEAD_DIM + 1, :])).T
        o_ref[:, c] = (xo * _silu(g_ref[:, c].astype(F32))).astype(BF16)


def _xattn(proj, mk, mvt, batch, seq, mem_len):
    nq = seq // XATT_TQ
    return pl.pallas_call(
        _xattn_kernel,
        grid=(batch, nq),
        in_specs=[pl.BlockSpec((XATT_TQ, XATTN_WIDTH), lambda b, q: (b * nq + q, COL_XQ // XATTN_WIDTH)),
                  pl.BlockSpec((XATT_TQ, XATTN_WIDTH), lambda b, q: (b * nq + q, COL_XG // XATTN_WIDTH)),
                  pl.BlockSpec((mem_len, XATTN_WIDTH), lambda b, q: (b, 0)),
                  pl.BlockSpec((XATTN_HEADS * XATT_V_ROWS, mem_len), lambda b, q: (b, 0))],
        out_specs=pl.BlockSpec((XATT_TQ, XATTN_WIDTH), lambda b, q: (b * nq + q, 0)),
        out_shape=jax.ShapeDtypeStruct((batch * seq, XATTN_WIDTH), BF16),
        compiler_params=pltpu.CompilerParams(
            dimension_semantics=("arbitrary", "arbitrary"), vmem_limit_bytes=32 * MIB),
        name="xattn",
    )(proj, proj, mk, mvt)


def _out_proj_kernel(d_ref, s_ref, a_ref, w_ref, nw_ref, x_ref, o_ref, wb_ref):
    @pl.when(pl.program_id(0) == 0)
    def _():
        for k0 in range(0, D_MIX, OUT_SUB):
            wb_ref[k0:k0 + OUT_SUB, :] = w_ref[k0:k0 + OUT_SUB, :].astype(BF16)

    for i in range(OUT_TM // OUT_SUB):
        r = slice(i * OUT_SUB, (i + 1) * OUT_SUB)
        y = jnp.dot(d_ref[r, :], wb_ref[0:DIFF_WIDTH, :], preferred_element_type=F32)
        y = y + jnp.dot(s_ref[r, :], wb_ref[DIFF_WIDTH:DIFF_WIDTH + SSD_WIDTH, :],
                        preferred_element_type=F32)
        y = y + jnp.dot(a_ref[r, :], wb_ref[DIFF_WIDTH + SSD_WIDTH:D_MIX, :],
                        preferred_element_type=F32)
        o_ref[r, :] = x_ref[r, :] + y * _rms_scale(y) * nw_ref[...]


def _out_proj(diff_out, ssd_out, xattn_out, w_out, post_norm_w, x2):
    m = x2.shape[0]
    return pl.pallas_call(
        _out_proj_kernel,
        grid=(m // OUT_TM,),
        in_specs=[pl.BlockSpec((OUT_TM, DIFF_WIDTH), lambda i: (i, 0)),
                  pl.BlockSpec((OUT_TM, SSD_WIDTH), lambda i: (i, 0)),
                  pl.BlockSpec((OUT_TM, XATTN_WIDTH), lambda i: (i, 0)),
                  pl.BlockSpec((D_MIX, D_MODEL), lambda i: (0, 0), pipeline_mode=pl.Buffered(1)),
                  pl.BlockSpec((1, D_MODEL), lambda i: (0, 0)),
                  pl.BlockSpec((OUT_TM, D_MODEL), lambda i: (i, 0))],
        out_specs=pl.BlockSpec((OUT_TM, D_MODEL), lambda i: (i, 0)),
        out_shape=jax.ShapeDtypeStruct((m, D_MODEL), F32),
        scratch_shapes=[pltpu.VMEM((D_MIX, D_MODEL), BF16)],
        compiler_params=pltpu.CompilerParams(
            dimension_semantics=("arbitrary",), vmem_limit_bytes=48 * MIB),
        name="out_proj",
    )(diff_out, ssd_out, xattn_out, w_out, post_norm_w.reshape(1, D_MODEL), x2)


def kernel(x, mem, positions, pre_norm_w, w_in, lambda_q1, lambda_k1, lambda_q2, lambda_k2,
           diff_subln_w, conv_w, conv_b, dt_bias, a_log, d_skip, ssd_norm_w, mem_norm_w,
           w_mem_kv, w_out, post_norm_w):
    batch, seq, _ = x.shape
    mem_len = mem.shape[1]
    assert pre_norm_w.shape[0] == 1, "single-layer kernel"
    x2 = x.reshape(batch * seq, D_MODEL)

    w_main, w_v, w_dt, cos, sin = _w_in_prep(jnp.transpose(w_in[0]), positions)
    lam_vecs = jnp.stack([lambda_q1[0], lambda_k1[0], lambda_q2[0], lambda_k2[0]]).astype(F32)

    proj, v_t, dt_raw_t = _in_proj(x2, pre_norm_w[0], w_main, w_v, w_dt, cos, sin)
    diff_out = _diff_attention(proj, v_t, lam_vecs, diff_subln_w[0], batch, seq)
    ssd_out = _ssd(proj, dt_raw_t, conv_w[0], conv_b[0], dt_bias[0], a_log[0], d_skip[0],
                   ssd_norm_w[0], batch, seq)
    mk, mvt = _mem_kv(mem.reshape(batch * mem_len, D_MODEL), mem_norm_w[0], w_mem_kv[0],
                      batch, mem_len)
    xattn_out = _xattn(proj, mk, mvt, batch, seq, mem_len)
    out = _out_proj(diff_out, ssd_out, xattn_out, w_out[0], post_norm_w[0], x2)
    return out.reshape(batch, seq, D_MODEL)
```

```python
import functools
import math

import jax
import jax.numpy as jnp
from jax import lax
from jax.experimental import pallas as pl
from jax.experimental.pallas import tpu as pltpu

F32 = jnp.float32
BF16 = jnp.bfloat16

D_MODEL = 2048
DIFF_HEADS = 8
DIFF_QK_DIM = 64
DIFF_V_DIM = 128
DIFF_WIDTH = DIFF_HEADS * DIFF_V_DIM
SSD_HEADS = 8
SSD_HEAD_DIM = 64
SSD_WIDTH = SSD_HEADS * SSD_HEAD_DIM
SSD_GROUPS = 2
SSD_STATE = 128
SSD_CONV = 4
SSD_CHUNK = 128
SSD_CONV_DIM = SSD_WIDTH + 2 * SSD_GROUPS * SSD_STATE
XATTN_HEADS = 4
XATTN_HEAD_DIM = 128
XATTN_WIDTH = XATTN_HEADS * XATTN_HEAD_DIM
D_MIX = DIFF_WIDTH + SSD_WIDTH + XATTN_WIDTH
ROPE_THETA = 10000.0
NORM_EPS = 1e-6
LAMBDA_INIT = 0.8 - 0.6 * math.exp(-0.3 * 0)

LANES = 128
SUBLANES = 8
MIB = 1024 * 1024

COL_Q = 0
COL_K = COL_Q + DIFF_WIDTH
COL_G = COL_K + DIFF_WIDTH
COL_Z = COL_G + DIFF_WIDTH
COL_XS = COL_Z + SSD_WIDTH
COL_B = COL_XS + SSD_WIDTH
COL_C = COL_B + SSD_GROUPS * SSD_STATE
COL_XQ = COL_C + SSD_GROUPS * SSD_STATE
COL_XG = COL_XQ + XATTN_WIDTH
N_MAIN = COL_XG + XATTN_WIDTH
REF_V = 2 * DIFF_WIDTH
REF_G = REF_V + DIFF_WIDTH
REF_DT = REF_G + DIFF_WIDTH + SSD_WIDTH + SSD_CONV_DIM
REF_XQ = REF_DT + SSD_HEADS
D_IN = REF_XQ + 2 * XATTN_WIDTH

PROJ_TM = 512
PROJ_TN = 512
NORM_ROWS = 256
ATT_TQ = 512
ATT_TK = 512
ATT_HEADS_PER_STEP = 4
ATT_SUM_ROWS = 16
ATT_AHEAD = 2
XATT_TQ = 512
OUT_TM = 512
OUT_SUB = 256
ROPE_TM = 1024
W_STAGE_ROWS = 256
SSD_CHUNKS_PER_STEP = 4
DT_ROWS = 16

NT_DIMS = (((1,), (1,)), ((), ()))
TN_DIMS = (((0,), (0,)), ((), ()))


def _rms_scale(x):
    return lax.rsqrt(jnp.mean(x * x, axis=-1, keepdims=True) + NORM_EPS)


def _silu(x):
    return x * jax.nn.sigmoid(x)


def _rope_table_rows(pos_ref, invf_ref, cos_ref, sin_ref):
    ang = pos_ref[...].astype(F32) * invf_ref[...]
    lane = lax.broadcasted_iota(jnp.int32, ang.shape, 1)
    first_half = (lane % DIFF_QK_DIM) < (DIFF_QK_DIM // 2)
    s = jnp.sin(ang)
    cos_ref[...] = jnp.cos(ang)
    sin_ref[...] = jnp.where(first_half, -s, s)


def _rope(x, cos, sin_signed):
    lane = lax.broadcasted_iota(jnp.int32, x.shape, 1)
    first_half = (lane % DIFF_QK_DIM) < (DIFF_QK_DIM // 2)
    half = DIFF_QK_DIM // 2
    rot = jnp.where(first_half, pltpu.roll(x, LANES - half, 1), pltpu.roll(x, half, 1))
    return x * cos + rot * sin_signed


def _rope_tables(positions):
    n = positions.size
    inv_freq = 1.0 / (ROPE_THETA ** (jnp.arange(0, DIFF_QK_DIM, 2, dtype=F32) / DIFF_QK_DIM))
    invf = jnp.tile(inv_freq, LANES // inv_freq.shape[0]).reshape(1, LANES)
    return pl.pallas_call(
        _rope_table_rows,
        grid=(n // ROPE_TM,),
        in_specs=[pl.BlockSpec((ROPE_TM, 1), lambda i: (i, 0)),
                  pl.BlockSpec((1, LANES), lambda i: (0, 0))],
        out_specs=[pl.BlockSpec((ROPE_TM, LANES), lambda i: (i, 0)),
                   pl.BlockSpec((ROPE_TM, LANES), lambda i: (i, 0))],
        out_shape=[jax.ShapeDtypeStruct((n, LANES), F32)] * 2,
        name="rope_tables",
    )(positions.reshape(n, 1), invf)


def _w_in_plan():
    plan = [(REF_DT, DT_ROWS, "dt", 0)]
    plan += [(REF_V + r, W_STAGE_ROWS, "v", r) for r in range(0, DIFF_WIDTH, W_STAGE_ROWS)]
    for c0 in range(0, N_MAIN, PROJ_TN):
        src = c0 if c0 < COL_G else c0 + DIFF_WIDTH + (SSD_HEADS if c0 >= COL_XQ else 0)
        plan += [(src + r, W_STAGE_ROWS, "main", c0 + r) for r in range(0, PROJ_TN, W_STAGE_ROWS)]
    return plan


def _in_proj_kernel(x_ref, nw_ref, wt_hbm, cos_ref, sin_ref, o_ref, vt_ref, dt_ref,
                    h_ref, w_ref, wv_ref, wdt_ref, stage_ref, sem_ref):
    plan = _w_in_plan()

    def copy(k):
        src, rows, _, _ = plan[k]
        return pltpu.make_async_copy(wt_hbm.at[pl.ds(src, rows), :],
                                     stage_ref.at[k % 2, pl.ds(0, rows), :],
                                     sem_ref.at[k % 2])

    def convert(k):
        _, rows, kind, dst = plan[k]
        blk = stage_ref[k % 2, 0:rows, :]
        if kind == "main":
            w_ref[:, dst:dst + rows] = blk.T.astype(BF16)
        elif kind == "v":
            wv_ref[dst:dst + rows, :] = blk.astype(BF16)
        else:
            row = lax.broadcasted_iota(jnp.int32, (rows, D_MODEL), 0)
            wdt_ref[...] = jnp.where(row < SSD_HEADS, blk, 0.0).astype(BF16)

    def project(first):
        if first:
            copy(0).start()
            copy(1).start()

        def norm_rows(i, carry):
            r = pl.ds(pl.multiple_of(i * NORM_ROWS, NORM_ROWS), NORM_ROWS)
            x = x_ref[r, :]
            h_ref[r, :] = (x * _rms_scale(x) * nw_ref[...]).astype(BF16)
            return carry
        lax.fori_loop(0, PROJ_TM // NORM_ROWS, norm_rows, 0)

        converted = [0]

        def need(upto):
            if first:
                for k in range(converted[0], upto):
                    copy(k).wait()
                    convert(k)
                    if k + 2 < len(plan):
                        copy(k + 2).start()
                converted[0] = upto

        need(1)
        dt_t = lax.dot_general(wdt_ref[...], h_ref[...], NT_DIMS, preferred_element_type=F32)
        dt_ref[...] = dt_t[0:SSD_HEADS, :]
        need(1 + DIFF_WIDTH // W_STAGE_ROWS)
        vt_ref[...] = lax.dot_general(wv_ref[...], h_ref[...], NT_DIMS,
                                      preferred_element_type=F32).astype(BF16)
        q_scale = DIFF_QK_DIM ** -0.5 * math.log2(math.e)
        for n in range(N_MAIN // PROJ_TN):
            c0 = n * PROJ_TN
            need(1 + (DIFF_WIDTH + c0 + PROJ_TN) // W_STAGE_ROWS)
            y = jnp.dot(h_ref[...], w_ref[:, c0:c0 + PROJ_TN], preferred_element_type=F32)
            if c0 < COL_G:
                scale = q_scale if c0 < COL_K else 1.0
                for hd in range(PROJ_TN // LANES):
                    y_h = _rope(y[:, hd * LANES:(hd + 1) * LANES], cos_ref[...], sin_ref[...])
                    o_ref[:, c0 + hd * LANES:c0 + (hd + 1) * LANES] = (y_h * scale).astype(BF16)
            else:
                o_ref[:, c0:c0 + PROJ_TN] = y.astype(BF16)

    step = pl.program_id(0)
    pl.when(step == 0)(functools.partial(project, True))
    pl.when(step > 0)(functools.partial(project, False))


def _in_proj(x2, pre_norm_w, w_t, cos, sin):
    m = x2.shape[0]
    assert w_t.shape == (D_IN, D_MODEL)
    return pl.pallas_call(
        _in_proj_kernel,
        grid=(m // PROJ_TM,),
        in_specs=[pl.BlockSpec((PROJ_TM, D_MODEL), lambda i: (i, 0)),
                  pl.BlockSpec((1, D_MODEL), lambda i: (0, 0)),
                  pl.BlockSpec(memory_space=pl.ANY),
                  pl.BlockSpec((PROJ_TM, LANES), lambda i: (i, 0)),
                  pl.BlockSpec((PROJ_TM, LANES), lambda i: (i, 0))],
        out_specs=[pl.BlockSpec((PROJ_TM, N_MAIN), lambda i: (i, 0)),
                   pl.BlockSpec((DIFF_WIDTH, PROJ_TM), lambda i: (0, i)),
                   pl.BlockSpec((SSD_HEADS, PROJ_TM), lambda i: (0, i))],
        out_shape=[jax.ShapeDtypeStruct((m, N_MAIN), BF16),
                   jax.ShapeDtypeStruct((DIFF_WIDTH, m), BF16),
                   jax.ShapeDtypeStruct((SSD_HEADS, m), F32)],
        scratch_shapes=[pltpu.VMEM((PROJ_TM, D_MODEL), BF16),
                        pltpu.VMEM((D_MODEL, N_MAIN), BF16),
                        pltpu.VMEM((DIFF_WIDTH, D_MODEL), BF16),
                        pltpu.VMEM((DT_ROWS, D_MODEL), BF16),
                        pltpu.VMEM((2, W_STAGE_ROWS, D_MODEL), F32),
                        pltpu.SemaphoreType.DMA((2,))],
        compiler_params=pltpu.CompilerParams(
            dimension_semantics=("arbitrary",),
            vmem_limit_bytes=58 * MIB),
        name="in_proj",
    )(x2, pre_norm_w.reshape(1, D_MODEL), w_t, cos, sin)


def _diff_attn_kernel(lam_ref, q_ref, k_ref, v_ref, g_ref, subw_ref, o_ref, vt_ref, *, seq):
    qi = pl.program_id(2)
    heads = ATT_HEADS_PER_STEP

    @pl.when(qi == 0)
    def _():
        for g in range(heads):
            vt_ref[g, 0:DIFF_V_DIM, :] = v_ref[g * DIFF_V_DIM:(g + 1) * DIFF_V_DIM, :]
        vt_ref[:, DIFF_V_DIM:, :] = jnp.ones((heads, ATT_SUM_ROWS, seq), BF16)

    lane = lax.broadcasted_iota(jnp.int32, (ATT_TQ, LANES), 1)
    q_maps = []
    for g in range(heads):
        q = q_ref[:, g * LANES:(g + 1) * LANES]
        q_maps.append((jnp.where(lane < DIFF_QK_DIM, q, jnp.zeros_like(q)),
                       jnp.where(lane >= DIFF_QK_DIM, q, jnp.zeros_like(q))))

    lv = lam_ref[...]
    lam = (jnp.exp(jnp.sum(lv[0:1] * lv[1:2], axis=-1, keepdims=True))
           - jnp.exp(jnp.sum(lv[2:3] * lv[3:4], axis=-1, keepdims=True)) + LAMBDA_INIT)

    def run(n_tiles):
        diag = (n_tiles - 1) * ATT_TK
        half = ATT_TK // 2
        blocks = [(j * ATT_TK, (j + 1) * ATT_TK, 0, False) for j in range(n_tiles - 1)]
        blocks += [(diag, diag + half, 0, True), (diag + half, diag + ATT_TK, half, True)]
        items = [(blk, g, c) for blk in blocks for g in range(heads) for c in range(2)]

        def scores(blk, g, c):
            k0, k1, q0, masked = blk
            kt = k_ref[k0:k1, g * LANES:(g + 1) * LANES]
            s = lax.dot_general(kt, q_maps[g][c][q0:, :], NT_DIMS, preferred_element_type=F32)
            if not masked:
                return s
            key = lax.broadcasted_iota(jnp.int32, s.shape, 0) + (k0 - diag)
            qry = lax.broadcasted_iota(jnp.int32, s.shape, 1) + q0
            return jnp.where(key <= qry, s, -jnp.inf)

        pending = {t: scores(*items[t]) for t in range(min(ATT_AHEAD, len(items)))}
        m, acc = {}, {}
        for t, (blk, g, c) in enumerate(items):
            k0, k1, q0, _ = blk
            i = 2 * g + c
            s = pending.pop(t)
            m_new = jnp.max(s, axis=0, keepdims=True)
            first = i not in m
            if not first:
                m_old = m[i][:, q0:]
                m_new = jnp.maximum(m_old, m_new)
            p = jnp.exp2(s - m_new).astype(BF16)
            if t + ATT_AHEAD < len(items):
                pending[t + ATT_AHEAD] = scores(*items[t + ATT_AHEAD])
            pv = jnp.dot(vt_ref[g, :, k0:k1], p, preferred_element_type=F32)
            if first:
                m[i], acc[i] = m_new, pv
            else:
                new_acc = jnp.exp2(m_old - m_new) * acc[i][:, q0:] + pv
                if q0:
                    m_new = jnp.concatenate([m[i][:, :q0], m_new], axis=1)
                    new_acc = jnp.concatenate([acc[i][:, :q0], new_acc], axis=1)
                m[i], acc[i] = m_new, new_acc

        for g in range(heads):
            c = slice(g * LANES, (g + 1) * LANES)
            a1, a2 = acc[2 * g], acc[2 * g + 1]
            l1 = a1[DIFF_V_DIM:DIFF_V_DIM + 1, :]
            l2 = a2[DIFF_V_DIM:DIFF_V_DIM + 1, :]
            o_t = (a1[0:DIFF_V_DIM, :] * (1.0 / l1)
                   - lam * (a2[0:DIFF_V_DIM, :] * (1.0 / l2)))
            o = o_t.T
            o = o * _rms_scale(o) * subw_ref[...] * (1.0 - LAMBDA_INIT)
            o_ref[:, c] = (o * _silu(g_ref[:, c].astype(F32))).astype(BF16)

    for n in range(seq // ATT_TQ):
        pl.when(qi == n)(functools.partial(run, n + 1))


def _diff_attention(proj, v_t, lam_vecs, subln_w, batch, seq):
    assert ATT_TQ == ATT_TK and seq % ATT_TQ == 0 and DIFF_V_DIM == LANES
    nq = seq // ATT_TQ
    heads = ATT_HEADS_PER_STEP
    width = heads * LANES
    kernel = functools.partial(_diff_attn_kernel, seq=seq)
    return pl.pallas_call(
        kernel,
        grid=(batch, DIFF_HEADS // heads, nq),
        in_specs=[
            pl.BlockSpec((4, DIFF_QK_DIM), lambda b, h, q: (0, 0)),
            pl.BlockSpec((ATT_TQ, width), lambda b, h, q: (b * nq + q, COL_Q // width + h)),
            pl.BlockSpec((seq, width), lambda b, h, q: (b, COL_K // width + h)),
            pl.BlockSpec((width, seq), lambda b, h, q: (h, b)),
            pl.BlockSpec((ATT_TQ, width), lambda b, h, q: (b * nq + q, COL_G // width + h)),
            pl.BlockSpec((1, DIFF_V_DIM), lambda b, h, q: (0, 0)),
        ],
        out_specs=pl.BlockSpec((ATT_TQ, width), lambda b, h, q: (b * nq + q, h)),
        out_shape=jax.ShapeDtypeStruct((batch * seq, DIFF_WIDTH), BF16),
        scratch_shapes=[pltpu.VMEM((heads, DIFF_V_DIM + ATT_SUM_ROWS, seq), BF16)],
        compiler_params=pltpu.CompilerParams(
            dimension_semantics=("arbitrary", "arbitrary", "arbitrary"),
            vmem_limit_bytes=48 * MIB),
        name="diff_attn",
    )(lam_vecs, proj, proj, v_t, proj, subln_w.reshape(1, DIFF_V_DIM))


def _ssd_kernel(xs_ref, b_ref, c_ref, z_ref, dt_ref, shift_ref, triu_ref, sel_ref, cw_ref,
                cb_ref, dtb_ref, alog_ref, dsk_ref, nw_ref, o_ref, xext_ref, state_ref):
    L = SSD_CHUNK
    T = SSD_CHUNKS_PER_STEP
    gs = SSD_GROUPS * SSD_STATE
    pairs = SSD_HEADS // 2

    @pl.when(pl.program_id(1) == 0)
    def _():
        xext_ref[0:L, :] = jnp.zeros((L, SSD_CONV_DIM), BF16)
        state_ref[...] = jnp.zeros(state_ref.shape, F32)

    xext_ref[L:(T + 1) * L, 0:SSD_WIDTH] = xs_ref[...]
    xext_ref[L:(T + 1) * L, SSD_WIDTH:SSD_WIDTH + gs] = b_ref[...]
    xext_ref[L:(T + 1) * L, SSD_WIDTH + gs:SSD_CONV_DIM] = c_ref[...]

    carry = {"state": [state_ref[p] for p in range(pairs)], "cols_split": [None] * T}
    programs = [_ssd_chunk(c, carry, xext_ref, z_ref, dt_ref, shift_ref, triu_ref, cw_ref,
                           cb_ref, dtb_ref, alog_ref, dsk_ref, nw_ref, o_ref)
                for c in range(T)]

    def stage():
        for prog in programs:
            next(prog)

    stage()
    stage()
    carry["spread"] = jnp.dot(jnp.concatenate(carry["cols_split"], axis=0), sel_ref[...],
                              preferred_element_type=F32)
    stage()
    stage()
    stage()
    for p in range(pairs):
        state_ref[p] = carry["state"][p]
    xext_ref[0:L, :] = xext_ref[T * L:(T + 1) * L, :]


SSD_PAIR_KINDS = (0, 2, 3)


def _ssd_chunk(c, carry, xext_ref, z_ref, dt_ref, shift_ref, triu_ref, cw_ref,
               cb_ref, dtb_ref, alog_ref, dsk_ref, nw_ref, o_ref):
    L = SSD_CHUNK
    H = SSD_HEADS
    gs = SSD_GROUPS * SSD_STATE
    rows_c = slice(c * L, (c + 1) * L)

    delayed = jnp.dot(shift_ref[...], xext_ref[c * L:(c + 2) * L, :],
                      preferred_element_type=F32)
    conv = (cb_ref[...] + xext_ref[(c + 1) * L:(c + 2) * L, :].astype(F32)
            * cw_ref[SSD_CONV - 1:SSD_CONV, :])
    for k in range(SSD_CONV - 1):
        conv = conv + delayed[k * L:(k + 1) * L, :] * cw_ref[k:k + 1, :]
    xbc = _silu(conv)
    xs = xbc[:, 0:SSD_WIDTH]
    bm = xbc[:, SSD_WIDTH:SSD_WIDTH + gs].astype(BF16)
    cm = xbc[:, SSD_WIDTH + gs:SSD_CONV_DIM].astype(BF16)
    yield

    dt_in = dt_ref[:, rows_c] + dtb_ref[...]
    dt = jnp.maximum(dt_in, 0.0) + jnp.log1p(jnp.exp(-jnp.abs(dt_in)))
    a_dt = dt * (-jnp.exp(alog_ref[...]))
    hi = a_dt.astype(BF16)
    r1 = a_dt - hi.astype(F32)
    mid = r1.astype(BF16)
    lo = (r1 - mid.astype(F32)).astype(BF16)
    triu = triu_ref[...]
    a_cs = (jnp.dot(hi, triu, preferred_element_type=F32)
            + jnp.dot(mid, triu, preferred_element_type=F32)
            + jnp.dot(lo, triu, preferred_element_type=F32))
    a_end = a_cs[:, L - 1:L]
    chunk_decay = jnp.exp(a_end)
    rows = jnp.concatenate([dt, a_cs, jnp.exp(a_cs), jnp.exp(a_end - a_cs),
                            jnp.zeros((L - 4 * H, L), F32)], axis=0)
    cols = rows.T
    c_hi = cols.astype(BF16)
    c_r1 = cols - c_hi.astype(F32)
    c_mid = c_r1.astype(BF16)
    c_lo = (c_r1 - c_mid.astype(F32)).astype(BF16)
    carry["cols_split"][c] = jnp.concatenate([c_hi, c_mid, c_lo], axis=1)
    yield
    spread = carry["spread"][rows_c, :]

    row = lax.broadcasted_iota(jnp.int32, (L, L), 0)
    col = lax.broadcasted_iota(jnp.int32, (L, L), 1)
    lower = row >= col
    lane = lax.broadcasted_iota(jnp.int32, (L, LANES), 1)
    left = lane < SSD_HEAD_DIM
    top = row < SSD_HEAD_DIM
    pairs = SSD_HEADS // 2

    def pair_lanes(slot, p):
        i = slot * pairs + p
        return spread[:, i * LANES:(i + 1) * LANES]

    def head_lanes(h):
        i = len(SSD_PAIR_KINDS) * pairs + h
        return spread[:, i * LANES:(i + 1) * LANES]

    group_cb = []
    for g in range(SSD_GROUPS):
        group_cb.append(lax.dot_general(cm[:, g * SSD_STATE:(g + 1) * SSD_STATE],
                                        bm[:, g * SSD_STATE:(g + 1) * SSD_STATE],
                                        NT_DIMS, preferred_element_type=F32))
    yield

    group_of = lambda p: (2 * p) // (SSD_HEADS // SSD_GROUPS)
    y_diag, chunk_state = [], []
    for p in range(pairs):
        bg = bm[:, group_of(p) * SSD_STATE:(group_of(p) + 1) * SSD_STATE]
        x_dt = xs[:, p * LANES:(p + 1) * LANES] * pair_lanes(0, p)
        x_dt_b = x_dt.astype(BF16)
        y_heads = []
        for hh in range(2):
            h = 2 * p + hh
            seg = head_lanes(h) - a_cs[h:h + 1, :]
            decay = jnp.exp(jnp.where(lower, seg, -jnp.inf))
            scores = (group_cb[group_of(p)] * decay).astype(BF16)
            y_heads.append(jnp.dot(scores, x_dt_b, preferred_element_type=F32))
        y_diag.append(jnp.where(left, y_heads[0], y_heads[1]))
        x_w = (x_dt * pair_lanes(2, p)).astype(BF16)
        chunk_state.append(lax.dot_general(x_w, bg, TN_DIMS, preferred_element_type=F32))
    yield

    ys, new_state = [], []
    for p in range(pairs):
        cg = cm[:, group_of(p) * SSD_STATE:(group_of(p) + 1) * SSD_STATE]
        st = carry["state"][p]
        y_off = (lax.dot_general(cg, st.astype(BF16), NT_DIMS, preferred_element_type=F32)
                 * pair_lanes(1, p))
        st_decay = jnp.where(top, chunk_decay[2 * p:2 * p + 1, :],
                             chunk_decay[2 * p + 1:2 * p + 2, :])
        new_state.append(st * st_decay + chunk_state[p])
        y = (y_diag[p] + y_off
             + xs[:, p * LANES:(p + 1) * LANES] * dsk_ref[:, p * LANES:(p + 1) * LANES])
        ys.append(y * _silu(z_ref[rows_c, p * LANES:(p + 1) * LANES].astype(F32)))
    carry["state"] = new_state

    grp_pairs = pairs // SSD_GROUPS
    grp_width = SSD_WIDTH // SSD_GROUPS
    for g in range(SSD_GROUPS):
        members = ys[g * grp_pairs:(g + 1) * grp_pairs]
        ssq = sum(jnp.sum(y * y, axis=-1, keepdims=True) for y in members)
        inv = lax.rsqrt(ssq / grp_width + NORM_EPS)
        for i, y in enumerate(members):
            c0 = (g * grp_pairs + i) * LANES
            o_ref[rows_c, c0:c0 + LANES] = (y * inv * nw_ref[:, c0:c0 + LANES]).astype(BF16)
    yield


def _ssd(proj, dt_raw_t, conv_w, conv_b, dt_bias, a_log, d_skip, ssd_norm_w, batch, seq):
    gs = SSD_GROUPS * SSD_STATE
    L = SSD_CHUNK
    rows = SSD_CHUNKS_PER_STEP * L
    nc = seq // rows
    row_map = lambda b, c: b * nc + c
    const = lambda b, c: (0, 0)
    t = jnp.arange(L)[:, None]
    src = jnp.arange(2 * L)[None, :]
    shift = jnp.concatenate([(src == L + t - (SSD_CONV - 1 - k)) for k in range(SSD_CONV - 1)],
                            axis=0).astype(BF16)
    triu = (jnp.arange(L)[:, None] <= jnp.arange(L)[None, :]).astype(BF16)
    lane = jnp.arange(LANES)
    sel_blocks = []
    for kind in SSD_PAIR_KINDS:
        for p in range(SSD_HEADS // 2):
            src_col = kind * SSD_HEADS + 2 * p + (lane >= SSD_HEAD_DIM)
            sel_blocks.append(jnp.arange(L)[:, None] == src_col[None, :])
    for h in range(SSD_HEADS):
        sel_blocks.append(jnp.broadcast_to(jnp.arange(L)[:, None] == SSD_HEADS + h, (L, LANES)))
    sel = jnp.tile(jnp.concatenate(sel_blocks, axis=1), (3, 1)).astype(BF16)
    per_head = lambda v: jnp.broadcast_to(v.astype(F32)[:, None], (SSD_HEADS, L))
    return pl.pallas_call(
        _ssd_kernel,
        grid=(batch, nc),
        in_specs=[
            pl.BlockSpec((rows, SSD_WIDTH), lambda b, c: (row_map(b, c), COL_XS // SSD_WIDTH)),
            pl.BlockSpec((rows, gs), lambda b, c: (row_map(b, c), COL_B // gs)),
            pl.BlockSpec((rows, gs), lambda b, c: (row_map(b, c), COL_C // gs)),
            pl.BlockSpec((rows, SSD_WIDTH), lambda b, c: (row_map(b, c), COL_Z // SSD_WIDTH)),
            pl.BlockSpec((SSD_HEADS, rows), lambda b, c: (0, row_map(b, c))),
            pl.BlockSpec(((SSD_CONV - 1) * L, 2 * L), const),
            pl.BlockSpec((L, L), const),
            pl.BlockSpec(sel.shape, const),
            pl.BlockSpec((SSD_CONV, SSD_CONV_DIM), const),
            pl.BlockSpec((1, SSD_CONV_DIM), const),
            pl.BlockSpec((SSD_HEADS, L), const),
            pl.BlockSpec((SSD_HEADS, L), const),
            pl.BlockSpec((1, SSD_WIDTH), const),
            pl.BlockSpec((1, SSD_WIDTH), const),
        ],
        out_specs=pl.BlockSpec((rows, SSD_WIDTH), lambda b, c: (row_map(b, c), 0)),
        out_shape=jax.ShapeDtypeStruct((batch * seq, SSD_WIDTH), BF16),
        scratch_shapes=[pltpu.VMEM((L + rows, SSD_CONV_DIM), BF16),
                        pltpu.VMEM((SSD_HEADS // 2, 2 * SSD_HEAD_DIM, SSD_STATE), F32)],
        compiler_params=pltpu.CompilerParams(
            dimension_semantics=("arbitrary", "arbitrary"),
            vmem_limit_bytes=32 * MIB),
        name="ssd",
    )(proj, proj, proj, proj, dt_raw_t, shift, triu, sel, conv_w, conv_b.reshape(1, SSD_CONV_DIM),
      per_head(dt_bias), per_head(a_log),
      jnp.repeat(d_skip, SSD_HEAD_DIM).reshape(1, SSD_WIDTH), ssd_norm_w.reshape(1, SSD_WIDTH))


XATT_V_ROWS = XATTN_HEAD_DIM + ATT_SUM_ROWS


def _mem_kv_kernel(mem_ref, nw_ref, w_ref, k_ref, vt_ref, wb_ref):
    @pl.when(pl.program_id(0) == 0)
    def _():
        wb_ref[...] = w_ref[...].astype(BF16)

    x = mem_ref[...]
    h = (x * _rms_scale(x) * nw_ref[...]).astype(BF16)
    kv = jnp.dot(h, wb_ref[...], preferred_element_type=F32)
    k_ref[...] = kv[:, 0:XATTN_WIDTH].astype(BF16)
    ones = jnp.ones((ATT_SUM_ROWS, x.shape[0]), BF16)
    for hd in range(XATTN_HEADS):
        c0 = XATTN_WIDTH + hd * XATTN_HEAD_DIM
        r0 = hd * XATT_V_ROWS
        vt_ref[r0:r0 + XATTN_HEAD_DIM, :] = kv[:, c0:c0 + XATTN_HEAD_DIM].T.astype(BF16)
        vt_ref[r0 + XATTN_HEAD_DIM:r0 + XATT_V_ROWS, :] = ones


def _mem_kv(mem2, mem_norm_w, w_kv, batch, mem_len):
    return pl.pallas_call(
        _mem_kv_kernel,
        grid=(batch,),
        in_specs=[pl.BlockSpec((mem_len, D_MODEL), lambda b: (b, 0)),
                  pl.BlockSpec((1, D_MODEL), lambda b: (0, 0)),
                  pl.BlockSpec((D_MODEL, 2 * XATTN_WIDTH), lambda b: (0, 0),
                               pipeline_mode=pl.Buffered(1))],
        out_specs=[pl.BlockSpec((mem_len, XATTN_WIDTH), lambda b: (b, 0)),
                   pl.BlockSpec((XATTN_HEADS * XATT_V_ROWS, mem_len), lambda b: (b, 0))],
        out_shape=[jax.ShapeDtypeStruct((batch * mem_len, XATTN_WIDTH), BF16),
                   jax.ShapeDtypeStruct((batch * XATTN_HEADS * XATT_V_ROWS, mem_len), BF16)],
        scratch_shapes=[pltpu.VMEM((D_MODEL, 2 * XATTN_WIDTH), BF16)],
        compiler_params=pltpu.CompilerParams(
            dimension_semantics=("arbitrary",), vmem_limit_bytes=32 * MIB),
        name="mem_kv",
    )(mem2, mem_norm_w.reshape(1, D_MODEL), w_kv)


def _xattn_kernel(q_ref, g_ref, k_ref, vt_ref, o_ref):
    scale = XATTN_HEAD_DIM ** -0.5 * math.log2(math.e)
    cols = [slice(h * XATTN_HEAD_DIM, (h + 1) * XATTN_HEAD_DIM) for h in range(XATTN_HEADS)]
    scores = [lax.dot_general(k_ref[:, c], q_ref[:, c], NT_DIMS, preferred_element_type=F32)
              for c in cols]
    for h, c in enumerate(cols):
        s = scores[h] * scale
        p = jnp.exp2(s - jnp.max(s, axis=0, keepdims=True)).astype(BF16)
        pv = jnp.dot(vt_ref[h * XATT_V_ROWS:(h + 1) * XATT_V_ROWS, :], p,
                     preferred_element_type=F32)
        xo = (pv[0:XATTN_HEAD_DIM, :]
              * (1.0 / pv[XATTN_HEAD_DIM:XATTN_HEAD_DIM + 1, :])).T
        o_ref[:, c] = (xo * _silu(g_ref[:, c].astype(F32))).astype(BF16)


def _xattn(proj, mk, mvt, batch, seq, mem_len):
    nq = seq // XATT_TQ
    return pl.pallas_call(
        _xattn_kernel,
        grid=(batch, nq),
        in_specs=[pl.BlockSpec((XATT_TQ, XATTN_WIDTH), lambda b, q: (b * nq + q, COL_XQ // XATTN_WIDTH)),
                  pl.BlockSpec((XATT_TQ, XATTN_WIDTH), lambda b, q: (b * nq + q, COL_XG // XATTN_WIDTH)),
                  pl.BlockSpec((mem_len, XATTN_WIDTH), lambda b, q: (b, 0)),
                  pl.BlockSpec((XATTN_HEADS * XATT_V_ROWS, mem_len), lambda b, q: (b, 0))],
        out_specs=pl.BlockSpec((XATT_TQ, XATTN_WIDTH), lambda b, q: (b * nq + q, 0)),
        out_shape=jax.ShapeDtypeStruct((batch * seq, XATTN_WIDTH), BF16),
        compiler_params=pltpu.CompilerParams(
            dimension_semantics=("arbitrary", "arbitrary"), vmem_limit_bytes=32 * MIB),
        name="xattn",
    )(proj, proj, mk, mvt)


def _out_proj_kernel(d_ref, s_ref, a_ref, w_ref, nw_ref, x_ref, o_ref, wb_ref):
    @pl.when(pl.program_id(0) == 0)
    def _():
        for k0 in range(0, D_MIX, OUT_SUB):
            wb_ref[k0:k0 + OUT_SUB, :] = w_ref[k0:k0 + OUT_SUB, :].astype(BF16)

    for i in range(OUT_TM // OUT_SUB):
        r = slice(i * OUT_SUB, (i + 1) * OUT_SUB)
        y = jnp.dot(d_ref[r, :], wb_ref[0:DIFF_WIDTH, :], preferred_element_type=F32)
        y = y + jnp.dot(s_ref[r, :], wb_ref[DIFF_WIDTH:DIFF_WIDTH + SSD_WIDTH, :],
                        preferred_element_type=F32)
        y = y + jnp.dot(a_ref[r, :], wb_ref[DIFF_WIDTH + SSD_WIDTH:D_MIX, :],
                        preferred_element_type=F32)
        o_ref[r, :] = x_ref[r, :] + y * _rms_scale(y) * nw_ref[...]


def _out_proj(diff_out, ssd_out, xattn_out, w_out, post_norm_w, x2):
    m = x2.shape[0]
    return pl.pallas_call(
        _out_proj_kernel,
        grid=(m // OUT_TM,),
        in_specs=[pl.BlockSpec((OUT_TM, DIFF_WIDTH), lambda i: (i, 0)),
                  pl.BlockSpec((OUT_TM, SSD_WIDTH), lambda i: (i, 0)),
                  pl.BlockSpec((OUT_TM, XATTN_WIDTH), lambda i: (i, 0)),
                  pl.BlockSpec((D_MIX, D_MODEL), lambda i: (0, 0), pipeline_mode=pl.Buffered(1)),
                  pl.BlockSpec((1, D_MODEL), lambda i: (0, 0)),
                  pl.BlockSpec((OUT_TM, D_MODEL), lambda i: (i, 0))],
        out_specs=pl.BlockSpec((OUT_TM, D_MODEL), lambda i: (i, 0)),
        out_shape=jax.ShapeDtypeStruct((m, D_MODEL), F32),
        scratch_shapes=[pltpu.VMEM((D_MIX, D_MODEL), BF16)],
        compiler_params=pltpu.CompilerParams(
            dimension_semantics=("arbitrary",), vmem_limit_bytes=48 * MIB),
        name="out_proj",
    )(diff_out, ssd_out, xattn_out, w_out, post_norm_w.reshape(1, D_MODEL), x2)


def kernel(x, mem, positions, pre_norm_w, w_in, lambda_q1, lambda_k1, lambda_q2, lambda_k2,
           diff_subln_w, conv_w, conv_b, dt_bias, a_log, d_skip, ssd_norm_w, mem_norm_w,
           w_mem_kv, w_out, post_norm_w):
    batch, seq, _ = x.shape
    mem_len = mem.shape[1]
    assert pre_norm_w.shape[0] == 1, "single-layer kernel"
    x2 = x.reshape(batch * seq, D_MODEL)

    lam_vecs = jnp.stack([lambda_q1[0], lambda_k1[0], lambda_q2[0], lambda_k2[0]]).astype(F32)

    cos, sin = _rope_tables(positions)
    proj, v_t, dt_raw_t = _in_proj(x2, pre_norm_w[0], jnp.transpose(w_in[0]), cos, sin)
    diff_out = _diff_attention(proj, v_t, lam_vecs, diff_subln_w[0], batch, seq)
    ssd_out = _ssd(proj, dt_raw_t, conv_w[0], conv_b[0], dt_bias[0], a_log[0], d_skip[0],
                   ssd_norm_w[0], batch, seq)
    mk, mvt = _mem_kv(mem.reshape(batch * mem_len, D_MODEL), mem_norm_w[0], w_mem_kv[0],
                      batch, mem_len)
    xattn_out = _xattn(proj, mk, mvt, batch, seq, mem_len)
    out = _out_proj(diff_out, ssd_out, xattn_out, w_out[0], post_norm_w[0], x2)
    return out.reshape(batch, seq, D_MODEL)
```

```python
import functools
import math

import jax
import jax.numpy as jnp
from jax import lax
from jax.experimental import pallas as pl
from jax.experimental.pallas import tpu as pltpu

F32 = jnp.float32
BF16 = jnp.bfloat16

D_MODEL = 2048
DIFF_HEADS = 8
DIFF_QK_DIM = 64
DIFF_V_DIM = 128
DIFF_WIDTH = DIFF_HEADS * DIFF_V_DIM
SSD_HEADS = 8
SSD_HEAD_DIM = 64
SSD_WIDTH = SSD_HEADS * SSD_HEAD_DIM
SSD_GROUPS = 2
SSD_STATE = 128
SSD_CONV = 4
SSD_CHUNK = 128
SSD_CONV_DIM = SSD_WIDTH + 2 * SSD_GROUPS * SSD_STATE
XATTN_HEADS = 4
XATTN_HEAD_DIM = 128
XATTN_WIDTH = XATTN_HEADS * XATTN_HEAD_DIM
D_MIX = DIFF_WIDTH + SSD_WIDTH + XATTN_WIDTH
ROPE_THETA = 10000.0
NORM_EPS = 1e-6
LAMBDA_INIT = 0.8 - 0.6 * math.exp(-0.3 * 0)

LANES = 128
SUBLANES = 8
MIB = 1024 * 1024

COL_Q = 0
COL_K = COL_Q + DIFF_WIDTH
COL_G = COL_K + DIFF_WIDTH
COL_Z = COL_G + DIFF_WIDTH
COL_XS = COL_Z + SSD_WIDTH
COL_B = COL_XS + SSD_WIDTH
COL_C = COL_B + SSD_GROUPS * SSD_STATE
COL_XQ = COL_C + SSD_GROUPS * SSD_STATE
COL_XG = COL_XQ + XATTN_WIDTH
N_MAIN = COL_XG + XATTN_WIDTH
REF_V = 2 * DIFF_WIDTH
REF_G = REF_V + DIFF_WIDTH
REF_DT = REF_G + DIFF_WIDTH + SSD_WIDTH + SSD_CONV_DIM
REF_XQ = REF_DT + SSD_HEADS
D_IN = REF_XQ + 2 * XATTN_WIDTH

PROJ_TM = 512
PROJ_TN = 512
NORM_ROWS = 256
ATT_TQ = 512
ATT_TK = 512
ATT_HEADS_PER_STEP = 4
ATT_SUM_ROWS = 16
ATT_AHEAD = 2
XATT_TQ = 512
OUT_TM = 512
OUT_SUB = 256
ROPE_TM = 1024
W_STAGE_ROWS = 256
SSD_CHUNKS_PER_STEP = 4
DT_ROWS = 16

NT_DIMS = (((1,), (1,)), ((), ()))
TN_DIMS = (((0,), (0,)), ((), ()))


def _rms_scale(x):
    return lax.rsqrt(jnp.mean(x * x, axis=-1, keepdims=True) + NORM_EPS)


def _silu(x):
    return x * jax.nn.sigmoid(x)


def _rope_table_rows(pos_ref, invf_ref, cos_ref, sin_ref):
    ang = pos_ref[...].astype(F32) * invf_ref[...]
    lane = lax.broadcasted_iota(jnp.int32, ang.shape, 1)
    first_half = (lane % DIFF_QK_DIM) < (DIFF_QK_DIM // 2)
    s = jnp.sin(ang)
    cos_ref[...] = jnp.cos(ang)
    sin_ref[...] = jnp.where(first_half, -s, s)


def _rope(x, cos, sin_signed):
    lane = lax.broadcasted_iota(jnp.int32, x.shape, 1)
    first_half = (lane % DIFF_QK_DIM) < (DIFF_QK_DIM // 2)
    half = DIFF_QK_DIM // 2
    rot = jnp.where(first_half, pltpu.roll(x, LANES - half, 1), pltpu.roll(x, half, 1))
    return x * cos + rot * sin_signed


def _rope_tables(positions):
    n = positions.size
    inv_freq = 1.0 / (ROPE_THETA ** (jnp.arange(0, DIFF_QK_DIM, 2, dtype=F32) / DIFF_QK_DIM))
    invf = jnp.tile(inv_freq, LANES // inv_freq.shape[0]).reshape(1, LANES)
    return pl.pallas_call(
        _rope_table_rows,
        grid=(n // ROPE_TM,),
        in_specs=[pl.BlockSpec((ROPE_TM, 1), lambda i: (i, 0)),
                  pl.BlockSpec((1, LANES), lambda i: (0, 0))],
        out_specs=[pl.BlockSpec((ROPE_TM, LANES), lambda i: (i, 0)),
                   pl.BlockSpec((ROPE_TM, LANES), lambda i: (i, 0))],
        out_shape=[jax.ShapeDtypeStruct((n, LANES), F32)] * 2,
        name="rope_tables",
    )(positions.reshape(n, 1), invf)


def _w_in_plan():
    plan = [(REF_DT, DT_ROWS, "dt", 0)]
    plan += [(REF_V + r, W_STAGE_ROWS, "v", r) for r in range(0, DIFF_WIDTH, W_STAGE_ROWS)]
    for c0 in range(0, N_MAIN, PROJ_TN):
        src = c0 if c0 < COL_G else c0 + DIFF_WIDTH + (SSD_HEADS if c0 >= COL_XQ else 0)
        plan += [(src + r, W_STAGE_ROWS, "main", c0 + r) for r in range(0, PROJ_TN, W_STAGE_ROWS)]
    return plan


def _in_proj_kernel(x_ref, nw_ref, wt_hbm, cos_ref, sin_ref, o_ref, vt_ref, dt_ref,
                    h_ref, w_ref, wv_ref, wdt_ref, stage_ref, sem_ref):
    plan = _w_in_plan()

    def copy(k):
        src, rows, _, _ = plan[k]
        return pltpu.make_async_copy(wt_hbm.at[pl.ds(src, rows), :],
                                     stage_ref.at[k % 2, pl.ds(0, rows), :],
                                     sem_ref.at[k % 2])

    def convert(k):
        _, rows, kind, dst = plan[k]
        blk = stage_ref[k % 2, 0:rows, :]
        if kind == "main":
            w_ref[:, dst:dst + rows] = blk.T.astype(BF16)
        elif kind == "v":
            wv_ref[dst:dst + rows, :] = blk.astype(BF16)
        else:
            row = lax.broadcasted_iota(jnp.int32, (rows, D_MODEL), 0)
            wdt_ref[...] = jnp.where(row < SSD_HEADS, blk, 0.0).astype(BF16)

    def project(first):
        if first:
            copy(0).start()
            copy(1).start()

        def norm_rows(i, carry):
            r = pl.ds(pl.multiple_of(i * NORM_ROWS, NORM_ROWS), NORM_ROWS)
            x = x_ref[r, :]
            h_ref[r, :] = (x * _rms_scale(x) * nw_ref[...]).astype(BF16)
            return carry
        lax.fori_loop(0, PROJ_TM // NORM_ROWS, norm_rows, 0)

        converted = [0]

        def need(upto):
            if first:
                for k in range(converted[0], upto):
                    copy(k).wait()
                    convert(k)
                    if k + 2 < len(plan):
                        copy(k + 2).start()
                converted[0] = upto

        def main_blocks_through(c0):
            return 1 + (DIFF_WIDTH + c0 + PROJ_TN) // W_STAGE_ROWS

        need(1 + DIFF_WIDTH // W_STAGE_ROWS)
        dt_t = lax.dot_general(wdt_ref[...], h_ref[...], NT_DIMS, preferred_element_type=F32)
        dt_ref[...] = dt_t[0:SSD_HEADS, :]
        need(main_blocks_through(0))
        vt_ref[...] = lax.dot_general(wv_ref[...], h_ref[...], NT_DIMS,
                                      preferred_element_type=F32).astype(BF16)
        q_scale = DIFF_QK_DIM ** -0.5 * math.log2(math.e)
        for n in range(N_MAIN // PROJ_TN):
            c0 = n * PROJ_TN
            if c0 + PROJ_TN < N_MAIN:
                need(main_blocks_through(c0 + PROJ_TN))
            y = jnp.dot(h_ref[...], w_ref[:, c0:c0 + PROJ_TN], preferred_element_type=F32)
            if c0 < COL_G:
                scale = q_scale if c0 < COL_K else 1.0
                for hd in range(PROJ_TN // LANES):
                    y_h = _rope(y[:, hd * LANES:(hd + 1) * LANES], cos_ref[...], sin_ref[...])
                    o_ref[:, c0 + hd * LANES:c0 + (hd + 1) * LANES] = (y_h * scale).astype(BF16)
            else:
                o_ref[:, c0:c0 + PROJ_TN] = y.astype(BF16)

    step = pl.program_id(0)
    pl.when(step == 0)(functools.partial(project, True))
    pl.when(step > 0)(functools.partial(project, False))


def _in_proj(x2, pre_norm_w, w_t, cos, sin):
    m = x2.shape[0]
    assert w_t.shape == (D_IN, D_MODEL)
    return pl.pallas_call(
        _in_proj_kernel,
        grid=(m // PROJ_TM,),
        in_specs=[pl.BlockSpec((PROJ_TM, D_MODEL), lambda i: (i, 0)),
                  pl.BlockSpec((1, D_MODEL), lambda i: (0, 0)),
                  pl.BlockSpec(memory_space=pl.ANY),
                  pl.BlockSpec((PROJ_TM, LANES), lambda i: (i, 0)),
                  pl.BlockSpec((PROJ_TM, LANES), lambda i: (i, 0))],
        out_specs=[pl.BlockSpec((PROJ_TM, N_MAIN), lambda i: (i, 0)),
                   pl.BlockSpec((DIFF_WIDTH, PROJ_TM), lambda i: (0, i)),
                   pl.BlockSpec((SSD_HEADS, PROJ_TM), lambda i: (0, i))],
        out_shape=[jax.ShapeDtypeStruct((m, N_MAIN), BF16),
                   jax.ShapeDtypeStruct((DIFF_WIDTH, m), BF16),
                   jax.ShapeDtypeStruct((SSD_HEADS, m), F32)],
        scratch_shapes=[pltpu.VMEM((PROJ_TM, D_MODEL), BF16),
                        pltpu.VMEM((D_MODEL, N_MAIN), BF16),
                        pltpu.VMEM((DIFF_WIDTH, D_MODEL), BF16),
                        pltpu.VMEM((DT_ROWS, D_MODEL), BF16),
                        pltpu.VMEM((2, W_STAGE_ROWS, D_MODEL), F32),
                        pltpu.SemaphoreType.DMA((2,))],
        compiler_params=pltpu.CompilerParams(
            dimension_semantics=("arbitrary",),
            vmem_limit_bytes=58 * MIB),
        name="in_proj",
    )(x2, pre_norm_w.reshape(1, D_MODEL), w_t, cos, sin)


def _diff_attn_kernel(lam_ref, q_ref, k_ref, v_ref, g_ref, subw_ref, o_ref, vt_ref, *, seq):
    qi = pl.program_id(2)
    heads = ATT_HEADS_PER_STEP

    @pl.when(qi == 0)
    def _():
        for g in range(heads):
            vt_ref[g, 0:DIFF_V_DIM, :] = v_ref[g * DIFF_V_DIM:(g + 1) * DIFF_V_DIM, :]
        vt_ref[:, DIFF_V_DIM:, :] = jnp.ones((heads, ATT_SUM_ROWS, seq), BF16)

    lane = lax.broadcasted_iota(jnp.int32, (ATT_TQ, LANES), 1)
    q_maps = []
    for g in range(heads):
        q = q_ref[:, g * LANES:(g + 1) * LANES]
        q_maps.append((jnp.where(lane < DIFF_QK_DIM, q, jnp.zeros_like(q)),
                       jnp.where(lane >= DIFF_QK_DIM, q, jnp.zeros_like(q))))

    lv = lam_ref[...]
    lam = (jnp.exp(jnp.sum(lv[0:1] * lv[1:2], axis=-1, keepdims=True))
           - jnp.exp(jnp.sum(lv[2:3] * lv[3:4], axis=-1, keepdims=True)) + LAMBDA_INIT)

    def run(n_tiles):
        diag = (n_tiles - 1) * ATT_TK
        half = ATT_TK // 2
        blocks = [(j * ATT_TK, (j + 1) * ATT_TK, 0, False) for j in range(n_tiles - 1)]
        blocks += [(diag, diag + half, 0, True), (diag + half, diag + ATT_TK, half, True)]
        items = [(blk, g, c) for blk in blocks for g in range(heads) for c in range(2)]

        def scores(blk, g, c):
            k0, k1, q0, masked = blk
            kt = k_ref[k0:k1, g * LANES:(g + 1) * LANES]
            s = lax.dot_general(kt, q_maps[g][c][q0:, :], NT_DIMS, preferred_element_type=F32)
            if not masked:
                return s
            key = lax.broadcasted_iota(jnp.int32, s.shape, 0) + (k0 - diag)
            qry = lax.broadcasted_iota(jnp.int32, s.shape, 1) + q0
            return jnp.where(key <= qry, s, -jnp.inf)

        pending = {t: scores(*items[t]) for t in range(min(ATT_AHEAD, len(items)))}
        m, acc = {}, {}
        for t, (blk, g, c) in enumerate(items):
            k0, k1, q0, _ = blk
            i = 2 * g + c
            s = pending.pop(t)
            m_new = jnp.max(s, axis=0, keepdims=True)
            first = i not in m
            if not first:
                m_old = m[i][:, q0:]
                m_new = jnp.maximum(m_old, m_new)
            p = jnp.exp2(s - m_new).astype(BF16)
            if t + ATT_AHEAD < len(items):
                pending[t + ATT_AHEAD] = scores(*items[t + ATT_AHEAD])
            pv = jnp.dot(vt_ref[g, :, k0:k1], p, preferred_element_type=F32)
            if first:
                m[i], acc[i] = m_new, pv
            else:
                new_acc = jnp.exp2(m_old - m_new) * acc[i][:, q0:] + pv
                if q0:
                    m_new = jnp.concatenate([m[i][:, :q0], m_new], axis=1)
                    new_acc = jnp.concatenate([acc[i][:, :q0], new_acc], axis=1)
                m[i], acc[i] = m_new, new_acc

        for g in range(heads):
            c = slice(g * LANES, (g + 1) * LANES)
            a1, a2 = acc[2 * g], acc[2 * g + 1]
            l1 = a1[DIFF_V_DIM:DIFF_V_DIM + 1, :]
            l2 = a2[DIFF_V_DIM:DIFF_V_DIM + 1, :]
            o_t = (a1[0:DIFF_V_DIM, :] * (1.0 / l1)
                   - lam * (a2[0:DIFF_V_DIM, :] * (1.0 / l2)))
            o = o_t.T
            o = o * _rms_scale(o) * subw_ref[...] * (1.0 - LAMBDA_INIT)
            o_ref[:, c] = (o * _silu(g_ref[:, c].astype(F32))).astype(BF16)

    for n in range(seq // ATT_TQ):
        pl.when(qi == n)(functools.partial(run, n + 1))


def _diff_attention(proj, v_t, lam_vecs, subln_w, batch, seq):
    assert ATT_TQ == ATT_TK and seq % ATT_TQ == 0 and DIFF_V_DIM == LANES
    nq = seq // ATT_TQ
    heads = ATT_HEADS_PER_STEP
    width = heads * LANES
    kernel = functools.partial(_diff_attn_kernel, seq=seq)
    return pl.pallas_call(
        kernel,
        grid=(batch, DIFF_HEADS // heads, nq),
        in_specs=[
            pl.BlockSpec((4, DIFF_QK_DIM), lambda b, h, q: (0, 0)),
            pl.BlockSpec((ATT_TQ, width), lambda b, h, q: (b * nq + q, COL_Q // width + h)),
            pl.BlockSpec((seq, width), lambda b, h, q: (b, COL_K // width + h)),
            pl.BlockSpec((width, seq), lambda b, h, q: (h, b)),
            pl.BlockSpec((ATT_TQ, width), lambda b, h, q: (b * nq + q, COL_G // width + h)),
            pl.BlockSpec((1, DIFF_V_DIM), lambda b, h, q: (0, 0)),
        ],
        out_specs=pl.BlockSpec((ATT_TQ, width), lambda b, h, q: (b * nq + q, h)),
        out_shape=jax.ShapeDtypeStruct((batch * seq, DIFF_WIDTH), BF16),
        scratch_shapes=[pltpu.VMEM((heads, DIFF_V_DIM + ATT_SUM_ROWS, seq), BF16)],
        compiler_params=pltpu.CompilerParams(
            dimension_semantics=("arbitrary", "arbitrary", "arbitrary"),
            vmem_limit_bytes=48 * MIB),
        name="diff_attn",
    )(lam_vecs, proj, proj, v_t, proj, subln_w.reshape(1, DIFF_V_DIM))


def _ssd_kernel(xs_ref, b_ref, c_ref, z_ref, dt_ref, shift_ref, triu_ref, sel_ref, cw_ref,
                cb_ref, dtb_ref, alog_ref, dsk_ref, nw_ref, o_ref, xext_ref, state_ref):
    L = SSD_CHUNK
    T = SSD_CHUNKS_PER_STEP
    gs = SSD_GROUPS * SSD_STATE
    pairs = SSD_HEADS // 2

    @pl.when(pl.program_id(1) == 0)
    def _():
        xext_ref[0:L, :] = jnp.zeros((L, SSD_CONV_DIM), BF16)
        state_ref[...] = jnp.zeros(state_ref.shape, F32)

    xext_ref[L:(T + 1) * L, 0:SSD_WIDTH] = xs_ref[...]
    xext_ref[L:(T + 1) * L, SSD_WIDTH:SSD_WIDTH + gs] = b_ref[...]
    xext_ref[L:(T + 1) * L, SSD_WIDTH + gs:SSD_CONV_DIM] = c_ref[...]

    carry = {"state": [state_ref[p] for p in range(pairs)], "cols_split": [None] * T}
    programs = [_ssd_chunk(c, carry, xext_ref, z_ref, dt_ref, shift_ref, triu_ref, cw_ref,
                           cb_ref, dtb_ref, alog_ref, dsk_ref, nw_ref, o_ref)
                for c in range(T)]

    def stage():
        for prog in programs:
            next(prog)

    stage()
    stage()
    carry["spread"] = jnp.dot(jnp.concatenate(carry["cols_split"], axis=0), sel_ref[...],
                              preferred_element_type=F32)
    stage()
    stage()
    stage()
    for p in range(pairs):
        state_ref[p] = carry["state"][p]
    xext_ref[0:L, :] = xext_ref[T * L:(T + 1) * L, :]


SSD_PAIR_KINDS = (0, 2, 3)


def _ssd_chunk(c, carry, xext_ref, z_ref, dt_ref, shift_ref, triu_ref, cw_ref,
               cb_ref, dtb_ref, alog_ref, dsk_ref, nw_ref, o_ref):
    L = SSD_CHUNK
    H = SSD_HEADS
    gs = SSD_GROUPS * SSD_STATE
    rows_c = slice(c * L, (c + 1) * L)

    delayed = jnp.dot(shift_ref[...], xext_ref[c * L:(c + 2) * L, :],
                      preferred_element_type=F32)
    conv = (cb_ref[...] + xext_ref[(c + 1) * L:(c + 2) * L, :].astype(F32)
            * cw_ref[SSD_CONV - 1:SSD_CONV, :])
    for k in range(SSD_CONV - 1):
        conv = conv + delayed[k * L:(k + 1) * L, :] * cw_ref[k:k + 1, :]
    xbc = _silu(conv)
    xs = xbc[:, 0:SSD_WIDTH]
    bm = xbc[:, SSD_WIDTH:SSD_WIDTH + gs].astype(BF16)
    cm = xbc[:, SSD_WIDTH + gs:SSD_CONV_DIM].astype(BF16)
    yield

    dt_in = dt_ref[:, rows_c] + dtb_ref[...]
    dt = jnp.maximum(dt_in, 0.0) + jnp.log1p(jnp.exp(-jnp.abs(dt_in)))
    a_dt = dt * (-jnp.exp(alog_ref[...]))
    hi = a_dt.astype(BF16)
    r1 = a_dt - hi.astype(F32)
    mid = r1.astype(BF16)
    lo = (r1 - mid.astype(F32)).astype(BF16)
    triu = triu_ref[...]
    a_cs = (jnp.dot(hi, triu, preferred_element_type=F32)
            + jnp.dot(mid, triu, preferred_element_type=F32)
            + jnp.dot(lo, triu, preferred_element_type=F32))
    a_end = a_cs[:, L - 1:L]
    chunk_decay = jnp.exp(a_end)
    rows = jnp.concatenate([dt, a_cs, jnp.exp(a_cs), jnp.exp(a_end - a_cs),
                            jnp.zeros((L - 4 * H, L), F32)], axis=0)
    cols = rows.T
    c_hi = cols.astype(BF16)
    c_r1 = cols - c_hi.astype(F32)
    c_mid = c_r1.astype(BF16)
    c_lo = (c_r1 - c_mid.astype(F32)).astype(BF16)
    carry["cols_split"][c] = jnp.concatenate([c_hi, c_mid, c_lo], axis=1)
    yield
    spread = carry["spread"][rows_c, :]

    row = lax.broadcasted_iota(jnp.int32, (L, L), 0)
    col = lax.broadcasted_iota(jnp.int32, (L, L), 1)
    lower = row >= col
    lane = lax.broadcasted_iota(jnp.int32, (L, LANES), 1)
    left = lane < SSD_HEAD_DIM
    top = row < SSD_HEAD_DIM
    pairs = SSD_HEADS // 2

    def pair_lanes(slot, p):
        i = slot * pairs + p
        return spread[:, i * LANES:(i + 1) * LANES]

    def head_lanes(h):
        i = len(SSD_PAIR_KINDS) * pairs + h
        return spread[:, i * LANES:(i + 1) * LANES]

    group_cb = []
    for g in range(SSD_GROUPS):
        group_cb.append(lax.dot_general(cm[:, g * SSD_STATE:(g + 1) * SSD_STATE],
                                        bm[:, g * SSD_STATE:(g + 1) * SSD_STATE],
                                        NT_DIMS, preferred_element_type=F32))
    yield

    group_of = lambda p: (2 * p) // (SSD_HEADS // SSD_GROUPS)
    y_diag, chunk_state = [], []
    for p in range(pairs):
        bg = bm[:, group_of(p) * SSD_STATE:(group_of(p) + 1) * SSD_STATE]
        x_dt = xs[:, p * LANES:(p + 1) * LANES] * pair_lanes(0, p)
        x_dt_b = x_dt.astype(BF16)
        y_heads = []
        for hh in range(2):
            h = 2 * p + hh
            seg = head_lanes(h) - a_cs[h:h + 1, :]
            decay = jnp.exp(jnp.where(lower, seg, -jnp.inf))
            scores = (group_cb[group_of(p)] * decay).astype(BF16)
            y_heads.append(jnp.dot(scores, x_dt_b, preferred_element_type=F32))
        y_diag.append(jnp.where(left, y_heads[0], y_heads[1]))
        x_w = (x_dt * pair_lanes(2, p)).astype(BF16)
        chunk_state.append(lax.dot_general(x_w, bg, TN_DIMS, preferred_element_type=F32))
    yield

    ys, new_state = [], []
    for p in range(pairs):
        cg = cm[:, group_of(p) * SSD_STATE:(group_of(p) + 1) * SSD_STATE]
        st = carry["state"][p]
        y_off = (lax.dot_general(cg, st.astype(BF16), NT_DIMS, preferred_element_type=F32)
                 * pair_lanes(1, p))
        st_decay = jnp.where(top, chunk_decay[2 * p:2 * p + 1, :],
                             chunk_decay[2 * p + 1:2 * p + 2, :])
        new_state.append(st * st_decay + chunk_state[p])
        y = (y_diag[p] + y_off
             + xs[:, p * LANES:(p + 1) * LANES] * dsk_ref[:, p * LANES:(p + 1) * LANES])
        ys.append(y * _silu(z_ref[rows_c, p * LANES:(p + 1) * LANES].astype(F32)))
    carry["state"] = new_state

    grp_pairs = pairs // SSD_GROUPS
    grp_width = SSD_WIDTH // SSD_GROUPS
    for g in range(SSD_GROUPS):
        members = ys[g * grp_pairs:(g + 1) * grp_pairs]
        ssq = sum(jnp.sum(y * y, axis=-1, keepdims=True) for y in members)
        inv = lax.rsqrt(ssq / grp_width + NORM_EPS)
        for i, y in enumerate(members):
            c0 = (g * grp_pairs + i) * LANES
            o_ref[rows_c, c0:c0 + LANES] = (y * inv * nw_ref[:, c0:c0 + LANES]).astype(BF16)
    yield


def _ssd(proj, dt_raw_t, conv_w, conv_b, dt_bias, a_log, d_skip, ssd_norm_w, batch, seq):
    gs = SSD_GROUPS * SSD_STATE
    L = SSD_CHUNK
    rows = SSD_CHUNKS_PER_STEP * L
    nc = seq // rows
    row_map = lambda b, c: b * nc + c
    const = lambda b, c: (0, 0)
    t = jnp.arange(L)[:, None]
    src = jnp.arange(2 * L)[None, :]
    shift = jnp.concatenate([(src == L + t - (SSD_CONV - 1 - k)) for k in range(SSD_CONV - 1)],
                            axis=0).astype(BF16)
    triu = (jnp.arange(L)[:, None] <= jnp.arange(L)[None, :]).astype(BF16)
    lane = jnp.arange(LANES)
    sel_blocks = []
    for kind in SSD_PAIR_KINDS:
        for p in range(SSD_HEADS // 2):
            src_col = kind * SSD_HEADS + 2 * p + (lane >= SSD_HEAD_DIM)
            sel_blocks.append(jnp.arange(L)[:, None] == src_col[None, :])
    for h in range(SSD_HEADS):
        sel_blocks.append(jnp.broadcast_to(jnp.arange(L)[:, None] == SSD_HEADS + h, (L, LANES)))
    sel = jnp.tile(jnp.concatenate(sel_blocks, axis=1), (3, 1)).astype(BF16)
    per_head = lambda v: jnp.broadcast_to(v.astype(F32)[:, None], (SSD_HEADS, L))
    return pl.pallas_call(
        _ssd_kernel,
        grid=(batch, nc),
        in_specs=[
            pl.BlockSpec((rows, SSD_WIDTH), lambda b, c: (row_map(b, c), COL_XS // SSD_WIDTH)),
            pl.BlockSpec((rows, gs), lambda b, c: (row_map(b, c), COL_B // gs)),
            pl.BlockSpec((rows, gs), lambda b, c: (row_map(b, c), COL_C // gs)),
            pl.BlockSpec((rows, SSD_WIDTH), lambda b, c: (row_map(b, c), COL_Z // SSD_WIDTH)),
            pl.BlockSpec((SSD_HEADS, rows), lambda b, c: (0, row_map(b, c))),
            pl.BlockSpec(((SSD_CONV - 1) * L, 2 * L), const),
            pl.BlockSpec((L, L), const),
            pl.BlockSpec(sel.shape, const),
            pl.BlockSpec((SSD_CONV, SSD_CONV_DIM), const),
            pl.BlockSpec((1, SSD_CONV_DIM), const),
            pl.BlockSpec((SSD_HEADS, L), const),
            pl.BlockSpec((SSD_HEADS, L), const),
            pl.BlockSpec((1, SSD_WIDTH), const),
            pl.BlockSpec((1, SSD_WIDTH), const),
        ],
        out_specs=pl.BlockSpec((rows, SSD_WIDTH), lambda b, c: (row_map(b, c), 0)),
        out_shape=jax.ShapeDtypeStruct((batch * seq, SSD_WIDTH), BF16),
        scratch_shapes=[pltpu.VMEM((L + rows, SSD_CONV_DIM), BF16),
                        pltpu.VMEM((SSD_HEADS // 2, 2 * SSD_HEAD_DIM, SSD_STATE), F32)],
        compiler_params=pltpu.CompilerParams(
            dimension_semantics=("arbitrary", "arbitrary"),
            vmem_limit_bytes=32 * MIB),
        name="ssd",
    )(proj, proj, proj, proj, dt_raw_t, shift, triu, sel, conv_w, conv_b.reshape(1, SSD_CONV_DIM),
      per_head(dt_bias), per_head(a_log),
      jnp.repeat(d_skip, SSD_HEAD_DIM).reshape(1, SSD_WIDTH), ssd_norm_w.reshape(1, SSD_WIDTH))


XATT_V_ROWS = XATTN_HEAD_DIM + ATT_SUM_ROWS


def _mem_kv_kernel(mem_ref, nw_ref, w_ref, k_ref, vt_ref, wb_ref):
    @pl.when(pl.program_id(0) == 0)
    def _():
        wb_ref[...] = w_ref[...].astype(BF16)

    x = mem_ref[...]
    h = (x * _rms_scale(x) * nw_ref[...]).astype(BF16)
    kv = jnp.dot(h, wb_ref[...], preferred_element_type=F32)
    k_ref[...] = kv[:, 0:XATTN_WIDTH].astype(BF16)
    ones = jnp.ones((ATT_SUM_ROWS, x.shape[0]), BF16)
    for hd in range(XATTN_HEADS):
        c0 = XATTN_WIDTH + hd * XATTN_HEAD_DIM
        r0 = hd * XATT_V_ROWS
        vt_ref[r0:r0 + XATTN_HEAD_DIM, :] = kv[:, c0:c0 + XATTN_HEAD_DIM].T.astype(BF16)
        vt_ref[r0 + XATTN_HEAD_DIM:r0 + XATT_V_ROWS, :] = ones


def _mem_kv(mem2, mem_norm_w, w_kv, batch, mem_len):
    return pl.pallas_call(
        _mem_kv_kernel,
        grid=(batch,),
        in_specs=[pl.BlockSpec((mem_len, D_MODEL), lambda b: (b, 0)),
                  pl.BlockSpec((1, D_MODEL), lambda b: (0, 0)),
                  pl.BlockSpec((D_MODEL, 2 * XATTN_WIDTH), lambda b: (0, 0),
                               pipeline_mode=pl.Buffered(1))],
        out_specs=[pl.BlockSpec((mem_len, XATTN_WIDTH), lambda b: (b, 0)),
                   pl.BlockSpec((XATTN_HEADS * XATT_V_ROWS, mem_len), lambda b: (b, 0))],
        out_shape=[jax.ShapeDtypeStruct((batch * mem_len, XATTN_WIDTH), BF16),
                   jax.ShapeDtypeStruct((batch * XATTN_HEADS * XATT_V_ROWS, mem_len), BF16)],
        scratch_shapes=[pltpu.VMEM((D_MODEL, 2 * XATTN_WIDTH), BF16)],
        compiler_params=pltpu.CompilerParams(
            dimension_semantics=("arbitrary",), vmem_limit_bytes=32 * MIB),
        name="mem_kv",
    )(mem2, mem_norm_w.reshape(1, D_MODEL), w_kv)


def _xattn_kernel(q_ref, g_ref, k_ref, vt_ref, o_ref):
    scale = XATTN_HEAD_DIM ** -0.5 * math.log2(math.e)
    cols = [slice(h * XATTN_HEAD_DIM, (h + 1) * XATTN_HEAD_DIM) for h in range(XATTN_HEADS)]
    scores = [lax.dot_general(k_ref[:, c], q_ref[:, c], NT_DIMS, preferred_element_type=F32)
              for c in cols]
    for h, c in enumerate(cols):
        s = scores[h] * scale
        p = jnp.exp2(s - jnp.max(s, axis=0, keepdims=True)).astype(BF16)
        pv = jnp.dot(vt_ref[h * XATT_V_ROWS:(h + 1) * XATT_V_ROWS, :], p,
                     preferred_element_type=F32)
        xo = (pv[0:XATTN_HEAD_DIM, :]
              * (1.0 / pv[XATTN_HEAD_DIM:XATTN_HEAD_DIM + 1, :])).T
        o_ref[:, c] = (xo * _silu(g_ref[:, c].astype(F32))).astype(BF16)


def _xattn(proj, mk, mvt, batch, seq, mem_len):
    nq = seq // XATT_TQ
    return pl.pallas_call(
        _xattn_kernel,
        grid=(batch, nq),
        in_specs=[pl.BlockSpec((XATT_TQ, XATTN_WIDTH), lambda b, q: (b * nq + q, COL_XQ // XATTN_WIDTH)),
                  pl.BlockSpec((XATT_TQ, XATTN_WIDTH), lambda b, q: (b * nq + q, COL_XG // XATTN_WIDTH)),
                  pl.BlockSpec((mem_len, XATTN_WIDTH), lambda b, q: (b, 0)),
                  pl.BlockSpec((XATTN_HEADS * XATT_V_ROWS, mem_len), lambda b, q: (b, 0))],
        out_specs=pl.BlockSpec((XATT_TQ, XATTN_WIDTH), lambda b, q: (b * nq + q, 0)),
        out_shape=jax.ShapeDtypeStruct((batch * seq, XATTN_WIDTH), BF16),
        compiler_params=pltpu.CompilerParams(
            dimension_semantics=("arbitrary", "arbitrary"), vmem_limit_bytes=32 * MIB),
        name="xattn",
    )(proj, proj, mk, mvt)


def _out_proj_kernel(d_ref, s_ref, a_ref, w_ref, nw_ref, x_ref, o_ref, wb_ref):
    @pl.when(pl.program_id(0) == 0)
    def _():
        for k0 in range(0, D_MIX, OUT_SUB):
            wb_ref[k0:k0 + OUT_SUB, :] = w_ref[k0:k0 + OUT_SUB, :].astype(BF16)

    for i in range(OUT_TM // OUT_SUB):
        r = slice(i * OUT_SUB, (i + 1) * OUT_SUB)
        y = jnp.dot(d_ref[r, :], wb_ref[0:DIFF_WIDTH, :], preferred_element_type=F32)
        y = y + jnp.dot(s_ref[r, :], wb_ref[DIFF_WIDTH:DIFF_WIDTH + SSD_WIDTH, :],
                        preferred_element_type=F32)
        y = y + jnp.dot(a_ref[r, :], wb_ref[DIFF_WIDTH + SSD_WIDTH:D_MIX, :],
                        preferred_element_type=F32)
        o_ref[r, :] = x_ref[r, :] + y * _rms_scale(y) * nw_ref[...]


def _out_proj(diff_out, ssd_out, xattn_out, w_out, post_norm_w, x2):
    m = x2.shape[0]
    return pl.pallas_call(
        _out_proj_kernel,
        grid=(m // OUT_TM,),
        in_specs=[pl.BlockSpec((OUT_TM, DIFF_WIDTH), lambda i: (i, 0)),
                  pl.BlockSpec((OUT_TM, SSD_WIDTH), lambda i: (i, 0)),
                  pl.BlockSpec((OUT_TM, XATTN_WIDTH), lambda i: (i, 0)),
                  pl.BlockSpec((D_MIX, D_MODEL), lambda i: (0, 0), pipeline_mode=pl.Buffered(1)),
                  pl.BlockSpec((1, D_MODEL), lambda i: (0, 0)),
                  pl.BlockSpec((OUT_TM, D_MODEL), lambda i: (i, 0))],
        out_specs=pl.BlockSpec((OUT_TM, D_MODEL), lambda i: (i, 0)),
        out_shape=jax.ShapeDtypeStruct((m, D_MODEL), F32),
        scratch_shapes=[pltpu.VMEM((D_MIX, D_MODEL), BF16)],
        compiler_params=pltpu.CompilerParams(
            dimension_semantics=("arbitrary",), vmem_limit_bytes=48 * MIB),
        name="out_proj",
    )(diff_out, ssd_out, xattn_out, w_out, post_norm_w.reshape(1, D_MODEL), x2)


def kernel(x, mem, positions, pre_norm_w, w_in, lambda_q1, lambda_k1, lambda_q2, lambda_k2,
           diff_subln_w, conv_w, conv_b, dt_bias, a_log, d_skip, ssd_norm_w, mem_norm_w,
           w_mem_kv, w_out, post_norm_w):
    batch, seq, _ = x.shape
    mem_len = mem.shape[1]
    assert pre_norm_w.shape[0] == 1, "single-layer kernel"
    x2 = x.reshape(batch * seq, D_MODEL)

    lam_vecs = jnp.stack([lambda_q1[0], lambda_k1[0], lambda_q2[0], lambda_k2[0]]).astype(F32)

    cos, sin = _rope_tables(positions)
    proj, v_t, dt_raw_t = _in_proj(x2, pre_norm_w[0], jnp.transpose(w_in[0]), cos, sin)
    diff_out = _diff_attention(proj, v_t, lam_vecs, diff_subln_w[0], batch, seq)
    ssd_out = _ssd(proj, dt_raw_t, conv_w[0], conv_b[0], dt_bias[0], a_log[0], d_skip[0],
                   ssd_norm_w[0], batch, seq)
    mk, mvt = _mem_kv(mem.reshape(batch * mem_len, D_MODEL), mem_norm_w[0], w_mem_kv[0],
                      batch, mem_len)
    xattn_out = _xattn(proj, mk, mvt, batch, seq, mem_len)
    out = _out_proj(diff_out, ssd_out, xattn_out, w_out[0], post_norm_w[0], x2)
    return out.reshape(batch, seq, D_MODEL)
```

```python
import functools
import math

import jax
import jax.numpy as jnp
from jax import lax
from jax.experimental import pallas as pl
from jax.experimental.pallas import tpu as pltpu

F32 = jnp.float32
BF16 = jnp.bfloat16

D_MODEL = 2048
DIFF_HEADS = 8
DIFF_QK_DIM = 64
DIFF_V_DIM = 128
DIFF_WIDTH = DIFF_HEADS * DIFF_V_DIM
SSD_HEADS = 8
SSD_HEAD_DIM = 64
SSD_WIDTH = SSD_HEADS * SSD_HEAD_DIM
SSD_GROUPS = 2
SSD_STATE = 128
SSD_CONV = 4
SSD_CHUNK = 128
SSD_CONV_DIM = SSD_WIDTH + 2 * SSD_GROUPS * SSD_STATE
XATTN_HEADS = 4
XATTN_HEAD_DIM = 128
XATTN_WIDTH = XATTN_HEADS * XATTN_HEAD_DIM
D_MIX = DIFF_WIDTH + SSD_WIDTH + XATTN_WIDTH
ROPE_THETA = 10000.0
NORM_EPS = 1e-6
LAMBDA_INIT = 0.8 - 0.6 * math.exp(-0.3 * 0)

LANES = 128
SUBLANES = 8
MIB = 1024 * 1024

COL_Q = 0
COL_K = COL_Q + DIFF_WIDTH
COL_G = COL_K + DIFF_WIDTH
COL_Z = COL_G + DIFF_WIDTH
COL_XS = COL_Z + SSD_WIDTH
COL_B = COL_XS + SSD_WIDTH
COL_C = COL_B + SSD_GROUPS * SSD_STATE
COL_XQ = COL_C + SSD_GROUPS * SSD_STATE
COL_XG = COL_XQ + XATTN_WIDTH
N_MAIN = COL_XG + XATTN_WIDTH
REF_V = 2 * DIFF_WIDTH
REF_G = REF_V + DIFF_WIDTH
REF_DT = REF_G + DIFF_WIDTH + SSD_WIDTH + SSD_CONV_DIM
REF_XQ = REF_DT + SSD_HEADS
D_IN = REF_XQ + 2 * XATTN_WIDTH

PROJ_TM = 512
PROJ_TN = 512
NORM_ROWS = 256
ATT_TQ = 512
ATT_TK = 512
ATT_HEADS_PER_STEP = 4
ATT_SUM_ROWS = 16
ATT_AHEAD = 2
XATT_TQ = 1024
OUT_TM = 512
OUT_SUB = 256
ROPE_TM = 512
SSD_CHUNKS_PER_STEP = 4
DT_ROWS = 16

NT_DIMS = (((1,), (1,)), ((), ()))
TN_DIMS = (((0,), (0,)), ((), ()))


def _rms_scale(x):
    return lax.rsqrt(jnp.mean(x * x, axis=-1, keepdims=True) + NORM_EPS)


def _silu(x):
    return x * jax.nn.sigmoid(x)


def _rope_table_rows(pos_ref, invf_ref, cos_ref, sin_ref):
    ang = pos_ref[...].astype(F32) * invf_ref[...]
    lane = lax.broadcasted_iota(jnp.int32, ang.shape, 1)
    first_half = (lane % DIFF_QK_DIM) < (DIFF_QK_DIM // 2)
    s = jnp.sin(ang)
    cos_ref[...] = jnp.cos(ang)
    sin_ref[...] = jnp.where(first_half, -s, s)


def _rope(x, cos, sin_signed):
    lane = lax.broadcasted_iota(jnp.int32, x.shape, 1)
    first_half = (lane % DIFF_QK_DIM) < (DIFF_QK_DIM // 2)
    half = DIFF_QK_DIM // 2
    rot = jnp.where(first_half, pltpu.roll(x, LANES - half, 1), pltpu.roll(x, half, 1))
    return x * cos + rot * sin_signed


def _w_in_prep_kernel(wt_ref, dt_rows_ref, pos_ref, invf_ref,
                      wm_ref, wv_ref, wdt_ref, cos_ref, sin_ref):
    j = pl.program_id(0)
    main_blocks = N_MAIN // PROJ_TN
    _rope_table_rows(pos_ref, invf_ref, cos_ref, sin_ref)

    @pl.when(j < main_blocks)
    def _():
        wm_ref[...] = wt_ref[...].T.astype(BF16)

    @pl.when((j >= main_blocks) & (j < main_blocks + DIFF_WIDTH // PROJ_TN))
    def _():
        wv_ref[...] = wt_ref[...].astype(BF16)

    @pl.when(j == 0)
    def _():
        row = lax.broadcasted_iota(jnp.int32, (DT_ROWS, D_MODEL), 0)
        wdt_ref[...] = jnp.where(row < SSD_HEADS, dt_rows_ref[...], 0.0).astype(BF16)


def _w_in_prep(w_t, positions):
    n, k = w_t.shape
    assert n == D_IN and k == D_MODEL
    main_blocks = N_MAIN // PROJ_TN
    v_blocks = DIFF_WIDTH // PROJ_TN
    last_w = main_blocks + v_blocks - 1
    n_pos = positions.size
    steps = n_pos // ROPE_TM
    assert steps > last_w
    inv_freq = 1.0 / (ROPE_THETA ** (jnp.arange(0, DIFF_QK_DIM, 2, dtype=F32) / DIFF_QK_DIM))
    invf = jnp.tile(inv_freq, LANES // inv_freq.shape[0]).reshape(1, LANES)

    def src_row(j):
        main = jnp.where(j < REF_V // PROJ_TN, j * PROJ_TN,
                         jnp.where(j < (REF_DT - DIFF_WIDTH) // PROJ_TN,
                                   j * PROJ_TN + DIFF_WIDTH,
                                   j * PROJ_TN + DIFF_WIDTH + SSD_HEADS))
        jw = jnp.minimum(j, last_w)
        return pl.multiple_of(jnp.where(jw < main_blocks, main,
                                        REF_V + (jw - main_blocks) * PROJ_TN), SUBLANES)

    return pl.pallas_call(
        _w_in_prep_kernel,
        grid=(steps,),
        in_specs=[pl.BlockSpec((pl.Element(PROJ_TN), pl.Element(D_MODEL)),
                               lambda j: (src_row(j), 0)),
                  pl.BlockSpec((DT_ROWS, D_MODEL), lambda j: (REF_DT // DT_ROWS, 0)),
                  pl.BlockSpec((ROPE_TM, 1), lambda j: (j, 0)),
                  pl.BlockSpec((1, LANES), lambda j: (0, 0))],
        out_specs=[pl.BlockSpec((D_MODEL, PROJ_TN),
                                lambda j: (0, jnp.minimum(j, main_blocks - 1))),
                   pl.BlockSpec((PROJ_TN, D_MODEL),
                                lambda j: (jnp.clip(j - main_blocks, 0, v_blocks - 1), 0)),
                   pl.BlockSpec((DT_ROWS, D_MODEL), lambda j: (0, 0)),
                   pl.BlockSpec((ROPE_TM, LANES), lambda j: (j, 0)),
                   pl.BlockSpec((ROPE_TM, LANES), lambda j: (j, 0))],
        out_shape=[jax.ShapeDtypeStruct((k, N_MAIN), BF16),
                   jax.ShapeDtypeStruct((DIFF_WIDTH, k), BF16),
                   jax.ShapeDtypeStruct((DT_ROWS, k), BF16),
                   jax.ShapeDtypeStruct((n_pos, LANES), F32),
                   jax.ShapeDtypeStruct((n_pos, LANES), F32)],
        compiler_params=pltpu.CompilerParams(
            dimension_semantics=("arbitrary",), vmem_limit_bytes=32 * MIB),
        name="w_in_prep",
    )(w_t, w_t, positions.reshape(n_pos, 1), invf)


def _in_proj_kernel(x_ref, nw_ref, w_ref, wv_ref, wdt_ref, cos_ref, sin_ref,
                    o_ref, vt_ref, dt_ref, h_ref):
    def norm_rows(i, carry):
        r = pl.ds(pl.multiple_of(i * NORM_ROWS, NORM_ROWS), NORM_ROWS)
        x = x_ref[r, :]
        h_ref[r, :] = (x * _rms_scale(x) * nw_ref[...]).astype(BF16)
        return carry
    lax.fori_loop(0, PROJ_TM // NORM_ROWS, norm_rows, 0)
    dt_t = lax.dot_general(wdt_ref[...], h_ref[...], NT_DIMS, preferred_element_type=F32)
    dt_ref[...] = dt_t[0:SSD_HEADS, :]
    vt_ref[...] = lax.dot_general(wv_ref[...], h_ref[...], NT_DIMS,
                                  preferred_element_type=F32).astype(BF16)
    q_scale = DIFF_QK_DIM ** -0.5 * math.log2(math.e)
    for n in range(N_MAIN // PROJ_TN):
        c0 = n * PROJ_TN
        y = jnp.dot(h_ref[...], w_ref[:, c0:c0 + PROJ_TN], preferred_element_type=F32)
        if c0 < COL_G:
            scale = q_scale if c0 < COL_K else 1.0
            for hd in range(PROJ_TN // LANES):
                y_h = _rope(y[:, hd * LANES:(hd + 1) * LANES], cos_ref[...], sin_ref[...])
                o_ref[:, c0 + hd * LANES:c0 + (hd + 1) * LANES] = (y_h * scale).astype(BF16)
        else:
            o_ref[:, c0:c0 + PROJ_TN] = y.astype(BF16)


def _in_proj(x2, pre_norm_w, w_main, w_v, w_dt, cos, sin):
    m = x2.shape[0]
    resident = pl.Buffered(1)
    return pl.pallas_call(
        _in_proj_kernel,
        grid=(m // PROJ_TM,),
        in_specs=[pl.BlockSpec((PROJ_TM, D_MODEL), lambda i: (i, 0)),
                  pl.BlockSpec((1, D_MODEL), lambda i: (0, 0)),
                  pl.BlockSpec((D_MODEL, N_MAIN), lambda i: (0, 0), pipeline_mode=resident),
                  pl.BlockSpec((DIFF_WIDTH, D_MODEL), lambda i: (0, 0), pipeline_mode=resident),
                  pl.BlockSpec((DT_ROWS, D_MODEL), lambda i: (0, 0), pipeline_mode=resident),
                  pl.BlockSpec((PROJ_TM, LANES), lambda i: (i, 0)),
                  pl.BlockSpec((PROJ_TM, LANES), lambda i: (i, 0))],
        out_specs=[pl.BlockSpec((PROJ_TM, N_MAIN), lambda i: (i, 0)),
                   pl.BlockSpec((DIFF_WIDTH, PROJ_TM), lambda i: (0, i)),
                   pl.BlockSpec((SSD_HEADS, PROJ_TM), lambda i: (0, i))],
        out_shape=[jax.ShapeDtypeStruct((m, N_MAIN), BF16),
                   jax.ShapeDtypeStruct((DIFF_WIDTH, m), BF16),
                   jax.ShapeDtypeStruct((SSD_HEADS, m), F32)],
        scratch_shapes=[pltpu.VMEM((PROJ_TM, D_MODEL), BF16)],
        compiler_params=pltpu.CompilerParams(
            dimension_semantics=("arbitrary",),
            vmem_limit_bytes=56 * MIB),
        name="in_proj",
    )(x2, pre_norm_w.reshape(1, D_MODEL), w_main, w_v, w_dt, cos, sin)


def _diff_attn_kernel(lam_ref, q_ref, k_ref, v_ref, g_ref, subw_ref, o_ref, vt_ref, *, seq):
    qi = pl.program_id(2)
    heads = ATT_HEADS_PER_STEP

    @pl.when(qi == 0)
    def _():
        for g in range(heads):
            vt_ref[g, 0:DIFF_V_DIM, :] = v_ref[g * DIFF_V_DIM:(g + 1) * DIFF_V_DIM, :]
        vt_ref[:, DIFF_V_DIM:, :] = jnp.ones((heads, ATT_SUM_ROWS, seq), BF16)

    lane = lax.broadcasted_iota(jnp.int32, (ATT_TQ, LANES), 1)
    q_maps = []
    for g in range(heads):
        q = q_ref[:, g * LANES:(g + 1) * LANES]
        q_maps.append((jnp.where(lane < DIFF_QK_DIM, q, jnp.zeros_like(q)),
                       jnp.where(lane >= DIFF_QK_DIM, q, jnp.zeros_like(q))))

    lv = lam_ref[...]
    lam = (jnp.exp(jnp.sum(lv[0:1] * lv[1:2], axis=-1, keepdims=True))
           - jnp.exp(jnp.sum(lv[2:3] * lv[3:4], axis=-1, keepdims=True)) + LAMBDA_INIT)

    def run(n_tiles):
        diag = (n_tiles - 1) * ATT_TK
        half = ATT_TK // 2
        blocks = [(j * ATT_TK, (j + 1) * ATT_TK, 0, False) for j in range(n_tiles - 1)]
        blocks += [(diag, diag + half, 0, True), (diag + half, diag + ATT_TK, half, True)]
        items = [(blk, g, c) for blk in blocks for g in range(heads) for c in range(2)]

        def scores(blk, g, c):
            k0, k1, q0, masked = blk
            kt = k_ref[k0:k1, g * LANES:(g + 1) * LANES]
            s = lax.dot_general(kt, q_maps[g][c][q0:, :], NT_DIMS, preferred_element_type=F32)
            if not masked:
                return s
            key = lax.broadcasted_iota(jnp.int32, s.shape, 0) + (k0 - diag)
            qry = lax.broadcasted_iota(jnp.int32, s.shape, 1) + q0
            return jnp.where(key <= qry, s, -jnp.inf)

        pending = {t: scores(*items[t]) for t in range(min(ATT_AHEAD, len(items)))}
        m, acc = {}, {}
        for t, (blk, g, c) in enumerate(items):
            k0, k1, q0, _ = blk
            i = 2 * g + c
            s = pending.pop(t)
            m_new = jnp.max(s, axis=0, keepdims=True)
            first = i not in m
            if not first:
                m_old = m[i][:, q0:]
                m_new = jnp.maximum(m_old, m_new)
            p = jnp.exp2(s - m_new).astype(BF16)
            if t + ATT_AHEAD < len(items):
                pending[t + ATT_AHEAD] = scores(*items[t + ATT_AHEAD])
            pv = jnp.dot(vt_ref[g, :, k0:k1], p, preferred_element_type=F32)
            if first:
                m[i], acc[i] = m_new, pv
            else:
                new_acc = jnp.exp2(m_old - m_new) * acc[i][:, q0:] + pv
                if q0:
                    m_new = jnp.concatenate([m[i][:, :q0], m_new], axis=1)
                    new_acc = jnp.concatenate([acc[i][:, :q0], new_acc], axis=1)
                m[i], acc[i] = m_new, new_acc

        for g in range(heads):
            c = slice(g * LANES, (g + 1) * LANES)
            a1, a2 = acc[2 * g], acc[2 * g + 1]
            l1 = a1[DIFF_V_DIM:DIFF_V_DIM + 1, :]
            l2 = a2[DIFF_V_DIM:DIFF_V_DIM + 1, :]
            o_t = (a1[0:DIFF_V_DIM, :] * (1.0 / l1)
                   - lam * (a2[0:DIFF_V_DIM, :] * (1.0 / l2)))
            o = o_t.T
            o = o * _rms_scale(o) * subw_ref[...] * (1.0 - LAMBDA_INIT)
            o_ref[:, c] = (o * _silu(g_ref[:, c].astype(F32))).astype(BF16)

    for n in range(seq // ATT_TQ):
        pl.when(qi == n)(functools.partial(run, n + 1))


def _diff_attention(proj, v_t, lam_vecs, subln_w, batch, seq):
    assert ATT_TQ == ATT_TK and seq % ATT_TQ == 0 and DIFF_V_DIM == LANES
    nq = seq // ATT_TQ
    heads = ATT_HEADS_PER_STEP
    width = heads * LANES
    kernel = functools.partial(_diff_attn_kernel, seq=seq)
    return pl.pallas_call(
        kernel,
        grid=(batch, DIFF_HEADS // heads, nq),
        in_specs=[
            pl.BlockSpec((4, DIFF_QK_DIM), lambda b, h, q: (0, 0)),
            pl.BlockSpec((ATT_TQ, width), lambda b, h, q: (b * nq + q, COL_Q // width + h)),
            pl.BlockSpec((seq, width), lambda b, h, q: (b, COL_K // width + h)),
            pl.BlockSpec((width, seq), lambda b, h, q: (h, b)),
            pl.BlockSpec((ATT_TQ, width), lambda b, h, q: (b * nq + q, COL_G // width + h)),
            pl.BlockSpec((1, DIFF_V_DIM), lambda b, h, q: (0, 0)),
        ],
        out_specs=pl.BlockSpec((ATT_TQ, width), lambda b, h, q: (b * nq + q, h)),
        out_shape=jax.ShapeDtypeStruct((batch * seq, DIFF_WIDTH), BF16),
        scratch_shapes=[pltpu.VMEM((heads, DIFF_V_DIM + ATT_SUM_ROWS, seq), BF16)],
        compiler_params=pltpu.CompilerParams(
            dimension_semantics=("arbitrary", "arbitrary", "arbitrary"),
            vmem_limit_bytes=48 * MIB),
        name="diff_attn",
    )(lam_vecs, proj, proj, v_t, proj, subln_w.reshape(1, DIFF_V_DIM))


def _ssd_kernel(xs_ref, b_ref, c_ref, z_ref, dt_ref, shift_ref, triu_ref, sel_ref, cw_ref,
                cb_ref, dtb_ref, alog_ref, dsk_ref, nw_ref, o_ref, xext_ref, state_ref):
    L = SSD_CHUNK
    T = SSD_CHUNKS_PER_STEP
    gs = SSD_GROUPS * SSD_STATE
    pairs = SSD_HEADS // 2

    @pl.when(pl.program_id(1) == 0)
    def _():
        xext_ref[0:L, :] = jnp.zeros((L, SSD_CONV_DIM), BF16)
        state_ref[...] = jnp.zeros(state_ref.shape, F32)

    xext_ref[L:(T + 1) * L, 0:SSD_WIDTH] = xs_ref[...]
    xext_ref[L:(T + 1) * L, SSD_WIDTH:SSD_WIDTH + gs] = b_ref[...]
    xext_ref[L:(T + 1) * L, SSD_WIDTH + gs:SSD_CONV_DIM] = c_ref[...]

    carry = {"state": [state_ref[p] for p in range(pairs)], "cols_split": [None] * T}
    programs = [_ssd_chunk(c, carry, xext_ref, z_ref, dt_ref, shift_ref, triu_ref, cw_ref,
                           cb_ref, dtb_ref, alog_ref, dsk_ref, nw_ref, o_ref)
                for c in range(T)]

    def stage():
        for prog in programs:
            next(prog)

    stage()
    stage()
    carry["spread"] = jnp.dot(jnp.concatenate(carry["cols_split"], axis=0), sel_ref[...],
                              preferred_element_type=F32)
    stage()
    stage()
    stage()
    for p in range(pairs):
        state_ref[p] = carry["state"][p]
    xext_ref[0:L, :] = xext_ref[T * L:(T + 1) * L, :]


SSD_PAIR_KINDS = (0, 2)


def _ssd_chunk(c, carry, xext_ref, z_ref, dt_ref, shift_ref, triu_ref, cw_ref,
               cb_ref, dtb_ref, alog_ref, dsk_ref, nw_ref, o_ref):
    L = SSD_CHUNK
    H = SSD_HEADS
    gs = SSD_GROUPS * SSD_STATE
    rows_c = slice(c * L, (c + 1) * L)

    delayed = jnp.dot(shift_ref[...], xext_ref[c * L:(c + 2) * L, :],
                      preferred_element_type=F32)
    conv = (cb_ref[...] + xext_ref[(c + 1) * L:(c + 2) * L, :].astype(F32)
            * cw_ref[SSD_CONV - 1:SSD_CONV, :])
    for k in range(SSD_CONV - 1):
        conv = conv + delayed[k * L:(k + 1) * L, :] * cw_ref[k:k + 1, :]
    xbc = _silu(conv)
    xs = xbc[:, 0:SSD_WIDTH]
    bm = xbc[:, SSD_WIDTH:SSD_WIDTH + gs].astype(BF16)
    cm = xbc[:, SSD_WIDTH + gs:SSD_CONV_DIM].astype(BF16)
    yield

    dt_in = dt_ref[:, rows_c] + dtb_ref[...]
    dt = jnp.maximum(dt_in, 0.0) + jnp.log1p(jnp.exp(-jnp.abs(dt_in)))
    a_dt = dt * (-jnp.exp(alog_ref[...]))
    hi = a_dt.astype(BF16)
    r1 = a_dt - hi.astype(F32)
    mid = r1.astype(BF16)
    lo = (r1 - mid.astype(F32)).astype(BF16)
    triu = triu_ref[...]
    a_cs = (jnp.dot(hi, triu, preferred_element_type=F32)
            + jnp.dot(mid, triu, preferred_element_type=F32)
            + jnp.dot(lo, triu, preferred_element_type=F32))
    a_end = a_cs[:, L - 1:L]
    chunk_decay = jnp.exp(a_end)
    rows = jnp.concatenate([dt * jnp.exp(a_end - a_cs), a_cs, jnp.exp(a_cs),
                            jnp.zeros((L - 3 * H, L), F32)], axis=0)
    cols = rows.T
    c_hi = cols.astype(BF16)
    c_r1 = cols - c_hi.astype(F32)
    c_mid = c_r1.astype(BF16)
    c_lo = (c_r1 - c_mid.astype(F32)).astype(BF16)
    carry["cols_split"][c] = jnp.concatenate([c_hi, c_mid, c_lo], axis=1)
    yield
    spread = carry["spread"][rows_c, :]

    row = lax.broadcasted_iota(jnp.int32, (L, L), 0)
    col = lax.broadcasted_iota(jnp.int32, (L, L), 1)
    lower = row >= col
    lane = lax.broadcasted_iota(jnp.int32, (L, LANES), 1)
    left = lane < SSD_HEAD_DIM
    top = row < SSD_HEAD_DIM
    pairs = SSD_HEADS // 2

    def pair_lanes(slot, p):
        i = slot * pairs + p
        return spread[:, i * LANES:(i + 1) * LANES]

    def head_lanes(h):
        i = len(SSD_PAIR_KINDS) * pairs + h
        return spread[:, i * LANES:(i + 1) * LANES]

    group_cb = []
    for g in range(SSD_GROUPS):
        group_cb.append(lax.dot_general(cm[:, g * SSD_STATE:(g + 1) * SSD_STATE],
                                        bm[:, g * SSD_STATE:(g + 1) * SSD_STATE],
                                        NT_DIMS, preferred_element_type=F32))
    yield

    group_of = lambda p: (2 * p) // (SSD_HEADS // SSD_GROUPS)
    y_diag, chunk_state = [], []
    for p in range(pairs):
        bg = bm[:, group_of(p) * SSD_STATE:(group_of(p) + 1) * SSD_STATE]
        xs_p = xs[:, p * LANES:(p + 1) * LANES]
        xs_b = xs_p.astype(BF16)
        y_heads = []
        for hh in range(2):
            h = 2 * p + hh
            seg = head_lanes(h) - a_cs[h:h + 1, :]
            decay = jnp.exp(jnp.where(lower, seg, -jnp.inf))
            scores = (group_cb[group_of(p)] * decay * dt[h:h + 1, :]).astype(BF16)
            y_heads.append(jnp.dot(scores, xs_b, preferred_element_type=F32))
        y_diag.append(jnp.where(left, y_heads[0], y_heads[1]))
        x_w = (xs_p * pair_lanes(0, p)).astype(BF16)
        chunk_state.append(lax.dot_general(x_w, bg, TN_DIMS, preferred_element_type=F32))
    yield

    ys, new_state = [], []
    for p in range(pairs):
        cg = cm[:, group_of(p) * SSD_STATE:(group_of(p) + 1) * SSD_STATE]
        st = carry["state"][p]
        y_off = (lax.dot_general(cg, st.astype(BF16), NT_DIMS, preferred_element_type=F32)
                 * pair_lanes(1, p))
        st_decay = jnp.where(top, chunk_decay[2 * p:2 * p + 1, :],
                             chunk_decay[2 * p + 1:2 * p + 2, :])
        new_state.append(st * st_decay + chunk_state[p])
        y = (y_diag[p] + y_off
             + xs[:, p * LANES:(p + 1) * LANES] * dsk_ref[:, p * LANES:(p + 1) * LANES])
        ys.append(y * _silu(z_ref[rows_c, p * LANES:(p + 1) * LANES].astype(F32)))
    carry["state"] = new_state

    grp_pairs = pairs // SSD_GROUPS
    grp_width = SSD_WIDTH // SSD_GROUPS
    for g in range(SSD_GROUPS):
        members = ys[g * grp_pairs:(g + 1) * grp_pairs]
        ssq = sum(jnp.sum(y * y, axis=-1, keepdims=True) for y in members)
        inv = lax.rsqrt(ssq / grp_width + NORM_EPS)
        for i, y in enumerate(members):
            c0 = (g * grp_pairs + i) * LANES
            o_ref[rows_c, c0:c0 + LANES] = (y * inv * nw_ref[:, c0:c0 + LANES]).astype(BF16)
    yield


def _ssd(proj, dt_raw_t, conv_w, conv_b, dt_bias, a_log, d_skip, ssd_norm_w, batch, seq):
    gs = SSD_GROUPS * SSD_STATE
    L = SSD_CHUNK
    rows = SSD_CHUNKS_PER_STEP * L
    nc = seq // rows
    row_map = lambda b, c: b * nc + c
    const = lambda b, c: (0, 0)
    t = jnp.arange(L)[:, None]
    src = jnp.arange(2 * L)[None, :]
    shift = jnp.concatenate([(src == L + t - (SSD_CONV - 1 - k)) for k in range(SSD_CONV - 1)],
                            axis=0).astype(BF16)
    triu = (jnp.arange(L)[:, None] <= jnp.arange(L)[None, :]).astype(BF16)
    lane = jnp.arange(LANES)
    sel_blocks = []
    for kind in SSD_PAIR_KINDS:
        for p in range(SSD_HEADS // 2):
            src_col = kind * SSD_HEADS + 2 * p + (lane >= SSD_HEAD_DIM)
            sel_blocks.append(jnp.arange(L)[:, None] == src_col[None, :])
    for h in range(SSD_HEADS):
        sel_blocks.append(jnp.broadcast_to(jnp.arange(L)[:, None] == SSD_HEADS + h, (L, LANES)))
    sel = jnp.tile(jnp.concatenate(sel_blocks, axis=1), (3, 1)).astype(BF16)
    per_head = lambda v: jnp.broadcast_to(v.astype(F32)[:, None], (SSD_HEADS, L))
    return pl.pallas_call(
        _ssd_kernel,
        grid=(batch, nc),
        in_specs=[
            pl.BlockSpec((rows, SSD_WIDTH), lambda b, c: (row_map(b, c), COL_XS // SSD_WIDTH)),
            pl.BlockSpec((rows, gs), lambda b, c: (row_map(b, c), COL_B // gs)),
            pl.BlockSpec((rows, gs), lambda b, c: (row_map(b, c), COL_C // gs)),
            pl.BlockSpec((rows, SSD_WIDTH), lambda b, c: (row_map(b, c), COL_Z // SSD_WIDTH)),
            pl.BlockSpec((SSD_HEADS, rows), lambda b, c: (0, row_map(b, c))),
            pl.BlockSpec(((SSD_CONV - 1) * L, 2 * L), const),
            pl.BlockSpec((L, L), const),
            pl.BlockSpec(sel.shape, const),
            pl.BlockSpec((SSD_CONV, SSD_CONV_DIM), const),
            pl.BlockSpec((1, SSD_CONV_DIM), const),
            pl.BlockSpec((SSD_HEADS, L), const),
            pl.BlockSpec((SSD_HEADS, L), const),
            pl.BlockSpec((1, SSD_WIDTH), const),
            pl.BlockSpec((1, SSD_WIDTH), const),
        ],
        out_specs=pl.BlockSpec((rows, SSD_WIDTH), lambda b, c: (row_map(b, c), 0)),
        out_shape=jax.ShapeDtypeStruct((batch * seq, SSD_WIDTH), BF16),
        scratch_shapes=[pltpu.VMEM((L + rows, SSD_CONV_DIM), BF16),
                        pltpu.VMEM((SSD_HEADS // 2, 2 * SSD_HEAD_DIM, SSD_STATE), F32)],
        compiler_params=pltpu.CompilerParams(
            dimension_semantics=("arbitrary", "arbitrary"),
            vmem_limit_bytes=32 * MIB),
        name="ssd",
    )(proj, proj, proj, proj, dt_raw_t, shift, triu, sel, conv_w, conv_b.reshape(1, SSD_CONV_DIM),
      per_head(dt_bias), per_head(a_log),
      jnp.repeat(d_skip, SSD_HEAD_DIM).reshape(1, SSD_WIDTH), ssd_norm_w.reshape(1, SSD_WIDTH))


XATT_V_ROWS = XATTN_HEAD_DIM + ATT_SUM_ROWS


def _mem_kv_kernel(mem_ref, nw_ref, w_ref, k_ref, vt_ref, wb_ref):
    @pl.when(pl.program_id(0) == 0)
    def _():
        wb_ref[...] = w_ref[...].astype(BF16)

    x = mem_ref[...]
    h = (x * _rms_scale(x) * nw_ref[...]).astype(BF16)
    kv = jnp.dot(h, wb_ref[...], preferred_element_type=F32)
    k_ref[...] = kv[:, 0:XATTN_WIDTH].astype(BF16)
    ones = jnp.ones((ATT_SUM_ROWS, x.shape[0]), BF16)
    for hd in range(XATTN_HEADS):
        c0 = XATTN_WIDTH + hd * XATTN_HEAD_DIM
        r0 = hd * XATT_V_ROWS
        vt_ref[r0:r0 + XATTN_HEAD_DIM, :] = kv[:, c0:c0 + XATTN_HEAD_DIM].T.astype(BF16)
        vt_ref[r0 + XATTN_HEAD_DIM:r0 + XATT_V_ROWS, :] = ones


def _mem_kv(mem2, mem_norm_w, w_kv, batch, mem_len):
    return pl.pallas_call(
        _mem_kv_kernel,
        grid=(batch,),
        in_specs=[pl.BlockSpec((mem_len, D_MODEL), lambda b: (b, 0)),
                  pl.BlockSpec((1, D_MODEL), lambda b: (0, 0)),
                  pl.BlockSpec((D_MODEL, 2 * XATTN_WIDTH), lambda b: (0, 0),
                               pipeline_mode=pl.Buffered(1))],
        out_specs=[pl.BlockSpec((mem_len, XATTN_WIDTH), lambda b: (b, 0)),
                   pl.BlockSpec((XATTN_HEADS * XATT_V_ROWS, mem_len), lambda b: (b, 0))],
        out_shape=[jax.ShapeDtypeStruct((batch * mem_len, XATTN_WIDTH), BF16),
                   jax.ShapeDtypeStruct((batch * XATTN_HEADS * XATT_V_ROWS, mem_len), BF16)],
        scratch_shapes=[pltpu.VMEM((D_MODEL, 2 * XATTN_WIDTH), BF16)],
        compiler_params=pltpu.CompilerParams(
            dimension_semantics=("arbitrary",), vmem_limit_bytes=32 * MIB),
        name="mem_kv",
    )(mem2, mem_norm_w.reshape(1, D_MODEL), w_kv)


def _xattn_kernel(q_ref, g_ref, k_ref, vt_ref, o_ref):
    scale = XATTN_HEAD_DIM ** -0.5 * math.log2(math.e)
    cols = [slice(h * XATTN_HEAD_DIM, (h + 1) * XATTN_HEAD_DIM) for h in range(XATTN_HEADS)]
    scores = [lax.dot_general(k_ref[:, c], q_ref[:, c], NT_DIMS, preferred_element_type=F32)
              for c in cols]
    for h, c in enumerate(cols):
        s = scores[h] * scale
        p = jnp.exp2(s - jnp.max(s, axis=0, keepdims=True)).astype(BF16)
        pv = jnp.dot(vt_ref[h * XATT_V_ROWS:(h + 1) * XATT_V_ROWS, :], p,
                     preferred_element_type=F32)
        xo = (pv[0:XATTN_HEAD_DIM, :]
              * (1.0 / pv[XATTN_HEAD_DIM:XATTN_HEAD_DIM + 1, :])).T
        o_ref[:, c] = (xo * _silu(g_ref[:, c].astype(F32))).astype(BF16)


def _xattn(proj, mk, mvt, batch, seq, mem_len):
    nq = seq // XATT_TQ
    return pl.pallas_call(
        _xattn_kernel,
        grid=(batch, nq),
        in_specs=[pl.BlockSpec((XATT_TQ, XATTN_WIDTH), lambda b, q: (b * nq + q, COL_XQ // XATTN_WIDTH)),
                  pl.BlockSpec((XATT_TQ, XATTN_WIDTH), lambda b, q: (b * nq + q, COL_XG // XATTN_WIDTH)),
                  pl.BlockSpec((mem_len, XATTN_WIDTH), lambda b, q: (b, 0)),
                  pl.BlockSpec((XATTN_HEADS * XATT_V_ROWS, mem_len), lambda b, q: (b, 0))],
        out_specs=pl.BlockSpec((XATT_TQ, XATTN_WIDTH), lambda b, q: (b * nq + q, 0)),
        out_shape=jax.ShapeDtypeStruct((batch * seq, XATTN_WIDTH), BF16),
        compiler_params=pltpu.CompilerParams(
            dimension_semantics=("arbitrary", "arbitrary"), vmem_limit_bytes=32 * MIB),
        name="xattn",
    )(proj, proj, mk, mvt)


def _out_proj_kernel(d_ref, s_ref, a_ref, w_ref, nw_ref, x_ref, o_ref, wb_ref):
    @pl.when(pl.program_id(0) == 0)
    def _():
        for k0 in range(0, D_MIX, OUT_SUB):
            wb_ref[k0:k0 + OUT_SUB, :] = w_ref[k0:k0 + OUT_SUB, :].astype(BF16)

    for i in range(OUT_TM // OUT_SUB):
        r = slice(i * OUT_SUB, (i + 1) * OUT_SUB)
        y = jnp.dot(d_ref[r, :], wb_ref[0:DIFF_WIDTH, :], preferred_element_type=F32)
        y = y + jnp.dot(s_ref[r, :], wb_ref[DIFF_WIDTH:DIFF_WIDTH + SSD_WIDTH, :],
                        preferred_element_type=F32)
        y = y + jnp.dot(a_ref[r, :], wb_ref[DIFF_WIDTH + SSD_WIDTH:D_MIX, :],
                        preferred_element_type=F32)
        o_ref[r, :] = x_ref[r, :] + y * _rms_scale(y) * nw_ref[...]


def _out_proj(diff_out, ssd_out, xattn_out, w_out, post_norm_w, x2):
    m = x2.shape[0]
    return pl.pallas_call(
        _out_proj_kernel,
        grid=(m // OUT_TM,),
        in_specs=[pl.BlockSpec((OUT_TM, DIFF_WIDTH), lambda i: (i, 0)),
                  pl.BlockSpec((OUT_TM, SSD_WIDTH), lambda i: (i, 0)),
                  pl.BlockSpec((OUT_TM, XATTN_WIDTH), lambda i: (i, 0)),
                  pl.BlockSpec((D_MIX, D_MODEL), lambda i: (0, 0), pipeline_mode=pl.Buffered(1)),
                  pl.BlockSpec((1, D_MODEL), lambda i: (0, 0)),
                  pl.BlockSpec((OUT_TM, D_MODEL), lambda i: (i, 0))],
        out_specs=pl.BlockSpec((OUT_TM, D_MODEL), lambda i: (i, 0)),
        out_shape=jax.ShapeDtypeStruct((m, D_MODEL), F32),
        scratch_shapes=[pltpu.VMEM((D_MIX, D_MODEL), BF16)],
        compiler_params=pltpu.CompilerParams(
            dimension_semantics=("arbitrary",), vmem_limit_bytes=48 * MIB),
        name="out_proj",
    )(diff_out, ssd_out, xattn_out, w_out, post_norm_w.reshape(1, D_MODEL), x2)


def kernel(x, mem, positions, pre_norm_w, w_in, lambda_q1, lambda_k1, lambda_q2, lambda_k2,
           diff_subln_w, conv_w, conv_b, dt_bias, a_log, d_skip, ssd_norm_w, mem_norm_w,
           w_mem_kv, w_out, post_norm_w):
    batch, seq, _ = x.shape
    mem_len = mem.shape[1]
    assert pre_norm_w.shape[0] == 1, "single-layer kernel"
    x2 = x.reshape(batch * seq, D_MODEL)

    w_main, w_v, w_dt, cos, sin = _w_in_prep(jnp.transpose(w_in[0]), positions)
    lam_vecs = jnp.stack([lambda_q1[0], lambda_k1[0], lambda_q2[0], lambda_k2[0]]).astype(F32)

    proj, v_t, dt_raw_t = _in_proj(x2, pre_norm_w[0], w_main, w_v, w_dt, cos, sin)
    diff_out = _diff_attention(proj, v_t, lam_vecs, diff_subln_w[0], batch, seq)
    ssd_out = _ssd(proj, dt_raw_t, conv_w[0], conv_b[0], dt_bias[0], a_log[0], d_skip[0],
                   ssd_norm_w[0], batch, seq)
    mk, mvt = _mem_kv(mem.reshape(batch * mem_len, D_MODEL), mem_norm_w[0], w_mem_kv[0],
                      batch, mem_len)
    xattn_out = _xattn(proj, mk, mvt, batch, seq, mem_len)
    out = _out_proj(diff_out, ssd_out, xattn_out, w_out[0], post_norm_w[0], x2)
    return out.reshape(batch, seq, D_MODEL)
```

```python
import functools
import math

import jax
import jax.numpy as jnp
from jax import lax
from jax.experimental import pallas as pl
from jax.experimental.pallas import tpu as pltpu

F32 = jnp.float32
BF16 = jnp.bfloat16

D_MODEL = 2048
DIFF_HEADS = 8
DIFF_QK_DIM = 64
DIFF_V_DIM = 128
DIFF_WIDTH = DIFF_HEADS * DIFF_V_DIM
SSD_HEADS = 8
SSD_HEAD_DIM = 64
SSD_WIDTH = SSD_HEADS * SSD_HEAD_DIM
SSD_GROUPS = 2
SSD_STATE = 128
SSD_CONV = 4
SSD_CHUNK = 128
SSD_CONV_DIM = SSD_WIDTH + 2 * SSD_GROUPS * SSD_STATE
XATTN_HEADS = 4
XATTN_HEAD_DIM = 128
XATTN_WIDTH = XATTN_HEADS * XATTN_HEAD_DIM
D_MIX = DIFF_WIDTH + SSD_WIDTH + XATTN_WIDTH
ROPE_THETA = 10000.0
NORM_EPS = 1e-6
LAMBDA_INIT = 0.8 - 0.6 * math.exp(-0.3 * 0)

LANES = 128
SUBLANES = 8
MIB = 1024 * 1024

COL_Q = 0
COL_K = COL_Q + DIFF_WIDTH
COL_G = COL_K + DIFF_WIDTH
COL_Z = COL_G + DIFF_WIDTH
COL_XS = COL_Z + SSD_WIDTH
COL_B = COL_XS + SSD_WIDTH
COL_C = COL_B + SSD_GROUPS * SSD_STATE
COL_XQ = COL_C + SSD_GROUPS * SSD_STATE
COL_XG = COL_XQ + XATTN_WIDTH
N_MAIN = COL_XG + XATTN_WIDTH
REF_V = 2 * DIFF_WIDTH
REF_G = REF_V + DIFF_WIDTH
REF_DT = REF_G + DIFF_WIDTH + SSD_WIDTH + SSD_CONV_DIM
REF_XQ = REF_DT + SSD_HEADS
D_IN = REF_XQ + 2 * XATTN_WIDTH

PROJ_TM = 512
PROJ_TN = 512
NORM_ROWS = 256
ATT_TQ = 512
ATT_TK = 512
ATT_HEADS_PER_STEP = 4
ATT_SUM_ROWS = 16
ATT_AHEAD = 2
XATT_TQ = 1024
OUT_TM = 512
OUT_SUB = 256
ROPE_TM = 512
SSD_CHUNKS_PER_STEP = 8
DT_ROWS = 16

NT_DIMS = (((1,), (1,)), ((), ()))
TN_DIMS = (((0,), (0,)), ((), ()))


def _rms_scale(x):
    return lax.rsqrt(jnp.mean(x * x, axis=-1, keepdims=True) + NORM_EPS)


def _silu(x):
    return x * jax.nn.sigmoid(x)


def _rope_table_rows(pos_ref, invf_ref, cos_ref, sin_ref):
    ang = pos_ref[...].astype(F32) * invf_ref[...]
    lane = lax.broadcasted_iota(jnp.int32, ang.shape, 1)
    first_half = (lane % DIFF_QK_DIM) < (DIFF_QK_DIM // 2)
    s = jnp.sin(ang)
    cos_ref[...] = jnp.cos(ang)
    sin_ref[...] = jnp.where(first_half, -s, s)


def _rope(x, cos, sin_signed):
    lane = lax.broadcasted_iota(jnp.int32, x.shape, 1)
    first_half = (lane % DIFF_QK_DIM) < (DIFF_QK_DIM // 2)
    half = DIFF_QK_DIM // 2
    rot = jnp.where(first_half, pltpu.roll(x, LANES - half, 1), pltpu.roll(x, half, 1))
    return x * cos + rot * sin_signed


def _w_in_prep_kernel(wt_ref, dt_rows_ref, pos_ref, invf_ref,
                      wm_ref, wv_ref, wdt_ref, cos_ref, sin_ref):
    j = pl.program_id(0)
    main_blocks = N_MAIN // PROJ_TN
    last_w = main_blocks + DIFF_WIDTH // PROJ_TN
    rope = functools.partial(_rope_table_rows, pos_ref, invf_ref, cos_ref, sin_ref)

    @pl.when(j < main_blocks)
    def _():
        rope()
        wm_ref[...] = wt_ref[...].T.astype(BF16)

    @pl.when((j >= main_blocks) & (j < last_w))
    def _():
        rope()
        wv_ref[...] = wt_ref[...].astype(BF16)

    pl.when(j >= last_w)(rope)

    @pl.when(j == 0)
    def _():
        row = lax.broadcasted_iota(jnp.int32, (DT_ROWS, D_MODEL), 0)
        wdt_ref[...] = jnp.where(row < SSD_HEADS, dt_rows_ref[...], 0.0).astype(BF16)


def _w_in_prep(w_t, positions):
    n, k = w_t.shape
    assert n == D_IN and k == D_MODEL
    main_blocks = N_MAIN // PROJ_TN
    v_blocks = DIFF_WIDTH // PROJ_TN
    last_w = main_blocks + v_blocks - 1
    n_pos = positions.size
    steps = n_pos // ROPE_TM
    assert steps > last_w
    inv_freq = 1.0 / (ROPE_THETA ** (jnp.arange(0, DIFF_QK_DIM, 2, dtype=F32) / DIFF_QK_DIM))
    invf = jnp.tile(inv_freq, LANES // inv_freq.shape[0]).reshape(1, LANES)

    def src_row(j):
        main = jnp.where(j < REF_V // PROJ_TN, j * PROJ_TN,
                         jnp.where(j < (REF_DT - DIFF_WIDTH) // PROJ_TN,
                                   j * PROJ_TN + DIFF_WIDTH,
                                   j * PROJ_TN + DIFF_WIDTH + SSD_HEADS))
        jw = jnp.minimum(j, last_w)
        return pl.multiple_of(jnp.where(jw < main_blocks, main,
                                        REF_V + (jw - main_blocks) * PROJ_TN), SUBLANES)

    return pl.pallas_call(
        _w_in_prep_kernel,
        grid=(steps,),
        in_specs=[pl.BlockSpec((pl.Element(PROJ_TN), pl.Element(D_MODEL)),
                               lambda j: (src_row(j), 0)),
                  pl.BlockSpec((DT_ROWS, D_MODEL), lambda j: (REF_DT // DT_ROWS, 0)),
                  pl.BlockSpec((ROPE_TM, 1), lambda j: (j, 0)),
                  pl.BlockSpec((1, LANES), lambda j: (0, 0))],
        out_specs=[pl.BlockSpec((D_MODEL, PROJ_TN),
                                lambda j: (0, jnp.minimum(j, main_blocks - 1))),
                   pl.BlockSpec((PROJ_TN, D_MODEL),
                                lambda j: (jnp.clip(j - main_blocks, 0, v_blocks - 1), 0)),
                   pl.BlockSpec((DT_ROWS, D_MODEL), lambda j: (0, 0)),
                   pl.BlockSpec((ROPE_TM, LANES), lambda j: (j, 0)),
                   pl.BlockSpec((ROPE_TM, LANES), lambda j: (j, 0))],
        out_shape=[jax.ShapeDtypeStruct((k, N_MAIN), BF16),
                   jax.ShapeDtypeStruct((DIFF_WIDTH, k), BF16),
                   jax.ShapeDtypeStruct((DT_ROWS, k), BF16),
                   jax.ShapeDtypeStruct((n_pos, LANES), F32),
                   jax.ShapeDtypeStruct((n_pos, LANES), F32)],
        compiler_params=pltpu.CompilerParams(
            dimension_semantics=("arbitrary",), vmem_limit_bytes=32 * MIB),
        name="w_in_prep",
    )(w_t, w_t, positions.reshape(n_pos, 1), invf)


def _in_proj_kernel(x_ref, nw_ref, w_ref, wv_ref, wdt_ref, cos_ref, sin_ref,
                    o_ref, vt_ref, dt_ref, h_ref):
    def norm_rows(i, carry):
        r = pl.ds(pl.multiple_of(i * NORM_ROWS, NORM_ROWS), NORM_ROWS)
        x = x_ref[r, :]
        h_ref[r, :] = (x * _rms_scale(x) * nw_ref[...]).astype(BF16)
        return carry
    lax.fori_loop(0, PROJ_TM // NORM_ROWS, norm_rows, 0)
    dt_t = lax.dot_general(wdt_ref[...], h_ref[...], NT_DIMS, preferred_element_type=F32)
    dt_ref[...] = dt_t[0:SSD_HEADS, :]
    vt_ref[...] = lax.dot_general(wv_ref[...], h_ref[...], NT_DIMS,
                                  preferred_element_type=F32).astype(BF16)
    q_scale = DIFF_QK_DIM ** -0.5 * math.log2(math.e)
    for n in range(N_MAIN // PROJ_TN):
        c0 = n * PROJ_TN
        y = jnp.dot(h_ref[...], w_ref[:, c0:c0 + PROJ_TN], preferred_element_type=F32)
        if c0 < COL_G:
            scale = q_scale if c0 < COL_K else 1.0
            for hd in range(PROJ_TN // LANES):
                y_h = _rope(y[:, hd * LANES:(hd + 1) * LANES], cos_ref[...], sin_ref[...])
                o_ref[:, c0 + hd * LANES:c0 + (hd + 1) * LANES] = (y_h * scale).astype(BF16)
        else:
            o_ref[:, c0:c0 + PROJ_TN] = y.astype(BF16)


def _in_proj(x2, pre_norm_w, w_main, w_v, w_dt, cos, sin):
    m = x2.shape[0]
    resident = pl.Buffered(1)
    return pl.pallas_call(
        _in_proj_kernel,
        grid=(m // PROJ_TM,),
        in_specs=[pl.BlockSpec((PROJ_TM, D_MODEL), lambda i: (i, 0)),
                  pl.BlockSpec((1, D_MODEL), lambda i: (0, 0)),
                  pl.BlockSpec((D_MODEL, N_MAIN), lambda i: (0, 0), pipeline_mode=resident),
                  pl.BlockSpec((DIFF_WIDTH, D_MODEL), lambda i: (0, 0), pipeline_mode=resident),
                  pl.BlockSpec((DT_ROWS, D_MODEL), lambda i: (0, 0), pipeline_mode=resident),
                  pl.BlockSpec((PROJ_TM, LANES), lambda i: (i, 0)),
                  pl.BlockSpec((PROJ_TM, LANES), lambda i: (i, 0))],
        out_specs=[pl.BlockSpec((PROJ_TM, N_MAIN), lambda i: (i, 0)),
                   pl.BlockSpec((DIFF_WIDTH, PROJ_TM), lambda i: (0, i)),
                   pl.BlockSpec((SSD_HEADS, PROJ_TM), lambda i: (0, i))],
        out_shape=[jax.ShapeDtypeStruct((m, N_MAIN), BF16),
                   jax.ShapeDtypeStruct((DIFF_WIDTH, m), BF16),
                   jax.ShapeDtypeStruct((SSD_HEADS, m), F32)],
        scratch_shapes=[pltpu.VMEM((PROJ_TM, D_MODEL), BF16)],
        compiler_params=pltpu.CompilerParams(
            dimension_semantics=("arbitrary",),
            vmem_limit_bytes=56 * MIB),
        name="in_proj",
    )(x2, pre_norm_w.reshape(1, D_MODEL), w_main, w_v, w_dt, cos, sin)


def _diff_attn_kernel(lam_ref, q_ref, k_ref, v_ref, g_ref, subw_ref, o_ref, vt_ref, *, seq):
    qi = pl.program_id(2)
    heads = ATT_HEADS_PER_STEP

    @pl.when(qi == 0)
    def _():
        for g in range(heads):
            vt_ref[g, 0:DIFF_V_DIM, :] = v_ref[g * DIFF_V_DIM:(g + 1) * DIFF_V_DIM, :]
        vt_ref[:, DIFF_V_DIM:, :] = jnp.ones((heads, ATT_SUM_ROWS, seq), BF16)

    lane = lax.broadcasted_iota(jnp.int32, (ATT_TQ, LANES), 1)
    q_maps = []
    for g in range(heads):
        q = q_ref[:, g * LANES:(g + 1) * LANES]
        q_maps.append((jnp.where(lane < DIFF_QK_DIM, q, jnp.zeros_like(q)),
                       jnp.where(lane >= DIFF_QK_DIM, q, jnp.zeros_like(q))))

    lv = lam_ref[...]
    lam = (jnp.exp(jnp.sum(lv[0:1] * lv[1:2], axis=-1, keepdims=True))
           - jnp.exp(jnp.sum(lv[2:3] * lv[3:4], axis=-1, keepdims=True)) + LAMBDA_INIT)

    def run(n_tiles):
        diag = (n_tiles - 1) * ATT_TK
        half = ATT_TK // 2
        blocks = [(j * ATT_TK, (j + 1) * ATT_TK, 0, False) for j in range(n_tiles - 1)]
        blocks += [(diag, diag + half, 0, True), (diag + half, diag + ATT_TK, half, True)]
        items = [(blk, g, c) for blk in blocks for g in range(heads) for c in range(2)]

        def scores(blk, g, c):
            k0, k1, q0, masked = blk
            kt = k_ref[k0:k1, g * LANES:(g + 1) * LANES]
            s = lax.dot_general(kt, q_maps[g][c][q0:, :], NT_DIMS, preferred_element_type=F32)
            if not masked:
                return s
            key = lax.broadcasted_iota(jnp.int32, s.shape, 0) + (k0 - diag)
            qry = lax.broadcasted_iota(jnp.int32, s.shape, 1) + q0
            return jnp.where(key <= qry, s, -jnp.inf)

        pending = {t: scores(*items[t]) for t in range(min(ATT_AHEAD, len(items)))}
        m, acc = {}, {}
        for t, (blk, g, c) in enumerate(items):
            k0, k1, q0, _ = blk
            i = 2 * g + c
            s = pending.pop(t)
            m_new = jnp.max(s, axis=0, keepdims=True)
            first = i not in m
            if not first:
                m_old = m[i][:, q0:]
                m_new = jnp.maximum(m_old, m_new)
            p = jnp.exp2(s - m_new).astype(BF16)
            if t + ATT_AHEAD < len(items):
                pending[t + ATT_AHEAD] = scores(*items[t + ATT_AHEAD])
            pv = jnp.dot(vt_ref[g, :, k0:k1], p, preferred_element_type=F32)
            if first:
                m[i], acc[i] = m_new, pv
            else:
                new_acc = jnp.exp2(m_old - m_new) * acc[i][:, q0:] + pv
                if q0:
                    m_new = jnp.concatenate([m[i][:, :q0], m_new], axis=1)
                    new_acc = jnp.concatenate([acc[i][:, :q0], new_acc], axis=1)
                m[i], acc[i] = m_new, new_acc

        for g in range(heads):
            c = slice(g * LANES, (g + 1) * LANES)
            a1, a2 = acc[2 * g], acc[2 * g + 1]
            l1 = a1[DIFF_V_DIM:DIFF_V_DIM + 1, :]
            l2 = a2[DIFF_V_DIM:DIFF_V_DIM + 1, :]
            o_t = (a1[0:DIFF_V_DIM, :] * (1.0 / l1)
                   - lam * (a2[0:DIFF_V_DIM, :] * (1.0 / l2)))
            o = o_t.T
            o = o * _rms_scale(o) * subw_ref[...] * (1.0 - LAMBDA_INIT)
            o_ref[:, c] = (o * _silu(g_ref[:, c].astype(F32))).astype(BF16)

    for n in range(seq // ATT_TQ):
        pl.when(qi == n)(functools.partial(run, n + 1))


def _diff_attention(proj, v_t, lam_vecs, subln_w, batch, seq):
    assert ATT_TQ == ATT_TK and seq % ATT_TQ == 0 and DIFF_V_DIM == LANES
    nq = seq // ATT_TQ
    heads = ATT_HEADS_PER_STEP
    width = heads * LANES
    kernel = functools.partial(_diff_attn_kernel, seq=seq)
    return pl.pallas_call(
        kernel,
        grid=(batch, DIFF_HEADS // heads, nq),
        in_specs=[
            pl.BlockSpec((4, DIFF_QK_DIM), lambda b, h, q: (0, 0)),
            pl.BlockSpec((ATT_TQ, width), lambda b, h, q: (b * nq + q, COL_Q // width + h)),
            pl.BlockSpec((seq, width), lambda b, h, q: (b, COL_K // width + h)),
            pl.BlockSpec((width, seq), lambda b, h, q: (h, b)),
            pl.BlockSpec((ATT_TQ, width), lambda b, h, q: (b * nq + q, COL_G // width + h)),
            pl.BlockSpec((1, DIFF_V_DIM), lambda b, h, q: (0, 0)),
        ],
        out_specs=pl.BlockSpec((ATT_TQ, width), lambda b, h, q: (b * nq + q, h)),
        out_shape=jax.ShapeDtypeStruct((batch * seq, DIFF_WIDTH), BF16),
        scratch_shapes=[pltpu.VMEM((heads, DIFF_V_DIM + ATT_SUM_ROWS, seq), BF16)],
        compiler_params=pltpu.CompilerParams(
            dimension_semantics=("arbitrary", "arbitrary", "arbitrary"),
            vmem_limit_bytes=48 * MIB),
        name="diff_attn",
    )(lam_vecs, proj, proj, v_t, proj, subln_w.reshape(1, DIFF_V_DIM))


def _ssd_kernel(xs_ref, b_ref, c_ref, z_ref, dt_ref, shift_ref, triu_ref, selp_ref, selh_ref,
                cw_ref, cb_ref, dtb_ref, alog_ref, dsk_ref, nw_ref, o_ref, xext_ref, state_ref):
    L = SSD_CHUNK
    T = SSD_CHUNKS_PER_STEP
    gs = SSD_GROUPS * SSD_STATE
    pairs = SSD_HEADS // 2

    @pl.when(pl.program_id(1) == 0)
    def _():
        xext_ref[0:L, :] = jnp.zeros((L, SSD_CONV_DIM), BF16)
        state_ref[...] = jnp.zeros(state_ref.shape, F32)

    xext_ref[L:(T + 1) * L, 0:SSD_WIDTH] = xs_ref[...]
    xext_ref[L:(T + 1) * L, SSD_WIDTH:SSD_WIDTH + gs] = b_ref[...]
    xext_ref[L:(T + 1) * L, SSD_WIDTH + gs:SSD_CONV_DIM] = c_ref[...]

    carry = {"state": [state_ref[p] for p in range(pairs)], "cols_split": [None] * T}
    programs = [_ssd_chunk(c, carry, xext_ref, z_ref, dt_ref, shift_ref, triu_ref, cw_ref,
                           cb_ref, dtb_ref, alog_ref, dsk_ref, nw_ref, o_ref)
                for c in range(T)]

    def stage():
        for prog in programs:
            next(prog)

    stage()
    stage()
    split = jnp.concatenate(carry["cols_split"], axis=0)
    carry["spread"] = jnp.concatenate(
        [jnp.dot(split[:, 0:2 * LANES], selp_ref[...], preferred_element_type=F32),
         jnp.dot(split, selh_ref[...], preferred_element_type=F32)], axis=1)
    stage()
    stage()
    stage()
    for p in range(pairs):
        state_ref[p] = carry["state"][p]
    xext_ref[0:L, :] = xext_ref[T * L:(T + 1) * L, :]


SSD_PAIR_KINDS = (0, 2)


def _ssd_chunk(c, carry, xext_ref, z_ref, dt_ref, shift_ref, triu_ref, cw_ref,
               cb_ref, dtb_ref, alog_ref, dsk_ref, nw_ref, o_ref):
    L = SSD_CHUNK
    H = SSD_HEADS
    gs = SSD_GROUPS * SSD_STATE
    rows_c = slice(c * L, (c + 1) * L)

    delayed = jnp.dot(shift_ref[...], xext_ref[c * L:(c + 2) * L, :],
                      preferred_element_type=F32)
    conv = (cb_ref[...] + xext_ref[(c + 1) * L:(c + 2) * L, :].astype(F32)
            * cw_ref[SSD_CONV - 1:SSD_CONV, :])
    for k in range(SSD_CONV - 1):
        conv = conv + delayed[k * L:(k + 1) * L, :] * cw_ref[k:k + 1, :]
    xbc = _silu(conv)
    xs = xbc[:, 0:SSD_WIDTH]
    bm = xbc[:, SSD_WIDTH:SSD_WIDTH + gs].astype(BF16)
    cm = xbc[:, SSD_WIDTH + gs:SSD_CONV_DIM].astype(BF16)
    yield

    dt_in = dt_ref[:, rows_c] + dtb_ref[...]
    dt = jnp.maximum(dt_in, 0.0) + jnp.log1p(jnp.exp(-jnp.abs(dt_in)))
    a_dt = dt * (-jnp.exp(alog_ref[...]))
    hi = a_dt.astype(BF16)
    r1 = a_dt - hi.astype(F32)
    mid = r1.astype(BF16)
    lo = (r1 - mid.astype(F32)).astype(BF16)
    triu = triu_ref[...]
    a_cs = (jnp.dot(hi, triu, preferred_element_type=F32)
            + jnp.dot(mid, triu, preferred_element_type=F32)
            + jnp.dot(lo, triu, preferred_element_type=F32))
    a_end = a_cs[:, L - 1:L]
    chunk_decay = jnp.exp(a_end)
    rows = jnp.concatenate([dt * jnp.exp(a_end - a_cs), a_cs, jnp.exp(a_cs),
                            jnp.zeros((L - 3 * H, L), F32)], axis=0)
    cols = rows.T
    c_hi = cols.astype(BF16)
    c_r1 = cols - c_hi.astype(F32)
    c_mid = c_r1.astype(BF16)
    c_lo = (c_r1 - c_mid.astype(F32)).astype(BF16)
    carry["cols_split"][c] = jnp.concatenate([c_hi, c_mid, c_lo], axis=1)
    yield
    spread = carry["spread"][rows_c, :]

    row = lax.broadcasted_iota(jnp.int32, (L, L), 0)
    col = lax.broadcasted_iota(jnp.int32, (L, L), 1)
    lower = row >= col
    lane = lax.broadcasted_iota(jnp.int32, (L, LANES), 1)
    left = lane < SSD_HEAD_DIM
    top = row < SSD_HEAD_DIM
    pairs = SSD_HEADS // 2

    def pair_lanes(slot, p):
        i = slot * pairs + p
        return spread[:, i * LANES:(i + 1) * LANES]

    def head_lanes(h):
        i = len(SSD_PAIR_KINDS) * pairs + h
        return spread[:, i * LANES:(i + 1) * LANES]

    group_cb = []
    for g in range(SSD_GROUPS):
        group_cb.append(lax.dot_general(cm[:, g * SSD_STATE:(g + 1) * SSD_STATE],
                                        bm[:, g * SSD_STATE:(g + 1) * SSD_STATE],
                                        NT_DIMS, preferred_element_type=F32))
    yield

    group_of = lambda p: (2 * p) // (SSD_HEADS // SSD_GROUPS)
    y_diag, chunk_state = [], []
    for p in range(pairs):
        bg = bm[:, group_of(p) * SSD_STATE:(group_of(p) + 1) * SSD_STATE]
        xs_p = xs[:, p * LANES:(p + 1) * LANES]
        xs_b = xs_p.astype(BF16)
        y_heads = []
        for hh in range(2):
            h = 2 * p + hh
            seg = head_lanes(h) - a_cs[h:h + 1, :]
            decay = jnp.exp(jnp.where(lower, seg, -jnp.inf))
            scores = (group_cb[group_of(p)] * decay * dt[h:h + 1, :]).astype(BF16)
            y_heads.append(jnp.dot(scores, xs_b, preferred_element_type=F32))
        y_diag.append(jnp.where(left, y_heads[0], y_heads[1]))
        x_w = (xs_p * pair_lanes(0, p)).astype(BF16)
        chunk_state.append(lax.dot_general(x_w, bg, TN_DIMS, preferred_element_type=F32))
    yield

    ys, new_state = [], []
    for p in range(pairs):
        cg = cm[:, group_of(p) * SSD_STATE:(group_of(p) + 1) * SSD_STATE]
        st = carry["state"][p]
        y_off = (lax.dot_general(cg, st.astype(BF16), NT_DIMS, preferred_element_type=F32)
                 * pair_lanes(1, p))
        st_decay = jnp.where(top, chunk_decay[2 * p:2 * p + 1, :],
                             chunk_decay[2 * p + 1:2 * p + 2, :])
        new_state.append(st * st_decay + chunk_state[p])
        y = (y_diag[p] + y_off
             + xs[:, p * LANES:(p + 1) * LANES] * dsk_ref[:, p * LANES:(p + 1) * LANES])
        ys.append(y * _silu(z_ref[rows_c, p * LANES:(p + 1) * LANES].astype(F32)))
    carry["state"] = new_state

    grp_pairs = pairs // SSD_GROUPS
    grp_width = SSD_WIDTH // SSD_GROUPS
    for g in range(SSD_GROUPS):
        members = ys[g * grp_pairs:(g + 1) * grp_pairs]
        ssq = sum(jnp.sum(y * y, axis=-1, keepdims=True) for y in members)
        inv = lax.rsqrt(ssq / grp_width + NORM_EPS)
        for i, y in enumerate(members):
            c0 = (g * grp_pairs + i) * LANES
            o_ref[rows_c, c0:c0 + LANES] = (y * inv * nw_ref[:, c0:c0 + LANES]).astype(BF16)
    yield


def _ssd(proj, dt_raw_t, conv_w, conv_b, dt_bias, a_log, d_skip, ssd_norm_w, batch, seq):
    gs = SSD_GROUPS * SSD_STATE
    L = SSD_CHUNK
    rows = SSD_CHUNKS_PER_STEP * L
    nc = seq // rows
    row_map = lambda b, c: b * nc + c
    const = lambda b, c: (0, 0)
    t = jnp.arange(L)[:, None]
    src = jnp.arange(2 * L)[None, :]
    shift = jnp.concatenate([(src == L + t - (SSD_CONV - 1 - k)) for k in range(SSD_CONV - 1)],
                            axis=0).astype(BF16)
    triu = (jnp.arange(L)[:, None] <= jnp.arange(L)[None, :]).astype(BF16)
    lane = jnp.arange(LANES)
    pair_blocks, head_blocks = [], []
    for kind in SSD_PAIR_KINDS:
        for p in range(SSD_HEADS // 2):
            src_col = kind * SSD_HEADS + 2 * p + (lane >= SSD_HEAD_DIM)
            pair_blocks.append(jnp.arange(L)[:, None] == src_col[None, :])
    for h in range(SSD_HEADS):
        head_blocks.append(jnp.broadcast_to(jnp.arange(L)[:, None] == SSD_HEADS + h, (L, LANES)))
    sel_pair = jnp.tile(jnp.concatenate(pair_blocks, axis=1), (2, 1)).astype(BF16)
    sel_head = jnp.tile(jnp.concatenate(head_blocks, axis=1), (3, 1)).astype(BF16)
    per_head = lambda v: jnp.broadcast_to(v.astype(F32)[:, None], (SSD_HEADS, L))
    return pl.pallas_call(
        _ssd_kernel,
        grid=(batch, nc),
        in_specs=[
            pl.BlockSpec((rows, SSD_WIDTH), lambda b, c: (row_map(b, c), COL_XS // SSD_WIDTH)),
            pl.BlockSpec((rows, gs), lambda b, c: (row_map(b, c), COL_B // gs)),
            pl.BlockSpec((rows, gs), lambda b, c: (row_map(b, c), COL_C // gs)),
            pl.BlockSpec((rows, SSD_WIDTH), lambda b, c: (row_map(b, c), COL_Z // SSD_WIDTH)),
            pl.BlockSpec((SSD_HEADS, rows), lambda b, c: (0, row_map(b, c))),
            pl.BlockSpec(((SSD_CONV - 1) * L, 2 * L), const),
            pl.BlockSpec((L, L), const),
            pl.BlockSpec(sel_pair.shape, const),
            pl.BlockSpec(sel_head.shape, const),
            pl.BlockSpec((SSD_CONV, SSD_CONV_DIM), const),
            pl.BlockSpec((1, SSD_CONV_DIM), const),
            pl.BlockSpec((SSD_HEADS, L), const),
            pl.BlockSpec((SSD_HEADS, L), const),
            pl.BlockSpec((1, SSD_WIDTH), const),
            pl.BlockSpec((1, SSD_WIDTH), const),
        ],
        out_specs=pl.BlockSpec((rows, SSD_WIDTH), lambda b, c: (row_map(b, c), 0)),
        out_shape=jax.ShapeDtypeStruct((batch * seq, SSD_WIDTH), BF16),
        scratch_shapes=[pltpu.VMEM((L + rows, SSD_CONV_DIM), BF16),
                        pltpu.VMEM((SSD_HEADS // 2, 2 * SSD_HEAD_DIM, SSD_STATE), F32)],
        compiler_params=pltpu.CompilerParams(
            dimension_semantics=("arbitrary", "arbitrary"),
            vmem_limit_bytes=32 * MIB),
        name="ssd",
    )(proj, proj, proj, proj, dt_raw_t, shift, triu, sel_pair, sel_head, conv_w, conv_b.reshape(1, SSD_CONV_DIM),
      per_head(dt_bias), per_head(a_log),
      jnp.repeat(d_skip, SSD_HEAD_DIM).reshape(1, SSD_WIDTH), ssd_norm_w.reshape(1, SSD_WIDTH))


XATT_V_ROWS = XATTN_HEAD_DIM + ATT_SUM_ROWS


def _mem_kv_kernel(mem_ref, nw_ref, w_ref, k_ref, vt_ref, wb_ref):
    @pl.when(pl.program_id(0) == 0)
    def _():
        wb_ref[...] = w_ref[...].astype(BF16)

    x = mem_ref[...]
    h = (x * _rms_scale(x) * nw_ref[...]).astype(BF16)
    kv = jnp.dot(h, wb_ref[...], preferred_element_type=F32)
    k_ref[...] = kv[:, 0:XATTN_WIDTH].astype(BF16)
    ones = jnp.ones((ATT_SUM_ROWS, x.shape[0]), BF16)
    for hd in range(XATTN_HEADS):
        c0 = XATTN_WIDTH + hd * XATTN_HEAD_DIM
        r0 = hd * XATT_V_ROWS
        vt_ref[r0:r0 + XATTN_HEAD_DIM, :] = kv[:, c0:c0 + XATTN_HEAD_DIM].T.astype(BF16)
        vt_ref[r0 + XATTN_HEAD_DIM:r0 + XATT_V_ROWS, :] = ones


def _mem_kv(mem2, mem_norm_w, w_kv, batch, mem_len):
    return pl.pallas_call(
        _mem_kv_kernel,
        grid=(batch,),
        in_specs=[pl.BlockSpec((mem_len, D_MODEL), lambda b: (b, 0)),
                  pl.BlockSpec((1, D_MODEL), lambda b: (0, 0)),
                  pl.BlockSpec((D_MODEL, 2 * XATTN_WIDTH), lambda b: (0, 0),
                               pipeline_mode=pl.Buffered(1))],
        out_specs=[pl.BlockSpec((mem_len, XATTN_WIDTH), lambda b: (b, 0)),
                   pl.BlockSpec((XATTN_HEADS * XATT_V_ROWS, mem_len), lambda b: (b, 0))],
        out_shape=[jax.ShapeDtypeStruct((batch * mem_len, XATTN_WIDTH), BF16),
                   jax.ShapeDtypeStruct((batch * XATTN_HEADS * XATT_V_ROWS, mem_len), BF16)],
        scratch_shapes=[pltpu.VMEM((D_MODEL, 2 * XATTN_WIDTH), BF16)],
        compiler_params=pltpu.CompilerParams(
            dimension_semantics=("arbitrary",), vmem_limit_bytes=32 * MIB),
        name="mem_kv",
    )(mem2, mem_norm_w.reshape(1, D_MODEL), w_kv)


def _xattn_kernel(q_ref, g_ref, k_ref, vt_ref, o_ref):
    scale = XATTN_HEAD_DIM ** -0.5 * math.log2(math.e)
    cols = [slice(h * XATTN_HEAD_DIM, (h + 1) * XATTN_HEAD_DIM) for h in range(XATTN_HEADS)]
    scores = [lax.dot_general(k_ref[:, c], q_ref[:, c], NT_DIMS, preferred_element_type=F32)
              for c in cols]
    for h, c in enumerate(cols):
        s = scores[h] * scale
        p = jnp.exp2(s - jnp.max(s, axis=0, keepdims=True)).astype(BF16)
        pv = jnp.dot(vt_ref[h * XATT_V_ROWS:(h + 1) * XATT_V_ROWS, :], p,
                     preferred_element_type=F32)
        xo = (pv[0:XATTN_HEAD_DIM, :]
              * (1.0 / pv[XATTN_HEAD_DIM:XATTN_HEAD_DIM + 1, :])).T
        o_ref[:, c] = (xo * _silu(g_ref[:, c].astype(F32))).astype(BF16)


def _xattn(proj, mk, mvt, batch, seq, mem_len):
    nq = seq // XATT_TQ
    return pl.pallas_call(
        _xattn_kernel,
        grid=(batch, nq),
        in_specs=[pl.BlockSpec((XATT_TQ, XATTN_WIDTH), lambda b, q: (b * nq + q, COL_XQ // XATTN_WIDTH)),
                  pl.BlockSpec((XATT_TQ, XATTN_WIDTH), lambda b, q: (b * nq + q, COL_XG // XATTN_WIDTH)),
                  pl.BlockSpec((mem_len, XATTN_WIDTH), lambda b, q: (b, 0)),
                  pl.BlockSpec((XATTN_HEADS * XATT_V_ROWS, mem_len), lambda b, q: (b, 0))],
        out_specs=pl.BlockSpec((XATT_TQ, XATTN_WIDTH), lambda b, q: (b * nq + q, 0)),
        out_shape=jax.ShapeDtypeStruct((batch * seq, XATTN_WIDTH), BF16),
        compiler_params=pltpu.CompilerParams(
            dimension_semantics=("arbitrary", "arbitrary"), vmem_limit_bytes=32 * MIB),
        name="xattn",
    )(proj, proj, mk, mvt)


def _out_proj_kernel(d_ref, s_ref, a_ref, w_ref, nw_ref, x_ref, o_ref, wb_ref):
    @pl.when(pl.program_id(0) == 0)
    def _():
        for k0 in range(0, D_MIX, OUT_SUB):
            wb_ref[k0:k0 + OUT_SUB, :] = w_ref[k0:k0 + OUT_SUB, :].astype(BF16)

    for i in range(OUT_TM // OUT_SUB):
        r = slice(i * OUT_SUB, (i + 1) * OUT_SUB)
        y = jnp.dot(d_ref[r, :], wb_ref[0:DIFF_WIDTH, :], preferred_element_type=F32)
        y = y + jnp.dot(s_ref[r, :], wb_ref[DIFF_WIDTH:DIFF_WIDTH + SSD_WIDTH, :],
                        preferred_element_type=F32)
        y = y + jnp.dot(a_ref[r, :], wb_ref[DIFF_WIDTH + SSD_WIDTH:D_MIX, :],
                        preferred_element_type=F32)
        o_ref[r, :] = x_ref[r, :] + y * _rms_scale(y) * nw_ref[...]


def _out_proj(diff_out, ssd_out, xattn_out, w_out, post_norm_w, x2):
    m = x2.shape[0]
    return pl.pallas_call(
        _out_proj_kernel,
        grid=(m // OUT_TM,),
        in_specs=[pl.BlockSpec((OUT_TM, DIFF_WIDTH), lambda i: (i, 0)),
                  pl.BlockSpec((OUT_TM, SSD_WIDTH), lambda i: (i, 0)),
                  pl.BlockSpec((OUT_TM, XATTN_WIDTH), lambda i: (i, 0)),
                  pl.BlockSpec((D_MIX, D_MODEL), lambda i: (0, 0), pipeline_mode=pl.Buffered(1)),
                  pl.BlockSpec((1, D_MODEL), lambda i: (0, 0)),
                  pl.BlockSpec((OUT_TM, D_MODEL), lambda i: (i, 0))],
        out_specs=pl.BlockSpec((OUT_TM, D_MODEL), lambda i: (i, 0)),
        out_shape=jax.ShapeDtypeStruct((m, D_MODEL), F32),
        scratch_shapes=[pltpu.VMEM((D_MIX, D_MODEL), BF16)],
        compiler_params=pltpu.CompilerParams(
            dimension_semantics=("arbitrary",), vmem_limit_bytes=48 * MIB),
        name="out_proj",
    )(diff_out, ssd_out, xattn_out, w_out, post_norm_w.reshape(1, D_MODEL), x2)


def kernel(x, mem, positions, pre_norm_w, w_in, lambda_q1, lambda_k1, lambda_q2, lambda_k2,
           diff_subln_w, conv_w, conv_b, dt_bias, a_log, d_skip, ssd_norm_w, mem_norm_w,
           w_mem_kv, w_out, post_norm_w):
    batch, seq, _ = x.shape
    mem_len = mem.shape[1]
    assert pre_norm_w.shape[0] == 1, "single-layer kernel"
    x2 = x.reshape(batch * seq, D_MODEL)

    w_main, w_v, w_dt, cos, sin = _w_in_prep(jnp.transpose(w_in[0]), positions)
    lam_vecs = jnp.stack([lambda_q1[0], lambda_k1[0], lambda_q2[0], lambda_k2[0]]).astype(F32)

    proj, v_t, dt_raw_t = _in_proj(x2, pre_norm_w[0], w_main, w_v, w_dt, cos, sin)
    diff_out = _diff_attention(proj, v_t, lam_vecs, diff_subln_w[0], batch, seq)
    ssd_out = _ssd(proj, dt_raw_t, conv_w[0], conv_b[0], dt_bias[0], a_log[0], d_skip[0],
                   ssd_norm_w[0], batch, seq)
    mk, mvt = _mem_kv(mem.reshape(batch * mem_len, D_MODEL), mem_norm_w[0], w_mem_kv[0],
                      batch, mem_len)
    xattn_out = _xattn(proj, mk, mvt, batch, seq, mem_len)
    out = _out_proj(diff_out, ssd_out, xattn_out, w_out[0], post_norm_w[0], x2)
    return out.reshape(batch, seq, D_MODEL)
```

```python
import functools
import math

import jax
import jax.numpy as jnp
from jax import lax
from jax.experimental import pallas as pl
from jax.experimental.pallas import tpu as pltpu

F32 = jnp.float32
BF16 = jnp.bfloat16

D_MODEL = 2048
DIFF_HEADS = 8
DIFF_QK_DIM = 64
DIFF_V_DIM = 128
DIFF_WIDTH = DIFF_HEADS * DIFF_V_DIM
SSD_HEADS = 8
SSD_HEAD_DIM = 64
SSD_WIDTH = SSD_HEADS * SSD_HEAD_DIM
SSD_GROUPS = 2
SSD_STATE = 128
SSD_CONV = 4
SSD_CHUNK = 128
SSD_CONV_DIM = SSD_WIDTH + 2 * SSD_GROUPS * SSD_STATE
XATTN_HEADS = 4
XATTN_HEAD_DIM = 128
XATTN_WIDTH = XATTN_HEADS * XATTN_HEAD_DIM
D_MIX = DIFF_WIDTH + SSD_WIDTH + XATTN_WIDTH
ROPE_THETA = 10000.0
NORM_EPS = 1e-6
LAMBDA_INIT = 0.8 - 0.6 * math.exp(-0.3 * 0)

LANES = 128
SUBLANES = 8
MIB = 1024 * 1024

COL_Q = 0
COL_K = COL_Q + DIFF_WIDTH
COL_G = COL_K + DIFF_WIDTH
COL_Z = COL_G + DIFF_WIDTH
COL_XS = COL_Z + SSD_WIDTH
COL_B = COL_XS + SSD_WIDTH
COL_C = COL_B + SSD_GROUPS * SSD_STATE
COL_XQ = COL_C + SSD_GROUPS * SSD_STATE
COL_XG = COL_XQ + XATTN_WIDTH
N_MAIN = COL_XG + XATTN_WIDTH
REF_V = 2 * DIFF_WIDTH
REF_G = REF_V + DIFF_WIDTH
REF_DT = REF_G + DIFF_WIDTH + SSD_WIDTH + SSD_CONV_DIM
REF_XQ = REF_DT + SSD_HEADS
D_IN = REF_XQ + 2 * XATTN_WIDTH

PROJ_TM = 512
PROJ_TN = 512
NORM_ROWS = 256
ATT_TQ = 512
ATT_TK = 512
ATT_HEADS_PER_STEP = 4
ATT_SUM_ROWS = 16
ATT_AHEAD = 2
XATT_TQ = 1024
OUT_TM = 512
OUT_SUB = 256
ROPE_TM = 512
SSD_CHUNKS_PER_STEP = 8
DT_ROWS = 16

NT_DIMS = (((1,), (1,)), ((), ()))
TN_DIMS = (((0,), (0,)), ((), ()))


def _rms_scale(x):
    return lax.rsqrt(jnp.mean(x * x, axis=-1, keepdims=True) + NORM_EPS)


def _silu(x):
    return x * jax.nn.sigmoid(x)


def _rope_table_rows(pos_ref, invf_ref, cos_ref, sin_ref):
    ang = pos_ref[...].astype(F32) * invf_ref[...]
    lane = lax.broadcasted_iota(jnp.int32, ang.shape, 1)
    first_half = (lane % DIFF_QK_DIM) < (DIFF_QK_DIM // 2)
    s = jnp.sin(ang)
    cos_ref[...] = jnp.cos(ang)
    sin_ref[...] = jnp.where(first_half, -s, s)


def _rope(x, cos, sin_signed):
    lane = lax.broadcasted_iota(jnp.int32, x.shape, 1)
    first_half = (lane % DIFF_QK_DIM) < (DIFF_QK_DIM // 2)
    half = DIFF_QK_DIM // 2
    rot = jnp.where(first_half, pltpu.roll(x, LANES - half, 1), pltpu.roll(x, half, 1))
    return x * cos + rot * sin_signed


def _w_in_prep_kernel(wt_ref, dt_rows_ref, pos_ref, invf_ref,
                      wm_ref, wv_ref, wdt_ref, cos_ref, sin_ref):
    j = pl.program_id(0)
    main_blocks = N_MAIN // PROJ_TN
    last_w = main_blocks + DIFF_WIDTH // PROJ_TN
    rope = functools.partial(_rope_table_rows, pos_ref, invf_ref, cos_ref, sin_ref)

    @pl.when(j < main_blocks)
    def _():
        rope()
        wm_ref[0] = wt_ref[...].T.astype(BF16)

    @pl.when((j >= main_blocks) & (j < last_w))
    def _():
        rope()
        wv_ref[...] = wt_ref[...].astype(BF16)

    pl.when(j >= last_w)(rope)

    @pl.when(j == 0)
    def _():
        row = lax.broadcasted_iota(jnp.int32, (DT_ROWS, D_MODEL), 0)
        wdt_ref[...] = jnp.where(row < SSD_HEADS, dt_rows_ref[...], 0.0).astype(BF16)


def _w_in_prep(w_t, positions):
    n, k = w_t.shape
    assert n == D_IN and k == D_MODEL
    main_blocks = N_MAIN // PROJ_TN
    v_blocks = DIFF_WIDTH // PROJ_TN
    last_w = main_blocks + v_blocks - 1
    n_pos = positions.size
    steps = n_pos // ROPE_TM
    assert steps > last_w
    inv_freq = 1.0 / (ROPE_THETA ** (jnp.arange(0, DIFF_QK_DIM, 2, dtype=F32) / DIFF_QK_DIM))
    invf = jnp.tile(inv_freq, LANES // inv_freq.shape[0]).reshape(1, LANES)

    def src_row(j):
        main = jnp.where(j < REF_V // PROJ_TN, j * PROJ_TN,
                         jnp.where(j < (REF_DT - DIFF_WIDTH) // PROJ_TN,
                                   j * PROJ_TN + DIFF_WIDTH,
                                   j * PROJ_TN + DIFF_WIDTH + SSD_HEADS))
        jw = jnp.minimum(j, last_w)
        return pl.multiple_of(jnp.where(jw < main_blocks, main,
                                        REF_V + (jw - main_blocks) * PROJ_TN), SUBLANES)

    return pl.pallas_call(
        _w_in_prep_kernel,
        grid=(steps,),
        in_specs=[pl.BlockSpec((pl.Element(PROJ_TN), pl.Element(D_MODEL)),
                               lambda j: (src_row(j), 0)),
                  pl.BlockSpec((DT_ROWS, D_MODEL), lambda j: (REF_DT // DT_ROWS, 0)),
                  pl.BlockSpec((ROPE_TM, 1), lambda j: (j, 0)),
                  pl.BlockSpec((1, LANES), lambda j: (0, 0))],
        out_specs=[pl.BlockSpec((1, D_MODEL, PROJ_TN),
                                lambda j: (jnp.minimum(j, main_blocks - 1), 0, 0)),
                   pl.BlockSpec((PROJ_TN, D_MODEL),
                                lambda j: (jnp.clip(j - main_blocks, 0, v_blocks - 1), 0)),
                   pl.BlockSpec((DT_ROWS, D_MODEL), lambda j: (0, 0)),
                   pl.BlockSpec((ROPE_TM, LANES), lambda j: (j, 0)),
                   pl.BlockSpec((ROPE_TM, LANES), lambda j: (j, 0))],
        out_shape=[jax.ShapeDtypeStruct((main_blocks, k, PROJ_TN), BF16),
                   jax.ShapeDtypeStruct((DIFF_WIDTH, k), BF16),
                   jax.ShapeDtypeStruct((DT_ROWS, k), BF16),
                   jax.ShapeDtypeStruct((n_pos, LANES), F32),
                   jax.ShapeDtypeStruct((n_pos, LANES), F32)],
        compiler_params=pltpu.CompilerParams(
            dimension_semantics=("arbitrary",), vmem_limit_bytes=32 * MIB),
        name="w_in_prep",
    )(w_t, w_t, positions.reshape(n_pos, 1), invf)


def _in_proj_kernel(x_ref, nw_ref, w_ref, wv_ref, wdt_ref, cos_ref, sin_ref,
                    o_ref, vt_ref, dt_ref, h_ref):
    def norm_rows(i, carry):
        r = pl.ds(pl.multiple_of(i * NORM_ROWS, NORM_ROWS), NORM_ROWS)
        x = x_ref[r, :]
        h_ref[r, :] = (x * _rms_scale(x) * nw_ref[...]).astype(BF16)
        return carry
    lax.fori_loop(0, PROJ_TM // NORM_ROWS, norm_rows, 0)
    dt_t = lax.dot_general(wdt_ref[...], h_ref[...], NT_DIMS, preferred_element_type=F32)
    dt_ref[...] = dt_t[0:SSD_HEADS, :]
    vt_ref[...] = lax.dot_general(wv_ref[...], h_ref[...], NT_DIMS,
                                  preferred_element_type=F32).astype(BF16)
    q_scale = DIFF_QK_DIM ** -0.5 * math.log2(math.e)
    for n in range(N_MAIN // PROJ_TN):
        c0 = n * PROJ_TN
        y = jnp.dot(h_ref[...], w_ref[n], preferred_element_type=F32)
        if c0 < COL_G:
            scale = q_scale if c0 < COL_K else 1.0
            for hd in range(PROJ_TN // LANES):
                y_h = _rope(y[:, hd * LANES:(hd + 1) * LANES], cos_ref[...], sin_ref[...])
                o_ref[:, c0 + hd * LANES:c0 + (hd + 1) * LANES] = (y_h * scale).astype(BF16)
        else:
            o_ref[:, c0:c0 + PROJ_TN] = y.astype(BF16)


def _in_proj(x2, pre_norm_w, w_main, w_v, w_dt, cos, sin):
    m = x2.shape[0]
    resident = pl.Buffered(1)
    return pl.pallas_call(
        _in_proj_kernel,
        grid=(m // PROJ_TM,),
        in_specs=[pl.BlockSpec((PROJ_TM, D_MODEL), lambda i: (i, 0)),
                  pl.BlockSpec((1, D_MODEL), lambda i: (0, 0)),
                  pl.BlockSpec((N_MAIN // PROJ_TN, D_MODEL, PROJ_TN), lambda i: (0, 0, 0),
                               pipeline_mode=resident),
                  pl.BlockSpec((DIFF_WIDTH, D_MODEL), lambda i: (0, 0), pipeline_mode=resident),
                  pl.BlockSpec((DT_ROWS, D_MODEL), lambda i: (0, 0), pipeline_mode=resident),
                  pl.BlockSpec((PROJ_TM, LANES), lambda i: (i, 0)),
                  pl.BlockSpec((PROJ_TM, LANES), lambda i: (i, 0))],
        out_specs=[pl.BlockSpec((PROJ_TM, N_MAIN), lambda i: (i, 0)),
                   pl.BlockSpec((DIFF_WIDTH, PROJ_TM), lambda i: (0, i)),
                   pl.BlockSpec((SSD_HEADS, PROJ_TM), lambda i: (0, i))],
        out_shape=[jax.ShapeDtypeStruct((m, N_MAIN), BF16),
                   jax.ShapeDtypeStruct((DIFF_WIDTH, m), BF16),
                   jax.ShapeDtypeStruct((SSD_HEADS, m), F32)],
        scratch_shapes=[pltpu.VMEM((PROJ_TM, D_MODEL), BF16)],
        compiler_params=pltpu.CompilerParams(
            dimension_semantics=("arbitrary",),
            vmem_limit_bytes=56 * MIB),
        name="in_proj",
    )(x2, pre_norm_w.reshape(1, D_MODEL), w_main, w_v, w_dt, cos, sin)


def _diff_attn_kernel(lam_ref, q_ref, k_ref, v_ref, g_ref, subw_ref, o_ref, vt_ref, *, seq):
    qi = pl.program_id(2)
    heads = ATT_HEADS_PER_STEP

    @pl.when(qi == 0)
    def _():
        for g in range(heads):
            vt_ref[g, 0:DIFF_V_DIM, :] = v_ref[g * DIFF_V_DIM:(g + 1) * DIFF_V_DIM, :]
        vt_ref[:, DIFF_V_DIM:, :] = jnp.ones((heads, ATT_SUM_ROWS, seq), BF16)

    lane = lax.broadcasted_iota(jnp.int32, (ATT_TQ, LANES), 1)
    q_maps = []
    for g in range(heads):
        q = q_ref[:, g * LANES:(g + 1) * LANES]
        q_maps.append((jnp.where(lane < DIFF_QK_DIM, q, jnp.zeros_like(q)),
                       jnp.where(lane >= DIFF_QK_DIM, q, jnp.zeros_like(q))))

    lv = lam_ref[...]
    lam = (jnp.exp(jnp.sum(lv[0:1] * lv[1:2], axis=-1, keepdims=True))
           - jnp.exp(jnp.sum(lv[2:3] * lv[3:4], axis=-1, keepdims=True)) + LAMBDA_INIT)

    def run(n_tiles):
        diag = (n_tiles - 1) * ATT_TK
        half = ATT_TK // 2
        blocks = [(j * ATT_TK, (j + 1) * ATT_TK, 0, False) for j in range(n_tiles - 1)]
        blocks += [(diag, diag + half, 0, True), (diag + half, diag + ATT_TK, half, True)]
        items = [(blk, g, c) for blk in blocks for g in range(heads) for c in range(2)]

        def scores(blk, g, c):
            k0, k1, q0, masked = blk
            kt = k_ref[k0:k1, g * LANES:(g + 1) * LANES]
            s = lax.dot_general(kt, q_maps[g][c][q0:, :], NT_DIMS, preferred_element_type=F32)
            if not masked:
                return s
            key = lax.broadcasted_iota(jnp.int32, s.shape, 0) + (k0 - diag)
            qry = lax.broadcasted_iota(jnp.int32, s.shape, 1) + q0
            return jnp.where(key <= qry, s, -jnp.inf)

        pending = {t: scores(*items[t]) for t in range(min(ATT_AHEAD, len(items)))}
        m, acc = {}, {}
        for t, (blk, g, c) in enumerate(items):
            k0, k1, q0, _ = blk
            i = 2 * g + c
            s = pending.pop(t)
            m_new = jnp.max(s, axis=0, keepdims=True)
            first = i not in m
            if not first:
                m_old = m[i][:, q0:]
                m_new = jnp.maximum(m_old, m_new)
            p = jnp.exp2(s - m_new).astype(BF16)
            if t + ATT_AHEAD < len(items):
                pending[t + ATT_AHEAD] = scores(*items[t + ATT_AHEAD])
            pv = jnp.dot(vt_ref[g, :, k0:k1], p, preferred_element_type=F32)
            if first:
                m[i], acc[i] = m_new, pv
            else:
                new_acc = jnp.exp2(m_old - m_new) * acc[i][:, q0:] + pv
                if q0:
                    m_new = jnp.concatenate([m[i][:, :q0], m_new], axis=1)
                    new_acc = jnp.concatenate([acc[i][:, :q0], new_acc], axis=1)
                m[i], acc[i] = m_new, new_acc

        for g in range(heads):
            c = slice(g * LANES, (g + 1) * LANES)
            a1, a2 = acc[2 * g], acc[2 * g + 1]
            l1 = a1[DIFF_V_DIM:DIFF_V_DIM + 1, :]
            l2 = a2[DIFF_V_DIM:DIFF_V_DIM + 1, :]
            o_t = (a1[0:DIFF_V_DIM, :] * (1.0 / l1)
                   - lam * (a2[0:DIFF_V_DIM, :] * (1.0 / l2)))
            o = o_t.T
            o = o * _rms_scale(o) * subw_ref[...] * (1.0 - LAMBDA_INIT)
            o_ref[:, c] = (o * _silu(g_ref[:, c].astype(F32))).astype(BF16)

    for n in range(seq // ATT_TQ):
        pl.when(qi == n)(functools.partial(run, n + 1))


def _diff_attention(proj, v_t, lam_vecs, subln_w, batch, seq):
    assert ATT_TQ == ATT_TK and seq % ATT_TQ == 0 and DIFF_V_DIM == LANES
    nq = seq // ATT_TQ
    heads = ATT_HEADS_PER_STEP
    width = heads * LANES
    kernel = functools.partial(_diff_attn_kernel, seq=seq)
    return pl.pallas_call(
        kernel,
        grid=(batch, DIFF_HEADS // heads, nq),
        in_specs=[
            pl.BlockSpec((4, DIFF_QK_DIM), lambda b, h, q: (0, 0)),
            pl.BlockSpec((ATT_TQ, width), lambda b, h, q: (b * nq + q, COL_Q // width + h)),
            pl.BlockSpec((seq, width), lambda b, h, q: (b, COL_K // width + h)),
            pl.BlockSpec((width, seq), lambda b, h, q: (h, b)),
            pl.BlockSpec((ATT_TQ, width), lambda b, h, q: (b * nq + q, COL_G // width + h)),
            pl.BlockSpec((1, DIFF_V_DIM), lambda b, h, q: (0, 0)),
        ],
        out_specs=pl.BlockSpec((ATT_TQ, width), lambda b, h, q: (b * nq + q, h)),
        out_shape=jax.ShapeDtypeStruct((batch * seq, DIFF_WIDTH), BF16),
        scratch_shapes=[pltpu.VMEM((heads, DIFF_V_DIM + ATT_SUM_ROWS, seq), BF16)],
        compiler_params=pltpu.CompilerParams(
            dimension_semantics=("arbitrary", "arbitrary", "arbitrary"),
            vmem_limit_bytes=48 * MIB),
        name="diff_attn",
    )(lam_vecs, proj, proj, v_t, proj, subln_w.reshape(1, DIFF_V_DIM))


def _ssd_kernel(xs_ref, b_ref, c_ref, z_ref, dt_ref, shift_ref, triu_ref, selp_ref, selh_ref,
                cw_ref, cb_ref, dtb_ref, alog_ref, dsk_ref, nw_ref, o_ref, xext_ref, state_ref):
    L = SSD_CHUNK
    T = SSD_CHUNKS_PER_STEP
    gs = SSD_GROUPS * SSD_STATE
    pairs = SSD_HEADS // 2

    @pl.when(pl.program_id(1) == 0)
    def _():
        xext_ref[0:L, :] = jnp.zeros((L, SSD_CONV_DIM), BF16)
        state_ref[...] = jnp.zeros(state_ref.shape, F32)

    xext_ref[L:(T + 1) * L, 0:SSD_WIDTH] = xs_ref[...]
    xext_ref[L:(T + 1) * L, SSD_WIDTH:SSD_WIDTH + gs] = b_ref[...]
    xext_ref[L:(T + 1) * L, SSD_WIDTH + gs:SSD_CONV_DIM] = c_ref[...]

    carry = {"state": [state_ref[p] for p in range(pairs)], "cols_split": [None] * T}
    programs = [_ssd_chunk(c, carry, xext_ref, z_ref, dt_ref, shift_ref, triu_ref, cw_ref,
                           cb_ref, dtb_ref, alog_ref, dsk_ref, nw_ref, o_ref)
                for c in range(T)]

    def stage():
        for prog in programs:
            next(prog)

    stage()
    stage()
    split = jnp.concatenate(carry["cols_split"], axis=0)
    carry["spread"] = jnp.concatenate(
        [jnp.dot(split[:, 0:2 * LANES], selp_ref[...], preferred_element_type=F32),
         jnp.dot(split, selh_ref[...], preferred_element_type=F32)], axis=1)
    stage()
    stage()
    stage()
    for p in range(pairs):
        state_ref[p] = carry["state"][p]
    xext_ref[0:L, :] = xext_ref[T * L:(T + 1) * L, :]


SSD_PAIR_KINDS = (0, 2)


def _ssd_chunk(c, carry, xext_ref, z_ref, dt_ref, shift_ref, triu_ref, cw_ref,
               cb_ref, dtb_ref, alog_ref, dsk_ref, nw_ref, o_ref):
    L = SSD_CHUNK
    H = SSD_HEADS
    gs = SSD_GROUPS * SSD_STATE
    rows_c = slice(c * L, (c + 1) * L)

    delayed = jnp.dot(shift_ref[...], xext_ref[c * L:(c + 2) * L, :],
                      preferred_element_type=F32)
    conv = (cb_ref[...] + xext_ref[(c + 1) * L:(c + 2) * L, :].astype(F32)
            * cw_ref[SSD_CONV - 1:SSD_CONV, :])
    for k in range(SSD_CONV - 1):
        conv = conv + delayed[k * L:(k + 1) * L, :] * cw_ref[k:k + 1, :]
    xbc = _silu(conv)
    xs = xbc[:, 0:SSD_WIDTH]
    bm = xbc[:, SSD_WIDTH:SSD_WIDTH + gs].astype(BF16)
    cm = xbc[:, SSD_WIDTH + gs:SSD_CONV_DIM].astype(BF16)
    yield

    dt_in = dt_ref[:, rows_c] + dtb_ref[...]
    dt = jnp.maximum(dt_in, 0.0) + jnp.log1p(jnp.exp(-jnp.abs(dt_in)))
    a_dt = dt * (-jnp.exp(alog_ref[...]))
    hi = a_dt.astype(BF16)
    r1 = a_dt - hi.astype(F32)
    mid = r1.astype(BF16)
    lo = (r1 - mid.astype(F32)).astype(BF16)
    triu = triu_ref[...]
    a_cs = (jnp.dot(hi, triu, preferred_element_type=F32)
            + jnp.dot(mid, triu, preferred_element_type=F32)
            + jnp.dot(lo, triu, preferred_element_type=F32))
    a_end = a_cs[:, L - 1:L]
    chunk_decay = jnp.exp(a_end)
    rows = jnp.concatenate([dt * jnp.exp(a_end - a_cs), a_cs, jnp.exp(a_cs),
                            jnp.zeros((L - 3 * H, L), F32)], axis=0)
    cols = rows.T
    c_hi = cols.astype(BF16)
    c_r1 = cols - c_hi.astype(F32)
    c_mid = c_r1.astype(BF16)
    c_lo = (c_r1 - c_mid.astype(F32)).astype(BF16)
    carry["cols_split"][c] = jnp.concatenate([c_hi, c_mid, c_lo], axis=1)
    yield
    spread = carry["spread"][rows_c, :]

    row = lax.broadcasted_iota(jnp.int32, (L, L), 0)
    col = lax.broadcasted_iota(jnp.int32, (L, L), 1)
    lower = row >= col
    lane = lax.broadcasted_iota(jnp.int32, (L, LANES), 1)
    left = lane < SSD_HEAD_DIM
    top = row < SSD_HEAD_DIM
    pairs = SSD_HEADS // 2

    def pair_lanes(slot, p):
        i = slot * pairs + p
        return spread[:, i * LANES:(i + 1) * LANES]

    def head_lanes(h):
        i = len(SSD_PAIR_KINDS) * pairs + h
        return spread[:, i * LANES:(i + 1) * LANES]

    group_cb = []
    for g in range(SSD_GROUPS):
        group_cb.append(lax.dot_general(cm[:, g * SSD_STATE:(g + 1) * SSD_STATE],
                                        bm[:, g * SSD_STATE:(g + 1) * SSD_STATE],
                                        NT_DIMS, preferred_element_type=F32))
    yield

    group_of = lambda p: (2 * p) // (SSD_HEADS // SSD_GROUPS)
    y_diag, chunk_state = [], []
    for p in range(pairs):
        bg = bm[:, group_of(p) * SSD_STATE:(group_of(p) + 1) * SSD_STATE]
        xs_p = xs[:, p * LANES:(p + 1) * LANES]
        xs_b = xs_p.astype(BF16)
        y_heads = []
        for hh in range(2):
            h = 2 * p + hh
            seg = head_lanes(h) - a_cs[h:h + 1, :]
            decay = jnp.exp(jnp.where(lower, seg, -jnp.inf))
            scores = (group_cb[group_of(p)] * decay * dt[h:h + 1, :]).astype(BF16)
            y_heads.append(jnp.dot(scores, xs_b, preferred_element_type=F32))
        y_diag.append(jnp.where(left, y_heads[0], y_heads[1]))
        x_w = (xs_p * pair_lanes(0, p)).astype(BF16)
        chunk_state.append(lax.dot_general(x_w, bg, TN_DIMS, preferred_element_type=F32))
    yield

    ys, new_state = [], []
    for p in range(pairs):
        cg = cm[:, group_of(p) * SSD_STATE:(group_of(p) + 1) * SSD_STATE]
        st = carry["state"][p]
        y_off = (lax.dot_general(cg, st.astype(BF16), NT_DIMS, preferred_element_type=F32)
                 * pair_lanes(1, p))
        st_decay = jnp.where(top, chunk_decay[2 * p:2 * p + 1, :],
                             chunk_decay[2 * p + 1:2 * p + 2, :])
        new_state.append(st * st_decay + chunk_state[p])
        y = (y_diag[p] + y_off
             + xs[:, p * LANES:(p + 1) * LANES] * dsk_ref[:, p * LANES:(p + 1) * LANES])
        ys.append(y * _silu(z_ref[rows_c, p * LANES:(p + 1) * LANES].astype(F32)))
    carry["state"] = new_state

    grp_pairs = pairs // SSD_GROUPS
    grp_width = SSD_WIDTH // SSD_GROUPS
    for g in range(SSD_GROUPS):
        members = ys[g * grp_pairs:(g + 1) * grp_pairs]
        ssq = sum(jnp.sum(y * y, axis=-1, keepdims=True) for y in members)
        inv = lax.rsqrt(ssq / grp_width + NORM_EPS)
        for i, y in enumerate(members):
            c0 = (g * grp_pairs + i) * LANES
            o_ref[rows_c, c0:c0 + LANES] = (y * inv * nw_ref[:, c0:c0 + LANES]).astype(BF16)
    yield


def _ssd(proj, dt_raw_t, conv_w, conv_b, dt_bias, a_log, d_skip, ssd_norm_w, batch, seq):
    gs = SSD_GROUPS * SSD_STATE
    L = SSD_CHUNK
    rows = SSD_CHUNKS_PER_STEP * L
    nc = seq // rows
    row_map = lambda b, c: b * nc + c
    const = lambda b, c: (0, 0)
    t = jnp.arange(L)[:, None]
    src = jnp.arange(2 * L)[None, :]
    shift = jnp.concatenate([(src == L + t - (SSD_CONV - 1 - k)) for k in range(SSD_CONV - 1)],
                            axis=0).astype(BF16)
    triu = (jnp.arange(L)[:, None] <= jnp.arange(L)[None, :]).astype(BF16)
    lane = jnp.arange(LANES)
    pair_blocks, head_blocks = [], []
    for kind in SSD_PAIR_KINDS:
        for p in range(SSD_HEADS // 2):
            src_col = kind * SSD_HEADS + 2 * p + (lane >= SSD_HEAD_DIM)
            pair_blocks.append(jnp.arange(L)[:, None] == src_col[None, :])
    for h in range(SSD_HEADS):
        head_blocks.append(jnp.broadcast_to(jnp.arange(L)[:, None] == SSD_HEADS + h, (L, LANES)))
    sel_pair = jnp.tile(jnp.concatenate(pair_blocks, axis=1), (2, 1)).astype(BF16)
    sel_head = jnp.tile(jnp.concatenate(head_blocks, axis=1), (3, 1)).astype(BF16)
    per_head = lambda v: jnp.broadcast_to(v.astype(F32)[:, None], (SSD_HEADS, L))
    return pl.pallas_call(
        _ssd_kernel,
        grid=(batch, nc),
        in_specs=[
            pl.BlockSpec((rows, SSD_WIDTH), lambda b, c: (row_map(b, c), COL_XS // SSD_WIDTH)),
            pl.BlockSpec((rows, gs), lambda b, c: (row_map(b, c), COL_B // gs)),
            pl.BlockSpec((rows, gs), lambda b, c: (row_map(b, c), COL_C // gs)),
            pl.BlockSpec((rows, SSD_WIDTH), lambda b, c: (row_map(b, c), COL_Z // SSD_WIDTH)),
            pl.BlockSpec((SSD_HEADS, rows), lambda b, c: (0, row_map(b, c))),
            pl.BlockSpec(((SSD_CONV - 1) * L, 2 * L), const),
            pl.BlockSpec((L, L), const),
            pl.BlockSpec(sel_pair.shape, const),
            pl.BlockSpec(sel_head.shape, const),
            pl.BlockSpec((SSD_CONV, SSD_CONV_DIM), const),
            pl.BlockSpec((1, SSD_CONV_DIM), const),
            pl.BlockSpec((SSD_HEADS, L), const),
            pl.BlockSpec((SSD_HEADS, L), const),
            pl.BlockSpec((1, SSD_WIDTH), const),
            pl.BlockSpec((1, SSD_WIDTH), const),
        ],
        out_specs=pl.BlockSpec((rows, SSD_WIDTH), lambda b, c: (row_map(b, c), 0)),
        out_shape=jax.ShapeDtypeStruct((batch * seq, SSD_WIDTH), BF16),
        scratch_shapes=[pltpu.VMEM((L + rows, SSD_CONV_DIM), BF16),
                        pltpu.VMEM((SSD_HEADS // 2, 2 * SSD_HEAD_DIM, SSD_STATE), F32)],
        compiler_params=pltpu.CompilerParams(
            dimension_semantics=("arbitrary", "arbitrary"),
            vmem_limit_bytes=32 * MIB),
        name="ssd",
    )(proj, proj, proj, proj, dt_raw_t, shift, triu, sel_pair, sel_head, conv_w, conv_b.reshape(1, SSD_CONV_DIM),
      per_head(dt_bias), per_head(a_log),
      jnp.repeat(d_skip, SSD_HEAD_DIM).reshape(1, SSD_WIDTH), ssd_norm_w.reshape(1, SSD_WIDTH))


XATT_V_ROWS = XATTN_HEAD_DIM + ATT_SUM_ROWS


def _mem_kv_kernel(mem_ref, nw_ref, w_ref, k_ref, vt_ref, wb_ref):
    @pl.when(pl.program_id(0) == 0)
    def _():
        wb_ref[...] = w_ref[...].astype(BF16)

    x = mem_ref[...]
    h = (x * _rms_scale(x) * nw_ref[...]).astype(BF16)
    kv = jnp.dot(h, wb_ref[...], preferred_element_type=F32)
    k_ref[...] = kv[:, 0:XATTN_WIDTH].astype(BF16)
    ones = jnp.ones((ATT_SUM_ROWS, x.shape[0]), BF16)
    for hd in range(XATTN_HEADS):
        c0 = XATTN_WIDTH + hd * XATTN_HEAD_DIM
        r0 = hd * XATT_V_ROWS
        vt_ref[r0:r0 + XATTN_HEAD_DIM, :] = kv[:, c0:c0 + XATTN_HEAD_DIM].T.astype(BF16)
        vt_ref[r0 + XATTN_HEAD_DIM:r0 + XATT_V_ROWS, :] = ones


def _mem_kv(mem2, mem_norm_w, w_kv, batch, mem_len):
    return pl.pallas_call(
        _mem_kv_kernel,
        grid=(batch,),
        in_specs=[pl.BlockSpec((mem_len, D_MODEL), lambda b: (b, 0)),
                  pl.BlockSpec((1, D_MODEL), lambda b: (0, 0)),
                  pl.BlockSpec((D_MODEL, 2 * XATTN_WIDTH), lambda b: (0, 0),
                               pipeline_mode=pl.Buffered(1))],
        out_specs=[pl.BlockSpec((mem_len, XATTN_WIDTH), lambda b: (b, 0)),
                   pl.BlockSpec((XATTN_HEADS * XATT_V_ROWS, mem_len), lambda b: (b, 0))],
        out_shape=[jax.ShapeDtypeStruct((batch * mem_len, XATTN_WIDTH), BF16),
                   jax.ShapeDtypeStruct((batch * XATTN_HEADS * XATT_V_ROWS, mem_len), BF16)],
        scratch_shapes=[pltpu.VMEM((D_MODEL, 2 * XATTN_WIDTH), BF16)],
        compiler_params=pltpu.CompilerParams(
            dimension_semantics=("arbitrary",), vmem_limit_bytes=32 * MIB),
        name="mem_kv",
    )(mem2, mem_norm_w.reshape(1, D_MODEL), w_kv)


def _xattn_kernel(q_ref, g_ref, k_ref, vt_ref, o_ref):
    scale = XATTN_HEAD_DIM ** -0.5 * math.log2(math.e)
    cols = [slice(h * XATTN_HEAD_DIM, (h + 1) * XATTN_HEAD_DIM) for h in range(XATTN_HEADS)]
    scores = [lax.dot_general(k_ref[:, c], q_ref[:, c], NT_DIMS, preferred_element_type=F32)
              for c in cols]
    for h, c in enumerate(cols):
        s = scores[h] * scale
        p = jnp.exp2(s - jnp.max(s, axis=0, keepdims=True)).astype(BF16)
        pv = jnp.dot(vt_ref[h * XATT_V_ROWS:(h + 1) * XATT_V_ROWS, :], p,
                     preferred_element_type=F32)
        xo = (pv[0:XATTN_HEAD_DIM, :]
              * (1.0 / pv[XATTN_HEAD_DIM:XATTN_HEAD_DIM + 1, :])).T
        o_ref[:, c] = (xo * _silu(g_ref[:, c].astype(F32))).astype(BF16)


def _xattn(proj, mk, mvt, batch, seq, mem_len):
    nq = seq // XATT_TQ
    return pl.pallas_call(
        _xattn_kernel,
        grid=(batch, nq),
        in_specs=[pl.BlockSpec((XATT_TQ, XATTN_WIDTH), lambda b, q: (b * nq + q, COL_XQ // XATTN_WIDTH)),
                  pl.BlockSpec((XATT_TQ, XATTN_WIDTH), lambda b, q: (b * nq + q, COL_XG // XATTN_WIDTH)),
                  pl.BlockSpec((mem_len, XATTN_WIDTH), lambda b, q: (b, 0)),
                  pl.BlockSpec((XATTN_HEADS * XATT_V_ROWS, mem_len), lambda b, q: (b, 0))],
        out_specs=pl.BlockSpec((XATT_TQ, XATTN_WIDTH), lambda b, q: (b * nq + q, 0)),
        out_shape=jax.ShapeDtypeStruct((batch * seq, XATTN_WIDTH), BF16),
        compiler_params=pltpu.CompilerParams(
            dimension_semantics=("arbitrary", "arbitrary"), vmem_limit_bytes=32 * MIB),
        name="xattn",
    )(proj, proj, mk, mvt)


def _out_proj_kernel(d_ref, s_ref, a_ref, w_ref, nw_ref, x_ref, o_ref, wb_ref):
    @pl.when(pl.program_id(0) == 0)
    def _():
        for k0 in range(0, D_MIX, OUT_SUB):
            wb_ref[k0:k0 + OUT_SUB, :] = w_ref[k0:k0 + OUT_SUB, :].astype(BF16)

    for i in range(OUT_TM // OUT_SUB):
        r = slice(i * OUT_SUB, (i + 1) * OUT_SUB)
        y = jnp.dot(d_ref[r, :], wb_ref[0:DIFF_WIDTH, :], preferred_element_type=F32)
        y = y + jnp.dot(s_ref[r, :], wb_ref[DIFF_WIDTH:DIFF_WIDTH + SSD_WIDTH, :],
                        preferred_element_type=F32)
        y = y + jnp.dot(a_ref[r, :], wb_ref[DIFF_WIDTH + SSD_WIDTH:D_MIX, :],
                        preferred_element_type=F32)
        o_ref[r, :] = x_ref[r, :] + y * _rms_scale(y) * nw_ref[...]


def _out_proj(diff_out, ssd_out, xattn_out, w_out, post_norm_w, x2):
    m = x2.shape[0]
    return pl.pallas_call(
        _out_proj_kernel,
        grid=(m // OUT_TM,),
        in_specs=[pl.BlockSpec((OUT_TM, DIFF_WIDTH), lambda i: (i, 0)),
                  pl.BlockSpec((OUT_TM, SSD_WIDTH), lambda i: (i, 0)),
                  pl.BlockSpec((OUT_TM, XATTN_WIDTH), lambda i: (i, 0)),
                  pl.BlockSpec((D_MIX, D_MODEL), lambda i: (0, 0), pipeline_mode=pl.Buffered(1)),
                  pl.BlockSpec((1, D_MODEL), lambda i: (0, 0)),
                  pl.BlockSpec((OUT_TM, D_MODEL), lambda i: (i, 0))],
        out_specs=pl.BlockSpec((OUT_TM, D_MODEL), lambda i: (i, 0)),
        out_shape=jax.ShapeDtypeStruct((m, D_MODEL), F32),
        scratch_shapes=[pltpu.VMEM((D_MIX, D_MODEL), BF16)],
        compiler_params=pltpu.CompilerParams(
            dimension_semantics=("arbitrary",), vmem_limit_bytes=48 * MIB),
        name="out_proj",
    )(diff_out, ssd_out, xattn_out, w_out, post_norm_w.reshape(1, D_MODEL), x2)


def kernel(x, mem, positions, pre_norm_w, w_in, lambda_q1, lambda_k1, lambda_q2, lambda_k2,
           diff_subln_w, conv_w, conv_b, dt_bias, a_log, d_skip, ssd_norm_w, mem_norm_w,
           w_mem_kv, w_out, post_norm_w):
    batch, seq, _ = x.shape
    mem_len = mem.shape[1]
    assert pre_norm_w.shape[0] == 1, "single-layer kernel"
    x2 = x.reshape(batch * seq, D_MODEL)

    w_main, w_v, w_dt, cos, sin = _w_in_prep(jnp.transpose(w_in[0]), positions)
    lam_vecs = jnp.stack([lambda_q1[0], lambda_k1[0], lambda_q2[0], lambda_k2[0]]).astype(F32)

    proj, v_t, dt_raw_t = _in_proj(x2, pre_norm_w[0], w_main, w_v, w_dt, cos, sin)
    diff_out = _diff_attention(proj, v_t, lam_vecs, diff_subln_w[0], batch, seq)
    ssd_out = _ssd(proj, dt_raw_t, conv_w[0], conv_b[0], dt_bias[0], a_log[0], d_skip[0],
                   ssd_norm_w[0], batch, seq)
    mk, mvt = _mem_kv(mem.reshape(batch * mem_len, D_MODEL), mem_norm_w[0], w_mem_kv[0],
                      batch, mem_len)
    xattn_out = _xattn(proj, mk, mvt, batch, seq, mem_len)
    out = _out_proj(diff_out, ssd_out, xattn_out, w_out[0], post_norm_w[0], x2)
    return out.reshape(batch, seq, D_MODEL)
```

```python
import functools
import math

import jax
import jax.numpy as jnp
from jax import lax
from jax.experimental import pallas as pl
from jax.experimental.pallas import tpu as pltpu

F32 = jnp.float32
BF16 = jnp.bfloat16

D_MODEL = 2048
DIFF_HEADS = 8
DIFF_QK_DIM = 64
DIFF_V_DIM = 128
DIFF_WIDTH = DIFF_HEADS * DIFF_V_DIM
SSD_HEADS = 8
SSD_HEAD_DIM = 64
SSD_WIDTH = SSD_HEADS * SSD_HEAD_DIM
SSD_GROUPS = 2
SSD_STATE = 128
SSD_CONV = 4
SSD_CHUNK = 128
SSD_CONV_DIM = SSD_WIDTH + 2 * SSD_GROUPS * SSD_STATE
XATTN_HEADS = 4
XATTN_HEAD_DIM = 128
XATTN_WIDTH = XATTN_HEADS * XATTN_HEAD_DIM
D_MIX = DIFF_WIDTH + SSD_WIDTH + XATTN_WIDTH
ROPE_THETA = 10000.0
NORM_EPS = 1e-6
LAMBDA_INIT = 0.8 - 0.6 * math.exp(-0.3 * 0)

LANES = 128
SUBLANES = 8
MIB = 1024 * 1024

COL_Q = 0
COL_K = COL_Q + DIFF_WIDTH
COL_G = COL_K + DIFF_WIDTH
COL_Z = COL_G + DIFF_WIDTH
COL_XS = COL_Z + SSD_WIDTH
COL_B = COL_XS + SSD_WIDTH
COL_C = COL_B + SSD_GROUPS * SSD_STATE
COL_XQ = COL_C + SSD_GROUPS * SSD_STATE
COL_XG = COL_XQ + XATTN_WIDTH
N_MAIN = COL_XG + XATTN_WIDTH
REF_V = 2 * DIFF_WIDTH
REF_G = REF_V + DIFF_WIDTH
REF_DT = REF_G + DIFF_WIDTH + SSD_WIDTH + SSD_CONV_DIM
REF_XQ = REF_DT + SSD_HEADS
D_IN = REF_XQ + 2 * XATTN_WIDTH

PROJ_TM = 512
PROJ_TN = 512
NORM_ROWS = 256
ATT_TQ = 512
ATT_TK = 512
ATT_HEADS_PER_STEP = 4
ATT_SUM_ROWS = 16
ATT_AHEAD = 2
XATT_TQ = 1024
OUT_TM = 512
OUT_SUB = 256
ROPE_TM = 512
SSD_CHUNKS_PER_STEP = 8
DT_ROWS = 16

NT_DIMS = (((1,), (1,)), ((), ()))
TN_DIMS = (((0,), (0,)), ((), ()))


def _rms_scale(x):
    return lax.rsqrt(jnp.mean(x * x, axis=-1, keepdims=True) + NORM_EPS)


def _silu(x):
    return x * jax.nn.sigmoid(x)


def _rope_table_rows(pos_ref, invf_ref, cos_ref, sin_ref):
    ang = pos_ref[...].astype(F32) * invf_ref[...]
    lane = lax.broadcasted_iota(jnp.int32, ang.shape, 1)
    first_half = (lane % DIFF_QK_DIM) < (DIFF_QK_DIM // 2)
    s = jnp.sin(ang)
    cos_ref[...] = jnp.cos(ang)
    sin_ref[...] = jnp.where(first_half, -s, s)


def _rope(x, cos, sin_signed):
    lane = lax.broadcasted_iota(jnp.int32, x.shape, 1)
    first_half = (lane % DIFF_QK_DIM) < (DIFF_QK_DIM // 2)
    half = DIFF_QK_DIM // 2
    rot = jnp.where(first_half, pltpu.roll(x, LANES - half, 1), pltpu.roll(x, half, 1))
    return x * cos + rot * sin_signed


def _w_in_prep_kernel(wt_ref, dt_rows_ref, pos_ref, invf_ref,
                      wm_ref, wv_ref, wdt_ref, cos_ref, sin_ref):
    j = pl.program_id(0)
    main_blocks = N_MAIN // PROJ_TN
    last_w = main_blocks + DIFF_WIDTH // PROJ_TN
    rope = functools.partial(_rope_table_rows, pos_ref, invf_ref, cos_ref, sin_ref)

    @pl.when(j < main_blocks)
    def _():
        rope()
        wm_ref[0] = wt_ref[...].T.astype(BF16)

    @pl.when((j >= main_blocks) & (j < last_w))
    def _():
        rope()
        wv_ref[...] = wt_ref[...].astype(BF16)

    pl.when(j >= last_w)(rope)

    @pl.when(j == 0)
    def _():
        row = lax.broadcasted_iota(jnp.int32, (DT_ROWS, D_MODEL), 0)
        wdt_ref[...] = jnp.where(row < SSD_HEADS, dt_rows_ref[...], 0.0).astype(BF16)


def _w_in_prep(w_t, positions):
    n, k = w_t.shape
    assert n == D_IN and k == D_MODEL
    main_blocks = N_MAIN // PROJ_TN
    v_blocks = DIFF_WIDTH // PROJ_TN
    last_w = main_blocks + v_blocks - 1
    n_pos = positions.size
    steps = n_pos // ROPE_TM
    assert steps > last_w
    inv_freq = 1.0 / (ROPE_THETA ** (jnp.arange(0, DIFF_QK_DIM, 2, dtype=F32) / DIFF_QK_DIM))
    invf = jnp.tile(inv_freq, LANES // inv_freq.shape[0]).reshape(1, LANES)

    def src_row(j):
        main = jnp.where(j < REF_V // PROJ_TN, j * PROJ_TN,
                         jnp.where(j < (REF_DT - DIFF_WIDTH) // PROJ_TN,
                                   j * PROJ_TN + DIFF_WIDTH,
                                   j * PROJ_TN + DIFF_WIDTH + SSD_HEADS))
        jw = jnp.minimum(j, last_w)
        return pl.multiple_of(jnp.where(jw < main_blocks, main,
                                        REF_V + (jw - main_blocks) * PROJ_TN), SUBLANES)

    return pl.pallas_call(
        _w_in_prep_kernel,
        grid=(steps,),
        in_specs=[pl.BlockSpec((pl.Element(PROJ_TN), pl.Element(D_MODEL)),
                               lambda j: (src_row(j), 0)),
                  pl.BlockSpec((DT_ROWS, D_MODEL), lambda j: (REF_DT // DT_ROWS, 0)),
                  pl.BlockSpec((ROPE_TM, 1), lambda j: (j, 0)),
                  pl.BlockSpec((1, LANES), lambda j: (0, 0))],
        out_specs=[pl.BlockSpec((1, D_MODEL, PROJ_TN),
                                lambda j: (jnp.minimum(j, main_blocks - 1), 0, 0)),
                   pl.BlockSpec((PROJ_TN, D_MODEL),
                                lambda j: (jnp.clip(j - main_blocks, 0, v_blocks - 1), 0)),
                   pl.BlockSpec((DT_ROWS, D_MODEL), lambda j: (0, 0)),
                   pl.BlockSpec((ROPE_TM, LANES), lambda j: (j, 0)),
                   pl.BlockSpec((ROPE_TM, LANES), lambda j: (j, 0))],
        out_shape=[jax.ShapeDtypeStruct((main_blocks, k, PROJ_TN), BF16),
                   jax.ShapeDtypeStruct((DIFF_WIDTH, k), BF16),
                   jax.ShapeDtypeStruct((DT_ROWS, k), BF16),
                   jax.ShapeDtypeStruct((n_pos, LANES), F32),
                   jax.ShapeDtypeStruct((n_pos, LANES), F32)],
        compiler_params=pltpu.CompilerParams(
            dimension_semantics=("arbitrary",), vmem_limit_bytes=32 * MIB),
        name="w_in_prep",
    )(w_t, w_t, positions.reshape(n_pos, 1), invf)


def _in_proj_kernel(x_ref, nw_ref, w_ref, wv_ref, wdt_ref, cos_ref, sin_ref,
                    o_ref, vt_ref, dt_ref, h_ref):
    def norm_rows(i, carry):
        r = pl.ds(pl.multiple_of(i * NORM_ROWS, NORM_ROWS), NORM_ROWS)
        x = x_ref[r, :]
        h_ref[r, :] = (x * _rms_scale(x) * nw_ref[...]).astype(BF16)
        return carry
    lax.fori_loop(0, PROJ_TM // NORM_ROWS, norm_rows, 0)
    dt_t = lax.dot_general(wdt_ref[...], h_ref[...], NT_DIMS, preferred_element_type=F32)
    dt_ref[...] = dt_t[0:SSD_HEADS, :]
    vt_ref[...] = lax.dot_general(wv_ref[...], h_ref[...], NT_DIMS,
                                  preferred_element_type=F32).astype(BF16)
    q_scale = DIFF_QK_DIM ** -0.5 * math.log2(math.e)
    xq_scale = XATTN_HEAD_DIM ** -0.5 * math.log2(math.e)
    for n in range(N_MAIN // PROJ_TN):
        c0 = n * PROJ_TN
        y = jnp.dot(h_ref[...], w_ref[n], preferred_element_type=F32)
        if c0 < COL_G:
            scale = q_scale if c0 < COL_K else 1.0
            for hd in range(PROJ_TN // LANES):
                y_h = _rope(y[:, hd * LANES:(hd + 1) * LANES], cos_ref[...], sin_ref[...])
                o_ref[:, c0 + hd * LANES:c0 + (hd + 1) * LANES] = (y_h * scale).astype(BF16)
        elif COL_XQ <= c0 < COL_XG:
            o_ref[:, c0:c0 + PROJ_TN] = (y * xq_scale).astype(BF16)
        else:
            o_ref[:, c0:c0 + PROJ_TN] = y.astype(BF16)


def _in_proj(x2, pre_norm_w, w_main, w_v, w_dt, cos, sin):
    m = x2.shape[0]
    resident = pl.Buffered(1)
    return pl.pallas_call(
        _in_proj_kernel,
        grid=(m // PROJ_TM,),
        in_specs=[pl.BlockSpec((PROJ_TM, D_MODEL), lambda i: (i, 0)),
                  pl.BlockSpec((1, D_MODEL), lambda i: (0, 0)),
                  pl.BlockSpec((N_MAIN // PROJ_TN, D_MODEL, PROJ_TN), lambda i: (0, 0, 0),
                               pipeline_mode=resident),
                  pl.BlockSpec((DIFF_WIDTH, D_MODEL), lambda i: (0, 0), pipeline_mode=resident),
                  pl.BlockSpec((DT_ROWS, D_MODEL), lambda i: (0, 0), pipeline_mode=resident),
                  pl.BlockSpec((PROJ_TM, LANES), lambda i: (i, 0)),
                  pl.BlockSpec((PROJ_TM, LANES), lambda i: (i, 0))],
        out_specs=[pl.BlockSpec((PROJ_TM, N_MAIN), lambda i: (i, 0)),
                   pl.BlockSpec((DIFF_WIDTH, PROJ_TM), lambda i: (0, i)),
                   pl.BlockSpec((SSD_HEADS, PROJ_TM), lambda i: (0, i))],
        out_shape=[jax.ShapeDtypeStruct((m, N_MAIN), BF16),
                   jax.ShapeDtypeStruct((DIFF_WIDTH, m), BF16),
                   jax.ShapeDtypeStruct((SSD_HEADS, m), F32)],
        scratch_shapes=[pltpu.VMEM((PROJ_TM, D_MODEL), BF16)],
        compiler_params=pltpu.CompilerParams(
            dimension_semantics=("arbitrary",),
            vmem_limit_bytes=56 * MIB),
        name="in_proj",
    )(x2, pre_norm_w.reshape(1, D_MODEL), w_main, w_v, w_dt, cos, sin)


def _diff_attn_kernel(lam_ref, q_ref, k_ref, v_ref, g_ref, subw_ref, o_ref, vt_ref, *, seq):
    qi = pl.program_id(2)
    heads = ATT_HEADS_PER_STEP

    @pl.when(qi == 0)
    def _():
        for g in range(heads):
            vt_ref[g, 0:DIFF_V_DIM, :] = v_ref[g * DIFF_V_DIM:(g + 1) * DIFF_V_DIM, :]
        vt_ref[:, DIFF_V_DIM:, :] = jnp.ones((heads, ATT_SUM_ROWS, seq), BF16)

    lane = lax.broadcasted_iota(jnp.int32, (ATT_TQ, LANES), 1)
    q_maps = []
    for g in range(heads):
        q = q_ref[:, g * LANES:(g + 1) * LANES]
        q_maps.append((jnp.where(lane < DIFF_QK_DIM, q, jnp.zeros_like(q)),
                       jnp.where(lane >= DIFF_QK_DIM, q, jnp.zeros_like(q))))

    lv = lam_ref[...]
    lam = (jnp.exp(jnp.sum(lv[0:1] * lv[1:2], axis=-1, keepdims=True))
           - jnp.exp(jnp.sum(lv[2:3] * lv[3:4], axis=-1, keepdims=True)) + LAMBDA_INIT)

    def run(n_tiles):
        diag = (n_tiles - 1) * ATT_TK
        half = ATT_TK // 2
        blocks = [(j * ATT_TK, (j + 1) * ATT_TK, 0, False) for j in range(n_tiles - 1)]
        blocks += [(diag, diag + half, 0, True), (diag + half, diag + ATT_TK, half, True)]
        items = [(blk, g, c) for blk in blocks for g in range(heads) for c in range(2)]

        def scores(blk, g, c):
            k0, k1, q0, masked = blk
            kt = k_ref[k0:k1, g * LANES:(g + 1) * LANES]
            s = lax.dot_general(kt, q_maps[g][c][q0:, :], NT_DIMS, preferred_element_type=F32)
            if not masked:
                return s
            key = lax.broadcasted_iota(jnp.int32, s.shape, 0) + (k0 - diag)
            qry = lax.broadcasted_iota(jnp.int32, s.shape, 1) + q0
            return jnp.where(key <= qry, s, -jnp.inf)

        pending = {t: scores(*items[t]) for t in range(min(ATT_AHEAD, len(items)))}
        m, acc = {}, {}
        for t, (blk, g, c) in enumerate(items):
            k0, k1, q0, _ = blk
            i = 2 * g + c
            s = pending.pop(t)
            m_new = jnp.max(s, axis=0, keepdims=True)
            first = i not in m
            if not first:
                m_old = m[i][:, q0:]
                m_new = jnp.maximum(m_old, m_new)
            p = jnp.exp2(s - m_new).astype(BF16)
            if t + ATT_AHEAD < len(items):
                pending[t + ATT_AHEAD] = scores(*items[t + ATT_AHEAD])
            pv = jnp.dot(vt_ref[g, :, k0:k1], p, preferred_element_type=F32)
            if first:
                m[i], acc[i] = m_new, pv
            else:
                new_acc = jnp.exp2(m_old - m_new) * acc[i][:, q0:] + pv
                if q0:
                    m_new = jnp.concatenate([m[i][:, :q0], m_new], axis=1)
                    new_acc = jnp.concatenate([acc[i][:, :q0], new_acc], axis=1)
                m[i], acc[i] = m_new, new_acc

        for g in range(heads):
            c = slice(g * LANES, (g + 1) * LANES)
            a1, a2 = acc[2 * g], acc[2 * g + 1]
            l1 = a1[DIFF_V_DIM:DIFF_V_DIM + 1, :]
            l2 = a2[DIFF_V_DIM:DIFF_V_DIM + 1, :]
            o_t = (a1[0:DIFF_V_DIM, :] * (1.0 / l1)
                   - lam * (a2[0:DIFF_V_DIM, :] * (1.0 / l2)))
            o = o_t.T
            o = o * _rms_scale(o) * subw_ref[...] * (1.0 - LAMBDA_INIT)
            o_ref[:, c] = (o * _silu(g_ref[:, c].astype(F32))).astype(BF16)

    for n in range(seq // ATT_TQ):
        pl.when(qi == n)(functools.partial(run, n + 1))


def _diff_attention(proj, v_t, lam_vecs, subln_w, batch, seq):
    assert ATT_TQ == ATT_TK and seq % ATT_TQ == 0 and DIFF_V_DIM == LANES
    nq = seq // ATT_TQ
    heads = ATT_HEADS_PER_STEP
    width = heads * LANES
    kernel = functools.partial(_diff_attn_kernel, seq=seq)
    return pl.pallas_call(
        kernel,
        grid=(batch, DIFF_HEADS // heads, nq),
        in_specs=[
            pl.BlockSpec((4, DIFF_QK_DIM), lambda b, h, q: (0, 0)),
            pl.BlockSpec((ATT_TQ, width), lambda b, h, q: (b * nq + q, COL_Q // width + h)),
            pl.BlockSpec((seq, width), lambda b, h, q: (b, COL_K // width + h)),
            pl.BlockSpec((width, seq), lambda b, h, q: (h, b)),
            pl.BlockSpec((ATT_TQ, width), lambda b, h, q: (b * nq + q, COL_G // width + h)),
            pl.BlockSpec((1, DIFF_V_DIM), lambda b, h, q: (0, 0)),
        ],
        out_specs=pl.BlockSpec((ATT_TQ, width), lambda b, h, q: (b * nq + q, h)),
        out_shape=jax.ShapeDtypeStruct((batch * seq, DIFF_WIDTH), BF16),
        scratch_shapes=[pltpu.VMEM((heads, DIFF_V_DIM + ATT_SUM_ROWS, seq), BF16)],
        compiler_params=pltpu.CompilerParams(
            dimension_semantics=("arbitrary", "arbitrary", "arbitrary"),
            vmem_limit_bytes=48 * MIB),
        name="diff_attn",
    )(lam_vecs, proj, proj, v_t, proj, subln_w.reshape(1, DIFF_V_DIM))


def _ssd_kernel(xs_ref, b_ref, c_ref, z_ref, dt_ref, shift_ref, triu_ref, selp_ref, selh_ref,
                cw_ref, cb_ref, dtb_ref, alog_ref, dsk_ref, nw_ref, o_ref, xext_ref, state_ref):
    L = SSD_CHUNK
    T = SSD_CHUNKS_PER_STEP
    gs = SSD_GROUPS * SSD_STATE
    pairs = SSD_HEADS // 2

    @pl.when(pl.program_id(1) == 0)
    def _():
        xext_ref[0:L, :] = jnp.zeros((L, SSD_CONV_DIM), BF16)
        state_ref[...] = jnp.zeros(state_ref.shape, F32)

    xext_ref[L:(T + 1) * L, 0:SSD_WIDTH] = xs_ref[...]
    xext_ref[L:(T + 1) * L, SSD_WIDTH:SSD_WIDTH + gs] = b_ref[...]
    xext_ref[L:(T + 1) * L, SSD_WIDTH + gs:SSD_CONV_DIM] = c_ref[...]

    carry = {"state": [state_ref[p] for p in range(pairs)], "cols_split": [None] * T}
    programs = [_ssd_chunk(c, carry, xext_ref, z_ref, dt_ref, shift_ref, triu_ref, cw_ref,
                           cb_ref, dtb_ref, alog_ref, dsk_ref, nw_ref, o_ref)
                for c in range(T)]

    def stage():
        for prog in programs:
            next(prog)

    stage()
    stage()
    split = jnp.concatenate(carry["cols_split"], axis=0)
    carry["spread"] = jnp.concatenate(
        [jnp.dot(split[:, 0:2 * LANES], selp_ref[...], preferred_element_type=F32),
         jnp.dot(split, selh_ref[...], preferred_element_type=F32)], axis=1)
    stage()
    stage()
    stage()
    for p in range(pairs):
        state_ref[p] = carry["state"][p]
    xext_ref[0:L, :] = xext_ref[T * L:(T + 1) * L, :]


SSD_PAIR_KINDS = (0, 2)


def _ssd_chunk(c, carry, xext_ref, z_ref, dt_ref, shift_ref, triu_ref, cw_ref,
               cb_ref, dtb_ref, alog_ref, dsk_ref, nw_ref, o_ref):
    L = SSD_CHUNK
    H = SSD_HEADS
    gs = SSD_GROUPS * SSD_STATE
    rows_c = slice(c * L, (c + 1) * L)

    delayed = jnp.dot(shift_ref[...], xext_ref[c * L:(c + 2) * L, :],
                      preferred_element_type=F32)
    conv = (cb_ref[...] + xext_ref[(c + 1) * L:(c + 2) * L, :].astype(F32)
            * cw_ref[SSD_CONV - 1:SSD_CONV, :])
    for k in range(SSD_CONV - 1):
        conv = conv + delayed[k * L:(k + 1) * L, :] * cw_ref[k:k + 1, :]
    xbc = _silu(conv)
    xs = xbc[:, 0:SSD_WIDTH]
    bm = xbc[:, SSD_WIDTH:SSD_WIDTH + gs].astype(BF16)
    cm = xbc[:, SSD_WIDTH + gs:SSD_CONV_DIM].astype(BF16)
    yield

    dt_in = dt_ref[:, rows_c] + dtb_ref[...]
    dt = jnp.maximum(dt_in, 0.0) + jnp.log1p(jnp.exp(-jnp.abs(dt_in)))
    a_dt = dt * (-jnp.exp(alog_ref[...]))
    hi = a_dt.astype(BF16)
    r1 = a_dt - hi.astype(F32)
    mid = r1.astype(BF16)
    lo = (r1 - mid.astype(F32)).astype(BF16)
    triu = triu_ref[...]
    a_cs = (jnp.dot(hi, triu, preferred_element_type=F32)
            + jnp.dot(mid, triu, preferred_element_type=F32)
            + jnp.dot(lo, triu, preferred_element_type=F32))
    a_end = a_cs[:, L - 1:L]
    chunk_decay = jnp.exp(a_end)
    rows = jnp.concatenate([dt * jnp.exp(a_end - a_cs), a_cs, jnp.exp(a_cs),
                            jnp.zeros((L - 3 * H, L), F32)], axis=0)
    cols = rows.T
    c_hi = cols.astype(BF16)
    c_r1 = cols - c_hi.astype(F32)
    c_mid = c_r1.astype(BF16)
    c_lo = (c_r1 - c_mid.astype(F32)).astype(BF16)
    carry["cols_split"][c] = jnp.concatenate([c_hi, c_mid, c_lo], axis=1)
    yield
    spread = carry["spread"][rows_c, :]

    row = lax.broadcasted_iota(jnp.int32, (L, L), 0)
    col = lax.broadcasted_iota(jnp.int32, (L, L), 1)
    lower = row >= col
    lane = lax.broadcasted_iota(jnp.int32, (L, LANES), 1)
    left = lane < SSD_HEAD_DIM
    top = row < SSD_HEAD_DIM
    pairs = SSD_HEADS // 2

    def pair_lanes(slot, p):
        i = slot * pairs + p
        return spread[:, i * LANES:(i + 1) * LANES]

    def head_lanes(h):
        i = len(SSD_PAIR_KINDS) * pairs + h
        return spread[:, i * LANES:(i + 1) * LANES]

    group_cb = []
    for g in range(SSD_GROUPS):
        group_cb.append(lax.dot_general(cm[:, g * SSD_STATE:(g + 1) * SSD_STATE],
                                        bm[:, g * SSD_STATE:(g + 1) * SSD_STATE],
                                        NT_DIMS, preferred_element_type=F32))
    yield

    group_of = lambda p: (2 * p) // (SSD_HEADS // SSD_GROUPS)
    y_diag, chunk_state = [], []
    for p in range(pairs):
        bg = bm[:, group_of(p) * SSD_STATE:(group_of(p) + 1) * SSD_STATE]
        xs_p = xs[:, p * LANES:(p + 1) * LANES]
        xs_b = xs_p.astype(BF16)
        y_heads = []
        for hh in range(2):
            h = 2 * p + hh
            seg = head_lanes(h) - a_cs[h:h + 1, :]
            decay = jnp.exp(jnp.where(lower, seg, -jnp.inf))
            scores = (group_cb[group_of(p)] * decay * dt[h:h + 1, :]).astype(BF16)
            y_heads.append(jnp.dot(scores, xs_b, preferred_element_type=F32))
        y_diag.append(jnp.where(left, y_heads[0], y_heads[1]))
        x_w = (xs_p * pair_lanes(0, p)).astype(BF16)
        chunk_state.append(lax.dot_general(x_w, bg, TN_DIMS, preferred_element_type=F32))
    yield

    ys, new_state = [], []
    for p in range(pairs):
        cg = cm[:, group_of(p) * SSD_STATE:(group_of(p) + 1) * SSD_STATE]
        st = carry["state"][p]
        y_off = (lax.dot_general(cg, st.astype(BF16), NT_DIMS, preferred_element_type=F32)
                 * pair_lanes(1, p))
        st_decay = jnp.where(top, chunk_decay[2 * p:2 * p + 1, :],
                             chunk_decay[2 * p + 1:2 * p + 2, :])
        new_state.append(st * st_decay + chunk_state[p])
        y = (y_diag[p] + y_off
             + xs[:, p * LANES:(p + 1) * LANES] * dsk_ref[:, p * LANES:(p + 1) * LANES])
        ys.append(y * _silu(z_ref[rows_c, p * LANES:(p + 1) * LANES].astype(F32)))
    carry["state"] = new_state

    grp_pairs = pairs // SSD_GROUPS
    grp_width = SSD_WIDTH // SSD_GROUPS
    for g in range(SSD_GROUPS):
        members = ys[g * grp_pairs:(g + 1) * grp_pairs]
        ssq = sum(jnp.sum(y * y, axis=-1, keepdims=True) for y in members)
        inv = lax.rsqrt(ssq / grp_width + NORM_EPS)
        for i, y in enumerate(members):
            c0 = (g * grp_pairs + i) * LANES
            o_ref[rows_c, c0:c0 + LANES] = (y * inv * nw_ref[:, c0:c0 + LANES]).astype(BF16)
    yield


def _ssd(proj, dt_raw_t, conv_w, conv_b, dt_bias, a_log, d_skip, ssd_norm_w, batch, seq):
    gs = SSD_GROUPS * SSD_STATE
    L = SSD_CHUNK
    rows = SSD_CHUNKS_PER_STEP * L
    nc = seq // rows
    row_map = lambda b, c: b * nc + c
    const = lambda b, c: (0, 0)
    t = jnp.arange(L)[:, None]
    src = jnp.arange(2 * L)[None, :]
    shift = jnp.concatenate([(src == L + t - (SSD_CONV - 1 - k)) for k in range(SSD_CONV - 1)],
                            axis=0).astype(BF16)
    triu = (jnp.arange(L)[:, None] <= jnp.arange(L)[None, :]).astype(BF16)
    lane = jnp.arange(LANES)
    pair_blocks, head_blocks = [], []
    for kind in SSD_PAIR_KINDS:
        for p in range(SSD_HEADS // 2):
            src_col = kind * SSD_HEADS + 2 * p + (lane >= SSD_HEAD_DIM)
            pair_blocks.append(jnp.arange(L)[:, None] == src_col[None, :])
    for h in range(SSD_HEADS):
        head_blocks.append(jnp.broadcast_to(jnp.arange(L)[:, None] == SSD_HEADS + h, (L, LANES)))
    sel_pair = jnp.tile(jnp.concatenate(pair_blocks, axis=1), (2, 1)).astype(BF16)
    sel_head = jnp.tile(jnp.concatenate(head_blocks, axis=1), (3, 1)).astype(BF16)
    per_head = lambda v: jnp.broadcast_to(v.astype(F32)[:, None], (SSD_HEADS, L))
    return pl.pallas_call(
        _ssd_kernel,
        grid=(batch, nc),
        in_specs=[
            pl.BlockSpec((rows, SSD_WIDTH), lambda b, c: (row_map(b, c), COL_XS // SSD_WIDTH)),
            pl.BlockSpec((rows, gs), lambda b, c: (row_map(b, c), COL_B // gs)),
            pl.BlockSpec((rows, gs), lambda b, c: (row_map(b, c), COL_C // gs)),
            pl.BlockSpec((rows, SSD_WIDTH), lambda b, c: (row_map(b, c), COL_Z // SSD_WIDTH)),
            pl.BlockSpec((SSD_HEADS, rows), lambda b, c: (0, row_map(b, c))),
            pl.BlockSpec(((SSD_CONV - 1) * L, 2 * L), const),
            pl.BlockSpec((L, L), const),
            pl.BlockSpec(sel_pair.shape, const),
            pl.BlockSpec(sel_head.shape, const),
            pl.BlockSpec((SSD_CONV, SSD_CONV_DIM), const),
            pl.BlockSpec((1, SSD_CONV_DIM), const),
            pl.BlockSpec((SSD_HEADS, L), const),
            pl.BlockSpec((SSD_HEADS, L), const),
            pl.BlockSpec((1, SSD_WIDTH), const),
            pl.BlockSpec((1, SSD_WIDTH), const),
        ],
        out_specs=pl.BlockSpec((rows, SSD_WIDTH), lambda b, c: (row_map(b, c), 0)),
        out_shape=jax.ShapeDtypeStruct((batch * seq, SSD_WIDTH), BF16),
        scratch_shapes=[pltpu.VMEM((L + rows, SSD_CONV_DIM), BF16),
                        pltpu.VMEM((SSD_HEADS // 2, 2 * SSD_HEAD_DIM, SSD_STATE), F32)],
        compiler_params=pltpu.CompilerParams(
            dimension_semantics=("arbitrary", "arbitrary"),
            vmem_limit_bytes=32 * MIB),
        name="ssd",
    )(proj, proj, proj, proj, dt_raw_t, shift, triu, sel_pair, sel_head, conv_w, conv_b.reshape(1, SSD_CONV_DIM),
      per_head(dt_bias), per_head(a_log),
      jnp.repeat(d_skip, SSD_HEAD_DIM).reshape(1, SSD_WIDTH), ssd_norm_w.reshape(1, SSD_WIDTH))


XATT_V_ROWS = XATTN_HEAD_DIM + ATT_SUM_ROWS


def _mem_kv_kernel(mem_ref, nw_ref, w_ref, k_ref, vt_ref, wb_ref):
    @pl.when(pl.program_id(0) == 0)
    def _():
        wb_ref[...] = w_ref[...].astype(BF16)

    x = mem_ref[...]
    h = (x * _rms_scale(x) * nw_ref[...]).astype(BF16)
    kv = jnp.dot(h, wb_ref[...], preferred_element_type=F32)
    k_ref[...] = kv[:, 0:XATTN_WIDTH].astype(BF16)
    ones = jnp.ones((ATT_SUM_ROWS, x.shape[0]), BF16)
    for hd in range(XATTN_HEADS):
        c0 = XATTN_WIDTH + hd * XATTN_HEAD_DIM
        r0 = hd * XATT_V_ROWS
        vt_ref[r0:r0 + XATTN_HEAD_DIM, :] = kv[:, c0:c0 + XATTN_HEAD_DIM].T.astype(BF16)
        vt_ref[r0 + XATTN_HEAD_DIM:r0 + XATT_V_ROWS, :] = ones


def _mem_kv(mem2, mem_norm_w, w_kv, batch, mem_len):
    return pl.pallas_call(
        _mem_kv_kernel,
        grid=(batch,),
        in_specs=[pl.BlockSpec((mem_len, D_MODEL), lambda b: (b, 0)),
                  pl.BlockSpec((1, D_MODEL), lambda b: (0, 0)),
                  pl.BlockSpec((D_MODEL, 2 * XATTN_WIDTH), lambda b: (0, 0),
                               pipeline_mode=pl.Buffered(1))],
        out_specs=[pl.BlockSpec((mem_len, XATTN_WIDTH), lambda b: (b, 0)),
                   pl.BlockSpec((XATTN_HEADS * XATT_V_ROWS, mem_len), lambda b: (b, 0))],
        out_shape=[jax.ShapeDtypeStruct((batch * mem_len, XATTN_WIDTH), BF16),
                   jax.ShapeDtypeStruct((batch * XATTN_HEADS * XATT_V_ROWS, mem_len), BF16)],
        scratch_shapes=[pltpu.VMEM((D_MODEL, 2 * XATTN_WIDTH), BF16)],
        compiler_params=pltpu.CompilerParams(
            dimension_semantics=("arbitrary",), vmem_limit_bytes=32 * MIB),
        name="mem_kv",
    )(mem2, mem_norm_w.reshape(1, D_MODEL), w_kv)


def _xattn_kernel(q_ref, g_ref, k_ref, vt_ref, o_ref):
    cols = [slice(h * XATTN_HEAD_DIM, (h + 1) * XATTN_HEAD_DIM) for h in range(XATTN_HEADS)]
    scores = [lax.dot_general(k_ref[:, c], q_ref[:, c], NT_DIMS, preferred_element_type=F32)
              for c in cols]
    for h, c in enumerate(cols):
        s = scores[h]
        p = jnp.exp2(s - jnp.max(s, axis=0, keepdims=True)).astype(BF16)
        pv = jnp.dot(vt_ref[h * XATT_V_ROWS:(h + 1) * XATT_V_ROWS, :], p,
                     preferred_element_type=F32)
        xo = (pv[0:XATTN_HEAD_DIM, :]
              * (1.0 / pv[XATTN_HEAD_DIM:XATTN_HEAD_DIM + 1, :])).T
        o_ref[:, c] = (xo * _silu(g_ref[:, c].astype(F32))).astype(BF16)


def _xattn(proj, mk, mvt, batch, seq, mem_len):
    nq = seq // XATT_TQ
    return pl.pallas_call(
        _xattn_kernel,
        grid=(batch, nq),
        in_specs=[pl.BlockSpec((XATT_TQ, XATTN_WIDTH), lambda b, q: (b * nq + q, COL_XQ // XATTN_WIDTH)),
                  pl.BlockSpec((XATT_TQ, XATTN_WIDTH), lambda b, q: (b * nq + q, COL_XG // XATTN_WIDTH)),
                  pl.BlockSpec((mem_len, XATTN_WIDTH), lambda b, q: (b, 0)),
                  pl.BlockSpec((XATTN_HEADS * XATT_V_ROWS, mem_len), lambda b, q: (b, 0))],
        out_specs=pl.BlockSpec((XATT_TQ, XATTN_WIDTH), lambda b, q: (b * nq + q, 0)),
        out_shape=jax.ShapeDtypeStruct((batch * seq, XATTN_WIDTH), BF16),
        compiler_params=pltpu.CompilerParams(
            dimension_semantics=("arbitrary", "arbitrary"), vmem_limit_bytes=32 * MIB),
        name="xattn",
    )(proj, proj, mk, mvt)


def _out_proj_kernel(d_ref, s_ref, a_ref, w_ref, nw_ref, x_ref, o_ref, wb_ref):
    @pl.when(pl.program_id(0) == 0)
    def _():
        for k0 in range(0, D_MIX, OUT_SUB):
            wb_ref[k0:k0 + OUT_SUB, :] = w_ref[k0:k0 + OUT_SUB, :].astype(BF16)

    for i in range(OUT_TM // OUT_SUB):
        r = slice(i * OUT_SUB, (i + 1) * OUT_SUB)
        y = jnp.dot(d_ref[r, :], wb_ref[0:DIFF_WIDTH, :], preferred_element_type=F32)
        y = y + jnp.dot(s_ref[r, :], wb_ref[DIFF_WIDTH:DIFF_WIDTH + SSD_WIDTH, :],
                        preferred_element_type=F32)
        y = y + jnp.dot(a_ref[r, :], wb_ref[DIFF_WIDTH + SSD_WIDTH:D_MIX, :],
                        preferred_element_type=F32)
        o_ref[r, :] = x_ref[r, :] + y * _rms_scale(y) * nw_ref[...]


def _out_proj(diff_out, ssd_out, xattn_out, w_out, post_norm_w, x2):
    m = x2.shape[0]
    return pl.pallas_call(
        _out_proj_kernel,
        grid=(m // OUT_TM,),
        in_specs=[pl.BlockSpec((OUT_TM, DIFF_WIDTH), lambda i: (i, 0)),
                  pl.BlockSpec((OUT_TM, SSD_WIDTH), lambda i: (i, 0)),
                  pl.BlockSpec((OUT_TM, XATTN_WIDTH), lambda i: (i, 0)),
                  pl.BlockSpec((D_MIX, D_MODEL), lambda i: (0, 0), pipeline_mode=pl.Buffered(1)),
                  pl.BlockSpec((1, D_MODEL), lambda i: (0, 0)),
                  pl.BlockSpec((OUT_TM, D_MODEL), lambda i: (i, 0))],
        out_specs=pl.BlockSpec((OUT_TM, D_MODEL), lambda i: (i, 0)),
        out_shape=jax.ShapeDtypeStruct((m, D_MODEL), F32),
        scratch_shapes=[pltpu.VMEM((D_MIX, D_MODEL), BF16)],
        compiler_params=pltpu.CompilerParams(
            dimension_semantics=("arbitrary",), vmem_limit_bytes=48 * MIB),
        name="out_proj",
    )(diff_out, ssd_out, xattn_out, w_out, post_norm_w.reshape(1, D_MODEL), x2)


def kernel(x, mem, positions, pre_norm_w, w_in, lambda_q1, lambda_k1, lambda_q2, lambda_k2,
           diff_subln_w, conv_w, conv_b, dt_bias, a_log, d_skip, ssd_norm_w, mem_norm_w,
           w_mem_kv, w_out, post_norm_w):
    batch, seq, _ = x.shape
    mem_len = mem.shape[1]
    assert pre_norm_w.shape[0] == 1, "single-layer kernel"
    x2 = x.reshape(batch * seq, D_MODEL)

    w_main, w_v, w_dt, cos, sin = _w_in_prep(jnp.transpose(w_in[0]), positions)
    lam_vecs = jnp.stack([lambda_q1[0], lambda_k1[0], lambda_q2[0], lambda_k2[0]]).astype(F32)

    proj, v_t, dt_raw_t = _in_proj(x2, pre_norm_w[0], w_main, w_v, w_dt, cos, sin)
    diff_out = _diff_attention(proj, v_t, lam_vecs, diff_subln_w[0], batch, seq)
    ssd_out = _ssd(proj, dt_raw_t, conv_w[0], conv_b[0], dt_bias[0], a_log[0], d_skip[0],
                   ssd_norm_w[0], batch, seq)
    mk, mvt = _mem_kv(mem.reshape(batch * mem_len, D_MODEL), mem_norm_w[0], w_mem_kv[0],
                      batch, mem_len)
    xattn_out = _xattn(proj, mk, mvt, batch, seq, mem_len)
    out = _out_proj(diff_out, ssd_out, xattn_out, w_out[0], post_norm_w[0], x2)
    return out.reshape(batch, seq, D_MODEL)
```

```python
import functools
import math

import jax
import jax.numpy as jnp
from jax import lax
from jax.experimental import pallas as pl
from jax.experimental.pallas import tpu as pltpu

F32 = jnp.float32
BF16 = jnp.bfloat16

D_MODEL = 2048
DIFF_HEADS = 8
DIFF_QK_DIM = 64
DIFF_V_DIM = 128
DIFF_WIDTH = DIFF_HEADS * DIFF_V_DIM
SSD_HEADS = 8
SSD_HEAD_DIM = 64
SSD_WIDTH = SSD_HEADS * SSD_HEAD_DIM
SSD_GROUPS = 2
SSD_STATE = 128
SSD_CONV = 4
SSD_CHUNK = 128
SSD_CONV_DIM = SSD_WIDTH + 2 * SSD_GROUPS * SSD_STATE
XATTN_HEADS = 4
XATTN_HEAD_DIM = 128
XATTN_WIDTH = XATTN_HEADS * XATTN_HEAD_DIM
D_MIX = DIFF_WIDTH + SSD_WIDTH + XATTN_WIDTH
ROPE_THETA = 10000.0
NORM_EPS = 1e-6
LAMBDA_INIT = 0.8 - 0.6 * math.exp(-0.3 * 0)

LANES = 128
SUBLANES = 8
MIB = 1024 * 1024

COL_Q = 0
COL_K = COL_Q + DIFF_WIDTH
COL_G = COL_K + DIFF_WIDTH
COL_Z = COL_G + DIFF_WIDTH
COL_XS = COL_Z + SSD_WIDTH
COL_B = COL_XS + SSD_WIDTH
COL_C = COL_B + SSD_GROUPS * SSD_STATE
COL_XQ = COL_C + SSD_GROUPS * SSD_STATE
COL_XG = COL_XQ + XATTN_WIDTH
N_MAIN = COL_XG + XATTN_WIDTH
REF_V = 2 * DIFF_WIDTH
REF_G = REF_V + DIFF_WIDTH
REF_DT = REF_G + DIFF_WIDTH + SSD_WIDTH + SSD_CONV_DIM
REF_XQ = REF_DT + SSD_HEADS
D_IN = REF_XQ + 2 * XATTN_WIDTH

PROJ_TM = 512
PROJ_TN = 512
NORM_ROWS = 256
ATT_TQ = 512
ATT_TK = 512
ATT_HEADS_PER_STEP = 4
ATT_SUM_ROWS = 16
ATT_AHEAD = 2
XATT_TQ = 1024
OUT_TM = 512
OUT_SUB = 256
ROPE_TM = 512
SSD_CHUNKS_PER_STEP = 8
DT_ROWS = 16

NT_DIMS = (((1,), (1,)), ((), ()))
TN_DIMS = (((0,), (0,)), ((), ()))


def _rms_scale(x):
    return lax.rsqrt(jnp.mean(x * x, axis=-1, keepdims=True) + NORM_EPS)


def _silu(x):
    return x * jax.nn.sigmoid(x)


def _rope_table_rows(pos_ref, invf_ref, cos_ref, sin_ref):
    ang = pos_ref[...].astype(F32) * invf_ref[...]
    lane = lax.broadcasted_iota(jnp.int32, ang.shape, 1)
    first_half = (lane % DIFF_QK_DIM) < (DIFF_QK_DIM // 2)
    s = jnp.sin(ang)
    cos_ref[...] = jnp.cos(ang)
    sin_ref[...] = jnp.where(first_half, -s, s)


def _rope(x, cos, sin_signed):
    lane = lax.broadcasted_iota(jnp.int32, x.shape, 1)
    first_half = (lane % DIFF_QK_DIM) < (DIFF_QK_DIM // 2)
    half = DIFF_QK_DIM // 2
    rot = jnp.where(first_half, pltpu.roll(x, LANES - half, 1), pltpu.roll(x, half, 1))
    return x * cos + rot * sin_signed


def _w_in_prep_kernel(wt_ref, dt_rows_ref, nw_ref, pos_ref, invf_ref,
                      wm_ref, wv_ref, wdt_ref, cos_ref, sin_ref):
    j = pl.program_id(0)
    main_blocks = N_MAIN // PROJ_TN
    last_w = main_blocks + DIFF_WIDTH // PROJ_TN
    rope = functools.partial(_rope_table_rows, pos_ref, invf_ref, cos_ref, sin_ref)

    @pl.when(j < main_blocks)
    def _():
        rope()
        wm_ref[0] = (wt_ref[...] * nw_ref[...]).T.astype(BF16)

    @pl.when((j >= main_blocks) & (j < last_w))
    def _():
        rope()
        wv_ref[...] = (wt_ref[...] * nw_ref[...]).astype(BF16)

    pl.when(j >= last_w)(rope)

    @pl.when(j == 0)
    def _():
        row = lax.broadcasted_iota(jnp.int32, (DT_ROWS, D_MODEL), 0)
        wdt_ref[...] = jnp.where(row < SSD_HEADS, dt_rows_ref[...] * nw_ref[...],
                                 0.0).astype(BF16)


def _w_in_prep(w_t, pre_norm_w, positions):
    n, k = w_t.shape
    assert n == D_IN and k == D_MODEL
    main_blocks = N_MAIN // PROJ_TN
    v_blocks = DIFF_WIDTH // PROJ_TN
    last_w = main_blocks + v_blocks - 1
    n_pos = positions.size
    steps = n_pos // ROPE_TM
    assert steps > last_w
    inv_freq = 1.0 / (ROPE_THETA ** (jnp.arange(0, DIFF_QK_DIM, 2, dtype=F32) / DIFF_QK_DIM))
    invf = jnp.tile(inv_freq, LANES // inv_freq.shape[0]).reshape(1, LANES)

    def src_row(j):
        main = jnp.where(j < REF_V // PROJ_TN, j * PROJ_TN,
                         jnp.where(j < (REF_DT - DIFF_WIDTH) // PROJ_TN,
                                   j * PROJ_TN + DIFF_WIDTH,
                                   j * PROJ_TN + DIFF_WIDTH + SSD_HEADS))
        jw = jnp.minimum(j, last_w)
        return pl.multiple_of(jnp.where(jw < main_blocks, main,
                                        REF_V + (jw - main_blocks) * PROJ_TN), SUBLANES)

    return pl.pallas_call(
        _w_in_prep_kernel,
        grid=(steps,),
        in_specs=[pl.BlockSpec((pl.Element(PROJ_TN), pl.Element(D_MODEL)),
                               lambda j: (src_row(j), 0)),
                  pl.BlockSpec((DT_ROWS, D_MODEL), lambda j: (REF_DT // DT_ROWS, 0)),
                  pl.BlockSpec((1, D_MODEL), lambda j: (0, 0)),
                  pl.BlockSpec((ROPE_TM, 1), lambda j: (j, 0)),
                  pl.BlockSpec((1, LANES), lambda j: (0, 0))],
        out_specs=[pl.BlockSpec((1, D_MODEL, PROJ_TN),
                                lambda j: (jnp.minimum(j, main_blocks - 1), 0, 0)),
                   pl.BlockSpec((PROJ_TN, D_MODEL),
                                lambda j: (jnp.clip(j - main_blocks, 0, v_blocks - 1), 0)),
                   pl.BlockSpec((DT_ROWS, D_MODEL), lambda j: (0, 0)),
                   pl.BlockSpec((ROPE_TM, LANES), lambda j: (j, 0)),
                   pl.BlockSpec((ROPE_TM, LANES), lambda j: (j, 0))],
        out_shape=[jax.ShapeDtypeStruct((main_blocks, k, PROJ_TN), BF16),
                   jax.ShapeDtypeStruct((DIFF_WIDTH, k), BF16),
                   jax.ShapeDtypeStruct((DT_ROWS, k), BF16),
                   jax.ShapeDtypeStruct((n_pos, LANES), F32),
                   jax.ShapeDtypeStruct((n_pos, LANES), F32)],
        compiler_params=pltpu.CompilerParams(
            dimension_semantics=("arbitrary",), vmem_limit_bytes=32 * MIB),
        name="w_in_prep",
    )(w_t, w_t, pre_norm_w.reshape(1, D_MODEL), positions.reshape(n_pos, 1), invf)


def _in_proj_kernel(x_ref, w_ref, wv_ref, wdt_ref, cos_ref, sin_ref,
                    o_ref, vt_ref, dt_ref, h_ref):
    mean_sq = []
    for r0 in range(0, PROJ_TM, NORM_ROWS):
        x = x_ref[r0:r0 + NORM_ROWS, :]
        h_ref[r0:r0 + NORM_ROWS, :] = x.astype(BF16)
        mean_sq.append(jnp.mean(x * x, axis=-1, keepdims=True))
    rs = lax.rsqrt(jnp.concatenate(mean_sq, axis=0) + NORM_EPS)
    rs_row = jnp.broadcast_to(rs, (PROJ_TM, LANES)).T[0:1, :]
    dt_t = lax.dot_general(wdt_ref[...], h_ref[...], NT_DIMS, preferred_element_type=F32)
    dt_ref[...] = dt_t[0:SSD_HEADS, :] * rs_row
    vt_ref[...] = (lax.dot_general(wv_ref[...], h_ref[...], NT_DIMS,
                                   preferred_element_type=F32) * rs_row).astype(BF16)
    q_scale = DIFF_QK_DIM ** -0.5 * math.log2(math.e)
    xq_scale = XATTN_HEAD_DIM ** -0.5 * math.log2(math.e)
    for n in range(N_MAIN // PROJ_TN):
        c0 = n * PROJ_TN
        y = jnp.dot(h_ref[...], w_ref[n], preferred_element_type=F32) * rs
        if c0 < COL_G:
            scale = q_scale if c0 < COL_K else 1.0
            for hd in range(PROJ_TN // LANES):
                y_h = _rope(y[:, hd * LANES:(hd + 1) * LANES], cos_ref[...], sin_ref[...])
                o_ref[:, c0 + hd * LANES:c0 + (hd + 1) * LANES] = (y_h * scale).astype(BF16)
        elif COL_XQ <= c0 < COL_XG:
            o_ref[:, c0:c0 + PROJ_TN] = (y * xq_scale).astype(BF16)
        else:
            o_ref[:, c0:c0 + PROJ_TN] = y.astype(BF16)


def _in_proj(x2, w_main, w_v, w_dt, cos, sin):
    m = x2.shape[0]
    resident = pl.Buffered(1)
    return pl.pallas_call(
        _in_proj_kernel,
        grid=(m // PROJ_TM,),
        in_specs=[pl.BlockSpec((PROJ_TM, D_MODEL), lambda i: (i, 0)),
                  pl.BlockSpec((N_MAIN // PROJ_TN, D_MODEL, PROJ_TN), lambda i: (0, 0, 0),
                               pipeline_mode=resident),
                  pl.BlockSpec((DIFF_WIDTH, D_MODEL), lambda i: (0, 0), pipeline_mode=resident),
                  pl.BlockSpec((DT_ROWS, D_MODEL), lambda i: (0, 0), pipeline_mode=resident),
                  pl.BlockSpec((PROJ_TM, LANES), lambda i: (i, 0)),
                  pl.BlockSpec((PROJ_TM, LANES), lambda i: (i, 0))],
        out_specs=[pl.BlockSpec((PROJ_TM, N_MAIN), lambda i: (i, 0)),
                   pl.BlockSpec((DIFF_WIDTH, PROJ_TM), lambda i: (0, i)),
                   pl.BlockSpec((SSD_HEADS, PROJ_TM), lambda i: (0, i))],
        out_shape=[jax.ShapeDtypeStruct((m, N_MAIN), BF16),
                   jax.ShapeDtypeStruct((DIFF_WIDTH, m), BF16),
                   jax.ShapeDtypeStruct((SSD_HEADS, m), F32)],
        scratch_shapes=[pltpu.VMEM((PROJ_TM, D_MODEL), BF16)],
        compiler_params=pltpu.CompilerParams(
            dimension_semantics=("arbitrary",),
            vmem_limit_bytes=56 * MIB),
        name="in_proj",
    )(x2, w_main, w_v, w_dt, cos, sin)


def _diff_attn_kernel(lam_ref, q_ref, k_ref, v_ref, g_ref, subw_ref, o_ref, vt_ref, *, seq):
    qi = pl.program_id(2)
    heads = ATT_HEADS_PER_STEP

    @pl.when(qi == 0)
    def _():
        for g in range(heads):
            vt_ref[g, 0:DIFF_V_DIM, :] = v_ref[g * DIFF_V_DIM:(g + 1) * DIFF_V_DIM, :]
        vt_ref[:, DIFF_V_DIM:, :] = jnp.ones((heads, ATT_SUM_ROWS, seq), BF16)

    lane = lax.broadcasted_iota(jnp.int32, (ATT_TQ, LANES), 1)
    q_maps = []
    for g in range(heads):
        q = q_ref[:, g * LANES:(g + 1) * LANES]
        q_maps.append((jnp.where(lane < DIFF_QK_DIM, q, jnp.zeros_like(q)),
                       jnp.where(lane >= DIFF_QK_DIM, q, jnp.zeros_like(q))))

    lv = lam_ref[...]
    lam = (jnp.exp(jnp.sum(lv[0:1] * lv[1:2], axis=-1, keepdims=True))
           - jnp.exp(jnp.sum(lv[2:3] * lv[3:4], axis=-1, keepdims=True)) + LAMBDA_INIT)

    def run(n_tiles):
        diag = (n_tiles - 1) * ATT_TK
        half = ATT_TK // 2
        blocks = [(j * ATT_TK, (j + 1) * ATT_TK, 0, False) for j in range(n_tiles - 1)]
        blocks += [(diag, diag + half, 0, True), (diag + half, diag + ATT_TK, half, True)]
        items = [(blk, g, c) for blk in blocks for g in range(heads) for c in range(2)]

        def scores(blk, g, c):
            k0, k1, q0, masked = blk
            kt = k_ref[k0:k1, g * LANES:(g + 1) * LANES]
            s = lax.dot_general(kt, q_maps[g][c][q0:, :], NT_DIMS, preferred_element_type=F32)
            if not masked:
                return s
            key = lax.broadcasted_iota(jnp.int32, s.shape, 0) + (k0 - diag)
            qry = lax.broadcasted_iota(jnp.int32, s.shape, 1) + q0
            return jnp.where(key <= qry, s, -jnp.inf)

        pending = {t: scores(*items[t]) for t in range(min(ATT_AHEAD, len(items)))}
        m, acc = {}, {}
        for t, (blk, g, c) in enumerate(items):
            k0, k1, q0, _ = blk
            i = 2 * g + c
            s = pending.pop(t)
            m_new = jnp.max(s, axis=0, keepdims=True)
            first = i not in m
            if not first:
                m_old = m[i][:, q0:]
                m_new = jnp.maximum(m_old, m_new)
            p = jnp.exp2(s - m_new).astype(BF16)
            if t + ATT_AHEAD < len(items):
                pending[t + ATT_AHEAD] = scores(*items[t + ATT_AHEAD])
            pv = jnp.dot(vt_ref[g, :, k0:k1], p, preferred_element_type=F32)
            if first:
                m[i], acc[i] = m_new, pv
            else:
                new_acc = jnp.exp2(m_old - m_new) * acc[i][:, q0:] + pv
                if q0:
                    m_new = jnp.concatenate([m[i][:, :q0], m_new], axis=1)
                    new_acc = jnp.concatenate([acc[i][:, :q0], new_acc], axis=1)
                m[i], acc[i] = m_new, new_acc

        for g in range(heads):
            c = slice(g * LANES, (g + 1) * LANES)
            a1, a2 = acc[2 * g], acc[2 * g + 1]
            l1 = a1[DIFF_V_DIM:DIFF_V_DIM + 1, :]
            l2 = a2[DIFF_V_DIM:DIFF_V_DIM + 1, :]
            o_t = (a1[0:DIFF_V_DIM, :] * (1.0 / l1)
                   - lam * (a2[0:DIFF_V_DIM, :] * (1.0 / l2)))
            o = o_t.T
            o = o * _rms_scale(o) * subw_ref[...] * (1.0 - LAMBDA_INIT)
            o_ref[:, c] = (o * _silu(g_ref[:, c].astype(F32))).astype(BF16)

    for n in range(seq // ATT_TQ):
        pl.when(qi == n)(functools.partial(run, n + 1))


def _diff_attention(proj, v_t, lam_vecs, subln_w, batch, seq):
    assert ATT_TQ == ATT_TK and seq % ATT_TQ == 0 and DIFF_V_DIM == LANES
    nq = seq // ATT_TQ
    heads = ATT_HEADS_PER_STEP
    width = heads * LANES
    kernel = functools.partial(_diff_attn_kernel, seq=seq)
    return pl.pallas_call(
        kernel,
        grid=(batch, DIFF_HEADS // heads, nq),
        in_specs=[
            pl.BlockSpec((4, DIFF_QK_DIM), lambda b, h, q: (0, 0)),
            pl.BlockSpec((ATT_TQ, width), lambda b, h, q: (b * nq + q, COL_Q // width + h)),
            pl.BlockSpec((seq, width), lambda b, h, q: (b, COL_K // width + h)),
            pl.BlockSpec((width, seq), lambda b, h, q: (h, b)),
            pl.BlockSpec((ATT_TQ, width), lambda b, h, q: (b * nq + q, COL_G // width + h)),
            pl.BlockSpec((1, DIFF_V_DIM), lambda b, h, q: (0, 0)),
        ],
        out_specs=pl.BlockSpec((ATT_TQ, width), lambda b, h, q: (b * nq + q, h)),
        out_shape=jax.ShapeDtypeStruct((batch * seq, DIFF_WIDTH), BF16),
        scratch_shapes=[pltpu.VMEM((heads, DIFF_V_DIM + ATT_SUM_ROWS, seq), BF16)],
        compiler_params=pltpu.CompilerParams(
            dimension_semantics=("arbitrary", "arbitrary", "arbitrary"),
            vmem_limit_bytes=48 * MIB),
        name="diff_attn",
    )(lam_vecs, proj, proj, v_t, proj, subln_w.reshape(1, DIFF_V_DIM))


def _ssd_kernel(xs_ref, b_ref, c_ref, z_ref, dt_ref, shift_ref, triu_ref, selp_ref, selh_ref,
                cw_ref, cb_ref, dtb_ref, alog_ref, dsk_ref, nw_ref, o_ref, xext_ref, state_ref):
    L = SSD_CHUNK
    T = SSD_CHUNKS_PER_STEP
    gs = SSD_GROUPS * SSD_STATE
    pairs = SSD_HEADS // 2

    @pl.when(pl.program_id(1) == 0)
    def _():
        xext_ref[0:L, :] = jnp.zeros((L, SSD_CONV_DIM), BF16)
        state_ref[...] = jnp.zeros(state_ref.shape, F32)

    xext_ref[L:(T + 1) * L, 0:SSD_WIDTH] = xs_ref[...]
    xext_ref[L:(T + 1) * L, SSD_WIDTH:SSD_WIDTH + gs] = b_ref[...]
    xext_ref[L:(T + 1) * L, SSD_WIDTH + gs:SSD_CONV_DIM] = c_ref[...]

    carry = {"state": [state_ref[p] for p in range(pairs)], "cols_split": [None] * T}
    programs = [_ssd_chunk(c, carry, xext_ref, z_ref, dt_ref, shift_ref, triu_ref, cw_ref,
                           cb_ref, dtb_ref, alog_ref, dsk_ref, nw_ref, o_ref)
                for c in range(T)]

    def stage():
        for prog in programs:
            next(prog)

    stage()
    stage()
    split = jnp.concatenate(carry["cols_split"], axis=0)
    carry["spread"] = jnp.concatenate(
        [jnp.dot(split[:, 0:2 * LANES], selp_ref[...], preferred_element_type=F32),
         jnp.dot(split, selh_ref[...], preferred_element_type=F32)], axis=1)
    stage()
    stage()
    stage()
    for p in range(pairs):
        state_ref[p] = carry["state"][p]
    xext_ref[0:L, :] = xext_ref[T * L:(T + 1) * L, :]


SSD_PAIR_KINDS = (0, 2)


def _ssd_chunk(c, carry, xext_ref, z_ref, dt_ref, shift_ref, triu_ref, cw_ref,
               cb_ref, dtb_ref, alog_ref, dsk_ref, nw_ref, o_ref):
    L = SSD_CHUNK
    H = SSD_HEADS
    gs = SSD_GROUPS * SSD_STATE
    rows_c = slice(c * L, (c + 1) * L)

    delayed = jnp.dot(shift_ref[...], xext_ref[c * L:(c + 2) * L, :],
                      preferred_element_type=F32)
    conv = (cb_ref[...] + xext_ref[(c + 1) * L:(c + 2) * L, :].astype(F32)
            * cw_ref[SSD_CONV - 1:SSD_CONV, :])
    for k in range(SSD_CONV - 1):
        conv = conv + delayed[k * L:(k + 1) * L, :] * cw_ref[k:k + 1, :]
    xbc = _silu(conv)
    xs = xbc[:, 0:SSD_WIDTH]
    bm = xbc[:, SSD_WIDTH:SSD_WIDTH + gs].astype(BF16)
    cm = xbc[:, SSD_WIDTH + gs:SSD_CONV_DIM].astype(BF16)
    yield

    dt_in = dt_ref[:, rows_c] + dtb_ref[...]
    dt = jnp.maximum(dt_in, 0.0) + jnp.log1p(jnp.exp(-jnp.abs(dt_in)))
    a_dt = dt * (-jnp.exp(alog_ref[...]))
    hi = a_dt.astype(BF16)
    r1 = a_dt - hi.astype(F32)
    mid = r1.astype(BF16)
    lo = (r1 - mid.astype(F32)).astype(BF16)
    triu = triu_ref[...]
    a_cs = (jnp.dot(hi, triu, preferred_element_type=F32)
            + jnp.dot(mid, triu, preferred_element_type=F32)
            + jnp.dot(lo, triu, preferred_element_type=F32))
    a_end = a_cs[:, L - 1:L]
    chunk_decay = jnp.exp(a_end)
    rows = jnp.concatenate([dt * jnp.exp(a_end - a_cs), a_cs, jnp.exp(a_cs),
                            jnp.zeros((L - 3 * H, L), F32)], axis=0)
    cols = rows.T
    c_hi = cols.astype(BF16)
    c_r1 = cols - c_hi.astype(F32)
    c_mid = c_r1.astype(BF16)
    c_lo = (c_r1 - c_mid.astype(F32)).astype(BF16)
    carry["cols_split"][c] = jnp.concatenate([c_hi, c_mid, c_lo], axis=1)
    yield
    spread = carry["spread"][rows_c, :]

    row = lax.broadcasted_iota(jnp.int32, (L, L), 0)
    col = lax.broadcasted_iota(jnp.int32, (L, L), 1)
    lower = row >= col
    lane = lax.broadcasted_iota(jnp.int32, (L, LANES), 1)
    left = lane < SSD_HEAD_DIM
    top = row < SSD_HEAD_DIM
    pairs = SSD_HEADS // 2

    def pair_lanes(slot, p):
        i = slot * pairs + p
        return spread[:, i * LANES:(i + 1) * LANES]

    def head_lanes(h):
        i = len(SSD_PAIR_KINDS) * pairs + h
        return spread[:, i * LANES:(i + 1) * LANES]

    group_cb = []
    for g in range(SSD_GROUPS):
        group_cb.append(lax.dot_general(cm[:, g * SSD_STATE:(g + 1) * SSD_STATE],
                                        bm[:, g * SSD_STATE:(g + 1) * SSD_STATE],
                                        NT_DIMS, preferred_element_type=F32))
    yield

    group_of = lambda p: (2 * p) // (SSD_HEADS // SSD_GROUPS)
    y_diag, chunk_state = [], []
    for p in range(pairs):
        bg = bm[:, group_of(p) * SSD_STATE:(group_of(p) + 1) * SSD_STATE]
        xs_p = xs[:, p * LANES:(p + 1) * LANES]
        xs_b = xs_p.astype(BF16)
        y_heads = []
        for hh in range(2):
            h = 2 * p + hh
            seg = head_lanes(h) - a_cs[h:h + 1, :]
            decay = jnp.exp(jnp.where(lower, seg, -jnp.inf))
            scores = (group_cb[group_of(p)] * decay * dt[h:h + 1, :]).astype(BF16)
            y_heads.append(jnp.dot(scores, xs_b, preferred_element_type=F32))
        y_diag.append(jnp.where(left, y_heads[0], y_heads[1]))
        x_w = (xs_p * pair_lanes(0, p)).astype(BF16)
        chunk_state.append(lax.dot_general(x_w, bg, TN_DIMS, preferred_element_type=F32))
    yield

    ys, new_state = [], []
    for p in range(pairs):
        cg = cm[:, group_of(p) * SSD_STATE:(group_of(p) + 1) * SSD_STATE]
        st = carry["state"][p]
        y_off = (lax.dot_general(cg, st.astype(BF16), NT_DIMS, preferred_element_type=F32)
                 * pair_lanes(1, p))
        st_decay = jnp.where(top, chunk_decay[2 * p:2 * p + 1, :],
                             chunk_decay[2 * p + 1:2 * p + 2, :])
        new_state.append(st * st_decay + chunk_state[p])
        y = (y_diag[p] + y_off
             + xs[:, p * LANES:(p + 1) * LANES] * dsk_ref[:, p * LANES:(p + 1) * LANES])
        ys.append(y * _silu(z_ref[rows_c, p * LANES:(p + 1) * LANES].astype(F32)))
    carry["state"] = new_state

    grp_pairs = pairs // SSD_GROUPS
    grp_width = SSD_WIDTH // SSD_GROUPS
    for g in range(SSD_GROUPS):
        members = ys[g * grp_pairs:(g + 1) * grp_pairs]
        ssq = sum(jnp.sum(y * y, axis=-1, keepdims=True) for y in members)
        inv = lax.rsqrt(ssq / grp_width + NORM_EPS)
        for i, y in enumerate(members):
            c0 = (g * grp_pairs + i) * LANES
            o_ref[rows_c, c0:c0 + LANES] = (y * inv * nw_ref[:, c0:c0 + LANES]).astype(BF16)
    yield


def _ssd(proj, dt_raw_t, conv_w, conv_b, dt_bias, a_log, d_skip, ssd_norm_w, batch, seq):
    gs = SSD_GROUPS * SSD_STATE
    L = SSD_CHUNK
    rows = SSD_CHUNKS_PER_STEP * L
    nc = seq // rows
    row_map = lambda b, c: b * nc + c
    const = lambda b, c: (0, 0)
    t = jnp.arange(L)[:, None]
    src = jnp.arange(2 * L)[None, :]
    shift = jnp.concatenate([(src == L + t - (SSD_CONV - 1 - k)) for k in range(SSD_CONV - 1)],
                            axis=0).astype(BF16)
    triu = (jnp.arange(L)[:, None] <= jnp.arange(L)[None, :]).astype(BF16)
    lane = jnp.arange(LANES)
    pair_blocks, head_blocks = [], []
    for kind in SSD_PAIR_KINDS:
        for p in range(SSD_HEADS // 2):
            src_col = kind * SSD_HEADS + 2 * p + (lane >= SSD_HEAD_DIM)
            pair_blocks.append(jnp.arange(L)[:, None] == src_col[None, :])
    for h in range(SSD_HEADS):
        head_blocks.append(jnp.broadcast_to(jnp.arange(L)[:, None] == SSD_HEADS + h, (L, LANES)))
    sel_pair = jnp.tile(jnp.concatenate(pair_blocks, axis=1), (2, 1)).astype(BF16)
    sel_head = jnp.tile(jnp.concatenate(head_blocks, axis=1), (3, 1)).astype(BF16)
    per_head = lambda v: jnp.broadcast_to(v.astype(F32)[:, None], (SSD_HEADS, L))
    return pl.pallas_call(
        _ssd_kernel,
        grid=(batch, nc),
        in_specs=[
            pl.BlockSpec((rows, SSD_WIDTH), lambda b, c: (row_map(b, c), COL_XS // SSD_WIDTH)),
            pl.BlockSpec((rows, gs), lambda b, c: (row_map(b, c), COL_B // gs)),
            pl.BlockSpec((rows, gs), lambda b, c: (row_map(b, c), COL_C // gs)),
            pl.BlockSpec((rows, SSD_WIDTH), lambda b, c: (row_map(b, c), COL_Z // SSD_WIDTH)),
            pl.BlockSpec((SSD_HEADS, rows), lambda b, c: (0, row_map(b, c))),
            pl.BlockSpec(((SSD_CONV - 1) * L, 2 * L), const),
            pl.BlockSpec((L, L), const),
            pl.BlockSpec(sel_pair.shape, const),
            pl.BlockSpec(sel_head.shape, const),
            pl.BlockSpec((SSD_CONV, SSD_CONV_DIM), const),
            pl.BlockSpec((1, SSD_CONV_DIM), const),
            pl.BlockSpec((SSD_HEADS, L), const),
            pl.BlockSpec((SSD_HEADS, L), const),
            pl.BlockSpec((1, SSD_WIDTH), const),
            pl.BlockSpec((1, SSD_WIDTH), const),
        ],
        out_specs=pl.BlockSpec((rows, SSD_WIDTH), lambda b, c: (row_map(b, c), 0)),
        out_shape=jax.ShapeDtypeStruct((batch * seq, SSD_WIDTH), BF16),
        scratch_shapes=[pltpu.VMEM((L + rows, SSD_CONV_DIM), BF16),
                        pltpu.VMEM((SSD_HEADS // 2, 2 * SSD_HEAD_DIM, SSD_STATE), F32)],
        compiler_params=pltpu.CompilerParams(
            dimension_semantics=("arbitrary", "arbitrary"),
            vmem_limit_bytes=32 * MIB),
        name="ssd",
    )(proj, proj, proj, proj, dt_raw_t, shift, triu, sel_pair, sel_head, conv_w, conv_b.reshape(1, SSD_CONV_DIM),
      per_head(dt_bias), per_head(a_log),
      jnp.repeat(d_skip, SSD_HEAD_DIM).reshape(1, SSD_WIDTH), ssd_norm_w.reshape(1, SSD_WIDTH))


XATT_V_ROWS = XATTN_HEAD_DIM + ATT_SUM_ROWS


def _mem_kv_kernel(mem_ref, nw_ref, w_ref, k_ref, vt_ref, wb_ref):
    @pl.when(pl.program_id(0) == 0)
    def _():
        wb_ref[...] = w_ref[...].astype(BF16)

    x = mem_ref[...]
    h = (x * _rms_scale(x) * nw_ref[...]).astype(BF16)
    kv = jnp.dot(h, wb_ref[...], preferred_element_type=F32)
    k_ref[...] = kv[:, 0:XATTN_WIDTH].astype(BF16)
    ones = jnp.ones((ATT_SUM_ROWS, x.shape[0]), BF16)
    for hd in range(XATTN_HEADS):
        c0 = XATTN_WIDTH + hd * XATTN_HEAD_DIM
        r0 = hd * XATT_V_ROWS
        vt_ref[r0:r0 + XATTN_HEAD_DIM, :] = kv[:, c0:c0 + XATTN_HEAD_DIM].T.astype(BF16)
        vt_ref[r0 + XATTN_HEAD_DIM:r0 + XATT_V_ROWS, :] = ones


def _mem_kv(mem2, mem_norm_w, w_kv, batch, mem_len):
    return pl.pallas_call(
        _mem_kv_kernel,
        grid=(batch,),
        in_specs=[pl.BlockSpec((mem_len, D_MODEL), lambda b: (b, 0)),
                  pl.BlockSpec((1, D_MODEL), lambda b: (0, 0)),
                  pl.BlockSpec((D_MODEL, 2 * XATTN_WIDTH), lambda b: (0, 0),
                               pipeline_mode=pl.Buffered(1))],
        out_specs=[pl.BlockSpec((mem_len, XATTN_WIDTH), lambda b: (b, 0)),
                   pl.BlockSpec((XATTN_HEADS * XATT_V_ROWS, mem_len), lambda b: (b, 0))],
        out_shape=[jax.ShapeDtypeStruct((batch * mem_len, XATTN_WIDTH), BF16),
                   jax.ShapeDtypeStruct((batch * XATTN_HEADS * XATT_V_ROWS, mem_len), BF16)],
        scratch_shapes=[pltpu.VMEM((D_MODEL, 2 * XATTN_WIDTH), BF16)],
        compiler_params=pltpu.CompilerParams(
            dimension_semantics=("arbitrary",), vmem_limit_bytes=32 * MIB),
        name="mem_kv",
    )(mem2, mem_norm_w.reshape(1, D_MODEL), w_kv)


def _xattn_kernel(q_ref, g_ref, k_ref, vt_ref, o_ref):
    cols = [slice(h * XATTN_HEAD_DIM, (h + 1) * XATTN_HEAD_DIM) for h in range(XATTN_HEADS)]
    scores = [lax.dot_general(k_ref[:, c], q_ref[:, c], NT_DIMS, preferred_element_type=F32)
              for c in cols]
    for h, c in enumerate(cols):
        s = scores[h]
        p = jnp.exp2(s - jnp.max(s, axis=0, keepdims=True)).astype(BF16)
        pv = jnp.dot(vt_ref[h * XATT_V_ROWS:(h + 1) * XATT_V_ROWS, :], p,
                     preferred_element_type=F32)
        xo = (pv[0:XATTN_HEAD_DIM, :]
              * (1.0 / pv[XATTN_HEAD_DIM:XATTN_HEAD_DIM + 1, :])).T
        o_ref[:, c] = (xo * _silu(g_ref[:, c].astype(F32))).astype(BF16)


def _xattn(proj, mk, mvt, batch, seq, mem_len):
    nq = seq // XATT_TQ
    return pl.pallas_call(
        _xattn_kernel,
        grid=(batch, nq),
        in_specs=[pl.BlockSpec((XATT_TQ, XATTN_WIDTH), lambda b, q: (b * nq + q, COL_XQ // XATTN_WIDTH)),
                  pl.BlockSpec((XATT_TQ, XATTN_WIDTH), lambda b, q: (b * nq + q, COL_XG // XATTN_WIDTH)),
                  pl.BlockSpec((mem_len, XATTN_WIDTH), lambda b, q: (b, 0)),
                  pl.BlockSpec((XATTN_HEADS * XATT_V_ROWS, mem_len), lambda b, q: (b, 0))],
        out_specs=pl.BlockSpec((XATT_TQ, XATTN_WIDTH), lambda b, q: (b * nq + q, 0)),
        out_shape=jax.ShapeDtypeStruct((batch * seq, XATTN_WIDTH), BF16),
        compiler_params=pltpu.CompilerParams(
            dimension_semantics=("arbitrary", "arbitrary"), vmem_limit_bytes=32 * MIB),
        name="xattn",
    )(proj, proj, mk, mvt)


def _out_proj_kernel(d_ref, s_ref, a_ref, w_ref, nw_ref, x_ref, o_ref, wb_ref):
    @pl.when(pl.program_id(0) == 0)
    def _():
        for k0 in range(0, D_MIX, OUT_SUB):
            wb_ref[k0:k0 + OUT_SUB, :] = w_ref[k0:k0 + OUT_SUB, :].astype(BF16)

    for i in range(OUT_TM // OUT_SUB):
        r = slice(i * OUT_SUB, (i + 1) * OUT_SUB)
        y = jnp.dot(d_ref[r, :], wb_ref[0:DIFF_WIDTH, :], preferred_element_type=F32)
        y = y + jnp.dot(s_ref[r, :], wb_ref[DIFF_WIDTH:DIFF_WIDTH + SSD_WIDTH, :],
                        preferred_element_type=F32)
        y = y + jnp.dot(a_ref[r, :], wb_ref[DIFF_WIDTH + SSD_WIDTH:D_MIX, :],
                        preferred_element_type=F32)
        o_ref[r, :] = x_ref[r, :] + y * _rms_scale(y) * nw_ref[...]


def _out_proj(diff_out, ssd_out, xattn_out, w_out, post_norm_w, x2):
    m = x2.shape[0]
    return pl.pallas_call(
        _out_proj_kernel,
        grid=(m // OUT_TM,),
        in_specs=[pl.BlockSpec((OUT_TM, DIFF_WIDTH), lambda i: (i, 0)),
                  pl.BlockSpec((OUT_TM, SSD_WIDTH), lambda i: (i, 0)),
                  pl.BlockSpec((OUT_TM, XATTN_WIDTH), lambda i: (i, 0)),
                  pl.BlockSpec((D_MIX, D_MODEL), lambda i: (0, 0), pipeline_mode=pl.Buffered(1)),
                  pl.BlockSpec((1, D_MODEL), lambda i: (0, 0)),
                  pl.BlockSpec((OUT_TM, D_MODEL), lambda i: (i, 0))],
        out_specs=pl.BlockSpec((OUT_TM, D_MODEL), lambda i: (i, 0)),
        out_shape=jax.ShapeDtypeStruct((m, D_MODEL), F32),
        scratch_shapes=[pltpu.VMEM((D_MIX, D_MODEL), BF16)],
        compiler_params=pltpu.CompilerParams(
            dimension_semantics=("arbitrary",), vmem_limit_bytes=48 * MIB),
        name="out_proj",
    )(diff_out, ssd_out, xattn_out, w_out, post_norm_w.reshape(1, D_MODEL), x2)


def kernel(x, mem, positions, pre_norm_w, w_in, lambda_q1, lambda_k1, lambda_q2, lambda_k2,
           diff_subln_w, conv_w, conv_b, dt_bias, a_log, d_skip, ssd_norm_w, mem_norm_w,
           w_mem_kv, w_out, post_norm_w):
    batch, seq, _ = x.shape
    mem_len = mem.shape[1]
    assert pre_norm_w.shape[0] == 1, "single-layer kernel"
    x2 = x.reshape(batch * seq, D_MODEL)

    w_main, w_v, w_dt, cos, sin = _w_in_prep(jnp.transpose(w_in[0]), pre_norm_w[0], positions)
    lam_vecs = jnp.stack([lambda_q1[0], lambda_k1[0], lambda_q2[0], lambda_k2[0]]).astype(F32)

    proj, v_t, dt_raw_t = _in_proj(x2, w_main, w_v, w_dt, cos, sin)
    diff_out = _diff_attention(proj, v_t, lam_vecs, diff_subln_w[0], batch, seq)
    ssd_out = _ssd(proj, dt_raw_t, conv_w[0], conv_b[0], dt_bias[0], a_log[0], d_skip[0],
                   ssd_norm_w[0], batch, seq)
    mk, mvt = _mem_kv(mem.reshape(batch * mem_len, D_MODEL), mem_norm_w[0], w_mem_kv[0],
                      batch, mem_len)
    xattn_out = _xattn(proj, mk, mvt, batch, seq, mem_len)
    out = _out_proj(diff_out, ssd_out, xattn_out, w_out[0], post_norm_w[0], x2)
    return out.reshape(batch, seq, D_MODEL)
```

```python
import functools
import math

import jax
import jax.numpy as jnp
from jax import lax
from jax.experimental import pallas as pl
from jax.experimental.pallas import tpu as pltpu

F32 = jnp.float32
BF16 = jnp.bfloat16

D_MODEL = 2048
DIFF_HEADS = 8
DIFF_QK_DIM = 64
DIFF_V_DIM = 128
DIFF_WIDTH = DIFF_HEADS * DIFF_V_DIM
SSD_HEADS = 8
SSD_HEAD_DIM = 64
SSD_WIDTH = SSD_HEADS * SSD_HEAD_DIM
SSD_GROUPS = 2
SSD_STATE = 128
SSD_CONV = 4
SSD_CHUNK = 128
SSD_CONV_DIM = SSD_WIDTH + 2 * SSD_GROUPS * SSD_STATE
XATTN_HEADS = 4
XATTN_HEAD_DIM = 128
XATTN_WIDTH = XATTN_HEADS * XATTN_HEAD_DIM
D_MIX = DIFF_WIDTH + SSD_WIDTH + XATTN_WIDTH
ROPE_THETA = 10000.0
NORM_EPS = 1e-6
LAMBDA_INIT = 0.8 - 0.6 * math.exp(-0.3 * 0)

LANES = 128
SUBLANES = 8
MIB = 1024 * 1024

COL_Q = 0
COL_K = COL_Q + DIFF_WIDTH
COL_G = COL_K + DIFF_WIDTH
COL_Z = COL_G + DIFF_WIDTH
COL_XS = COL_Z + SSD_WIDTH
COL_B = COL_XS + SSD_WIDTH
COL_C = COL_B + SSD_GROUPS * SSD_STATE
COL_XQ = COL_C + SSD_GROUPS * SSD_STATE
COL_XG = COL_XQ + XATTN_WIDTH
N_MAIN = COL_XG + XATTN_WIDTH
REF_V = 2 * DIFF_WIDTH
REF_G = REF_V + DIFF_WIDTH
REF_DT = REF_G + DIFF_WIDTH + SSD_WIDTH + SSD_CONV_DIM
REF_XQ = REF_DT + SSD_HEADS
D_IN = REF_XQ + 2 * XATTN_WIDTH

PROJ_TM = 512
PROJ_TN = 512
NORM_ROWS = 256
ATT_TQ = 512
ATT_TK = 512
ATT_HEADS_PER_STEP = 4
ATT_SUM_ROWS = 16
ATT_AHEAD = 2
XATT_TQ = 1024
OUT_TM = 512
OUT_SUB = 256
ROPE_TM = 512
SSD_CHUNKS_PER_STEP = 8
DT_ROWS = 16

NT_DIMS = (((1,), (1,)), ((), ()))
TN_DIMS = (((0,), (0,)), ((), ()))


def _rms_scale(x):
    return lax.rsqrt(jnp.mean(x * x, axis=-1, keepdims=True) + NORM_EPS)


def _silu(x):
    return x * jax.nn.sigmoid(x)


def _rope_table_rows(pos_ref, invf_ref, cos_ref, sin_ref):
    ang = pos_ref[...].astype(F32) * invf_ref[...]
    lane = lax.broadcasted_iota(jnp.int32, ang.shape, 1)
    first_half = (lane % DIFF_QK_DIM) < (DIFF_QK_DIM // 2)
    s = jnp.sin(ang)
    cos_ref[...] = jnp.cos(ang)
    sin_ref[...] = jnp.where(first_half, -s, s)


def _rope(x, cos, sin_signed):
    lane = lax.broadcasted_iota(jnp.int32, x.shape, 1)
    first_half = (lane % DIFF_QK_DIM) < (DIFF_QK_DIM // 2)
    half = DIFF_QK_DIM // 2
    rot = jnp.where(first_half, pltpu.roll(x, LANES - half, 1), pltpu.roll(x, half, 1))
    return x * cos + rot * sin_signed


def _w_in_prep_kernel(wt_ref, dt_rows_ref, nw_ref, pos_ref, invf_ref,
                      wm_ref, wv_ref, wdt_ref, cos_ref, sin_ref):
    j = pl.program_id(0)
    main_blocks = N_MAIN // PROJ_TN
    last_w = main_blocks + DIFF_WIDTH // PROJ_TN
    rope = functools.partial(_rope_table_rows, pos_ref, invf_ref, cos_ref, sin_ref)

    @pl.when(j < main_blocks)
    def _():
        rope()
        wm_ref[0] = (wt_ref[...] * nw_ref[...]).T.astype(BF16)

    @pl.when((j >= main_blocks) & (j < last_w))
    def _():
        rope()
        wv_ref[...] = (wt_ref[...] * nw_ref[...]).astype(BF16)

    pl.when(j >= last_w)(rope)

    @pl.when(j == 0)
    def _():
        row = lax.broadcasted_iota(jnp.int32, (DT_ROWS, D_MODEL), 0)
        wdt_ref[...] = jnp.where(row < SSD_HEADS, dt_rows_ref[...] * nw_ref[...],
                                 0.0).astype(BF16)


def _w_in_prep(w_t, pre_norm_w, positions):
    n, k = w_t.shape
    assert n == D_IN and k == D_MODEL
    main_blocks = N_MAIN // PROJ_TN
    v_blocks = DIFF_WIDTH // PROJ_TN
    last_w = main_blocks + v_blocks - 1
    n_pos = positions.size
    steps = n_pos // ROPE_TM
    assert steps > last_w
    inv_freq = 1.0 / (ROPE_THETA ** (jnp.arange(0, DIFF_QK_DIM, 2, dtype=F32) / DIFF_QK_DIM))
    invf = jnp.tile(inv_freq, LANES // inv_freq.shape[0]).reshape(1, LANES)

    def src_row(j):
        main = jnp.where(j < REF_V // PROJ_TN, j * PROJ_TN,
                         jnp.where(j < (REF_DT - DIFF_WIDTH) // PROJ_TN,
                                   j * PROJ_TN + DIFF_WIDTH,
                                   j * PROJ_TN + DIFF_WIDTH + SSD_HEADS))
        jw = jnp.minimum(j, last_w)
        return pl.multiple_of(jnp.where(jw < main_blocks, main,
                                        REF_V + (jw - main_blocks) * PROJ_TN), SUBLANES)

    return pl.pallas_call(
        _w_in_prep_kernel,
        grid=(steps,),
        in_specs=[pl.BlockSpec((pl.Element(PROJ_TN), pl.Element(D_MODEL)),
                               lambda j: (src_row(j), 0)),
                  pl.BlockSpec((DT_ROWS, D_MODEL), lambda j: (REF_DT // DT_ROWS, 0)),
                  pl.BlockSpec((1, D_MODEL), lambda j: (0, 0)),
                  pl.BlockSpec((ROPE_TM, 1), lambda j: (j, 0)),
                  pl.BlockSpec((1, LANES), lambda j: (0, 0))],
        out_specs=[pl.BlockSpec((1, D_MODEL, PROJ_TN),
                                lambda j: (jnp.minimum(j, main_blocks - 1), 0, 0)),
                   pl.BlockSpec((PROJ_TN, D_MODEL),
                                lambda j: (jnp.clip(j - main_blocks, 0, v_blocks - 1), 0)),
                   pl.BlockSpec((DT_ROWS, D_MODEL), lambda j: (0, 0)),
                   pl.BlockSpec((ROPE_TM, LANES), lambda j: (j, 0)),
                   pl.BlockSpec((ROPE_TM, LANES), lambda j: (j, 0))],
        out_shape=[jax.ShapeDtypeStruct((main_blocks, k, PROJ_TN), BF16),
                   jax.ShapeDtypeStruct((DIFF_WIDTH, k), BF16),
                   jax.ShapeDtypeStruct((DT_ROWS, k), BF16),
                   jax.ShapeDtypeStruct((n_pos, LANES), F32),
                   jax.ShapeDtypeStruct((n_pos, LANES), F32)],
        compiler_params=pltpu.CompilerParams(
            dimension_semantics=("arbitrary",), vmem_limit_bytes=32 * MIB),
        name="w_in_prep",
    )(w_t, w_t, pre_norm_w.reshape(1, D_MODEL), positions.reshape(n_pos, 1), invf)


def _in_proj_kernel(x_ref, w_ref, wv_ref, wdt_ref, cos_ref, sin_ref,
                    o_ref, vt_ref, dt_ref, h_ref):
    mean_sq = []
    for r0 in range(0, PROJ_TM, NORM_ROWS):
        x = x_ref[r0:r0 + NORM_ROWS, :]
        h_ref[r0:r0 + NORM_ROWS, :] = x.astype(BF16)
        mean_sq.append(jnp.mean(x * x, axis=-1, keepdims=True))
    rs = lax.rsqrt(jnp.concatenate(mean_sq, axis=0) + NORM_EPS)
    q_scale = DIFF_QK_DIM ** -0.5 * math.log2(math.e)
    xq_scale = XATTN_HEAD_DIM ** -0.5 * math.log2(math.e)
    for n in range(N_MAIN // PROJ_TN):
        c0 = n * PROJ_TN
        y = jnp.dot(h_ref[...], w_ref[n], preferred_element_type=F32) * rs
        if c0 < COL_G:
            scale = q_scale if c0 < COL_K else 1.0
            for hd in range(PROJ_TN // LANES):
                y_h = _rope(y[:, hd * LANES:(hd + 1) * LANES], cos_ref[...], sin_ref[...])
                o_ref[:, c0 + hd * LANES:c0 + (hd + 1) * LANES] = (y_h * scale).astype(BF16)
        elif COL_XQ <= c0 < COL_XG:
            o_ref[:, c0:c0 + PROJ_TN] = (y * xq_scale).astype(BF16)
        else:
            o_ref[:, c0:c0 + PROJ_TN] = y.astype(BF16)
    rs_row = jnp.broadcast_to(rs, (PROJ_TM, LANES)).T[0:1, :]
    vt_ref[...] = (lax.dot_general(wv_ref[...], h_ref[...], NT_DIMS,
                                   preferred_element_type=F32) * rs_row).astype(BF16)
    dt_t = lax.dot_general(wdt_ref[...], h_ref[...], NT_DIMS, preferred_element_type=F32)
    dt_ref[...] = dt_t[0:SSD_HEADS, :] * rs_row


def _in_proj(x2, w_main, w_v, w_dt, cos, sin):
    m = x2.shape[0]
    resident = pl.Buffered(1)
    return pl.pallas_call(
        _in_proj_kernel,
        grid=(m // PROJ_TM,),
        in_specs=[pl.BlockSpec((PROJ_TM, D_MODEL), lambda i: (i, 0)),
                  pl.BlockSpec((N_MAIN // PROJ_TN, D_MODEL, PROJ_TN), lambda i: (0, 0, 0),
                               pipeline_mode=resident),
                  pl.BlockSpec((DIFF_WIDTH, D_MODEL), lambda i: (0, 0), pipeline_mode=resident),
                  pl.BlockSpec((DT_ROWS, D_MODEL), lambda i: (0, 0), pipeline_mode=resident),
                  pl.BlockSpec((PROJ_TM, LANES), lambda i: (i, 0)),
                  pl.BlockSpec((PROJ_TM, LANES), lambda i: (i, 0))],
        out_specs=[pl.BlockSpec((PROJ_TM, N_MAIN), lambda i: (i, 0)),
                   pl.BlockSpec((DIFF_WIDTH, PROJ_TM), lambda i: (0, i)),
                   pl.BlockSpec((SSD_HEADS, PROJ_TM), lambda i: (0, i))],
        out_shape=[jax.ShapeDtypeStruct((m, N_MAIN), BF16),
                   jax.ShapeDtypeStruct((DIFF_WIDTH, m), BF16),
                   jax.ShapeDtypeStruct((SSD_HEADS, m), F32)],
        scratch_shapes=[pltpu.VMEM((PROJ_TM, D_MODEL), BF16)],
        compiler_params=pltpu.CompilerParams(
            dimension_semantics=("arbitrary",),
            vmem_limit_bytes=56 * MIB),
        name="in_proj",
    )(x2, w_main, w_v, w_dt, cos, sin)


def _diff_attn_kernel(lam_ref, q_ref, k_ref, v_ref, g_ref, subw_ref, o_ref, vt_ref, *, seq):
    qi = pl.program_id(2)
    heads = ATT_HEADS_PER_STEP

    @pl.when(qi == 0)
    def _():
        for g in range(heads):
            vt_ref[g, 0:DIFF_V_DIM, :] = v_ref[g * DIFF_V_DIM:(g + 1) * DIFF_V_DIM, :]
        vt_ref[:, DIFF_V_DIM:, :] = jnp.ones((heads, ATT_SUM_ROWS, seq), BF16)

    lane = lax.broadcasted_iota(jnp.int32, (ATT_TQ, LANES), 1)
    q_maps = []
    for g in range(heads):
        q = q_ref[:, g * LANES:(g + 1) * LANES]
        q_maps.append((jnp.where(lane < DIFF_QK_DIM, q, jnp.zeros_like(q)),
                       jnp.where(lane >= DIFF_QK_DIM, q, jnp.zeros_like(q))))

    lv = lam_ref[...]
    lam = (jnp.exp(jnp.sum(lv[0:1] * lv[1:2], axis=-1, keepdims=True))
           - jnp.exp(jnp.sum(lv[2:3] * lv[3:4], axis=-1, keepdims=True)) + LAMBDA_INIT)

    def run(n_tiles):
        diag = (n_tiles - 1) * ATT_TK
        half = ATT_TK // 2
        blocks = [(j * ATT_TK, (j + 1) * ATT_TK, 0, False) for j in range(n_tiles - 1)]
        blocks += [(diag, diag + half, 0, True), (diag + half, diag + ATT_TK, half, True)]
        items = [(blk, g, c) for blk in blocks for g in range(heads) for c in range(2)]

        def scores(blk, g, c):
            k0, k1, q0, masked = blk
            kt = k_ref[k0:k1, g * LANES:(g + 1) * LANES]
            s = lax.dot_general(kt, q_maps[g][c][q0:, :], NT_DIMS, preferred_element_type=F32)
            if not masked:
                return s
            key = lax.broadcasted_iota(jnp.int32, s.shape, 0) + (k0 - diag)
            qry = lax.broadcasted_iota(jnp.int32, s.shape, 1) + q0
            return jnp.where(key <= qry, s, -jnp.inf)

        pending = {t: scores(*items[t]) for t in range(min(ATT_AHEAD, len(items)))}
        m, acc = {}, {}
        for t, (blk, g, c) in enumerate(items):
            k0, k1, q0, _ = blk
            i = 2 * g + c
            s = pending.pop(t)
            m_new = jnp.max(s, axis=0, keepdims=True)
            first = i not in m
            if not first:
                m_old = m[i][:, q0:]
                m_new = jnp.maximum(m_old, m_new)
            p = jnp.exp2(s - m_new).astype(BF16)
            if t + ATT_AHEAD < len(items):
                pending[t + ATT_AHEAD] = scores(*items[t + ATT_AHEAD])
            pv = jnp.dot(vt_ref[g, :, k0:k1], p, preferred_element_type=F32)
            if first:
                m[i], acc[i] = m_new, pv
            else:
                new_acc = jnp.exp2(m_old - m_new) * acc[i][:, q0:] + pv
                if q0:
                    m_new = jnp.concatenate([m[i][:, :q0], m_new], axis=1)
                    new_acc = jnp.concatenate([acc[i][:, :q0], new_acc], axis=1)
                m[i], acc[i] = m_new, new_acc

        for g in range(heads):
            c = slice(g * LANES, (g + 1) * LANES)
            a1, a2 = acc[2 * g], acc[2 * g + 1]
            l1 = a1[DIFF_V_DIM:DIFF_V_DIM + 1, :]
            l2 = a2[DIFF_V_DIM:DIFF_V_DIM + 1, :]
            o_t = (a1[0:DIFF_V_DIM, :] * (1.0 / l1)
                   - lam * (a2[0:DIFF_V_DIM, :] * (1.0 / l2)))
            o = o_t.T
            o = o * _rms_scale(o) * subw_ref[...] * (1.0 - LAMBDA_INIT)
            o_ref[:, c] = (o * _silu(g_ref[:, c].astype(F32))).astype(BF16)

    for n in range(seq // ATT_TQ):
        pl.when(qi == n)(functools.partial(run, n + 1))


def _diff_attention(proj, v_t, lam_vecs, subln_w, batch, seq):
    assert ATT_TQ == ATT_TK and seq % ATT_TQ == 0 and DIFF_V_DIM == LANES
    nq = seq // ATT_TQ
    heads = ATT_HEADS_PER_STEP
    width = heads * LANES
    kernel = functools.partial(_diff_attn_kernel, seq=seq)
    return pl.pallas_call(
        kernel,
        grid=(batch, DIFF_HEADS // heads, nq),
        in_specs=[
            pl.BlockSpec((4, DIFF_QK_DIM), lambda b, h, q: (0, 0)),
            pl.BlockSpec((ATT_TQ, width), lambda b, h, q: (b * nq + q, COL_Q // width + h)),
            pl.BlockSpec((seq, width), lambda b, h, q: (b, COL_K // width + h)),
            pl.BlockSpec((width, seq), lambda b, h, q: (h, b)),
            pl.BlockSpec((ATT_TQ, width), lambda b, h, q: (b * nq + q, COL_G // width + h)),
            pl.BlockSpec((1, DIFF_V_DIM), lambda b, h, q: (0, 0)),
        ],
        out_specs=pl.BlockSpec((ATT_TQ, width), lambda b, h, q: (b * nq + q, h)),
        out_shape=jax.ShapeDtypeStruct((batch * seq, DIFF_WIDTH), BF16),
        scratch_shapes=[pltpu.VMEM((heads, DIFF_V_DIM + ATT_SUM_ROWS, seq), BF16)],
        compiler_params=pltpu.CompilerParams(
            dimension_semantics=("arbitrary", "arbitrary", "arbitrary"),
            vmem_limit_bytes=48 * MIB),
        name="diff_attn",
    )(lam_vecs, proj, proj, v_t, proj, subln_w.reshape(1, DIFF_V_DIM))


def _ssd_kernel(xs_ref, b_ref, c_ref, z_ref, dt_ref, shift_ref, triu_ref, selp_ref, selh_ref,
                cw_ref, cb_ref, dtb_ref, alog_ref, dsk_ref, nw_ref, o_ref, xext_ref, state_ref):
    L = SSD_CHUNK
    T = SSD_CHUNKS_PER_STEP
    gs = SSD_GROUPS * SSD_STATE
    pairs = SSD_HEADS // 2

    @pl.when(pl.program_id(1) == 0)
    def _():
        xext_ref[0:L, :] = jnp.zeros((L, SSD_CONV_DIM), BF16)
        state_ref[...] = jnp.zeros(state_ref.shape, F32)

    xext_ref[L:(T + 1) * L, 0:SSD_WIDTH] = xs_ref[...]
    xext_ref[L:(T + 1) * L, SSD_WIDTH:SSD_WIDTH + gs] = b_ref[...]
    xext_ref[L:(T + 1) * L, SSD_WIDTH + gs:SSD_CONV_DIM] = c_ref[...]

    carry = {"state": [state_ref[p] for p in range(pairs)], "cols_split": [None] * T}
    programs = [_ssd_chunk(c, carry, xext_ref, z_ref, dt_ref, shift_ref, triu_ref, cw_ref,
                           cb_ref, dtb_ref, alog_ref, dsk_ref, nw_ref, o_ref)
                for c in range(T)]

    def stage():
        for prog in programs:
            next(prog)

    stage()
    stage()
    split = jnp.concatenate(carry["cols_split"], axis=0)
    carry["spread"] = jnp.concatenate(
        [jnp.dot(split[:, 0:2 * LANES], selp_ref[...], preferred_element_type=F32),
         jnp.dot(split, selh_ref[...], preferred_element_type=F32)], axis=1)
    stage()
    stage()
    stage()
    for p in range(pairs):
        state_ref[p] = carry["state"][p]
    xext_ref[0:L, :] = xext_ref[T * L:(T + 1) * L, :]


SSD_PAIR_KINDS = (0, 2)


def _ssd_chunk(c, carry, xext_ref, z_ref, dt_ref, shift_ref, triu_ref, cw_ref,
               cb_ref, dtb_ref, alog_ref, dsk_ref, nw_ref, o_ref):
    L = SSD_CHUNK
    H = SSD_HEADS
    gs = SSD_GROUPS * SSD_STATE
    rows_c = slice(c * L, (c + 1) * L)

    delayed = jnp.dot(shift_ref[...], xext_ref[c * L:(c + 2) * L, :],
                      preferred_element_type=F32)
    conv = (cb_ref[...] + xext_ref[(c + 1) * L:(c + 2) * L, :].astype(F32)
            * cw_ref[SSD_CONV - 1:SSD_CONV, :])
    for k in range(SSD_CONV - 1):
        conv = conv + delayed[k * L:(k + 1) * L, :] * cw_ref[k:k + 1, :]
    xbc = _silu(conv)
    xs = xbc[:, 0:SSD_WIDTH]
    bm = xbc[:, SSD_WIDTH:SSD_WIDTH + gs].astype(BF16)
    cm = xbc[:, SSD_WIDTH + gs:SSD_CONV_DIM].astype(BF16)
    yield

    dt_in = dt_ref[:, rows_c] + dtb_ref[...]
    dt = jnp.maximum(dt_in, 0.0) + jnp.log1p(jnp.exp(-jnp.abs(dt_in)))
    a_dt = dt * (-jnp.exp(alog_ref[...]))
    hi = a_dt.astype(BF16)
    r1 = a_dt - hi.astype(F32)
    mid = r1.astype(BF16)
    lo = (r1 - mid.astype(F32)).astype(BF16)
    triu = triu_ref[...]
    a_cs = (jnp.dot(hi, triu, preferred_element_type=F32)
            + jnp.dot(mid, triu, preferred_element_type=F32)
            + jnp.dot(lo, triu, preferred_element_type=F32))
    a_end = a_cs[:, L - 1:L]
    chunk_decay = jnp.exp(a_end)
    rows = jnp.concatenate([dt * jnp.exp(a_end - a_cs), a_cs, jnp.exp(a_cs),
                            jnp.zeros((L - 3 * H, L), F32)], axis=0)
    cols = rows.T
    c_hi = cols.astype(BF16)
    c_r1 = cols - c_hi.astype(F32)
    c_mid = c_r1.astype(BF16)
    c_lo = (c_r1 - c_mid.astype(F32)).astype(BF16)
    carry["cols_split"][c] = jnp.concatenate([c_hi, c_mid, c_lo], axis=1)
    yield
    spread = carry["spread"][rows_c, :]

    row = lax.broadcasted_iota(jnp.int32, (L, L), 0)
    col = lax.broadcasted_iota(jnp.int32, (L, L), 1)
    lower = row >= col
    lane = lax.broadcasted_iota(jnp.int32, (L, LANES), 1)
    left = lane < SSD_HEAD_DIM
    top = row < SSD_HEAD_DIM
    pairs = SSD_HEADS // 2

    def pair_lanes(slot, p):
        i = slot * pairs + p
        return spread[:, i * LANES:(i + 1) * LANES]

    def head_lanes(h):
        i = len(SSD_PAIR_KINDS) * pairs + h
        return spread[:, i * LANES:(i + 1) * LANES]

    group_cb = []
    for g in range(SSD_GROUPS):
        group_cb.append(lax.dot_general(cm[:, g * SSD_STATE:(g + 1) * SSD_STATE],
                                        bm[:, g * SSD_STATE:(g + 1) * SSD_STATE],
                                        NT_DIMS, preferred_element_type=F32))
    yield

    group_of = lambda p: (2 * p) // (SSD_HEADS // SSD_GROUPS)
    y_diag, chunk_state = [], []
    for p in range(pairs):
        bg = bm[:, group_of(p) * SSD_STATE:(group_of(p) + 1) * SSD_STATE]
        xs_p = xs[:, p * LANES:(p + 1) * LANES]
        xs_b = xs_p.astype(BF16)
        y_heads = []
        for hh in range(2):
            h = 2 * p + hh
            seg = head_lanes(h) - a_cs[h:h + 1, :]
            decay = jnp.exp(jnp.where(lower, seg, -jnp.inf))
            scores = (group_cb[group_of(p)] * decay * dt[h:h + 1, :]).astype(BF16)
            y_heads.append(jnp.dot(scores, xs_b, preferred_element_type=F32))
        y_diag.append(jnp.where(left, y_heads[0], y_heads[1]))
        x_w = (xs_p * pair_lanes(0, p)).astype(BF16)
        chunk_state.append(lax.dot_general(x_w, bg, TN_DIMS, preferred_element_type=F32))
    yield

    ys, new_state = [], []
    for p in range(pairs):
        cg = cm[:, group_of(p) * SSD_STATE:(group_of(p) + 1) * SSD_STATE]
        st = carry["state"][p]
        y_off = (lax.dot_general(cg, st.astype(BF16), NT_DIMS, preferred_element_type=F32)
                 * pair_lanes(1, p))
        st_decay = jnp.where(top, chunk_decay[2 * p:2 * p + 1, :],
                             chunk_decay[2 * p + 1:2 * p + 2, :])
        new_state.append(st * st_decay + chunk_state[p])
        y = (y_diag[p] + y_off
             + xs[:, p * LANES:(p + 1) * LANES] * dsk_ref[:, p * LANES:(p + 1) * LANES])
        ys.append(y * _silu(z_ref[rows_c, p * LANES:(p + 1) * LANES].astype(F32)))
    carry["state"] = new_state

    grp_pairs = pairs // SSD_GROUPS
    grp_width = SSD_WIDTH // SSD_GROUPS
    for g in range(SSD_GROUPS):
        members = ys[g * grp_pairs:(g + 1) * grp_pairs]
        ssq = sum(jnp.sum(y * y, axis=-1, keepdims=True) for y in members)
        inv = lax.rsqrt(ssq / grp_width + NORM_EPS)
        for i, y in enumerate(members):
            c0 = (g * grp_pairs + i) * LANES
            o_ref[rows_c, c0:c0 + LANES] = (y * inv * nw_ref[:, c0:c0 + LANES]).astype(BF16)
    yield


def _ssd(proj, dt_raw_t, conv_w, conv_b, dt_bias, a_log, d_skip, ssd_norm_w, batch, seq):
    gs = SSD_GROUPS * SSD_STATE
    L = SSD_CHUNK
    rows = SSD_CHUNKS_PER_STEP * L
    nc = seq // rows
    row_map = lambda b, c: b * nc + c
    const = lambda b, c: (0, 0)
    t = jnp.arange(L)[:, None]
    src = jnp.arange(2 * L)[None, :]
    shift = jnp.concatenate([(src == L + t - (SSD_CONV - 1 - k)) for k in range(SSD_CONV - 1)],
                            axis=0).astype(BF16)
    triu = (jnp.arange(L)[:, None] <= jnp.arange(L)[None, :]).astype(BF16)
    lane = jnp.arange(LANES)
    pair_blocks, head_blocks = [], []
    for kind in SSD_PAIR_KINDS:
        for p in range(SSD_HEADS // 2):
            src_col = kind * SSD_HEADS + 2 * p + (lane >= SSD_HEAD_DIM)
            pair_blocks.append(jnp.arange(L)[:, None] == src_col[None, :])
    for h in range(SSD_HEADS):
        head_blocks.append(jnp.broadcast_to(jnp.arange(L)[:, None] == SSD_HEADS + h, (L, LANES)))
    sel_pair = jnp.tile(jnp.concatenate(pair_blocks, axis=1), (2, 1)).astype(BF16)
    sel_head = jnp.tile(jnp.concatenate(head_blocks, axis=1), (3, 1)).astype(BF16)
    per_head = lambda v: jnp.broadcast_to(v.astype(F32)[:, None], (SSD_HEADS, L))
    return pl.pallas_call(
        _ssd_kernel,
        grid=(batch, nc),
        in_specs=[
            pl.BlockSpec((rows, SSD_WIDTH), lambda b, c: (row_map(b, c), COL_XS // SSD_WIDTH)),
            pl.BlockSpec((rows, gs), lambda b, c: (row_map(b, c), COL_B // gs)),
            pl.BlockSpec((rows, gs), lambda b, c: (row_map(b, c), COL_C // gs)),
            pl.BlockSpec((rows, SSD_WIDTH), lambda b, c: (row_map(b, c), COL_Z // SSD_WIDTH)),
            pl.BlockSpec((SSD_HEADS, rows), lambda b, c: (0, row_map(b, c))),
            pl.BlockSpec(((SSD_CONV - 1) * L, 2 * L), const),
            pl.BlockSpec((L, L), const),
            pl.BlockSpec(sel_pair.shape, const),
            pl.BlockSpec(sel_head.shape, const),
            pl.BlockSpec((SSD_CONV, SSD_CONV_DIM), const),
            pl.BlockSpec((1, SSD_CONV_DIM), const),
            pl.BlockSpec((SSD_HEADS, L), const),
            pl.BlockSpec((SSD_HEADS, L), const),
            pl.BlockSpec((1, SSD_WIDTH), const),
            pl.BlockSpec((1, SSD_WIDTH), const),
        ],
        out_specs=pl.BlockSpec((rows, SSD_WIDTH), lambda b, c: (row_map(b, c), 0)),
        out_shape=jax.ShapeDtypeStruct((batch * seq, SSD_WIDTH), BF16),
        scratch_shapes=[pltpu.VMEM((L + rows, SSD_CONV_DIM), BF16),
                        pltpu.VMEM((SSD_HEADS // 2, 2 * SSD_HEAD_DIM, SSD_STATE), F32)],
        compiler_params=pltpu.CompilerParams(
            dimension_semantics=("arbitrary", "arbitrary"),
            vmem_limit_bytes=32 * MIB),
        name="ssd",
    )(proj, proj, proj, proj, dt_raw_t, shift, triu, sel_pair, sel_head, conv_w, conv_b.reshape(1, SSD_CONV_DIM),
      per_head(dt_bias), per_head(a_log),
      jnp.repeat(d_skip, SSD_HEAD_DIM).reshape(1, SSD_WIDTH), ssd_norm_w.reshape(1, SSD_WIDTH))


XATT_V_ROWS = XATTN_HEAD_DIM + ATT_SUM_ROWS


def _mem_kv_kernel(mem_ref, nw_ref, w_ref, k_ref, vt_ref, wb_ref):
    @pl.when(pl.program_id(0) == 0)
    def _():
        wb_ref[...] = w_ref[...].astype(BF16)

    x = mem_ref[...]
    h = (x * _rms_scale(x) * nw_ref[...]).astype(BF16)
    kv = jnp.dot(h, wb_ref[...], preferred_element_type=F32)
    k_ref[...] = kv[:, 0:XATTN_WIDTH].astype(BF16)
    ones = jnp.ones((ATT_SUM_ROWS, x.shape[0]), BF16)
    for hd in range(XATTN_HEADS):
        c0 = XATTN_WIDTH + hd * XATTN_HEAD_DIM
        r0 = hd * XATT_V_ROWS
        vt_ref[r0:r0 + XATTN_HEAD_DIM, :] = kv[:, c0:c0 + XATTN_HEAD_DIM].T.astype(BF16)
        vt_ref[r0 + XATTN_HEAD_DIM:r0 + XATT_V_ROWS, :] = ones


def _mem_kv(mem2, mem_norm_w, w_kv, batch, mem_len):
    return pl.pallas_call(
        _mem_kv_kernel,
        grid=(batch,),
        in_specs=[pl.BlockSpec((mem_len, D_MODEL), lambda b: (b, 0)),
                  pl.BlockSpec((1, D_MODEL), lambda b: (0, 0)),
                  pl.BlockSpec((D_MODEL, 2 * XATTN_WIDTH), lambda b: (0, 0),
                               pipeline_mode=pl.Buffered(1))],
        out_specs=[pl.BlockSpec((mem_len, XATTN_WIDTH), lambda b: (b, 0)),
                   pl.BlockSpec((XATTN_HEADS * XATT_V_ROWS, mem_len), lambda b: (b, 0))],
        out_shape=[jax.ShapeDtypeStruct((batch * mem_len, XATTN_WIDTH), BF16),
                   jax.ShapeDtypeStruct((batch * XATTN_HEADS * XATT_V_ROWS, mem_len), BF16)],
        scratch_shapes=[pltpu.VMEM((D_MODEL, 2 * XATTN_WIDTH), BF16)],
        compiler_params=pltpu.CompilerParams(
            dimension_semantics=("arbitrary",), vmem_limit_bytes=32 * MIB),
        name="mem_kv",
    )(mem2, mem_norm_w.reshape(1, D_MODEL), w_kv)


def _xattn_kernel(q_ref, g_ref, k_ref, vt_ref, o_ref):
    cols = [slice(h * XATTN_HEAD_DIM, (h + 1) * XATTN_HEAD_DIM) for h in range(XATTN_HEADS)]
    scores = [lax.dot_general(k_ref[:, c], q_ref[:, c], NT_DIMS, preferred_element_type=F32)
              for c in cols]
    for h, c in enumerate(cols):
        s = scores[h]
        p = jnp.exp2(s - jnp.max(s, axis=0, keepdims=True)).astype(BF16)
        pv = jnp.dot(vt_ref[h * XATT_V_ROWS:(h + 1) * XATT_V_ROWS, :], p,
                     preferred_element_type=F32)
        xo = (pv[0:XATTN_HEAD_DIM, :]
              * (1.0 / pv[XATTN_HEAD_DIM:XATTN_HEAD_DIM + 1, :])).T
        o_ref[:, c] = (xo * _silu(g_ref[:, c].astype(F32))).astype(BF16)


def _xattn(proj, mk, mvt, batch, seq, mem_len):
    nq = seq // XATT_TQ
    return pl.pallas_call(
        _xattn_kernel,
        grid=(batch, nq),
        in_specs=[pl.BlockSpec((XATT_TQ, XATTN_WIDTH), lambda b, q: (b * nq + q, COL_XQ // XATTN_WIDTH)),
                  pl.BlockSpec((XATT_TQ, XATTN_WIDTH), lambda b, q: (b * nq + q, COL_XG // XATTN_WIDTH)),
                  pl.BlockSpec((mem_len, XATTN_WIDTH), lambda b, q: (b, 0)),
                  pl.BlockSpec((XATTN_HEADS * XATT_V_ROWS, mem_len), lambda b, q: (b, 0))],
        out_specs=pl.BlockSpec((XATT_TQ, XATTN_WIDTH), lambda b, q: (b * nq + q, 0)),
        out_shape=jax.ShapeDtypeStruct((batch * seq, XATTN_WIDTH), BF16),
        compiler_params=pltpu.CompilerParams(
            dimension_semantics=("arbitrary", "arbitrary"), vmem_limit_bytes=32 * MIB),
        name="xattn",
    )(proj, proj, mk, mvt)


def _out_proj_kernel(d_ref, s_ref, a_ref, w_ref, nw_ref, x_ref, o_ref, wb_ref):
    @pl.when(pl.program_id(0) == 0)
    def _():
        for k0 in range(0, D_MIX, OUT_SUB):
            wb_ref[k0:k0 + OUT_SUB, :] = w_ref[k0:k0 + OUT_SUB, :].astype(BF16)

    for i in range(OUT_TM // OUT_SUB):
        r = slice(i * OUT_SUB, (i + 1) * OUT_SUB)
        y = jnp.dot(d_ref[r, :], wb_ref[0:DIFF_WIDTH, :], preferred_element_type=F32)
        y = y + jnp.dot(s_ref[r, :], wb_ref[DIFF_WIDTH:DIFF_WIDTH + SSD_WIDTH, :],
                        preferred_element_type=F32)
        y = y + jnp.dot(a_ref[r, :], wb_ref[DIFF_WIDTH + SSD_WIDTH:D_MIX, :],
                        preferred_element_type=F32)
        o_ref[r, :] = x_ref[r, :] + y * _rms_scale(y) * nw_ref[...]


def _out_proj(diff_out, ssd_out, xattn_out, w_out, post_norm_w, x2):
    m = x2.shape[0]
    return pl.pallas_call(
        _out_proj_kernel,
        grid=(m // OUT_TM,),
        in_specs=[pl.BlockSpec((OUT_TM, DIFF_WIDTH), lambda i: (i, 0)),
                  pl.BlockSpec((OUT_TM, SSD_WIDTH), lambda i: (i, 0)),
                  pl.BlockSpec((OUT_TM, XATTN_WIDTH), lambda i: (i, 0)),
                  pl.BlockSpec((D_MIX, D_MODEL), lambda i: (0, 0), pipeline_mode=pl.Buffered(1)),
                  pl.BlockSpec((1, D_MODEL), lambda i: (0, 0)),
                  pl.BlockSpec((OUT_TM, D_MODEL), lambda i: (i, 0))],
        out_specs=pl.BlockSpec((OUT_TM, D_MODEL), lambda i: (i, 0)),
        out_shape=jax.ShapeDtypeStruct((m, D_MODEL), F32),
        scratch_shapes=[pltpu.VMEM((D_MIX, D_MODEL), BF16)],
        compiler_params=pltpu.CompilerParams(
            dimension_semantics=("arbitrary",), vmem_limit_bytes=48 * MIB),
        name="out_proj",
    )(diff_out, ssd_out, xattn_out, w_out, post_norm_w.reshape(1, D_MODEL), x2)


def kernel(x, mem, positions, pre_norm_w, w_in, lambda_q1, lambda_k1, lambda_q2, lambda_k2,
           diff_subln_w, conv_w, conv_b, dt_bias, a_log, d_skip, ssd_norm_w, mem_norm_w,
           w_mem_kv, w_out, post_norm_w):
    batch, seq, _ = x.shape
    mem_len = mem.shape[1]
    assert pre_norm_w.shape[0] == 1, "single-layer kernel"
    x2 = x.reshape(batch * seq, D_MODEL)

    w_main, w_v, w_dt, cos, sin = _w_in_prep(jnp.transpose(w_in[0]), pre_norm_w[0], positions)
    lam_vecs = jnp.stack([lambda_q1[0], lambda_k1[0], lambda_q2[0], lambda_k2[0]]).astype(F32)

    proj, v_t, dt_raw_t = _in_proj(x2, w_main, w_v, w_dt, cos, sin)
    diff_out = _diff_attention(proj, v_t, lam_vecs, diff_subln_w[0], batch, seq)
    ssd_out = _ssd(proj, dt_raw_t, conv_w[0], conv_b[0], dt_bias[0], a_log[0], d_skip[0],
                   ssd_norm_w[0], batch, seq)
    mk, mvt = _mem_kv(mem.reshape(batch * mem_len, D_MODEL), mem_norm_w[0], w_mem_kv[0],
                      batch, mem_len)
    xattn_out = _xattn(proj, mk, mvt, batch, seq, mem_len)
    out = _out_proj(diff_out, ssd_out, xattn_out, w_out[0], post_norm_w[0], x2)
    return out.reshape(batch, seq, D_MODEL)
```

```python
import functools
import math

import jax
import jax.numpy as jnp
from jax import lax
from jax.experimental import pallas as pl
from jax.experimental.pallas import tpu as pltpu

F32 = jnp.float32
BF16 = jnp.bfloat16

D_MODEL = 2048
DIFF_HEADS = 8
DIFF_QK_DIM = 64
DIFF_V_DIM = 128
DIFF_WIDTH = DIFF_HEADS * DIFF_V_DIM
SSD_HEADS = 8
SSD_HEAD_DIM = 64
SSD_WIDTH = SSD_HEADS * SSD_HEAD_DIM
SSD_GROUPS = 2
SSD_STATE = 128
SSD_CONV = 4
SSD_CHUNK = 128
SSD_CONV_DIM = SSD_WIDTH + 2 * SSD_GROUPS * SSD_STATE
XATTN_HEADS = 4
XATTN_HEAD_DIM = 128
XATTN_WIDTH = XATTN_HEADS * XATTN_HEAD_DIM
D_MIX = DIFF_WIDTH + SSD_WIDTH + XATTN_WIDTH
ROPE_THETA = 10000.0
NORM_EPS = 1e-6
LAMBDA_INIT = 0.8 - 0.6 * math.exp(-0.3 * 0)

LANES = 128
SUBLANES = 8
MIB = 1024 * 1024

COL_Q = 0
COL_K = COL_Q + DIFF_WIDTH
COL_G = COL_K + DIFF_WIDTH
COL_Z = COL_G + DIFF_WIDTH
COL_XS = COL_Z + SSD_WIDTH
COL_B = COL_XS + SSD_WIDTH
COL_C = COL_B + SSD_GROUPS * SSD_STATE
COL_XQ = COL_C + SSD_GROUPS * SSD_STATE
COL_XG = COL_XQ + XATTN_WIDTH
N_MAIN = COL_XG + XATTN_WIDTH
REF_V = 2 * DIFF_WIDTH
REF_G = REF_V + DIFF_WIDTH
REF_DT = REF_G + DIFF_WIDTH + SSD_WIDTH + SSD_CONV_DIM
REF_XQ = REF_DT + SSD_HEADS
D_IN = REF_XQ + 2 * XATTN_WIDTH

PROJ_TM = 512
PROJ_TN = 512
NORM_ROWS = 256
ATT_TQ = 512
ATT_TK = 512
ATT_HEADS_PER_STEP = 4
ATT_SUM_ROWS = 16
ATT_AHEAD = 2
XATT_TQ = 1024
OUT_TM = 512
OUT_SUB = 256
ROPE_TM = 512
PREP_SLOTS = 3
SSD_CHUNKS_PER_STEP = 8
DT_ROWS = 16

NT_DIMS = (((1,), (1,)), ((), ()))
TN_DIMS = (((0,), (0,)), ((), ()))


def _rms_scale(x):
    return lax.rsqrt(jnp.mean(x * x, axis=-1, keepdims=True) + NORM_EPS)


def _silu(x):
    return x * jax.nn.sigmoid(x)


def _rope_table_rows(pos_ref, invf_ref, cos_ref, sin_ref):
    ang = pos_ref[...].astype(F32) * invf_ref[...]
    lane = lax.broadcasted_iota(jnp.int32, ang.shape, 1)
    first_half = (lane % DIFF_QK_DIM) < (DIFF_QK_DIM // 2)
    s = jnp.sin(ang)
    cos_ref[...] = jnp.cos(ang)
    sin_ref[...] = jnp.where(first_half, -s, s)


def _rope(x, cos, sin_signed):
    lane = lax.broadcasted_iota(jnp.int32, x.shape, 1)
    first_half = (lane % DIFF_QK_DIM) < (DIFF_QK_DIM // 2)
    half = DIFF_QK_DIM // 2
    rot = jnp.where(first_half, pltpu.roll(x, LANES - half, 1), pltpu.roll(x, half, 1))
    return x * cos + rot * sin_signed


def _w_in_src_row(j):
    main_blocks = N_MAIN // PROJ_TN
    main = jnp.where(j < REF_V // PROJ_TN, j * PROJ_TN,
                     jnp.where(j < (REF_DT - DIFF_WIDTH) // PROJ_TN,
                               j * PROJ_TN + DIFF_WIDTH,
                               j * PROJ_TN + DIFF_WIDTH + SSD_HEADS))
    return pl.multiple_of(jnp.where(j < main_blocks, main,
                                    REF_V + (j - main_blocks) * PROJ_TN), SUBLANES)


def _w_in_prep_kernel(wt_hbm, dt_rows_ref, nw_ref, pos_ref, invf_ref,
                      wm_ref, wv_ref, wdt_ref, cos_ref, sin_ref, ring_ref, sem_ref):
    j = pl.program_id(0)
    main_blocks = N_MAIN // PROJ_TN
    last_w = main_blocks + DIFF_WIDTH // PROJ_TN
    rope = functools.partial(_rope_table_rows, pos_ref, invf_ref, cos_ref, sin_ref)

    def copy(step):
        slot = step % PREP_SLOTS
        return pltpu.make_async_copy(wt_hbm.at[pl.ds(_w_in_src_row(step), PROJ_TN), :],
                                     ring_ref.at[slot], sem_ref.at[slot])

    @pl.when(j == 0)
    def _():
        for s in range(PREP_SLOTS - 1):
            copy(s).start()

    @pl.when(j + PREP_SLOTS - 1 < last_w)
    def _():
        copy(j + PREP_SLOTS - 1).start()

    @pl.when(j < main_blocks)
    def _():
        rope()
        copy(j).wait()
        wm_ref[0] = (ring_ref[j % PREP_SLOTS] * nw_ref[...]).T.astype(BF16)

    @pl.when((j >= main_blocks) & (j < last_w))
    def _():
        rope()
        copy(j).wait()
        wv_ref[...] = (ring_ref[j % PREP_SLOTS] * nw_ref[...]).astype(BF16)

    pl.when(j >= last_w)(rope)

    @pl.when(j == 0)
    def _():
        row = lax.broadcasted_iota(jnp.int32, (DT_ROWS, D_MODEL), 0)
        wdt_ref[...] = jnp.where(row < SSD_HEADS, dt_rows_ref[...] * nw_ref[...],
                                 0.0).astype(BF16)


def _w_in_prep(w_t, pre_norm_w, positions):
    n, k = w_t.shape
    assert n == D_IN and k == D_MODEL
    main_blocks = N_MAIN // PROJ_TN
    v_blocks = DIFF_WIDTH // PROJ_TN
    last_w = main_blocks + v_blocks - 1
    n_pos = positions.size
    steps = n_pos // ROPE_TM
    assert steps > last_w
    inv_freq = 1.0 / (ROPE_THETA ** (jnp.arange(0, DIFF_QK_DIM, 2, dtype=F32) / DIFF_QK_DIM))
    invf = jnp.tile(inv_freq, LANES // inv_freq.shape[0]).reshape(1, LANES)

    return pl.pallas_call(
        _w_in_prep_kernel,
        grid=(steps,),
        in_specs=[pl.BlockSpec(memory_space=pl.ANY),
                  pl.BlockSpec((DT_ROWS, D_MODEL), lambda j: (REF_DT // DT_ROWS, 0)),
                  pl.BlockSpec((1, D_MODEL), lambda j: (0, 0)),
                  pl.BlockSpec((ROPE_TM, 1), lambda j: (j, 0)),
                  pl.BlockSpec((1, LANES), lambda j: (0, 0))],
        out_specs=[pl.BlockSpec((1, D_MODEL, PROJ_TN),
                                lambda j: (jnp.minimum(j, main_blocks - 1), 0, 0)),
                   pl.BlockSpec((PROJ_TN, D_MODEL),
                                lambda j: (jnp.clip(j - main_blocks, 0, v_blocks - 1), 0)),
                   pl.BlockSpec((DT_ROWS, D_MODEL), lambda j: (0, 0)),
                   pl.BlockSpec((ROPE_TM, LANES), lambda j: (j, 0)),
                   pl.BlockSpec((ROPE_TM, LANES), lambda j: (j, 0))],
        out_shape=[jax.ShapeDtypeStruct((main_blocks, k, PROJ_TN), BF16),
                   jax.ShapeDtypeStruct((DIFF_WIDTH, k), BF16),
                   jax.ShapeDtypeStruct((DT_ROWS, k), BF16),
                   jax.ShapeDtypeStruct((n_pos, LANES), F32),
                   jax.ShapeDtypeStruct((n_pos, LANES), F32)],
        scratch_shapes=[pltpu.VMEM((PREP_SLOTS, PROJ_TN, D_MODEL), F32),
                        pltpu.SemaphoreType.DMA((PREP_SLOTS,))],
        compiler_params=pltpu.CompilerParams(
            dimension_semantics=("arbitrary",), vmem_limit_bytes=32 * MIB),
        name="w_in_prep",
    )(w_t, w_t, pre_norm_w.reshape(1, D_MODEL), positions.reshape(n_pos, 1), invf)


def _in_proj_kernel(x_ref, w_ref, wv_ref, wdt_ref, cos_ref, sin_ref,
                    o_ref, vt_ref, dt_ref, h_ref):
    mean_sq = []
    for r0 in range(0, PROJ_TM, NORM_ROWS):
        x = x_ref[r0:r0 + NORM_ROWS, :]
        h_ref[r0:r0 + NORM_ROWS, :] = x.astype(BF16)
        mean_sq.append(jnp.mean(x * x, axis=-1, keepdims=True))
    rs = lax.rsqrt(jnp.concatenate(mean_sq, axis=0) + NORM_EPS)
    q_scale = DIFF_QK_DIM ** -0.5 * math.log2(math.e)
    xq_scale = XATTN_HEAD_DIM ** -0.5 * math.log2(math.e)
    for n in range(N_MAIN // PROJ_TN):
        c0 = n * PROJ_TN
        y = jnp.dot(h_ref[...], w_ref[n], preferred_element_type=F32) * rs
        if c0 < COL_G:
            scale = q_scale if c0 < COL_K else 1.0
            for hd in range(PROJ_TN // LANES):
                y_h = _rope(y[:, hd * LANES:(hd + 1) * LANES], cos_ref[...], sin_ref[...])
                o_ref[:, c0 + hd * LANES:c0 + (hd + 1) * LANES] = (y_h * scale).astype(BF16)
        elif COL_XQ <= c0 < COL_XG:
            o_ref[:, c0:c0 + PROJ_TN] = (y * xq_scale).astype(BF16)
        else:
            o_ref[:, c0:c0 + PROJ_TN] = y.astype(BF16)
    rs_row = jnp.broadcast_to(rs, (PROJ_TM, LANES)).T[0:1, :]
    vt_ref[...] = (lax.dot_general(wv_ref[...], h_ref[...], NT_DIMS,
                                   preferred_element_type=F32) * rs_row).astype(BF16)
    dt_t = lax.dot_general(wdt_ref[...], h_ref[...], NT_DIMS, preferred_element_type=F32)
    dt_ref[...] = dt_t[0:SSD_HEADS, :] * rs_row


def _in_proj(x2, w_main, w_v, w_dt, cos, sin):
    m = x2.shape[0]
    resident = pl.Buffered(1)
    return pl.pallas_call(
        _in_proj_kernel,
        grid=(m // PROJ_TM,),
        in_specs=[pl.BlockSpec((PROJ_TM, D_MODEL), lambda i: (i, 0)),
                  pl.BlockSpec((N_MAIN // PROJ_TN, D_MODEL, PROJ_TN), lambda i: (0, 0, 0),
                               pipeline_mode=resident),
                  pl.BlockSpec((DIFF_WIDTH, D_MODEL), lambda i: (0, 0), pipeline_mode=resident),
                  pl.BlockSpec((DT_ROWS, D_MODEL), lambda i: (0, 0), pipeline_mode=resident),
                  pl.BlockSpec((PROJ_TM, LANES), lambda i: (i, 0)),
                  pl.BlockSpec((PROJ_TM, LANES), lambda i: (i, 0))],
        out_specs=[pl.BlockSpec((PROJ_TM, N_MAIN), lambda i: (i, 0)),
                   pl.BlockSpec((DIFF_WIDTH, PROJ_TM), lambda i: (0, i)),
                   pl.BlockSpec((SSD_HEADS, PROJ_TM), lambda i: (0, i))],
        out_shape=[jax.ShapeDtypeStruct((m, N_MAIN), BF16),
                   jax.ShapeDtypeStruct((DIFF_WIDTH, m), BF16),
                   jax.ShapeDtypeStruct((SSD_HEADS, m), F32)],
        scratch_shapes=[pltpu.VMEM((PROJ_TM, D_MODEL), BF16)],
        compiler_params=pltpu.CompilerParams(
            dimension_semantics=("arbitrary",),
            vmem_limit_bytes=56 * MIB),
        name="in_proj",
    )(x2, w_main, w_v, w_dt, cos, sin)


def _diff_attn_kernel(lam_ref, q_ref, k_ref, v_ref, g_ref, subw_ref, o_ref, vt_ref, *, seq):
    qi = pl.program_id(2)
    heads = ATT_HEADS_PER_STEP

    @pl.when(qi == 0)
    def _():
        for g in range(heads):
            vt_ref[g, 0:DIFF_V_DIM, :] = v_ref[g * DIFF_V_DIM:(g + 1) * DIFF_V_DIM, :]
        vt_ref[:, DIFF_V_DIM:, :] = jnp.ones((heads, ATT_SUM_ROWS, seq), BF16)

    lane = lax.broadcasted_iota(jnp.int32, (ATT_TQ, LANES), 1)
    q_maps = []
    for g in range(heads):
        q = q_ref[:, g * LANES:(g + 1) * LANES]
        q_maps.append((jnp.where(lane < DIFF_QK_DIM, q, jnp.zeros_like(q)),
                       jnp.where(lane >= DIFF_QK_DIM, q, jnp.zeros_like(q))))

    lv = lam_ref[...]
    lam = (jnp.exp(jnp.sum(lv[0:1] * lv[1:2], axis=-1, keepdims=True))
           - jnp.exp(jnp.sum(lv[2:3] * lv[3:4], axis=-1, keepdims=True)) + LAMBDA_INIT)

    def run(n_tiles):
        diag = (n_tiles - 1) * ATT_TK
        half = ATT_TK // 2
        blocks = [(j * ATT_TK, (j + 1) * ATT_TK, 0, False) for j in range(n_tiles - 1)]
        blocks += [(diag, diag + half, 0, True), (diag + half, diag + ATT_TK, half, True)]
        items = [(blk, g, c) for blk in blocks for g in range(heads) for c in range(2)]

        def scores(blk, g, c):
            k0, k1, q0, masked = blk
            kt = k_ref[k0:k1, g * LANES:(g + 1) * LANES]
            s = lax.dot_general(kt, q_maps[g][c][q0:, :], NT_DIMS, preferred_element_type=F32)
            if not masked:
                return s
            key = lax.broadcasted_iota(jnp.int32, s.shape, 0) + (k0 - diag)
            qry = lax.broadcasted_iota(jnp.int32, s.shape, 1) + q0
            return jnp.where(key <= qry, s, -jnp.inf)

        pending = {t: scores(*items[t]) for t in range(min(ATT_AHEAD, len(items)))}
        m, acc = {}, {}
        for t, (blk, g, c) in enumerate(items):
            k0, k1, q0, _ = blk
            i = 2 * g + c
            s = pending.pop(t)
            m_new = jnp.max(s, axis=0, keepdims=True)
            first = i not in m
            if not first:
                m_old = m[i][:, q0:]
                m_new = jnp.maximum(m_old, m_new)
            p = jnp.exp2(s - m_new).astype(BF16)
            if t + ATT_AHEAD < len(items):
                pending[t + ATT_AHEAD] = scores(*items[t + ATT_AHEAD])
            pv = jnp.dot(vt_ref[g, :, k0:k1], p, preferred_element_type=F32)
            if first:
                m[i], acc[i] = m_new, pv
            else:
                new_acc = jnp.exp2(m_old - m_new) * acc[i][:, q0:] + pv
                if q0:
                    m_new = jnp.concatenate([m[i][:, :q0], m_new], axis=1)
                    new_acc = jnp.concatenate([acc[i][:, :q0], new_acc], axis=1)
                m[i], acc[i] = m_new, new_acc

        for g in range(heads):
            c = slice(g * LANES, (g + 1) * LANES)
            a1, a2 = acc[2 * g], acc[2 * g + 1]
            l1 = a1[DIFF_V_DIM:DIFF_V_DIM + 1, :]
            l2 = a2[DIFF_V_DIM:DIFF_V_DIM + 1, :]
            o_t = (a1[0:DIFF_V_DIM, :] * (1.0 / l1)
                   - lam * (a2[0:DIFF_V_DIM, :] * (1.0 / l2)))
            o = o_t.T
            o = o * _rms_scale(o) * subw_ref[...] * (1.0 - LAMBDA_INIT)
            o_ref[:, c] = (o * _silu(g_ref[:, c].astype(F32))).astype(BF16)

    for n in range(seq // ATT_TQ):
        pl.when(qi == n)(functools.partial(run, n + 1))


def _diff_attention(proj, v_t, lam_vecs, subln_w, batch, seq):
    assert ATT_TQ == ATT_TK and seq % ATT_TQ == 0 and DIFF_V_DIM == LANES
    nq = seq // ATT_TQ
    heads = ATT_HEADS_PER_STEP
    width = heads * LANES
    kernel = functools.partial(_diff_attn_kernel, seq=seq)
    return pl.pallas_call(
        kernel,
        grid=(batch, DIFF_HEADS // heads, nq),
        in_specs=[
            pl.BlockSpec((4, DIFF_QK_DIM), lambda b, h, q: (0, 0)),
            pl.BlockSpec((ATT_TQ, width), lambda b, h, q: (b * nq + q, COL_Q // width + h)),
            pl.BlockSpec((seq, width), lambda b, h, q: (b, COL_K // width + h)),
            pl.BlockSpec((width, seq), lambda b, h, q: (h, b)),
            pl.BlockSpec((ATT_TQ, width), lambda b, h, q: (b * nq + q, COL_G // width + h)),
            pl.BlockSpec((1, DIFF_V_DIM), lambda b, h, q: (0, 0)),
        ],
        out_specs=pl.BlockSpec((ATT_TQ, width), lambda b, h, q: (b * nq + q, h)),
        out_shape=jax.ShapeDtypeStruct((batch * seq, DIFF_WIDTH), BF16),
        scratch_shapes=[pltpu.VMEM((heads, DIFF_V_DIM + ATT_SUM_ROWS, seq), BF16)],
        compiler_params=pltpu.CompilerParams(
            dimension_semantics=("arbitrary", "arbitrary", "arbitrary"),
            vmem_limit_bytes=48 * MIB),
        name="diff_attn",
    )(lam_vecs, proj, proj, v_t, proj, subln_w.reshape(1, DIFF_V_DIM))


def _ssd_kernel(xs_ref, b_ref, c_ref, z_ref, dt_ref, shift_ref, triu_ref, selp_ref, selh_ref,
                cw_ref, cb_ref, dtb_ref, alog_ref, dsk_ref, nw_ref, o_ref, xext_ref, state_ref):
    L = SSD_CHUNK
    T = SSD_CHUNKS_PER_STEP
    gs = SSD_GROUPS * SSD_STATE
    pairs = SSD_HEADS // 2

    @pl.when(pl.program_id(1) == 0)
    def _():
        xext_ref[0:L, :] = jnp.zeros((L, SSD_CONV_DIM), BF16)
        state_ref[...] = jnp.zeros(state_ref.shape, F32)

    xext_ref[L:(T + 1) * L, 0:SSD_WIDTH] = xs_ref[...]
    xext_ref[L:(T + 1) * L, SSD_WIDTH:SSD_WIDTH + gs] = b_ref[...]
    xext_ref[L:(T + 1) * L, SSD_WIDTH + gs:SSD_CONV_DIM] = c_ref[...]

    carry = {"state": [state_ref[p] for p in range(pairs)], "cols_split": [None] * T}
    programs = [_ssd_chunk(c, carry, xext_ref, z_ref, dt_ref, shift_ref, triu_ref, cw_ref,
                           cb_ref, dtb_ref, alog_ref, dsk_ref, nw_ref, o_ref)
                for c in range(T)]

    def stage():
        for prog in programs:
            next(prog)

    stage()
    stage()
    split = jnp.concatenate(carry["cols_split"], axis=0)
    carry["spread"] = jnp.concatenate(
        [jnp.dot(split[:, 0:2 * LANES], selp_ref[...], preferred_element_type=F32),
         jnp.dot(split, selh_ref[...], preferred_element_type=F32)], axis=1)
    stage()
    stage()
    stage()
    for p in range(pairs):
        state_ref[p] = carry["state"][p]
    xext_ref[0:L, :] = xext_ref[T * L:(T + 1) * L, :]


SSD_PAIR_KINDS = (0, 2)


def _ssd_chunk(c, carry, xext_ref, z_ref, dt_ref, shift_ref, triu_ref, cw_ref,
               cb_ref, dtb_ref, alog_ref, dsk_ref, nw_ref, o_ref):
    L = SSD_CHUNK
    H = SSD_HEADS
    gs = SSD_GROUPS * SSD_STATE
    rows_c = slice(c * L, (c + 1) * L)

    delayed = jnp.dot(shift_ref[...], xext_ref[c * L:(c + 2) * L, :],
                      preferred_element_type=F32)
    conv = (cb_ref[...] + xext_ref[(c + 1) * L:(c + 2) * L, :].astype(F32)
            * cw_ref[SSD_CONV - 1:SSD_CONV, :])
    for k in range(SSD_CONV - 1):
        conv = conv + delayed[k * L:(k + 1) * L, :] * cw_ref[k:k + 1, :]
    xbc = _silu(conv)
    xs = xbc[:, 0:SSD_WIDTH]
    bm = xbc[:, SSD_WIDTH:SSD_WIDTH + gs].astype(BF16)
    cm = xbc[:, SSD_WIDTH + gs:SSD_CONV_DIM].astype(BF16)
    yield

    dt_in = dt_ref[:, rows_c] + dtb_ref[...]
    dt = jnp.maximum(dt_in, 0.0) + jnp.log1p(jnp.exp(-jnp.abs(dt_in)))
    a_dt = dt * (-jnp.exp(alog_ref[...]))
    hi = a_dt.astype(BF16)
    r1 = a_dt - hi.astype(F32)
    mid = r1.astype(BF16)
    lo = (r1 - mid.astype(F32)).astype(BF16)
    triu = triu_ref[...]
    a_cs = (jnp.dot(hi, triu, preferred_element_type=F32)
            + jnp.dot(mid, triu, preferred_element_type=F32)
            + jnp.dot(lo, triu, preferred_element_type=F32))
    a_end = a_cs[:, L - 1:L]
    chunk_decay = jnp.exp(a_end)
    rows = jnp.concatenate([dt * jnp.exp(a_end - a_cs), a_cs, jnp.exp(a_cs),
                            jnp.zeros((L - 3 * H, L), F32)], axis=0)
    cols = rows.T
    c_hi = cols.astype(BF16)
    c_r1 = cols - c_hi.astype(F32)
    c_mid = c_r1.astype(BF16)
    c_lo = (c_r1 - c_mid.astype(F32)).astype(BF16)
    carry["cols_split"][c] = jnp.concatenate([c_hi, c_mid, c_lo], axis=1)
    yield
    spread = carry["spread"][rows_c, :]

    row = lax.broadcasted_iota(jnp.int32, (L, L), 0)
    col = lax.broadcasted_iota(jnp.int32, (L, L), 1)
    lower = row >= col
    lane = lax.broadcasted_iota(jnp.int32, (L, LANES), 1)
    left = lane < SSD_HEAD_DIM
    top = row < SSD_HEAD_DIM
    pairs = SSD_HEADS // 2

    def pair_lanes(slot, p):
        i = slot * pairs + p
        return spread[:, i * LANES:(i + 1) * LANES]

    def head_lanes(h):
        i = len(SSD_PAIR_KINDS) * pairs + h
        return spread[:, i * LANES:(i + 1) * LANES]

    group_cb = []
    for g in range(SSD_GROUPS):
        group_cb.append(lax.dot_general(cm[:, g * SSD_STATE:(g + 1) * SSD_STATE],
                                        bm[:, g * SSD_STATE:(g + 1) * SSD_STATE],
                                        NT_DIMS, preferred_element_type=F32))
    yield

    group_of = lambda p: (2 * p) // (SSD_HEADS // SSD_GROUPS)
    y_diag, chunk_state = [], []
    for p in range(pairs):
        bg = bm[:, group_of(p) * SSD_STATE:(group_of(p) + 1) * SSD_STATE]
        xs_p = xs[:, p * LANES:(p + 1) * LANES]
        xs_b = xs_p.astype(BF16)
        y_heads = []
        for hh in range(2):
            h = 2 * p + hh
            seg = head_lanes(h) - a_cs[h:h + 1, :]
            decay = jnp.exp(jnp.where(lower, seg, -jnp.inf))
            scores = (group_cb[group_of(p)] * decay * dt[h:h + 1, :]).astype(BF16)
            y_heads.append(jnp.dot(scores, xs_b, preferred_element_type=F32))
        y_diag.append(jnp.where(left, y_heads[0], y_heads[1]))
        x_w = (xs_p * pair_lanes(0, p)).astype(BF16)
        chunk_state.append(lax.dot_general(x_w, bg, TN_DIMS, preferred_element_type=F32))
    yield

    ys, new_state = [], []
    for p in range(pairs):
        cg = cm[:, group_of(p) * SSD_STATE:(group_of(p) + 1) * SSD_STATE]
        st = carry["state"][p]
        y_off = (lax.dot_general(cg, st.astype(BF16), NT_DIMS, preferred_element_type=F32)
                 * pair_lanes(1, p))
        st_decay = jnp.where(top, chunk_decay[2 * p:2 * p + 1, :],
                             chunk_decay[2 * p + 1:2 * p + 2, :])
        new_state.append(st * st_decay + chunk_state[p])
        y = (y_diag[p] + y_off
             + xs[:, p * LANES:(p + 1) * LANES] * dsk_ref[:, p * LANES:(p + 1) * LANES])
        ys.append(y * _silu(z_ref[rows_c, p * LANES:(p + 1) * LANES].astype(F32)))
    carry["state"] = new_state

    grp_pairs = pairs // SSD_GROUPS
    grp_width = SSD_WIDTH // SSD_GROUPS
    for g in range(SSD_GROUPS):
        members = ys[g * grp_pairs:(g + 1) * grp_pairs]
        ssq = sum(jnp.sum(y * y, axis=-1, keepdims=True) for y in members)
        inv = lax.rsqrt(ssq / grp_width + NORM_EPS)
        for i, y in enumerate(members):
            c0 = (g * grp_pairs + i) * LANES
            o_ref[rows_c, c0:c0 + LANES] = (y * inv * nw_ref[:, c0:c0 + LANES]).astype(BF16)
    yield


def _ssd(proj, dt_raw_t, conv_w, conv_b, dt_bias, a_log, d_skip, ssd_norm_w, batch, seq):
    gs = SSD_GROUPS * SSD_STATE
    L = SSD_CHUNK
    rows = SSD_CHUNKS_PER_STEP * L
    nc = seq // rows
    row_map = lambda b, c: b * nc + c
    const = lambda b, c: (0, 0)
    t = jnp.arange(L)[:, None]
    src = jnp.arange(2 * L)[None, :]
    shift = jnp.concatenate([(src == L + t - (SSD_CONV - 1 - k)) for k in range(SSD_CONV - 1)],
                            axis=0).astype(BF16)
    triu = (jnp.arange(L)[:, None] <= jnp.arange(L)[None, :]).astype(BF16)
    lane = jnp.arange(LANES)
    pair_blocks, head_blocks = [], []
    for kind in SSD_PAIR_KINDS:
        for p in range(SSD_HEADS // 2):
            src_col = kind * SSD_HEADS + 2 * p + (lane >= SSD_HEAD_DIM)
            pair_blocks.append(jnp.arange(L)[:, None] == src_col[None, :])
    for h in range(SSD_HEADS):
        head_blocks.append(jnp.broadcast_to(jnp.arange(L)[:, None] == SSD_HEADS + h, (L, LANES)))
    sel_pair = jnp.tile(jnp.concatenate(pair_blocks, axis=1), (2, 1)).astype(BF16)
    sel_head = jnp.tile(jnp.concatenate(head_blocks, axis=1), (3, 1)).astype(BF16)
    per_head = lambda v: jnp.broadcast_to(v.astype(F32)[:, None], (SSD_HEADS, L))
    return pl.pallas_call(
        _ssd_kernel,
        grid=(batch, nc),
        in_specs=[
            pl.BlockSpec((rows, SSD_WIDTH), lambda b, c: (row_map(b, c), COL_XS // SSD_WIDTH)),
            pl.BlockSpec((rows, gs), lambda b, c: (row_map(b, c), COL_B // gs)),
            pl.BlockSpec((rows, gs), lambda b, c: (row_map(b, c), COL_C // gs)),
            pl.BlockSpec((rows, SSD_WIDTH), lambda b, c: (row_map(b, c), COL_Z // SSD_WIDTH)),
            pl.BlockSpec((SSD_HEADS, rows), lambda b, c: (0, row_map(b, c))),
            pl.BlockSpec(((SSD_CONV - 1) * L, 2 * L), const),
            pl.BlockSpec((L, L), const),
            pl.BlockSpec(sel_pair.shape, const),
            pl.BlockSpec(sel_head.shape, const),
            pl.BlockSpec((SSD_CONV, SSD_CONV_DIM), const),
            pl.BlockSpec((1, SSD_CONV_DIM), const),
            pl.BlockSpec((SSD_HEADS, L), const),
            pl.BlockSpec((SSD_HEADS, L), const),
            pl.BlockSpec((1, SSD_WIDTH), const),
            pl.BlockSpec((1, SSD_WIDTH), const),
        ],
        out_specs=pl.BlockSpec((rows, SSD_WIDTH), lambda b, c: (row_map(b, c), 0)),
        out_shape=jax.ShapeDtypeStruct((batch * seq, SSD_WIDTH), BF16),
        scratch_shapes=[pltpu.VMEM((L + rows, SSD_CONV_DIM), BF16),
                        pltpu.VMEM((SSD_HEADS // 2, 2 * SSD_HEAD_DIM, SSD_STATE), F32)],
        compiler_params=pltpu.CompilerParams(
            dimension_semantics=("arbitrary", "arbitrary"),
            vmem_limit_bytes=32 * MIB),
        name="ssd",
    )(proj, proj, proj, proj, dt_raw_t, shift, triu, sel_pair, sel_head, conv_w, conv_b.reshape(1, SSD_CONV_DIM),
      per_head(dt_bias), per_head(a_log),
      jnp.repeat(d_skip, SSD_HEAD_DIM).reshape(1, SSD_WIDTH), ssd_norm_w.reshape(1, SSD_WIDTH))


XATT_V_ROWS = XATTN_HEAD_DIM + ATT_SUM_ROWS


def _mem_kv_kernel(mem_ref, nw_ref, w_ref, k_ref, vt_ref, wb_ref):
    @pl.when(pl.program_id(0) == 0)
    def _():
        wb_ref[...] = w_ref[...].astype(BF16)

    x = mem_ref[...]
    h = (x * _rms_scale(x) * nw_ref[...]).astype(BF16)
    kv = jnp.dot(h, wb_ref[...], preferred_element_type=F32)
    k_ref[...] = kv[:, 0:XATTN_WIDTH].astype(BF16)
    ones = jnp.ones((ATT_SUM_ROWS, x.shape[0]), BF16)
    for hd in range(XATTN_HEADS):
        c0 = XATTN_WIDTH + hd * XATTN_HEAD_DIM
        r0 = hd * XATT_V_ROWS
        vt_ref[r0:r0 + XATTN_HEAD_DIM, :] = kv[:, c0:c0 + XATTN_HEAD_DIM].T.astype(BF16)
        vt_ref[r0 + XATTN_HEAD_DIM:r0 + XATT_V_ROWS, :] = ones


def _mem_kv(mem2, mem_norm_w, w_kv, batch, mem_len):
    return pl.pallas_call(
        _mem_kv_kernel,
        grid=(batch,),
        in_specs=[pl.BlockSpec((mem_len, D_MODEL), lambda b: (b, 0)),
                  pl.BlockSpec((1, D_MODEL), lambda b: (0, 0)),
                  pl.BlockSpec((D_MODEL, 2 * XATTN_WIDTH), lambda b: (0, 0),
                               pipeline_mode=pl.Buffered(1))],
        out_specs=[pl.BlockSpec((mem_len, XATTN_WIDTH), lambda b: (b, 0)),
                   pl.BlockSpec((XATTN_HEADS * XATT_V_ROWS, mem_len), lambda b: (b, 0))],
        out_shape=[jax.ShapeDtypeStruct((batch * mem_len, XATTN_WIDTH), BF16),
                   jax.ShapeDtypeStruct((batch * XATTN_HEADS * XATT_V_ROWS, mem_len), BF16)],
        scratch_shapes=[pltpu.VMEM((D_MODEL, 2 * XATTN_WIDTH), BF16)],
        compiler_params=pltpu.CompilerParams(
            dimension_semantics=("arbitrary",), vmem_limit_bytes=32 * MIB),
        name="mem_kv",
    )(mem2, mem_norm_w.reshape(1, D_MODEL), w_kv)


def _xattn_kernel(q_ref, g_ref, k_ref, vt_ref, o_ref):
    cols = [slice(h * XATTN_HEAD_DIM, (h + 1) * XATTN_HEAD_DIM) for h in range(XATTN_HEADS)]
    scores = [lax.dot_general(k_ref[:, c], q_ref[:, c], NT_DIMS, preferred_element_type=F32)
              for c in cols]
    for h, c in enumerate(cols):
        s = scores[h]
        p = jnp.exp2(s - jnp.max(s, axis=0, keepdims=True)).astype(BF16)
        pv = jnp.dot(vt_ref[h * XATT_V_ROWS:(h + 1) * XATT_V_ROWS, :], p,
                     preferred_element_type=F32)
        xo = (pv[0:XATTN_HEAD_DIM, :]
              * (1.0 / pv[XATTN_HEAD_DIM:XATTN_HEAD_DIM + 1, :])).T
        o_ref[:, c] = (xo * _silu(g_ref[:, c].astype(F32))).astype(BF16)


def _xattn(proj, mk, mvt, batch, seq, mem_len):
    nq = seq // XATT_TQ
    return pl.pallas_call(
        _xattn_kernel,
        grid=(batch, nq),
        in_specs=[pl.BlockSpec((XATT_TQ, XATTN_WIDTH), lambda b, q: (b * nq + q, COL_XQ // XATTN_WIDTH)),
                  pl.BlockSpec((XATT_TQ, XATTN_WIDTH), lambda b, q: (b * nq + q, COL_XG // XATTN_WIDTH)),
                  pl.BlockSpec((mem_len, XATTN_WIDTH), lambda b, q: (b, 0)),
                  pl.BlockSpec((XATTN_HEADS * XATT_V_ROWS, mem_len), lambda b, q: (b, 0))],
        out_specs=pl.BlockSpec((XATT_TQ, XATTN_WIDTH), lambda b, q: (b * nq + q, 0)),
        out_shape=jax.ShapeDtypeStruct((batch * seq, XATTN_WIDTH), BF16),
        compiler_params=pltpu.CompilerParams(
            dimension_semantics=("arbitrary", "arbitrary"), vmem_limit_bytes=32 * MIB),
        name="xattn",
    )(proj, proj, mk, mvt)


def _out_proj_kernel(d_ref, s_ref, a_ref, w_ref, nw_ref, x_ref, o_ref, wb_ref):
    @pl.when(pl.program_id(0) == 0)
    def _():
        for k0 in range(0, D_MIX, OUT_SUB):
            wb_ref[k0:k0 + OUT_SUB, :] = w_ref[k0:k0 + OUT_SUB, :].astype(BF16)

    for i in range(OUT_TM // OUT_SUB):
        r = slice(i * OUT_SUB, (i + 1) * OUT_SUB)
        y = jnp.dot(d_ref[r, :], wb_ref[0:DIFF_WIDTH, :], preferred_element_type=F32)
        y = y + jnp.dot(s_ref[r, :], wb_ref[DIFF_WIDTH:DIFF_WIDTH + SSD_WIDTH, :],
                        preferred_element_type=F32)
        y = y + jnp.dot(a_ref[r, :], wb_ref[DIFF_WIDTH + SSD_WIDTH:D_MIX, :],
                        preferred_element_type=F32)
        o_ref[r, :] = x_ref[r, :] + y * _rms_scale(y) * nw_ref[...]


def _out_proj(diff_out, ssd_out, xattn_out, w_out, post_norm_w, x2):
    m = x2.shape[0]
    return pl.pallas_call(
        _out_proj_kernel,
        grid=(m // OUT_TM,),
        in_specs=[pl.BlockSpec((OUT_TM, DIFF_WIDTH), lambda i: (i, 0)),
                  pl.BlockSpec((OUT_TM, SSD_WIDTH), lambda i: (i, 0)),
                  pl.BlockSpec((OUT_TM, XATTN_WIDTH), lambda i: (i, 0)),
                  pl.BlockSpec((D_MIX, D_MODEL), lambda i: (0, 0), pipeline_mode=pl.Buffered(1)),
                  pl.BlockSpec((1, D_MODEL), lambda i: (0, 0)),
                  pl.BlockSpec((OUT_TM, D_MODEL), lambda i: (i, 0))],
        out_specs=pl.BlockSpec((OUT_TM, D_MODEL), lambda i: (i, 0)),
        out_shape=jax.ShapeDtypeStruct((m, D_MODEL), F32),
        scratch_shapes=[pltpu.VMEM((D_MIX, D_MODEL), BF16)],
        compiler_params=pltpu.CompilerParams(
            dimension_semantics=("arbitrary",), vmem_limit_bytes=48 * MIB),
        name="out_proj",
    )(diff_out, ssd_out, xattn_out, w_out, post_norm_w.reshape(1, D_MODEL), x2)


def kernel(x, mem, positions, pre_norm_w, w_in, lambda_q1, lambda_k1, lambda_q2, lambda_k2,
           diff_subln_w, conv_w, conv_b, dt_bias, a_log, d_skip, ssd_norm_w, mem_norm_w,
           w_mem_kv, w_out, post_norm_w):
    batch, seq, _ = x.shape
    mem_len = mem.shape[1]
    assert pre_norm_w.shape[0] == 1, "single-layer kernel"
    x2 = x.reshape(batch * seq, D_MODEL)

    w_main, w_v, w_dt, cos, sin = _w_in_prep(jnp.transpose(w_in[0]), pre_norm_w[0], positions)
    lam_vecs = jnp.stack([lambda_q1[0], lambda_k1[0], lambda_q2[0], lambda_k2[0]]).astype(F32)

    proj, v_t, dt_raw_t = _in_proj(x2, w_main, w_v, w_dt, cos, sin)
    diff_out = _diff_attention(proj, v_t, lam_vecs, diff_subln_w[0], batch, seq)
    ssd_out = _ssd(proj, dt_raw_t, conv_w[0], conv_b[0], dt_bias[0], a_log[0], d_skip[0],
                   ssd_norm_w[0], batch, seq)
    mk, mvt = _mem_kv(mem.reshape(batch * mem_len, D_MODEL), mem_norm_w[0], w_mem_kv[0],
                      batch, mem_len)
    xattn_out = _xattn(proj, mk, mvt, batch, seq, mem_len)
    out = _out_proj(diff_out, ssd_out, xattn_out, w_out[0], post_norm_w[0], x2)
    return out.reshape(batch, seq, D_MODEL)
```
